```python
import jax, jax.numpy as jnp
from jax import lax
import numpy as np

D_MODEL = 2048
BATCH = 4
SEQ = 4096
DEPTH = 1

CHUNK = 64
N_MEM = 256
GLA_HEADS = 4
GLA_VW = D_MODEL // 2
GLA_KW = D_MODEL // 4
GLA_DK = GLA_KW // GLA_HEADS
GLA_DV = GLA_VW // GLA_HEADS
GLA_LOWRANK = 16
GLA_TAU = 16.0
FOX_HEADS = 4
FOX_W = D_MODEL // 4
FOX_DH = FOX_W // FOX_HEADS
FOX_QBLOCK = 128
MEM_HEADS = 4
MEM_W = D_MODEL // 4
MEM_DH = MEM_W // MEM_HEADS
D_MIX = GLA_VW + FOX_W + MEM_W
N_EXPERTS = 32
TOP_K = 4
D_FF = D_MODEL
SWIGLU_LIMIT = 7.0
SWIGLU_ALPHA = 1.702
MOE_BLOCK = 256
EPS = 1e-5

IN_SPLITS = (GLA_KW, GLA_KW, GLA_VW, GLA_VW, GLA_LOWRANK,
             FOX_W, FOX_W, FOX_W, FOX_HEADS,
             MEM_W)
D_IN = sum(IN_SPLITS)

kernel_name = 'hybrid_gla_fox_mem_moe_block'


def rms_norm(x, g):
    xf = x.astype(jnp.float32)
    y = xf * lax.rsqrt(jnp.mean(xf * xf, axis=-1, keepdims=True) + EPS)
    return (y * g.astype(jnp.float32)).astype(x.dtype)


def gla_mixer(q, k, v, a_low, w_a2, b_a):
    B, S = q.shape[:2]
    nc = S // CHUNK
    log_alpha = jax.nn.log_sigmoid((a_low @ w_a2 + b_a).astype(jnp.float32)) / GLA_TAU
    shp_k = (B, nc, CHUNK, GLA_HEADS, GLA_DK)
    qc = q.astype(jnp.float32).reshape(shp_k) * (GLA_DK ** -0.5)
    kc = k.astype(jnp.float32).reshape(shp_k)
    vc = v.astype(jnp.float32).reshape(B, nc, CHUNK, GLA_HEADS, GLA_DV)
    b = jnp.cumsum(log_alpha.reshape(shp_k), axis=2)
    b_end = b[:, :, -1:]
    k_dec = kc * jnp.exp(b_end - b)
    chunk_kv = jnp.einsum('bnchk,bnchv->nbhkv', k_dec, vc)
    chunk_decay = jnp.exp(b_end[:, :, 0]).transpose(1, 0, 2, 3)

    def step(state, inp):
        d, kv = inp
        state = d[..., None] * state + kv
        return state, state

    init = jnp.zeros((B, GLA_HEADS, GLA_DK, GLA_DV), jnp.float32)
    _, states = lax.scan(step, init, (chunk_decay, chunk_kv))
    o = jnp.einsum('bnchk,nbhkv->bnchv', qc, states)
    return o.reshape(B, S, GLA_HEADS, GLA_DV)


def fox_mixer(q, k, v, f_logit, b_f):
    B, S = q.shape[:2]
    qh = q.reshape(B, S, FOX_HEADS, FOX_DH).transpose(0, 2, 1, 3)
    kh = k.reshape(B, S, FOX_HEADS, FOX_DH).transpose(0, 2, 1, 3)
    vh = v.reshape(B, S, FOX_HEADS, FOX_DH).transpose(0, 2, 1, 3)
    log_f = jax.nn.log_sigmoid(f_logit.astype(jnp.float32) + b_f.astype(jnp.float32))
    F = jnp.cumsum(log_f, axis=1).transpose(0, 2, 1)
    scale = FOX_DH ** -0.5
    outs = []
    for i in range(S // FOX_QBLOCK):
        q0, q1 = i * FOX_QBLOCK, (i + 1) * FOX_QBLOCK
        s = jnp.einsum('bhqd,bhkd->bhqk', qh[:, :, q0:q1], kh[:, :, :q1]).astype(jnp.float32) * scale
        s = s + F[:, :, q0:q1, None] - F[:, :, None, :q1]
        mask = jnp.arange(q1)[None, :] <= jnp.arange(q0, q1)[:, None]
        p = jax.nn.softmax(jnp.where(mask, s, -jnp.inf), axis=-1)
        outs.append(jnp.einsum('bhqk,bhkd->bhqd', p.astype(vh.dtype), vh[:, :, :q1]))
    return jnp.concatenate(outs, axis=2).transpose(0, 2, 1, 3)


def mem_mixer(qm, mem_n, w_mem_kv):
    B, S = qm.shape[:2]
    kv = mem_n @ w_mem_kv
    km = kv[..., :MEM_W].reshape(B, -1, MEM_HEADS, MEM_DH)
    vm = kv[..., MEM_W:].reshape(B, -1, MEM_HEADS, MEM_DH)
    qh = qm.reshape(B, S, MEM_HEADS, MEM_DH)
    s = jnp.einsum('bshd,bmhd->bhsm', qh, km).astype(jnp.float32) * (MEM_DH ** -0.5)
    p = jax.nn.softmax(s, axis=-1)
    return jnp.einsum('bhsm,bmhd->bshd', p.astype(vm.dtype), vm)


def moe(xn, w_router, b_router, w_up, b_up, w_down, b_down):
    B, S, D = xn.shape
    N = B * S
    xf = xn.reshape(N, D)
    logits = (xf @ w_router).astype(jnp.float32) + b_router.astype(jnp.float32)
    top_vals, top_idx = lax.top_k(logits, TOP_K)
    gates = jax.nn.softmax(top_vals, axis=-1).astype(xn.dtype)
    A = N * TOP_K
    e_flat = top_idx.reshape(A).astype(jnp.int32)
    g_flat = gates.reshape(A)
    tok_flat = jnp.arange(A, dtype=jnp.int32) // TOP_K
    order = jnp.argsort(e_flat)
    e_sorted = e_flat[order]
    counts = jnp.zeros((N_EXPERTS,), jnp.int32).at[e_flat].add(1)
    padded = (counts + MOE_BLOCK - 1) // MOE_BLOCK * MOE_BLOCK
    pad_end = jnp.cumsum(padded)
    pad_start = pad_end - padded
    start = jnp.cumsum(counts) - counts
    dest = pad_start[e_sorted] + (jnp.arange(A, dtype=jnp.int32) - start[e_sorted])
    n_rows = -(-(A + N_EXPERTS * (MOE_BLOCK - 1)) // MOE_BLOCK) * MOE_BLOCK
    nb = n_rows // MOE_BLOCK
    row_tok = jnp.zeros((n_rows,), jnp.int32).at[dest].set(tok_flat[order])
    row_gate = jnp.zeros((n_rows,), xn.dtype).at[dest].set(g_flat[order])
    block_e = jnp.minimum(jnp.searchsorted(pad_end, jnp.arange(nb, dtype=jnp.int32) * MOE_BLOCK,
                                           side='right'), N_EXPERTS - 1).astype(jnp.int32)

    def expert_block(args):
        toks, e = args
        h = xf[toks] @ w_up[e] + b_up[e]
        gate = jnp.minimum(h[:, :D_FF], SWIGLU_LIMIT)
        up = jnp.clip(h[:, D_FF:], -SWIGLU_LIMIT, SWIGLU_LIMIT)
        act = gate * jax.nn.sigmoid(SWIGLU_ALPHA * gate) * (up + 1.0)
        return act @ w_down[e] + b_down[e]

    y_rows = lax.map(expert_block, (row_tok.reshape(nb, MOE_BLOCK), block_e))
    y_rows = y_rows.reshape(n_rows, D) * row_gate[:, None]
    y = jnp.zeros((N, D), y_rows.dtype).at[row_tok].add(y_rows)
    return y.reshape(B, S, D).astype(xn.dtype)


def setup_inputs(seed: int = 0) -> dict:
    key = jax.random.key(seed)
    ks = jax.random.split(key, 24)
    nrm = jax.random.normal
    L = DEPTH

    def gain(k, *shape):
        return 1.0 + 0.02 * nrm(k, (L,) + shape, jnp.float32)

    return {
        'x': nrm(ks[0], (BATCH, SEQ, D_MODEL), jnp.float32),
        'mem': nrm(ks[1], (BATCH, N_MEM, D_MODEL), jnp.float32),
        'g_attn_norm': gain(ks[2], D_MODEL),
        'g_mem_norm': gain(ks[3], D_MODEL),
        'w_in': nrm(ks[4], (L, D_MODEL, D_IN), jnp.float32) * D_MODEL ** -0.5,
        'w_gla_a2': nrm(ks[5], (L, GLA_LOWRANK, GLA_KW), jnp.float32) * GLA_LOWRANK ** -0.5,
        'b_gla_a': 0.1 * nrm(ks[6], (L, GLA_KW), jnp.float32),
        'g_gla_out': gain(ks[7], GLA_VW),
        'b_fox_f': jax.random.uniform(ks[8], (L, FOX_HEADS), jnp.float32, 1.0, 4.0),
        'g_fox_out': gain(ks[9], FOX_W),
        'w_mem_kv': nrm(ks[10], (L, D_MODEL, 2 * MEM_W), jnp.float32) * D_MODEL ** -0.5,
        'g_mem_out': gain(ks[11], MEM_W),
        'w_out': nrm(ks[12], (L, D_MIX, D_MODEL), jnp.float32) * D_MIX ** -0.5,
        'g_ffn_norm': gain(ks[13], D_MODEL),
        'w_router': nrm(ks[14], (L, D_MODEL, N_EXPERTS), jnp.float32) * D_MODEL ** -0.5,
        'b_router': 0.01 * nrm(ks[15], (L, N_EXPERTS), jnp.float32),
        'w_moe_up': nrm(ks[16], (L, N_EXPERTS, D_MODEL, 2 * D_FF), jnp.float32) * D_MODEL ** -0.5,
        'b_moe_up': 0.01 * nrm(ks[17], (L, N_EXPERTS, 2 * D_FF), jnp.float32),
        'w_moe_down': nrm(ks[18], (L, N_EXPERTS, D_FF, D_MODEL), jnp.float32) * D_FF ** -0.5,
        'b_moe_down': 0.01 * nrm(ks[19], (L, N_EXPERTS, D_MODEL), jnp.float32),
        'g_final': 1.0 + 0.02 * nrm(ks[20], (D_MODEL,), jnp.float32),
    }


def reference(x, mem, g_attn_norm, g_mem_norm, w_in, w_gla_a2, b_gla_a, g_gla_out, b_fox_f,
              g_fox_out, w_mem_kv, g_mem_out, w_out, g_ffn_norm, w_router, b_router,
              w_moe_up, b_moe_up, w_moe_down, b_moe_down, g_final):
    B, S, _ = x.shape
    split_idx = [int(i) for i in np.cumsum(IN_SPLITS)[:-1]]
    for l in range(DEPTH):
        h = rms_norm(x, g_attn_norm[l])
        proj = h @ w_in[l]
        q_g, k_g, v_g, gate_g, a_low, q_f, k_f, v_f, f_f, q_m = jnp.split(proj, split_idx, axis=-1)

        gla = gla_mixer(q_g, k_g, v_g, a_low, w_gla_a2[l], b_gla_a[l])
        gla = rms_norm(gla, g_gla_out[l].reshape(GLA_HEADS, GLA_DV)).reshape(B, S, GLA_VW)
        gla = gla.astype(x.dtype) * jax.nn.silu(gate_g)

        fox = fox_mixer(q_f, k_f, v_f, f_f, b_fox_f[l])
        fox = rms_norm(fox, g_fox_out[l].reshape(FOX_HEADS, FOX_DH)).reshape(B, S, FOX_W)

        memo = mem_mixer(q_m, rms_norm(mem, g_mem_norm[l]), w_mem_kv[l])
        memo = rms_norm(memo, g_mem_out[l].reshape(MEM_HEADS, MEM_DH)).reshape(B, S, MEM_W)

        mix = jnp.concatenate([gla.astype(x.dtype), fox.astype(x.dtype), memo.astype(x.dtype)], axis=-1)
        x = x + mix @ w_out[l]

        x = x + moe(rms_norm(x, g_ffn_norm[l]), w_router[l], b_router[l],
                    w_moe_up[l], b_moe_up[l], w_moe_down[l], b_moe_down[l])
    return rms_norm(x, g_final)
```

```python
import functools

import jax
import jax.numpy as jnp
from jax import lax
from jax.experimental import pallas as pl
from jax.experimental.pallas import tpu as pltpu

EPS = 1e-5
CHUNK = 64
GLA_HEADS = 4
GLA_DK = 128
GLA_DV = 256
GLA_LOWRANK = 16
GLA_TAU = 16.0
FOX_HEADS = 4
FOX_DH = 128
MEM_HEADS = 4
MEM_DH = 128
N_EXPERTS = 32
TOP_K = 4
SWIGLU_LIMIT = 7.0
SWIGLU_ALPHA = 1.702
LANES = 128
MOE_TM = 256
VMEM_LIMIT = 56 * 1024 * 1024

F32 = jnp.float32
BF16 = jnp.bfloat16


def _cparams(sem, vmem=VMEM_LIMIT):
    return pltpu.CompilerParams(dimension_semantics=sem, vmem_limit_bytes=vmem)


def _log_sigmoid(x):
    return jnp.minimum(x, 0.0) - jnp.log1p(jnp.exp(-jnp.abs(x)))


def _rms(x, g):
    return x * lax.rsqrt(jnp.mean(x * x, axis=-1, keepdims=True) + EPS) * g


def _dot(a, b, **kw):
    return jnp.dot(a, b, preferred_element_type=F32, **kw)


def _dot_nt(a, b):
    return lax.dot_general(a, b, (((1,), (1,)), ((), ())), preferred_element_type=F32)


def _dot_tn(a, b):
    return lax.dot_general(a, b, (((0,), (0,)), ((), ())), preferred_element_type=F32)


def _in_proj_kernel(x_ref, g_ref, wa_ref, wb_ref, proj_ref, small_ref, h_scr):
    @pl.when(pl.program_id(1) == 0)
    def _():
        hb = _rms(x_ref[...], g_ref[...]).astype(BF16)
        h_scr[...] = hb
        small_ref[...] = _dot(hb, wb_ref[...])

    proj_ref[...] = _dot(h_scr[...], wa_ref[...]).astype(proj_ref.dtype)


def _in_proj(x2d, g, wa, wb, tm=512, tn=1024):
    n, d = x2d.shape
    na = wa.shape[1]
    return pl.pallas_call(
        _in_proj_kernel,
        grid=(n // tm, na // tn),
        in_specs=[
            pl.BlockSpec((tm, d), lambda i, j: (i, 0)),
            pl.BlockSpec((1, d), lambda i, j: (0, 0)),
            pl.BlockSpec((d, tn), lambda i, j: (0, j)),
            pl.BlockSpec((d, LANES), lambda i, j: (0, 0)),
        ],
        out_specs=[
            pl.BlockSpec((tm, tn), lambda i, j: (i, j)),
            pl.BlockSpec((tm, LANES), lambda i, j: (i, 0)),
        ],
        out_shape=[
            jax.ShapeDtypeStruct((n, na), BF16),
            jax.ShapeDtypeStruct((n, LANES), F32),
        ],
        scratch_shapes=[pltpu.VMEM((tm, d), BF16)],
        compiler_params=_cparams(("parallel", "arbitrary")),
        name="in_proj",
    )(x2d, g, wa, wb)


def _gla_kernel(q_ref, k_ref, v_ref, gate_ref, small_ref, wa2_ref, ba_ref, bf_ref, gout_ref,
                o_ref, f_ref, state_scr, fcar_scr, la_scr, lf_scr, *, n_chunks):
    @pl.when(pl.program_id(1) == 0)
    def _():
        state_scr[...] = jnp.zeros_like(state_scr)
        fcar_scr[...] = jnp.zeros_like(fcar_scr)

    small = small_ref[...]
    la_scr[...] = _log_sigmoid(_dot(small.astype(BF16), wa2_ref[...]) + ba_ref[...]) * (1.0 / GLA_TAU)
    lf_scr[...] = _log_sigmoid(small + bf_ref[...])
    row = lax.broadcasted_iota(jnp.int32, (CHUNK, CHUNK), 0)
    col = lax.broadcasted_iota(jnp.int32, (CHUNK, CHUNK), 1)
    tri = (col <= row).astype(F32)
    scale = GLA_DK ** -0.5

    def chunk_body(c, carry):
        r = pl.ds(pl.multiple_of(c * CHUNK, CHUNK), CHUNK)
        b = _dot(tri, la_scr[r, :], precision=lax.Precision.HIGHEST)
        b_end = b[CHUNK - 1:CHUNK, :]
        k_dec = k_ref[r, :].astype(F32) * jnp.exp(b_end - b)
        decay = jnp.exp(b_end)
        f_cum = _dot(tri, lf_scr[r, :], precision=lax.Precision.HIGHEST) + fcar_scr[...]
        f_ref[r, :] = f_cum
        fcar_scr[...] = f_cum[CHUNK - 1:CHUNK, :]
        for h in range(GLA_HEADS):
            ks = slice(h * GLA_DK, (h + 1) * GLA_DK)
            vs = slice(h * GLA_DV, (h + 1) * GLA_DV)
            st = state_scr[h] * decay[:, ks] + _dot_tn(v_ref[r, vs], k_dec[:, ks].astype(BF16))
            state_scr[h] = st
            o = _dot_nt(q_ref[r, ks], st.astype(BF16)) * scale
            gt = gate_ref[r, vs].astype(F32)
            o_ref[r, vs] = (_rms(o, gout_ref[:, vs]) * (gt * jax.nn.sigmoid(gt))).astype(o_ref.dtype)
        return carry

    lax.fori_loop(0, n_chunks, chunk_body, 0)


def _gla(proj, small, wa2p, ba, bfv, gout, batch, seq, ts=512):
    n = proj.shape[0]
    nsb = seq // ts
    kw = GLA_HEADS * GLA_DK
    vw = GLA_HEADS * GLA_DV
    row = lambda b, s: b * nsb + s
    return pl.pallas_call(
        functools.partial(_gla_kernel, n_chunks=ts // CHUNK),
        grid=(batch, nsb),
        in_specs=[
            pl.BlockSpec((ts, kw), lambda b, s: (row(b, s), 0)),
            pl.BlockSpec((ts, kw), lambda b, s: (row(b, s), 1)),
            pl.BlockSpec((ts, vw), lambda b, s: (row(b, s), 1)),
            pl.BlockSpec((ts, vw), lambda b, s: (row(b, s), 2)),
            pl.BlockSpec((ts, LANES), lambda b, s: (row(b, s), 0)),
            pl.BlockSpec((LANES, kw), lambda b, s: (0, 0)),
            pl.BlockSpec((1, kw), lambda b, s: (0, 0)),
            pl.BlockSpec((1, LANES), lambda b, s: (0, 0)),
            pl.BlockSpec((1, vw), lambda b, s: (0, 0)),
        ],
        out_specs=[
            pl.BlockSpec((ts, vw), lambda b, s: (row(b, s), 0)),
            pl.BlockSpec((ts, LANES), lambda b, s: (row(b, s), 0)),
        ],
        out_shape=[
            jax.ShapeDtypeStruct((n, vw), BF16),
            jax.ShapeDtypeStruct((n, LANES), F32),
        ],
        scratch_shapes=[
            pltpu.VMEM((GLA_HEADS, GLA_DV, GLA_DK), F32),
            pltpu.VMEM((1, LANES), F32),
            pltpu.VMEM((ts, kw), F32),
            pltpu.VMEM((ts, LANES), F32),
        ],
        compiler_params=_cparams(("parallel", "arbitrary")),
        name="gla",
    )(proj, proj, proj, proj, small, wa2p, ba, bfv, gout)


def _fox_kernel(q_ref, k_ref, v_ref, fk_ref, g_ref, o_ref, m_scr, l_scr, acc_scr, *, tq, tk):
    i = pl.program_id(1)
    j = pl.program_id(2)

    @pl.when(j == 0)
    def _():
        m_scr[...] = jnp.full_like(m_scr, -jnp.inf)
        l_scr[...] = jnp.zeros_like(l_scr)
        acc_scr[...] = jnp.zeros_like(acc_scr)

    @pl.when(j <= i)
    def _():
        row = lax.broadcasted_iota(jnp.int32, (tq, tk), 0) + i * tq
        col = lax.broadcasted_iota(jnp.int32, (tq, tk), 1) + j * tk
        visible = col <= row
        scale = FOX_DH ** -0.5
        for h in range(FOX_HEADS):
            hs = slice(h * FOX_DH, (h + 1) * FOX_DH)
            s = _dot_nt(q_ref[:, hs], k_ref[:, hs]) * scale - fk_ref[h:h + 1, :]
            s = jnp.where(visible, s, -jnp.inf)
            m_prev = m_scr[h]
            m_new = jnp.maximum(m_prev, jnp.max(s, axis=-1, keepdims=True))
            alpha = jnp.exp(m_prev - m_new)
            p = jnp.exp(s - m_new)
            l_scr[h] = alpha * l_scr[h] + jnp.sum(p, axis=-1, keepdims=True)
            acc_scr[h] = alpha * acc_scr[h] + _dot(p.astype(BF16), v_ref[:, hs])
            m_scr[h] = m_new

    @pl.when(j == i)
    def _():
        for h in range(FOX_HEADS):
            hs = slice(h * FOX_DH, (h + 1) * FOX_DH)
            o = acc_scr[h] / l_scr[h]
            o_ref[:, hs] = _rms(o, g_ref[:, hs]).astype(o_ref.dtype)


def _fox(proj, f_t, g_fox, batch, seq, tq=512, tk=512):
    n = proj.shape[0]
    w = FOX_HEADS * FOX_DH
    nq, nk = seq // tq, seq // tk
    qcol, kcol, vcol = 3072 // w, 3584 // w, 4096 // w
    return pl.pallas_call(
        functools.partial(_fox_kernel, tq=tq, tk=tk),
        grid=(batch, nq, nk),
        in_specs=[
            pl.BlockSpec((tq, w), lambda b, i, j: (b * nq + i, qcol)),
            pl.BlockSpec((tk, w), lambda b, i, j: (b * nk + jnp.minimum(j, i), kcol)),
            pl.BlockSpec((tk, w), lambda b, i, j: (b * nk + jnp.minimum(j, i), vcol)),
            pl.BlockSpec((None, 8, tk), lambda b, i, j: (b, 0, jnp.minimum(j, i))),
            pl.BlockSpec((1, w), lambda b, i, j: (0, 0)),
        ],
        out_specs=pl.BlockSpec((tq, w), lambda b, i, j: (b * nq + i, 0)),
        out_shape=jax.ShapeDtypeStruct((n, w), BF16),
        scratch_shapes=[
            pltpu.VMEM((FOX_HEADS, tq, 1), F32),
            pltpu.VMEM((FOX_HEADS, tq, 1), F32),
            pltpu.VMEM((FOX_HEADS, tq, FOX_DH), F32),
        ],
        compiler_params=_cparams(("parallel", "parallel", "arbitrary")),
        name="fox",
    )(proj, proj, proj, f_t, g_fox)


def _mem_kv_kernel(m_ref, g_ref, w_ref, kv_ref):
    kv_ref[...] = _dot(_rms(m_ref[...], g_ref[...]).astype(BF16), w_ref[...]).astype(kv_ref.dtype)


def _mem_kv(mem2d, g, w, tm=256):
    n, d = mem2d.shape
    nw = w.shape[1]
    return pl.pallas_call(
        _mem_kv_kernel,
        grid=(n // tm,),
        in_specs=[
            pl.BlockSpec((tm, d), lambda i: (i, 0)),
            pl.BlockSpec((1, d), lambda i: (0, 0)),
            pl.BlockSpec((d, nw), lambda i: (0, 0)),
        ],
        out_specs=pl.BlockSpec((tm, nw), lambda i: (i, 0)),
        out_shape=jax.ShapeDtypeStruct((n, nw), BF16),
        compiler_params=_cparams(("parallel",)),
        name="mem_kv",
    )(mem2d, g, w)


def _mem_attn_kernel(q_ref, k_ref, v_ref, g_ref, o_ref):
    scale = MEM_DH ** -0.5
    for h in range(MEM_HEADS):
        hs = slice(h * MEM_DH, (h + 1) * MEM_DH)
        s = _dot_nt(q_ref[:, hs], k_ref[:, hs]) * scale
        p = jnp.exp(s - jnp.max(s, axis=-1, keepdims=True))
        l = jnp.sum(p, axis=-1, keepdims=True)
        o = _dot((p / l).astype(BF16), v_ref[:, hs])
        o_ref[:, hs] = _rms(o, g_ref[:, hs]).astype(o_ref.dtype)


def _mem_attn(proj, kv, g_mem_out, batch, seq, n_mem, tq=1024):
    n = proj.shape[0]
    w = MEM_HEADS * MEM_DH
    nq = seq // tq
    qcol = 4608 // w
    return pl.pallas_call(
        _mem_attn_kernel,
        grid=(n // tq,),
        in_specs=[
            pl.BlockSpec((tq, w), lambda i: (i, qcol)),
            pl.BlockSpec((n_mem, w), lambda i: (i // nq, 0)),
            pl.BlockSpec((n_mem, w), lambda i: (i // nq, 1)),
            pl.BlockSpec((1, w), lambda i: (0, 0)),
        ],
        out_specs=pl.BlockSpec((tq, w), lambda i: (i, 0)),
        out_shape=jax.ShapeDtypeStruct((n, w), BF16),
        compiler_params=_cparams(("parallel",)),
        name="mem_attn",
    )(proj, kv, kv, g_mem_out)


def _out_proj_kernel(x_ref, gla_ref, fox_ref, mem_ref, w1_ref, w2_ref, w3_ref, g_ref, wr_ref, br_ref,
                     x2_ref, xn_ref, logit_ref):
    x2 = (x_ref[...] + _dot(gla_ref[...], w1_ref[...]) + _dot(fox_ref[...], w2_ref[...])
          + _dot(mem_ref[...], w3_ref[...]))
    x2_ref[...] = x2
    xn = _rms(x2, g_ref[...])
    xn_ref[...] = xn
    logit_ref[...] = _dot(xn, wr_ref[...], precision=lax.Precision.HIGHEST) + br_ref[...]


def _out_proj(x2d, gla, fox, memo, w_out, g_ffn, wr, br, tm=256):
    n, d = x2d.shape
    w1, w2 = gla.shape[1], fox.shape[1]
    const = lambda i: (0, 0)
    return pl.pallas_call(
        _out_proj_kernel,
        grid=(n // tm,),
        in_specs=[
            pl.BlockSpec((tm, d), lambda i: (i, 0)),
            pl.BlockSpec((tm, w1), lambda i: (i, 0)),
            pl.BlockSpec((tm, w2), lambda i: (i, 0)),
            pl.BlockSpec((tm, w2), lambda i: (i, 0)),
            pl.BlockSpec((w1, d), lambda i: (0, 0)),
            pl.BlockSpec((w2, d), lambda i: (w1 // w2, 0)),
            pl.BlockSpec((w2, d), lambda i: (w1 // w2 + 1, 0)),
            pl.BlockSpec((1, d), const),
            pl.BlockSpec((d, LANES), const),
            pl.BlockSpec((1, LANES), const),
        ],
        out_specs=[
            pl.BlockSpec((tm, d), lambda i: (i, 0)),
            pl.BlockSpec((tm, d), lambda i: (i, 0)),
            pl.BlockSpec((tm, LANES), lambda i: (i, 0)),
        ],
        out_shape=[
            jax.ShapeDtypeStruct((n, d), F32),
            jax.ShapeDtypeStruct((n, d), F32),
            jax.ShapeDtypeStruct((n, LANES), F32),
        ],
        compiler_params=_cparams(("parallel",)),
        name="out_proj",
    )(x2d, gla, fox, memo, w_out, w_out, w_out, g_ffn, wr, br)


def _row_copy(src_hbm, dst, sem, src_row, dst_row):
    return pltpu.make_async_copy(src_hbm.at[pl.ds(src_row, 1), :], dst.at[pl.ds(dst_row, 1), :], sem)


def _gather_kernel(tok_ref, x_hbm, o_ref, buf, sem, *, rows):
    def issue(r, carry):
        _row_copy(x_hbm, buf, sem.at[0], tok_ref[0, r], r).start()
        return carry

    lax.fori_loop(0, rows, issue, 0, unroll=8)
    pltpu.make_async_copy(x_hbm.at[pl.ds(0, rows), :], buf, sem.at[0]).wait()
    o_ref[...] = buf[...].astype(o_ref.dtype)


def _gather_rows(xn, row_tok, rows=MOE_TM):
    n_rows = row_tok.shape[0]
    d = xn.shape[1]
    nb = n_rows // rows
    return pl.pallas_call(
        functools.partial(_gather_kernel, rows=rows),
        grid=(nb,),
        in_specs=[
            pl.BlockSpec((None, 1, rows), lambda i: (i, 0, 0), memory_space=pltpu.SMEM),
            pl.BlockSpec(memory_space=pl.ANY),
        ],
        out_specs=pl.BlockSpec((rows, d), lambda i: (i, 0)),
        out_shape=jax.ShapeDtypeStruct((n_rows, d), BF16),
        scratch_shapes=[pltpu.VMEM((rows, d), F32), pltpu.SemaphoreType.DMA((1,))],
        compiler_params=_cparams(("arbitrary",)),
        name="gather_rows",
    )(row_tok.reshape(nb, 1, rows), xn)


def _moe_up_kernel(rb_ref, ex_ref, ct_ref, ok_ref, x_ref, wg_ref, wu_ref, bg_ref, bu_ref, act_ref):
    @pl.when(ok_ref[pl.program_id(0)] == 1)
    def _():
        x = x_ref[...]
        gate = jnp.minimum(_dot(x, wg_ref[...]) + bg_ref[...], SWIGLU_LIMIT)
        up = jnp.clip(_dot(x, wu_ref[...]) + bu_ref[...], -SWIGLU_LIMIT, SWIGLU_LIMIT)
        act_ref[...] = (gate * jax.nn.sigmoid(SWIGLU_ALPHA * gate) * (up + 1.0)).astype(act_ref.dtype)

    @pl.when(ok_ref[pl.program_id(0)] == 0)
    def _():
        act_ref[...] = jnp.zeros_like(act_ref)


def _moe_up(x_rows, w_up, b_up, items, tn):
    n_rows, d = x_rows.shape
    d_ff = w_up.shape[2] // 2
    rb, ex, ct, ok = items
    up_off = d_ff // tn
    grid_spec = pltpu.PrefetchScalarGridSpec(
        num_scalar_prefetch=4,
        grid=(rb.shape[0],),
        in_specs=[
            pl.BlockSpec((MOE_TM, d), lambda t, rb, ex, ct, ok: (rb[t], 0)),
            pl.BlockSpec((None, d, tn), lambda t, rb, ex, ct, ok: (ex[t], 0, ct[t])),
            pl.BlockSpec((None, d, tn), lambda t, rb, ex, ct, ok: (ex[t], 0, ct[t] + up_off)),
            pl.BlockSpec((None, 1, tn), lambda t, rb, ex, ct, ok: (ex[t], 0, ct[t])),
            pl.BlockSpec((None, 1, tn), lambda t, rb, ex, ct, ok: (ex[t], 0, ct[t] + up_off)),
        ],
        out_specs=pl.BlockSpec((MOE_TM, tn), lambda t, rb, ex, ct, ok: (rb[t], ct[t])),
    )
    return pl.pallas_call(
        _moe_up_kernel,
        grid_spec=grid_spec,
        out_shape=jax.ShapeDtypeStruct((n_rows, d_ff), BF16),
        compiler_params=_cparams(("arbitrary",)),
        name="moe_up",
    )(rb, ex, ct, ok, x_rows, w_up, w_up, b_up, b_up)


def _moe_down_kernel(rb_ref, ex_ref, ct_ref, ok_ref, a_ref, w_ref, b_ref, rg_ref, y_ref):
    @pl.when(ok_ref[pl.program_id(0)] == 1)
    def _():
        y_ref[...] = (_dot(a_ref[...], w_ref[...]) + b_ref[...]) * rg_ref[...]

    @pl.when(ok_ref[pl.program_id(0)] == 0)
    def _():
        y_ref[...] = jnp.zeros_like(y_ref)


def _moe_down(act, w_down, b_down, row_gate, items, tn):
    n_rows, d_ff = act.shape
    d = w_down.shape[2]
    rb, ex, ct, ok = items
    grid_spec = pltpu.PrefetchScalarGridSpec(
        num_scalar_prefetch=4,
        grid=(rb.shape[0],),
        in_specs=[
            pl.BlockSpec((MOE_TM, d_ff), lambda t, rb, ex, ct, ok: (rb[t], 0)),
            pl.BlockSpec((None, d_ff, tn), lambda t, rb, ex, ct, ok: (ex[t], 0, ct[t])),
            pl.BlockSpec((None, 1, tn), lambda t, rb, ex, ct, ok: (ex[t], 0, ct[t])),
            pl.BlockSpec((MOE_TM, 1), lambda t, rb, ex, ct, ok: (rb[t], 0)),
        ],
        out_specs=pl.BlockSpec((MOE_TM, tn), lambda t, rb, ex, ct, ok: (rb[t], ct[t])),
    )
    return pl.pallas_call(
        _moe_down_kernel,
        grid_spec=grid_spec,
        out_shape=jax.ShapeDtypeStruct((n_rows, d), F32),
        compiler_params=_cparams(("arbitrary",)),
        name="moe_down",
    )(rb, ex, ct, ok, act, w_down, b_down, row_gate)


def _work_items(blk_start, blk_count, n_blocks_max, n_col_tiles):
    n_items = n_blocks_max * n_col_tiles
    blk_end = jnp.cumsum(blk_count)
    item_end = blk_end * n_col_tiles
    n_used = blk_end[-1]
    total = item_end[-1]
    idx = jnp.arange(n_items, dtype=jnp.int32)
    t = jnp.minimum(idx, total - 1)
    ex = jnp.searchsorted(item_end, t, side='right').astype(jnp.int32)
    local = t - blk_start[ex] * n_col_tiles
    cnt = jnp.maximum(blk_count[ex], 1)
    tail = idx - total
    n_unused = jnp.maximum(n_blocks_max - n_used, 1)
    is_tail = idx >= total
    ct = jnp.where(is_tail, tail // n_unused, local // cnt)
    rb = jnp.where(is_tail, n_used + tail % n_unused, blk_start[ex] + local % cnt)
    ok = jnp.logical_not(is_tail).astype(jnp.int32)
    return rb.astype(jnp.int32), ex, ct.astype(jnp.int32), ok


def _combine_kernel(pos_ref, y_hbm, x2_ref, g_ref, o_ref, buf, sem, *, tb):
    def issue(r, carry):
        _row_copy(y_hbm, buf, sem.at[0], pos_ref[0, r], r).start()
        return carry

    lax.fori_loop(0, TOP_K * tb, issue, 0, unroll=8)
    pltpu.make_async_copy(y_hbm.at[pl.ds(0, TOP_K * tb), :], buf, sem.at[0]).wait()
    x = x2_ref[...]
    for k in range(TOP_K):
        x = x + buf[k * tb:(k + 1) * tb, :]
    o_ref[...] = _rms(x, g_ref[...])


def _combine(y_rows, pos, x2, g_final, tb=128):
    n, d = x2.shape
    nb = n // tb
    pos_blk = pos.reshape(nb, tb, TOP_K).transpose(0, 2, 1).reshape(nb, 1, TOP_K * tb)
    return pl.pallas_call(
        functools.partial(_combine_kernel, tb=tb),
        grid=(nb,),
        in_specs=[
            pl.BlockSpec((None, 1, TOP_K * tb), lambda i: (i, 0, 0), memory_space=pltpu.SMEM),
            pl.BlockSpec(memory_space=pl.ANY),
            pl.BlockSpec((tb, d), lambda i: (i, 0)),
            pl.BlockSpec((1, d), lambda i: (0, 0)),
        ],
        out_specs=pl.BlockSpec((tb, d), lambda i: (i, 0)),
        out_shape=jax.ShapeDtypeStruct((n, d), F32),
        scratch_shapes=[pltpu.VMEM((TOP_K * tb, d), F32), pltpu.SemaphoreType.DMA((1,))],
        compiler_params=_cparams(("arbitrary",)),
        name="combine",
    )(pos_blk, y_rows, x2, g_final)


def _routing(logits, n_tok):
    top_vals, top_idx = lax.top_k(logits, TOP_K)
    gates = jax.nn.softmax(top_vals, axis=-1)
    a = n_tok * TOP_K
    e_flat = top_idx.reshape(a).astype(jnp.int32)
    g_flat = gates.reshape(a)
    order = jnp.argsort(e_flat)
    e_sorted = e_flat[order]
    counts = jnp.zeros((N_EXPERTS,), jnp.int32).at[e_flat].add(1)
    padded = (counts + MOE_TM - 1) // MOE_TM * MOE_TM
    pad_end = jnp.cumsum(padded)
    pad_start = pad_end - padded
    start = jnp.cumsum(counts) - counts
    dest = pad_start[e_sorted] + (jnp.arange(a, dtype=jnp.int32) - start[e_sorted])
    n_blocks_max = -(-(a + N_EXPERTS * (MOE_TM - 1)) // MOE_TM)
    n_rows = n_blocks_max * MOE_TM
    row_tok = jnp.zeros((n_rows,), jnp.int32).at[dest].set((order // TOP_K).astype(jnp.int32))
    row_gate = jnp.zeros((n_rows,), F32).at[dest].set(g_flat[order])
    pos = jnp.zeros((a,), jnp.int32).at[order].set(dest).reshape(n_tok, TOP_K)
    return row_tok, row_gate, pos, pad_start // MOE_TM, padded // MOE_TM, n_blocks_max


def kernel(x, mem, g_attn_norm, g_mem_norm, w_in, w_gla_a2, b_gla_a, g_gla_out, b_fox_f, g_fox_out,
           w_mem_kv, g_mem_out, w_out, g_ffn_norm, w_router, b_router, w_moe_up, b_moe_up,
           w_moe_down, b_moe_down, g_final):
    batch, seq, d = x.shape
    n_mem = mem.shape[1]
    n = batch * seq
    depth = w_in.shape[0]
    assert depth == 1, "the combine kernel applies the final norm, so exactly one layer is supported"
    kw = GLA_HEADS * GLA_DK
    vw = GLA_HEADS * GLA_DV
    fw = FOX_HEADS * FOX_DH
    mw = MEM_HEADS * MEM_DH
    o_q, o_k, o_v, o_g = 0, kw, 2 * kw, 2 * kw + vw
    o_a = o_g + vw
    o_fq = o_a + GLA_LOWRANK
    o_fk, o_fv = o_fq + fw, o_fq + 2 * fw
    o_ff = o_fq + 3 * fw
    o_mq = o_ff + FOX_HEADS
    f_lane = GLA_LOWRANK

    xf = x.reshape(n, d)
    for l in range(depth):
        wi = w_in[l]
        wa = jnp.concatenate([wi[:, o_q:o_a], wi[:, o_fq:o_ff], wi[:, o_mq:o_mq + mw]], axis=1).astype(BF16)
        wb = jnp.zeros((d, LANES), F32).at[:, :GLA_LOWRANK].set(wi[:, o_a:o_fq])
        wb = wb.at[:, f_lane:f_lane + FOX_HEADS].set(wi[:, o_ff:o_mq]).astype(BF16)
        proj, small = _in_proj(xf, g_attn_norm[l].reshape(1, d), wa, wb)

        wa2p = jnp.zeros((LANES, kw), F32).at[:GLA_LOWRANK].set(w_gla_a2[l]).astype(BF16)
        bfv = jnp.zeros((1, LANES), F32).at[0, f_lane:f_lane + FOX_HEADS].set(b_fox_f[l])
        gla, f_cum = _gla(proj, small, wa2p, b_gla_a[l].reshape(1, kw), bfv,
                          g_gla_out[l].reshape(1, vw), batch, seq)

        f_t = f_cum.reshape(batch, seq, LANES)[:, :, f_lane:f_lane + FOX_HEADS].transpose(0, 2, 1)
        f_t = jnp.concatenate([f_t, jnp.zeros_like(f_t)], axis=1)
        fox = _fox(proj, f_t, g_fox_out[l].reshape(1, fw), batch, seq)

        kv = _mem_kv(mem.reshape(batch * n_mem, d), g_mem_norm[l].reshape(1, d), w_mem_kv[l].astype(BF16))
        memo = _mem_attn(proj, kv, g_mem_out[l].reshape(1, mw), batch, seq, n_mem)

        wr = jnp.zeros((d, LANES), F32).at[:, :N_EXPERTS].set(w_router[l])
        br = jnp.zeros((1, LANES), F32).at[0, :N_EXPERTS].set(b_router[l])
        x2, xn, logits = _out_proj(xf, gla, fox, memo, w_out[l].astype(BF16),
                                   g_ffn_norm[l].reshape(1, d), wr, br)

        row_tok, row_gate, pos, blk_start, blk_count, n_blocks_max = _routing(logits[:, :N_EXPERTS], n)
        x_rows = _gather_rows(xn, row_tok)
        up_tn, down_tn = 1024, 2048
        d_ff = w_moe_up.shape[3] // 2
        act = _moe_up(x_rows, w_moe_up[l].astype(BF16), b_moe_up[l].reshape(N_EXPERTS, 1, 2 * d_ff),
                      _work_items(blk_start, blk_count, n_blocks_max, d_ff // up_tn), up_tn)
        y_rows = _moe_down(act, w_moe_down[l].astype(BF16), b_moe_down[l].reshape(N_EXPERTS, 1, d),
                           row_gate.reshape(-1, 1),
                           _work_items(blk_start, blk_count, n_blocks_max, d // down_tn), down_tn)
        xf = _combine(y_rows, pos, x2, g_final.reshape(1, d))
    return xf.reshape(batch, seq, d)
```

```python
import functools

import jax
import jax.numpy as jnp
from jax import lax
from jax.experimental import pallas as pl
from jax.experimental.pallas import tpu as pltpu

EPS = 1e-5
CHUNK = 64
GLA_HEADS = 4
GLA_DK = 128
GLA_DV = 256
GLA_LOWRANK = 16
GLA_TAU = 16.0
FOX_HEADS = 4
FOX_DH = 128
MEM_HEADS = 4
MEM_DH = 128
N_EXPERTS = 32
TOP_K = 4
SWIGLU_LIMIT = 7.0
SWIGLU_ALPHA = 1.702
LANES = 128
MOE_TM = 256
VMEM_LIMIT = 56 * 1024 * 1024

F32 = jnp.float32
BF16 = jnp.bfloat16


def _cparams(sem, vmem=VMEM_LIMIT):
    return pltpu.CompilerParams(dimension_semantics=sem, vmem_limit_bytes=vmem)


def _log_sigmoid(x):
    return jnp.minimum(x, 0.0) - jnp.log1p(jnp.exp(-jnp.abs(x)))


def _rms(x, g):
    return x * lax.rsqrt(jnp.mean(x * x, axis=-1, keepdims=True) + EPS) * g


def _dot(a, b, **kw):
    return jnp.dot(a, b, preferred_element_type=F32, **kw)


def _dot_nt(a, b):
    return lax.dot_general(a, b, (((1,), (1,)), ((), ())), preferred_element_type=F32)


def _dot_tn(a, b):
    return lax.dot_general(a, b, (((0,), (0,)), ((), ())), preferred_element_type=F32)


def _rowmajor_chunk(ref, c, rows, pitch, first=0):
    return ref[pl.ds(first * pitch + c, rows, stride=pitch), :]


def _store_rowmajor(ref, x):
    rows, d = x.shape
    pitch = d // LANES
    for c in range(pitch):
        ref[pl.ds(c, rows, stride=pitch), :] = x[:, c * LANES:(c + 1) * LANES]


def _in_proj_kernel(x_ref, g_ref, wa_ref, wb_ref, proj_ref, small_ref, h_scr):
    @pl.when(pl.program_id(1) == 0)
    def _():
        hb = _rms(x_ref[...], g_ref[...]).astype(BF16)
        h_scr[...] = hb
        small_ref[...] = _dot(hb, wb_ref[...])

    proj_ref[...] = _dot(h_scr[...], wa_ref[...]).astype(proj_ref.dtype)


def _in_proj(x2d, g, wa, wb, tm=512, tn=1024):
    n, d = x2d.shape
    na = wa.shape[1]
    return pl.pallas_call(
        _in_proj_kernel,
        grid=(n // tm, na // tn),
        in_specs=[
            pl.BlockSpec((tm, d), lambda i, j: (i, 0)),
            pl.BlockSpec((1, d), lambda i, j: (0, 0)),
            pl.BlockSpec((d, tn), lambda i, j: (0, j)),
            pl.BlockSpec((d, LANES), lambda i, j: (0, 0)),
        ],
        out_specs=[
            pl.BlockSpec((tm, tn), lambda i, j: (i, j)),
            pl.BlockSpec((tm, LANES), lambda i, j: (i, 0)),
        ],
        out_shape=[
            jax.ShapeDtypeStruct((n, na), BF16),
            jax.ShapeDtypeStruct((n, LANES), F32),
        ],
        scratch_shapes=[pltpu.VMEM((tm, d), BF16)],
        compiler_params=_cparams(("parallel", "arbitrary")),
        name="in_proj",
    )(x2d, g, wa, wb)


def _gla_kernel(q_ref, k_ref, v_ref, gate_ref, small_ref, wa2_ref, ba_ref, bf_ref, gout_ref,
                o_ref, f_ref, state_scr, fcar_scr, la_scr, lf_scr, *, n_chunks):
    @pl.when(pl.program_id(1) == 0)
    def _():
        state_scr[...] = jnp.zeros_like(state_scr)
        fcar_scr[...] = jnp.zeros_like(fcar_scr)

    small = small_ref[...]
    la_scr[...] = _log_sigmoid(_dot(small.astype(BF16), wa2_ref[...]) + ba_ref[...]) * (1.0 / GLA_TAU)
    lf_scr[...] = _log_sigmoid(small + bf_ref[...])
    row = lax.broadcasted_iota(jnp.int32, (CHUNK, CHUNK), 0)
    col = lax.broadcasted_iota(jnp.int32, (CHUNK, CHUNK), 1)
    tri = (col <= row).astype(F32)
    scale = GLA_DK ** -0.5

    def chunk_body(c, carry):
        r = pl.ds(pl.multiple_of(c * CHUNK, CHUNK), CHUNK)
        b = _dot(tri, la_scr[r, :], precision=lax.Precision.HIGHEST)
        b_end = b[CHUNK - 1:CHUNK, :]
        k_dec = k_ref[r, :].astype(F32) * jnp.exp(b_end - b)
        decay = jnp.exp(b_end)
        f_cum = _dot(tri, lf_scr[r, :], precision=lax.Precision.HIGHEST) + fcar_scr[...]
        f_ref[r, :] = f_cum
        fcar_scr[...] = f_cum[CHUNK - 1:CHUNK, :]
        for h in range(GLA_HEADS):
            ks = slice(h * GLA_DK, (h + 1) * GLA_DK)
            vs = slice(h * GLA_DV, (h + 1) * GLA_DV)
            st = state_scr[h] * decay[:, ks] + _dot_tn(v_ref[r, vs], k_dec[:, ks].astype(BF16))
            state_scr[h] = st
            o = _dot_nt(q_ref[r, ks], st.astype(BF16)) * scale
            gt = gate_ref[r, vs].astype(F32)
            o_ref[r, vs] = (_rms(o, gout_ref[:, vs]) * (gt * jax.nn.sigmoid(gt))).astype(o_ref.dtype)
        return carry

    lax.fori_loop(0, n_chunks, chunk_body, 0)


def _gla(proj, small, wa2p, ba, bfv, gout, batch, seq, ts=512):
    n = proj.shape[0]
    nsb = seq // ts
    kw = GLA_HEADS * GLA_DK
    vw = GLA_HEADS * GLA_DV
    row = lambda b, s: b * nsb + s
    return pl.pallas_call(
        functools.partial(_gla_kernel, n_chunks=ts // CHUNK),
        grid=(batch, nsb),
        in_specs=[
            pl.BlockSpec((ts, kw), lambda b, s: (row(b, s), 0)),
            pl.BlockSpec((ts, kw), lambda b, s: (row(b, s), 1)),
            pl.BlockSpec((ts, vw), lambda b, s: (row(b, s), 1)),
            pl.BlockSpec((ts, vw), lambda b, s: (row(b, s), 2)),
            pl.BlockSpec((ts, LANES), lambda b, s: (row(b, s), 0)),
            pl.BlockSpec((LANES, kw), lambda b, s: (0, 0)),
            pl.BlockSpec((1, kw), lambda b, s: (0, 0)),
            pl.BlockSpec((1, LANES), lambda b, s: (0, 0)),
            pl.BlockSpec((1, vw), lambda b, s: (0, 0)),
        ],
        out_specs=[
            pl.BlockSpec((ts, vw), lambda b, s: (row(b, s), 0)),
            pl.BlockSpec((ts, LANES), lambda b, s: (row(b, s), 0)),
        ],
        out_shape=[
            jax.ShapeDtypeStruct((n, vw), BF16),
            jax.ShapeDtypeStruct((n, LANES), F32),
        ],
        scratch_shapes=[
            pltpu.VMEM((GLA_HEADS, GLA_DV, GLA_DK), F32),
            pltpu.VMEM((1, LANES), F32),
            pltpu.VMEM((ts, kw), F32),
            pltpu.VMEM((ts, LANES), F32),
        ],
        compiler_params=_cparams(("parallel", "arbitrary")),
        name="gla",
    )(proj, proj, proj, proj, small, wa2p, ba, bfv, gout)


def _fox_kernel(q_ref, k_ref, v_ref, fk_ref, g_ref, o_ref, m_scr, l_scr, acc_scr, *, tq, tk):
    i = pl.program_id(1)
    j = pl.program_id(2)

    @pl.when(j == 0)
    def _():
        m_scr[...] = jnp.full_like(m_scr, -jnp.inf)
        l_scr[...] = jnp.zeros_like(l_scr)
        acc_scr[...] = jnp.zeros_like(acc_scr)

    @pl.when(j <= i)
    def _():
        row = lax.broadcasted_iota(jnp.int32, (tq, tk), 0) + i * tq
        col = lax.broadcasted_iota(jnp.int32, (tq, tk), 1) + j * tk
        visible = col <= row
        scale = FOX_DH ** -0.5
        for h in range(FOX_HEADS):
            hs = slice(h * FOX_DH, (h + 1) * FOX_DH)
            s = _dot_nt(q_ref[:, hs], k_ref[:, hs]) * scale - fk_ref[h:h + 1, :]
            s = jnp.where(visible, s, -jnp.inf)
            m_prev = m_scr[h]
            m_new = jnp.maximum(m_prev, jnp.max(s, axis=-1, keepdims=True))
            alpha = jnp.exp(m_prev - m_new)
            p = jnp.exp(s - m_new)
            l_scr[h] = alpha * l_scr[h] + jnp.sum(p, axis=-1, keepdims=True)
            acc_scr[h] = alpha * acc_scr[h] + _dot(p.astype(BF16), v_ref[:, hs])
            m_scr[h] = m_new

    @pl.when(j == i)
    def _():
        for h in range(FOX_HEADS):
            hs = slice(h * FOX_DH, (h + 1) * FOX_DH)
            o = acc_scr[h] / l_scr[h]
            o_ref[:, hs] = _rms(o, g_ref[:, hs]).astype(o_ref.dtype)


def _fox(proj, f_t, g_fox, batch, seq, tq=512, tk=512):
    n = proj.shape[0]
    w = FOX_HEADS * FOX_DH
    nq, nk = seq // tq, seq // tk
    qcol, kcol, vcol = 3072 // w, 3584 // w, 4096 // w
    return pl.pallas_call(
        functools.partial(_fox_kernel, tq=tq, tk=tk),
        grid=(batch, nq, nk),
        in_specs=[
            pl.BlockSpec((tq, w), lambda b, i, j: (b * nq + i, qcol)),
            pl.BlockSpec((tk, w), lambda b, i, j: (b * nk + jnp.minimum(j, i), kcol)),
            pl.BlockSpec((tk, w), lambda b, i, j: (b * nk + jnp.minimum(j, i), vcol)),
            pl.BlockSpec((None, 8, tk), lambda b, i, j: (b, 0, jnp.minimum(j, i))),
            pl.BlockSpec((1, w), lambda b, i, j: (0, 0)),
        ],
        out_specs=pl.BlockSpec((tq, w), lambda b, i, j: (b * nq + i, 0)),
        out_shape=jax.ShapeDtypeStruct((n, w), BF16),
        scratch_shapes=[
            pltpu.VMEM((FOX_HEADS, tq, 1), F32),
            pltpu.VMEM((FOX_HEADS, tq, 1), F32),
            pltpu.VMEM((FOX_HEADS, tq, FOX_DH), F32),
        ],
        compiler_params=_cparams(("parallel", "parallel", "arbitrary")),
        name="fox",
    )(proj, proj, proj, f_t, g_fox)


def _mem_kv_kernel(m_ref, g_ref, w_ref, kv_ref):
    kv_ref[...] = _dot(_rms(m_ref[...], g_ref[...]).astype(BF16), w_ref[...]).astype(kv_ref.dtype)


def _mem_kv(mem2d, g, w, tm=256):
    n, d = mem2d.shape
    nw = w.shape[1]
    return pl.pallas_call(
        _mem_kv_kernel,
        grid=(n // tm,),
        in_specs=[
            pl.BlockSpec((tm, d), lambda i: (i, 0)),
            pl.BlockSpec((1, d), lambda i: (0, 0)),
            pl.BlockSpec((d, nw), lambda i: (0, 0)),
        ],
        out_specs=pl.BlockSpec((tm, nw), lambda i: (i, 0)),
        out_shape=jax.ShapeDtypeStruct((n, nw), BF16),
        compiler_params=_cparams(("parallel",)),
        name="mem_kv",
    )(mem2d, g, w)


def _mem_attn_kernel(q_ref, k_ref, v_ref, g_ref, o_ref):
    scale = MEM_DH ** -0.5
    for h in range(MEM_HEADS):
        hs = slice(h * MEM_DH, (h + 1) * MEM_DH)
        s = _dot_nt(q_ref[:, hs], k_ref[:, hs]) * scale
        p = jnp.exp(s - jnp.max(s, axis=-1, keepdims=True))
        l = jnp.sum(p, axis=-1, keepdims=True)
        o = _dot((p / l).astype(BF16), v_ref[:, hs])
        o_ref[:, hs] = _rms(o, g_ref[:, hs]).astype(o_ref.dtype)


def _mem_attn(proj, kv, g_mem_out, batch, seq, n_mem, tq=1024):
    n = proj.shape[0]
    w = MEM_HEADS * MEM_DH
    nq = seq // tq
    qcol = 4608 // w
    return pl.pallas_call(
        _mem_attn_kernel,
        grid=(n // tq,),
        in_specs=[
            pl.BlockSpec((tq, w), lambda i: (i, qcol)),
            pl.BlockSpec((n_mem, w), lambda i: (i // nq, 0)),
            pl.BlockSpec((n_mem, w), lambda i: (i // nq, 1)),
            pl.BlockSpec((1, w), lambda i: (0, 0)),
        ],
        out_specs=pl.BlockSpec((tq, w), lambda i: (i, 0)),
        out_shape=jax.ShapeDtypeStruct((n, w), BF16),
        compiler_params=_cparams(("parallel",)),
        name="mem_attn",
    )(proj, kv, kv, g_mem_out)


def _out_proj_kernel(x_ref, gla_ref, fox_ref, mem_ref, w1_ref, w2_ref, w3_ref, g_ref, wr_ref, br_ref,
                     x2_ref, xn_ref, logit_ref):
    x2 = (x_ref[...] + _dot(gla_ref[...], w1_ref[...]) + _dot(fox_ref[...], w2_ref[...])
          + _dot(mem_ref[...], w3_ref[...]))
    x2_ref[...] = x2
    xn = _rms(x2, g_ref[...])
    _store_rowmajor(xn_ref, xn)
    logit_ref[...] = _dot(xn, wr_ref[...], precision=lax.Precision.HIGHEST) + br_ref[...]


def _out_proj(x2d, gla, fox, memo, w_out, g_ffn, wr, br, tm=256):
    n, d = x2d.shape
    w1, w2 = gla.shape[1], fox.shape[1]
    const = lambda i: (0, 0)
    return pl.pallas_call(
        _out_proj_kernel,
        grid=(n // tm,),
        in_specs=[
            pl.BlockSpec((tm, d), lambda i: (i, 0)),
            pl.BlockSpec((tm, w1), lambda i: (i, 0)),
            pl.BlockSpec((tm, w2), lambda i: (i, 0)),
            pl.BlockSpec((tm, w2), lambda i: (i, 0)),
            pl.BlockSpec((w1, d), lambda i: (0, 0)),
            pl.BlockSpec((w2, d), lambda i: (w1 // w2, 0)),
            pl.BlockSpec((w2, d), lambda i: (w1 // w2 + 1, 0)),
            pl.BlockSpec((1, d), const),
            pl.BlockSpec((d, LANES), const),
            pl.BlockSpec((1, LANES), const),
        ],
        out_specs=[
            pl.BlockSpec((tm, d), lambda i: (i, 0)),
            pl.BlockSpec((tm * (d // LANES), LANES), lambda i: (i, 0)),
            pl.BlockSpec((tm, LANES), lambda i: (i, 0)),
        ],
        out_shape=[
            jax.ShapeDtypeStruct((n, d), F32),
            jax.ShapeDtypeStruct((n * (d // LANES), LANES), F32),
            jax.ShapeDtypeStruct((n, LANES), F32),
        ],
        compiler_params=_cparams(("parallel",)),
        name="out_proj",
    )(x2d, gla, fox, memo, w_out, w_out, w_out, g_ffn, wr, br)


def _route_kernel(l_ref, idx_ref, gate_ref, rank_ref, cnt_ref, carry_scr, *, tb):
    @pl.when(pl.program_id(0) == 0)
    def _():
        carry_scr[...] = jnp.zeros_like(carry_scr)

    lane = lax.broadcasted_iota(jnp.int32, (tb, LANES), 1)
    logit = jnp.where(lane < N_EXPERTS, l_ref[...], -jnp.inf)
    vals, hots = [], []
    idx_out = jnp.zeros((tb, LANES), jnp.int32)
    for k in range(TOP_K):
        m = jnp.max(logit, axis=-1, keepdims=True)
        ik = jnp.min(jnp.where(logit == m, lane, LANES), axis=-1, keepdims=True)
        hot = lane == ik
        logit = jnp.where(hot, -jnp.inf, logit)
        vals.append(m)
        hots.append(hot)
        idx_out = jnp.where(lane == k, ik, idx_out)
    idx_ref[...] = idx_out

    e = [jnp.exp(v - vals[0]) for v in vals]
    den = e[0] + e[1] + e[2] + e[3]
    gate_out = jnp.zeros((tb, LANES), F32)
    for k in range(TOP_K):
        gate_out = jnp.where(lane == k, e[k] / den, gate_out)
    gate_ref[...] = gate_out

    member = jnp.zeros((tb, LANES), F32)
    for hot in hots:
        member = member + hot.astype(F32)
    row = lax.broadcasted_iota(jnp.int32, (tb, tb), 0)
    col = lax.broadcasted_iota(jnp.int32, (tb, tb), 1)
    before = (col < row).astype(BF16)
    rank = _dot(before, member.astype(BF16)) + carry_scr[...]
    rank_out = jnp.zeros((tb, LANES), F32)
    for k in range(TOP_K):
        rk = jnp.sum(jnp.where(hots[k], rank, 0.0), axis=-1, keepdims=True)
        rank_out = jnp.where(lane == k, rk, rank_out)
    rank_ref[...] = rank_out.astype(jnp.int32)
    total = carry_scr[...] + jnp.sum(member, axis=0, keepdims=True)
    carry_scr[...] = total
    cnt_ref[...] = total.astype(jnp.int32)


def _route(logits, tb=512):
    n = logits.shape[0]
    blk = pl.BlockSpec((tb, LANES), lambda i: (i, 0))
    return pl.pallas_call(
        functools.partial(_route_kernel, tb=tb),
        grid=(n // tb,),
        in_specs=[blk],
        out_specs=[blk, blk, blk, pl.BlockSpec((1, LANES), lambda i: (0, 0))],
        out_shape=[
            jax.ShapeDtypeStruct((n, LANES), jnp.int32),
            jax.ShapeDtypeStruct((n, LANES), F32),
            jax.ShapeDtypeStruct((n, LANES), jnp.int32),
            jax.ShapeDtypeStruct((1, LANES), jnp.int32),
        ],
        scratch_shapes=[pltpu.VMEM((1, LANES), F32)],
        compiler_params=_cparams(("arbitrary",)),
        name="route",
    )(logits)


def _dispatch_kernel(zblk_ref, zok_ref, used_ref, pos_ref, xn_ref, xr_hbm, zero_scr, sem,
                     *, tb, pitch, n_blocks_max):
    blk_rows = MOE_TM * pitch

    def zero_copy(b):
        dst = xr_hbm.at[pl.ds(pl.multiple_of(b * blk_rows, blk_rows), blk_rows), :]
        return pltpu.make_async_copy(zero_scr, dst, sem.at[1])

    @pl.when(pl.program_id(0) == 0)
    def _():
        zero_scr[...] = jnp.zeros_like(zero_scr)

        def per_expert(action):
            def body(e, carry):
                @pl.when(zok_ref[e] == 1)
                def _():
                    action(zero_copy(zblk_ref[e]))
                return carry
            lax.fori_loop(0, N_EXPERTS, body, 0)

        def per_tail(action):
            def body(b, carry):
                action(zero_copy(b))
                return carry
            lax.fori_loop(used_ref[0], n_blocks_max, body, 0)

        per_expert(lambda c: c.start())
        per_tail(lambda c: c.start())
        per_expert(lambda c: c.wait())
        per_tail(lambda c: c.wait())

    def issue(t, carry):
        src = xn_ref.at[pl.ds(pl.multiple_of(t * pitch, pitch), pitch), :]
        for k in range(TOP_K):
            p = pos_ref[0, t * TOP_K + k]
            dst = xr_hbm.at[pl.ds(pl.multiple_of(p * pitch, pitch), pitch), :]
            pltpu.make_async_copy(src, dst, sem.at[0]).start()
        return carry

    lax.fori_loop(0, tb, issue, 0, unroll=4)
    for k in range(TOP_K):
        pltpu.make_async_copy(xn_ref, xr_hbm.at[pl.ds(0, tb * pitch), :], sem.at[0]).wait()


def _dispatch(xn_rm, pos, zblk, zok, n_used, n_blocks_max, d, tb=256):
    pitch = d // LANES
    n = xn_rm.shape[0] // pitch
    nb = n // tb
    grid_spec = pltpu.PrefetchScalarGridSpec(
        num_scalar_prefetch=3,
        grid=(nb,),
        in_specs=[
            pl.BlockSpec((None, 1, TOP_K * tb), lambda i, *_: (i, 0, 0), memory_space=pltpu.SMEM),
            pl.BlockSpec((tb * pitch, LANES), lambda i, *_: (i, 0)),
        ],
        out_specs=pl.BlockSpec(memory_space=pl.ANY),
        scratch_shapes=[pltpu.VMEM((MOE_TM * pitch, LANES), F32), pltpu.SemaphoreType.DMA((2,))],
    )
    return pl.pallas_call(
        functools.partial(_dispatch_kernel, tb=tb, pitch=pitch, n_blocks_max=n_blocks_max),
        grid_spec=grid_spec,
        out_shape=jax.ShapeDtypeStruct((n_blocks_max * MOE_TM * pitch, LANES), F32),
        compiler_params=_cparams(("arbitrary",)),
        name="dispatch",
    )(zblk, zok, n_used, pos.reshape(nb, 1, TOP_K * tb), xn_rm)


def _moe_up_kernel(rb_ref, ex_ref, ct_ref, ok_ref, x_ref, wg_ref, wu_ref, bg_ref, bu_ref, act_ref, lhs_scr):
    @pl.when(ok_ref[pl.program_id(0)] == 1)
    def _():
        rows, d = lhs_scr.shape
        pitch = d // LANES
        for c in range(pitch):
            lhs_scr[:, c * LANES:(c + 1) * LANES] = _rowmajor_chunk(x_ref, c, rows, pitch).astype(BF16)
        x = lhs_scr[...]
        gate = jnp.minimum(_dot(x, wg_ref[...]) + bg_ref[...], SWIGLU_LIMIT)
        up = jnp.clip(_dot(x, wu_ref[...]) + bu_ref[...], -SWIGLU_LIMIT, SWIGLU_LIMIT)
        act_ref[...] = (gate * jax.nn.sigmoid(SWIGLU_ALPHA * gate) * (up + 1.0)).astype(act_ref.dtype)

    @pl.when(ok_ref[pl.program_id(0)] == 0)
    def _():
        act_ref[...] = jnp.zeros_like(act_ref)


def _moe_up(x_rows_rm, w_up, b_up, items, tn):
    d = w_up.shape[1]
    pitch = d // LANES
    n_rows = x_rows_rm.shape[0] // pitch
    d_ff = w_up.shape[2] // 2
    rb, ex, ct, ok = items
    up_off = d_ff // tn
    grid_spec = pltpu.PrefetchScalarGridSpec(
        num_scalar_prefetch=4,
        grid=(rb.shape[0],),
        in_specs=[
            pl.BlockSpec((MOE_TM * pitch, LANES), lambda t, rb, ex, ct, ok: (rb[t], 0)),
            pl.BlockSpec((None, d, tn), lambda t, rb, ex, ct, ok: (ex[t], 0, ct[t])),
            pl.BlockSpec((None, d, tn), lambda t, rb, ex, ct, ok: (ex[t], 0, ct[t] + up_off)),
            pl.BlockSpec((None, 1, tn), lambda t, rb, ex, ct, ok: (ex[t], 0, ct[t])),
            pl.BlockSpec((None, 1, tn), lambda t, rb, ex, ct, ok: (ex[t], 0, ct[t] + up_off)),
        ],
        out_specs=pl.BlockSpec((MOE_TM, tn), lambda t, rb, ex, ct, ok: (rb[t], ct[t])),
        scratch_shapes=[pltpu.VMEM((MOE_TM, d), BF16)],
    )
    return pl.pallas_call(
        _moe_up_kernel,
        grid_spec=grid_spec,
        out_shape=jax.ShapeDtypeStruct((n_rows, d_ff), BF16),
        compiler_params=_cparams(("arbitrary",)),
        name="moe_up",
    )(rb, ex, ct, ok, x_rows_rm, w_up, w_up, b_up, b_up)


def _moe_down_kernel(rb_ref, ex_ref, ct_ref, ok_ref, a_ref, w_ref, b_ref, y_ref):
    @pl.when(ok_ref[pl.program_id(0)] == 1)
    def _():
        _store_rowmajor(y_ref, _dot(a_ref[...], w_ref[...]) + b_ref[...])

    @pl.when(ok_ref[pl.program_id(0)] == 0)
    def _():
        y_ref[...] = jnp.zeros_like(y_ref)


def _moe_down(act, w_down, b_down, items):
    n_rows, d_ff = act.shape
    d = w_down.shape[2]
    pitch = d // LANES
    rb, ex, ct, ok = items
    grid_spec = pltpu.PrefetchScalarGridSpec(
        num_scalar_prefetch=4,
        grid=(rb.shape[0],),
        in_specs=[
            pl.BlockSpec((MOE_TM, d_ff), lambda t, rb, ex, ct, ok: (rb[t], 0)),
            pl.BlockSpec((None, d_ff, d), lambda t, rb, ex, ct, ok: (ex[t], 0, 0)),
            pl.BlockSpec((None, 1, d), lambda t, rb, ex, ct, ok: (ex[t], 0, 0)),
        ],
        out_specs=pl.BlockSpec((MOE_TM * pitch, LANES), lambda t, rb, ex, ct, ok: (rb[t], 0)),
    )
    return pl.pallas_call(
        _moe_down_kernel,
        grid_spec=grid_spec,
        out_shape=jax.ShapeDtypeStruct((n_rows * pitch, LANES), F32),
        compiler_params=_cparams(("arbitrary",)),
        name="moe_down",
    )(rb, ex, ct, ok, act, w_down, b_down)


def _work_items(blk_start, blk_count, n_blocks_max, n_col_tiles):
    n_items = n_blocks_max * n_col_tiles
    blk_end = jnp.cumsum(blk_count)
    item_end = blk_end * n_col_tiles
    n_used = blk_end[-1]
    total = item_end[-1]
    idx = jnp.arange(n_items, dtype=jnp.int32)
    t = jnp.minimum(idx, total - 1)
    ex = jnp.searchsorted(item_end, t, side='right').astype(jnp.int32)
    local = t - blk_start[ex] * n_col_tiles
    cnt = jnp.maximum(blk_count[ex], 1)
    tail = idx - total
    n_unused = jnp.maximum(n_blocks_max - n_used, 1)
    is_tail = idx >= total
    ct = jnp.where(is_tail, tail // n_unused, local // cnt)
    rb = jnp.where(is_tail, n_used + tail % n_unused, blk_start[ex] + local % cnt)
    ok = jnp.logical_not(is_tail).astype(jnp.int32)
    return rb.astype(jnp.int32), ex, ct.astype(jnp.int32), ok


def _combine_kernel(pos_ref, y_hbm, x2_ref, gate_ref, g_ref, o_ref, buf, sem, *, tb, pitch):
    def issue(r, carry):
        src = y_hbm.at[pl.ds(pl.multiple_of(pos_ref[0, r] * pitch, pitch), pitch), :]
        dst = buf.at[pl.ds(pl.multiple_of(r * pitch, pitch), pitch), :]
        pltpu.make_async_copy(src, dst, sem.at[0]).start()
        return carry

    lax.fori_loop(0, TOP_K * tb, issue, 0, unroll=8)
    pltpu.make_async_copy(y_hbm.at[pl.ds(0, TOP_K * tb * pitch), :], buf, sem.at[0]).wait()
    gates = gate_ref[...]
    ssq = jnp.zeros((tb, 1), F32)
    for c in range(pitch):
        cs = slice(c * LANES, (c + 1) * LANES)
        z = x2_ref[:, cs]
        for k in range(TOP_K):
            z = z + gates[:, k:k + 1] * _rowmajor_chunk(buf, c, tb, pitch, first=k * tb)
        o_ref[:, cs] = z
        ssq = ssq + jnp.sum(z * z, axis=-1, keepdims=True)
    o_ref[...] = o_ref[...] * lax.rsqrt(ssq * (1.0 / (pitch * LANES)) + EPS) * g_ref[...]


def _combine(y_rows_rm, pos, gates, x2, g_final, tb=128):
    n, d = x2.shape
    nb = n // tb
    pitch = d // LANES
    pos_blk = pos.reshape(nb, tb, TOP_K).transpose(0, 2, 1).reshape(nb, 1, TOP_K * tb)
    return pl.pallas_call(
        functools.partial(_combine_kernel, tb=tb, pitch=pitch),
        grid=(nb,),
        in_specs=[
            pl.BlockSpec((None, 1, TOP_K * tb), lambda i: (i, 0, 0), memory_space=pltpu.SMEM),
            pl.BlockSpec(memory_space=pl.ANY),
            pl.BlockSpec((tb, d), lambda i: (i, 0)),
            pl.BlockSpec((tb, LANES), lambda i: (i, 0)),
            pl.BlockSpec((1, d), lambda i: (0, 0)),
        ],
        out_specs=pl.BlockSpec((tb, d), lambda i: (i, 0)),
        out_shape=jax.ShapeDtypeStruct((n, d), F32),
        scratch_shapes=[pltpu.VMEM((TOP_K * tb * pitch, LANES), F32), pltpu.SemaphoreType.DMA((1,))],
        compiler_params=_cparams(("arbitrary",)),
        name="combine",
    )(pos_blk, y_rows_rm, x2, gates, g_final)


def _routing_tables(idx, rank, cnt):
    counts = cnt[0, :N_EXPERTS]
    blk_count = (counts + MOE_TM - 1) // MOE_TM
    blk_end = jnp.cumsum(blk_count)
    blk_start = blk_end - blk_count
    hot = idx[:, :TOP_K, None] == jnp.arange(N_EXPERTS, dtype=jnp.int32)
    pos = jnp.sum(jnp.where(hot, blk_start * MOE_TM, 0), axis=-1) + rank[:, :TOP_K]
    return pos.astype(jnp.int32), blk_start.astype(jnp.int32), blk_count.astype(jnp.int32)


def kernel(x, mem, g_attn_norm, g_mem_norm, w_in, w_gla_a2, b_gla_a, g_gla_out, b_fox_f, g_fox_out,
           w_mem_kv, g_mem_out, w_out, g_ffn_norm, w_router, b_router, w_moe_up, b_moe_up,
           w_moe_down, b_moe_down, g_final):
    batch, seq, d = x.shape
    n_mem = mem.shape[1]
    n = batch * seq
    depth = w_in.shape[0]
    assert depth == 1, "the combine kernel applies the final norm, so exactly one layer is supported"
    kw = GLA_HEADS * GLA_DK
    vw = GLA_HEADS * GLA_DV
    fw = FOX_HEADS * FOX_DH
    mw = MEM_HEADS * MEM_DH
    o_q, o_k, o_v, o_g = 0, kw, 2 * kw, 2 * kw + vw
    o_a = o_g + vw
    o_fq = o_a + GLA_LOWRANK
    o_fk, o_fv = o_fq + fw, o_fq + 2 * fw
    o_ff = o_fq + 3 * fw
    o_mq = o_ff + FOX_HEADS
    f_lane = GLA_LOWRANK

    xf = x.reshape(n, d)
    for l in range(depth):
        wi = w_in[l]
        wa = jnp.concatenate([wi[:, o_q:o_a], wi[:, o_fq:o_ff], wi[:, o_mq:o_mq + mw]], axis=1).astype(BF16)
        wb = jnp.zeros((d, LANES), F32).at[:, :GLA_LOWRANK].set(wi[:, o_a:o_fq])
        wb = wb.at[:, f_lane:f_lane + FOX_HEADS].set(wi[:, o_ff:o_mq]).astype(BF16)
        proj, small = _in_proj(xf, g_attn_norm[l].reshape(1, d), wa, wb)

        wa2p = jnp.zeros((LANES, kw), F32).at[:GLA_LOWRANK].set(w_gla_a2[l]).astype(BF16)
        bfv = jnp.zeros((1, LANES), F32).at[0, f_lane:f_lane + FOX_HEADS].set(b_fox_f[l])
        gla, f_cum = _gla(proj, small, wa2p, b_gla_a[l].reshape(1, kw), bfv,
                          g_gla_out[l].reshape(1, vw), batch, seq)

        f_t = f_cum.reshape(batch, seq, LANES)[:, :, f_lane:f_lane + FOX_HEADS].transpose(0, 2, 1)
        f_t = jnp.concatenate([f_t, jnp.zeros_like(f_t)], axis=1)
        fox = _fox(proj, f_t, g_fox_out[l].reshape(1, fw), batch, seq)

        kv = _mem_kv(mem.reshape(batch * n_mem, d), g_mem_norm[l].reshape(1, d), w_mem_kv[l].astype(BF16))
        memo = _mem_attn(proj, kv, g_mem_out[l].reshape(1, mw), batch, seq, n_mem)

        wr = jnp.zeros((d, LANES), F32).at[:, :N_EXPERTS].set(w_router[l])
        br = jnp.zeros((1, LANES), F32).at[0, :N_EXPERTS].set(b_router[l])
        x2, xn, logits = _out_proj(xf, gla, fox, memo, w_out[l].astype(BF16),
                                   g_ffn_norm[l].reshape(1, d), wr, br)

        idx, gates, rank, cnt = _route(logits)
        pos, blk_start, blk_count = _routing_tables(idx, rank, cnt)
        n_blocks_max = -(-(n * TOP_K + N_EXPERTS * (MOE_TM - 1)) // MOE_TM)
        blk_end = blk_start + blk_count
        x_rows = _dispatch(xn, pos, blk_end - 1, (blk_count > 0).astype(jnp.int32), blk_end[-1:],
                           n_blocks_max, d)
        up_tn = 1024
        d_ff = w_moe_up.shape[3] // 2
        act = _moe_up(x_rows, w_moe_up[l].astype(BF16), b_moe_up[l].reshape(N_EXPERTS, 1, 2 * d_ff),
                      _work_items(blk_start, blk_count, n_blocks_max, d_ff // up_tn), up_tn)
        y_rows = _moe_down(act, w_moe_down[l].astype(BF16), b_moe_down[l].reshape(N_EXPERTS, 1, d),
                           _work_items(blk_start, blk_count, n_blocks_max, 1))
        xf = _combine(y_rows, pos, gates, x2, g_final.reshape(1, d))
    return xf.reshape(batch, seq, d)
```

```python
import functools

import jax
import jax.numpy as jnp
from jax import lax
from jax.experimental import pallas as pl
from jax.experimental.pallas import tpu as pltpu

EPS = 1e-5
CHUNK = 64
GLA_HEADS = 4
GLA_DK = 128
GLA_DV = 256
GLA_LOWRANK = 16
GLA_TAU = 16.0
FOX_HEADS = 4
FOX_DH = 128
MEM_HEADS = 4
MEM_DH = 128
N_EXPERTS = 32
TOP_K = 4
SWIGLU_LIMIT = 7.0
SWIGLU_ALPHA = 1.702
LANES = 128
MOE_TM = 256
VMEM_LIMIT = 56 * 1024 * 1024

F32 = jnp.float32
BF16 = jnp.bfloat16


def _cparams(sem, vmem=VMEM_LIMIT):
    return pltpu.CompilerParams(dimension_semantics=sem, vmem_limit_bytes=vmem)


def _log_sigmoid(x):
    return jnp.minimum(x, 0.0) - jnp.log1p(jnp.exp(-jnp.abs(x)))


def _rms(x, g):
    return x * lax.rsqrt(jnp.mean(x * x, axis=-1, keepdims=True) + EPS) * g


def _dot(a, b, **kw):
    return jnp.dot(a, b, preferred_element_type=F32, **kw)


def _dot_nt(a, b):
    return lax.dot_general(a, b, (((1,), (1,)), ((), ())), preferred_element_type=F32)


def _dot_tn(a, b):
    return lax.dot_general(a, b, (((0,), (0,)), ((), ())), preferred_element_type=F32)


def _rowmajor_chunk(ref, c, rows, pitch, first=0):
    return ref[pl.ds(first * pitch + c, rows, stride=pitch), :]


def _store_rowmajor(ref, x):
    rows, d = x.shape
    pitch = d // LANES
    for c in range(pitch):
        ref[pl.ds(c, rows, stride=pitch), :] = x[:, c * LANES:(c + 1) * LANES]


def _in_proj_kernel(x_ref, g_ref, wa_ref, wb_ref, proj_ref, small_ref, h_scr):
    @pl.when(pl.program_id(1) == 0)
    def _():
        hb = _rms(x_ref[...], g_ref[...]).astype(BF16)
        h_scr[...] = hb
        small_ref[...] = _dot(hb, wb_ref[...])

    proj_ref[...] = _dot(h_scr[...], wa_ref[...]).astype(proj_ref.dtype)


def _in_proj(x2d, g, wa, wb, tm=1024, tn=1280):
    n, d = x2d.shape
    na = wa.shape[1]
    return pl.pallas_call(
        _in_proj_kernel,
        grid=(n // tm, na // tn),
        in_specs=[
            pl.BlockSpec((tm, d), lambda i, j: (i, 0)),
            pl.BlockSpec((1, d), lambda i, j: (0, 0)),
            pl.BlockSpec((d, tn), lambda i, j: (0, j)),
            pl.BlockSpec((d, LANES), lambda i, j: (0, 0)),
        ],
        out_specs=[
            pl.BlockSpec((tm, tn), lambda i, j: (i, j)),
            pl.BlockSpec((tm, LANES), lambda i, j: (i, 0)),
        ],
        out_shape=[
            jax.ShapeDtypeStruct((n, na), BF16),
            jax.ShapeDtypeStruct((n, LANES), F32),
        ],
        scratch_shapes=[pltpu.VMEM((tm, d), BF16)],
        compiler_params=_cparams(("parallel", "arbitrary")),
        name="in_proj",
    )(x2d, g, wa, wb)


def _gla_kernel(q_ref, k_ref, v_ref, gate_ref, small_ref, wa2_ref, ba_ref, bf_ref, gout_ref,
                o_ref, f_ref, state_scr, fcar_scr, la_scr, lf_scr, *, n_chunks):
    @pl.when(pl.program_id(1) == 0)
    def _():
        state_scr[...] = jnp.zeros_like(state_scr)
        fcar_scr[...] = jnp.zeros_like(fcar_scr)

    small = small_ref[...]
    la_scr[...] = _log_sigmoid(_dot(small.astype(BF16), wa2_ref[...]) + ba_ref[...]) * (1.0 / GLA_TAU)
    lf_scr[...] = _log_sigmoid(small + bf_ref[...])
    row = lax.broadcasted_iota(jnp.int32, (CHUNK, CHUNK), 0)
    col = lax.broadcasted_iota(jnp.int32, (CHUNK, CHUNK), 1)
    tri = (col <= row).astype(F32)
    scale = GLA_DK ** -0.5

    def chunk_body(c, carry):
        r = pl.ds(pl.multiple_of(c * CHUNK, CHUNK), CHUNK)
        b = _dot(tri, la_scr[r, :], precision=lax.Precision.HIGHEST)
        b_end = b[CHUNK - 1:CHUNK, :]
        k_dec = k_ref[r, :].astype(F32) * jnp.exp(b_end - b)
        decay = jnp.exp(b_end)
        f_cum = _dot(tri, lf_scr[r, :], precision=lax.Precision.HIGHEST) + fcar_scr[...]
        f_ref[r, :] = f_cum
        fcar_scr[...] = f_cum[CHUNK - 1:CHUNK, :]
        for h in range(GLA_HEADS):
            ks = slice(h * GLA_DK, (h + 1) * GLA_DK)
            vs = slice(h * GLA_DV, (h + 1) * GLA_DV)
            st = state_scr[h] * decay[:, ks] + _dot_tn(v_ref[r, vs], k_dec[:, ks].astype(BF16))
            state_scr[h] = st
            o = _dot_nt(q_ref[r, ks], st.astype(BF16)) * scale
            gt = gate_ref[r, vs].astype(F32)
            o_ref[r, vs] = (_rms(o, gout_ref[:, vs]) * (gt * jax.nn.sigmoid(gt))).astype(o_ref.dtype)
        return carry

    lax.fori_loop(0, n_chunks, chunk_body, 0)


def _gla(proj, small, wa2p, ba, bfv, gout, batch, seq, ts=512):
    n = proj.shape[0]
    nsb = seq // ts
    kw = GLA_HEADS * GLA_DK
    vw = GLA_HEADS * GLA_DV
    row = lambda b, s: b * nsb + s
    return pl.pallas_call(
        functools.partial(_gla_kernel, n_chunks=ts // CHUNK),
        grid=(batch, nsb),
        in_specs=[
            pl.BlockSpec((ts, kw), lambda b, s: (row(b, s), 0)),
            pl.BlockSpec((ts, kw), lambda b, s: (row(b, s), 1)),
            pl.BlockSpec((ts, vw), lambda b, s: (row(b, s), 1)),
            pl.BlockSpec((ts, vw), lambda b, s: (row(b, s), 2)),
            pl.BlockSpec((ts, LANES), lambda b, s: (row(b, s), 0)),
            pl.BlockSpec((LANES, kw), lambda b, s: (0, 0)),
            pl.BlockSpec((1, kw), lambda b, s: (0, 0)),
            pl.BlockSpec((1, LANES), lambda b, s: (0, 0)),
            pl.BlockSpec((1, vw), lambda b, s: (0, 0)),
        ],
        out_specs=[
            pl.BlockSpec((ts, vw), lambda b, s: (row(b, s), 0)),
            pl.BlockSpec((ts, LANES), lambda b, s: (row(b, s), 0)),
        ],
        out_shape=[
            jax.ShapeDtypeStruct((n, vw), BF16),
            jax.ShapeDtypeStruct((n, LANES), F32),
        ],
        scratch_shapes=[
            pltpu.VMEM((GLA_HEADS, GLA_DV, GLA_DK), F32),
            pltpu.VMEM((1, LANES), F32),
            pltpu.VMEM((ts, kw), F32),
            pltpu.VMEM((ts, LANES), F32),
        ],
        compiler_params=_cparams(("parallel", "arbitrary")),
        name="gla",
    )(proj, proj, proj, proj, small, wa2p, ba, bfv, gout)


LOG2E = 1.4426950408889634
def _fox_kernel(q_ref, k_ref, v_ref, fk_ref, g_ref, o_ref, m_scr, l_scr, acc_scr, *, tq, tk):
    i = pl.program_id(1)
    j = pl.program_id(2)

    @pl.when(j == 0)
    def _():
        m_scr[...] = jnp.full_like(m_scr, -jnp.inf)
        l_scr[...] = jnp.zeros_like(l_scr)
        acc_scr[...] = jnp.zeros_like(acc_scr)

    def update(on_diagonal):
        for h in range(FOX_HEADS):
            hs = slice(h * FOX_DH, (h + 1) * FOX_DH)
            s = _dot_nt(q_ref[:, hs], k_ref[:, hs]) - fk_ref[h:h + 1, :] * LOG2E
            if on_diagonal:
                row = lax.broadcasted_iota(jnp.int32, (tq, tk), 0)
                col = lax.broadcasted_iota(jnp.int32, (tq, tk), 1)
                s = jnp.where(col <= row, s, -jnp.inf)
            m_prev = m_scr[h]
            m_new = jnp.maximum(m_prev, jnp.max(s, axis=-1, keepdims=True))
            alpha = jnp.exp2(m_prev - m_new)
            p = jnp.exp2(s - m_new)
            l_scr[h] = alpha * l_scr[h] + jnp.sum(p, axis=-1, keepdims=True)
            acc_scr[h] = alpha * acc_scr[h] + _dot(p.astype(BF16), v_ref[:, hs])
            m_scr[h] = m_new

    @pl.when(j < i)
    def _():
        update(False)

    @pl.when(j == i)
    def _():
        update(True)
        for h in range(FOX_HEADS):
            hs = slice(h * FOX_DH, (h + 1) * FOX_DH)
            o = acc_scr[h] / l_scr[h]
            o_ref[:, hs] = _rms(o, g_ref[:, hs]).astype(o_ref.dtype)


def _fox(proj, f_t, g_fox, batch, seq, tq=512, tk=512):
    assert tq == tk, "the diagonal-block mask assumes square blocks"
    n = proj.shape[0]
    w = FOX_HEADS * FOX_DH
    nq, nk = seq // tq, seq // tk
    qcol, kcol, vcol = 3072 // w, 3584 // w, 4096 // w
    return pl.pallas_call(
        functools.partial(_fox_kernel, tq=tq, tk=tk),
        grid=(batch, nq, nk),
        in_specs=[
            pl.BlockSpec((tq, w), lambda b, i, j: (b * nq + i, qcol)),
            pl.BlockSpec((tk, w), lambda b, i, j: (b * nk + jnp.minimum(j, i), kcol)),
            pl.BlockSpec((tk, w), lambda b, i, j: (b * nk + jnp.minimum(j, i), vcol)),
            pl.BlockSpec((None, 8, tk), lambda b, i, j: (b, 0, jnp.minimum(j, i))),
            pl.BlockSpec((1, w), lambda b, i, j: (0, 0)),
        ],
        out_specs=pl.BlockSpec((tq, w), lambda b, i, j: (b * nq + i, 0)),
        out_shape=jax.ShapeDtypeStruct((n, w), BF16),
        scratch_shapes=[
            pltpu.VMEM((FOX_HEADS, tq, 1), F32),
            pltpu.VMEM((FOX_HEADS, tq, 1), F32),
            pltpu.VMEM((FOX_HEADS, tq, FOX_DH), F32),
        ],
        compiler_params=_cparams(("parallel", "parallel", "arbitrary")),
        name="fox",
    )(proj, proj, proj, f_t, g_fox)


def _mem_kv_kernel(m_ref, g_ref, w_ref, kv_ref):
    kv_ref[...] = _dot(_rms(m_ref[...], g_ref[...]).astype(BF16), w_ref[...]).astype(kv_ref.dtype)


def _mem_kv(mem2d, g, w, tm=256):
    n, d = mem2d.shape
    nw = w.shape[1]
    return pl.pallas_call(
        _mem_kv_kernel,
        grid=(n // tm,),
        in_specs=[
            pl.BlockSpec((tm, d), lambda i: (i, 0)),
            pl.BlockSpec((1, d), lambda i: (0, 0)),
            pl.BlockSpec((d, nw), lambda i: (0, 0)),
        ],
        out_specs=pl.BlockSpec((tm, nw), lambda i: (i, 0)),
        out_shape=jax.ShapeDtypeStruct((n, nw), BF16),
        compiler_params=_cparams(("parallel",)),
        name="mem_kv",
    )(mem2d, g, w)


def _mem_attn_kernel(q_ref, k_ref, v_ref, g_ref, o_ref):
    scale = MEM_DH ** -0.5
    for h in range(MEM_HEADS):
        hs = slice(h * MEM_DH, (h + 1) * MEM_DH)
        s = _dot_nt(q_ref[:, hs], k_ref[:, hs]) * scale
        p = jnp.exp(s - jnp.max(s, axis=-1, keepdims=True))
        l = jnp.sum(p, axis=-1, keepdims=True)
        o = _dot((p / l).astype(BF16), v_ref[:, hs])
        o_ref[:, hs] = _rms(o, g_ref[:, hs]).astype(o_ref.dtype)


def _mem_attn(proj, kv, g_mem_out, batch, seq, n_mem, tq=1024):
    n = proj.shape[0]
    w = MEM_HEADS * MEM_DH
    nq = seq // tq
    qcol = 4608 // w
    return pl.pallas_call(
        _mem_attn_kernel,
        grid=(n // tq,),
        in_specs=[
            pl.BlockSpec((tq, w), lambda i: (i, qcol)),
            pl.BlockSpec((n_mem, w), lambda i: (i // nq, 0)),
            pl.BlockSpec((n_mem, w), lambda i: (i // nq, 1)),
            pl.BlockSpec((1, w), lambda i: (0, 0)),
        ],
        out_specs=pl.BlockSpec((tq, w), lambda i: (i, 0)),
        out_shape=jax.ShapeDtypeStruct((n, w), BF16),
        compiler_params=_cparams(("parallel",)),
        name="mem_attn",
    )(proj, kv, kv, g_mem_out)


def _out_proj_kernel(x_ref, gla_ref, fox_ref, mem_ref, w1_ref, w2_ref, w3_ref, g_ref, wr_ref, br_ref,
                     x2_ref, xn_ref, logit_ref):
    x2 = (x_ref[...] + _dot(gla_ref[...], w1_ref[...]) + _dot(fox_ref[...], w2_ref[...])
          + _dot(mem_ref[...], w3_ref[...]))
    x2_ref[...] = x2
    xn = _rms(x2, g_ref[...])
    _store_rowmajor(xn_ref, xn)
    xh = xn.astype(BF16)
    xl = (xn - xh.astype(F32)).astype(BF16)
    hi = _dot(xh, wr_ref[...])
    logit_ref[...] = hi[:, :LANES] + hi[:, LANES:] + _dot(xl, wr_ref[:, :LANES]) + br_ref[...]


def _out_proj(x2d, gla, fox, memo, w_out, g_ffn, wr, br, tm=256):
    n, d = x2d.shape
    w1, w2 = gla.shape[1], fox.shape[1]
    const = lambda i: (0, 0)
    return pl.pallas_call(
        _out_proj_kernel,
        grid=(n // tm,),
        in_specs=[
            pl.BlockSpec((tm, d), lambda i: (i, 0)),
            pl.BlockSpec((tm, w1), lambda i: (i, 0)),
            pl.BlockSpec((tm, w2), lambda i: (i, 0)),
            pl.BlockSpec((tm, w2), lambda i: (i, 0)),
            pl.BlockSpec((w1, d), lambda i: (0, 0)),
            pl.BlockSpec((w2, d), lambda i: (w1 // w2, 0)),
            pl.BlockSpec((w2, d), lambda i: (w1 // w2 + 1, 0)),
            pl.BlockSpec((1, d), const),
            pl.BlockSpec((d, 2 * LANES), const),
            pl.BlockSpec((1, LANES), const),
        ],
        out_specs=[
            pl.BlockSpec((tm, d), lambda i: (i, 0)),
            pl.BlockSpec((tm * (d // LANES), LANES), lambda i: (i, 0)),
            pl.BlockSpec((tm, LANES), lambda i: (i, 0)),
        ],
        out_shape=[
            jax.ShapeDtypeStruct((n, d), F32),
            jax.ShapeDtypeStruct((n * (d // LANES), LANES), F32),
            jax.ShapeDtypeStruct((n, LANES), F32),
        ],
        compiler_params=_cparams(("parallel",)),
        name="out_proj",
    )(x2d, gla, fox, memo, w_out, w_out, w_out, g_ffn, wr, br)


def _route_kernel(l_ref, idx_ref, gate_ref, rank_ref, cnt_ref, carry_scr, *, tb):
    @pl.when(pl.program_id(0) == 0)
    def _():
        carry_scr[...] = jnp.zeros_like(carry_scr)

    lane = lax.broadcasted_iota(jnp.int32, (tb, LANES), 1)
    logit = jnp.where(lane < N_EXPERTS, l_ref[...], -jnp.inf)
    vals, hots = [], []
    idx_out = jnp.zeros((tb, LANES), jnp.int32)
    for k in range(TOP_K):
        m = jnp.max(logit, axis=-1, keepdims=True)
        ik = jnp.min(jnp.where(logit == m, lane, LANES), axis=-1, keepdims=True)
        hot = lane == ik
        logit = jnp.where(hot, -jnp.inf, logit)
        vals.append(m)
        hots.append(hot)
        idx_out = jnp.where(lane == k, ik, idx_out)
    idx_ref[...] = idx_out

    e = [jnp.exp(v - vals[0]) for v in vals]
    den = e[0] + e[1] + e[2] + e[3]
    gate_out = jnp.zeros((tb, LANES), F32)
    for k in range(TOP_K):
        gate_out = jnp.where(lane == k, e[k] / den, gate_out)
    gate_ref[...] = gate_out

    member = jnp.zeros((tb, LANES), F32)
    for hot in hots:
        member = member + hot.astype(F32)
    row = lax.broadcasted_iota(jnp.int32, (tb, tb), 0)
    col = lax.broadcasted_iota(jnp.int32, (tb, tb), 1)
    before = (col < row).astype(BF16)
    rank = _dot(before, member.astype(BF16)) + carry_scr[...]
    rank_out = jnp.zeros((tb, LANES), F32)
    for k in range(TOP_K):
        rk = jnp.sum(jnp.where(hots[k], rank, 0.0), axis=-1, keepdims=True)
        rank_out = jnp.where(lane == k, rk, rank_out)
    rank_ref[...] = rank_out.astype(jnp.int32)
    total = carry_scr[...] + jnp.sum(member, axis=0, keepdims=True)
    carry_scr[...] = total
    cnt_ref[...] = total.astype(jnp.int32)


def _route(logits, tb=512):
    n = logits.shape[0]
    blk = pl.BlockSpec((tb, LANES), lambda i: (i, 0))
    return pl.pallas_call(
        functools.partial(_route_kernel, tb=tb),
        grid=(n // tb,),
        in_specs=[blk],
        out_specs=[blk, blk, blk, pl.BlockSpec((1, LANES), lambda i: (0, 0))],
        out_shape=[
            jax.ShapeDtypeStruct((n, LANES), jnp.int32),
            jax.ShapeDtypeStruct((n, LANES), F32),
            jax.ShapeDtypeStruct((n, LANES), jnp.int32),
            jax.ShapeDtypeStruct((1, LANES), jnp.int32),
        ],
        scratch_shapes=[pltpu.VMEM((1, LANES), F32)],
        compiler_params=_cparams(("arbitrary",)),
        name="route",
    )(logits)


def _dispatch_kernel(zblk_ref, zok_ref, used_ref, pos_ref, xn_ref, xr_hbm, zero_scr, sem,
                     *, tb, pitch, n_blocks_max):
    blk_rows = MOE_TM * pitch

    def zero_copy(b):
        dst = xr_hbm.at[pl.ds(pl.multiple_of(b * blk_rows, blk_rows), blk_rows), :]
        return pltpu.make_async_copy(zero_scr, dst, sem.at[1])

    @pl.when(pl.program_id(0) == 0)
    def _():
        zero_scr[...] = jnp.zeros_like(zero_scr)

        def per_expert(action):
            def body(e, carry):
                @pl.when(zok_ref[e] == 1)
                def _():
                    action(zero_copy(zblk_ref[e]))
                return carry
            lax.fori_loop(0, N_EXPERTS, body, 0)

        def per_tail(action):
            def body(b, carry):
                action(zero_copy(b))
                return carry
            lax.fori_loop(used_ref[0], n_blocks_max, body, 0)

        per_expert(lambda c: c.start())
        per_tail(lambda c: c.start())
        per_expert(lambda c: c.wait())
        per_tail(lambda c: c.wait())

    def issue(t, carry):
        src = xn_ref.at[pl.ds(pl.multiple_of(t * pitch, pitch), pitch), :]
        for k in range(TOP_K):
            p = pos_ref[0, t * TOP_K + k]
            dst = xr_hbm.at[pl.ds(pl.multiple_of(p * pitch, pitch), pitch), :]
            pltpu.make_async_copy(src, dst, sem.at[0]).start()
        return carry

    lax.fori_loop(0, tb, issue, 0, unroll=4)
    for k in range(TOP_K):
        pltpu.make_async_copy(xn_ref, xr_hbm.at[pl.ds(0, tb * pitch), :], sem.at[0]).wait()


def _dispatch(xn_rm, pos, zblk, zok, n_used, n_blocks_max, d, tb=256):
    pitch = d // LANES
    n = xn_rm.shape[0] // pitch
    nb = n // tb
    grid_spec = pltpu.PrefetchScalarGridSpec(
        num_scalar_prefetch=3,
        grid=(nb,),
        in_specs=[
            pl.BlockSpec((None, 1, TOP_K * tb), lambda i, *_: (i, 0, 0), memory_space=pltpu.SMEM),
            pl.BlockSpec((tb * pitch, LANES), lambda i, *_: (i, 0)),
        ],
        out_specs=pl.BlockSpec(memory_space=pl.ANY),
        scratch_shapes=[pltpu.VMEM((MOE_TM * pitch, LANES), F32), pltpu.SemaphoreType.DMA((2,))],
    )
    return pl.pallas_call(
        functools.partial(_dispatch_kernel, tb=tb, pitch=pitch, n_blocks_max=n_blocks_max),
        grid_spec=grid_spec,
        out_shape=jax.ShapeDtypeStruct((n_blocks_max * MOE_TM * pitch, LANES), F32),
        compiler_params=_cparams(("arbitrary",)),
        name="dispatch",
    )(zblk, zok, n_used, pos.reshape(nb, 1, TOP_K * tb), xn_rm)


def _moe_up_kernel(rb_ref, ex_ref, ct_ref, ok_ref, x_ref, wg_ref, wu_ref, bg_ref, bu_ref, act_ref, lhs_scr):
    @pl.when(ok_ref[pl.program_id(0)] == 1)
    def _():
        rows, d = lhs_scr.shape
        pitch = d // LANES
        for c in range(pitch):
            lhs_scr[:, c * LANES:(c + 1) * LANES] = _rowmajor_chunk(x_ref, c, rows, pitch).astype(BF16)
        x = lhs_scr[...]
        gate = jnp.minimum(_dot(x, wg_ref[...]) + bg_ref[...], SWIGLU_LIMIT)
        up = jnp.clip(_dot(x, wu_ref[...]) + bu_ref[...], -SWIGLU_LIMIT, SWIGLU_LIMIT)
        act_ref[...] = (gate * jax.nn.sigmoid(SWIGLU_ALPHA * gate) * (up + 1.0)).astype(act_ref.dtype)

    @pl.when(ok_ref[pl.program_id(0)] == 0)
    def _():
        act_ref[...] = jnp.zeros_like(act_ref)


def _moe_up(x_rows_rm, w_up, b_up, items, tn):
    d = w_up.shape[1]
    pitch = d // LANES
    n_rows = x_rows_rm.shape[0] // pitch
    d_ff = w_up.shape[2] // 2
    rb, ex, ct, ok = items
    up_off = d_ff // tn
    grid_spec = pltpu.PrefetchScalarGridSpec(
        num_scalar_prefetch=4,
        grid=(rb.shape[0],),
        in_specs=[
            pl.BlockSpec((MOE_TM * pitch, LANES), lambda t, rb, ex, ct, ok: (rb[t], 0)),
            pl.BlockSpec((None, d, tn), lambda t, rb, ex, ct, ok: (ex[t], 0, ct[t])),
            pl.BlockSpec((None, d, tn), lambda t, rb, ex, ct, ok: (ex[t], 0, ct[t] + up_off)),
            pl.BlockSpec((None, 1, tn), lambda t, rb, ex, ct, ok: (ex[t], 0, ct[t])),
            pl.BlockSpec((None, 1, tn), lambda t, rb, ex, ct, ok: (ex[t], 0, ct[t] + up_off)),
        ],
        out_specs=pl.BlockSpec((MOE_TM, tn), lambda t, rb, ex, ct, ok: (rb[t], ct[t])),
        scratch_shapes=[pltpu.VMEM((MOE_TM, d), BF16)],
    )
    return pl.pallas_call(
        _moe_up_kernel,
        grid_spec=grid_spec,
        out_shape=jax.ShapeDtypeStruct((n_rows, d_ff), BF16),
        compiler_params=_cparams(("arbitrary",)),
        name="moe_up",
    )(rb, ex, ct, ok, x_rows_rm, w_up, w_up, b_up, b_up)


def _moe_down_kernel(rb_ref, ex_ref, ct_ref, ok_ref, a_ref, w_ref, b_ref, y_ref):
    @pl.when(ok_ref[pl.program_id(0)] == 1)
    def _():
        _store_rowmajor(y_ref, _dot(a_ref[...], w_ref[...]) + b_ref[...])

    @pl.when(ok_ref[pl.program_id(0)] == 0)
    def _():
        y_ref[...] = jnp.zeros_like(y_ref)


def _moe_down(act, w_down, b_down, items):
    n_rows, d_ff = act.shape
    d = w_down.shape[2]
    pitch = d // LANES
    rb, ex, ct, ok = items
    grid_spec = pltpu.PrefetchScalarGridSpec(
        num_scalar_prefetch=4,
        grid=(rb.shape[0],),
        in_specs=[
            pl.BlockSpec((MOE_TM, d_ff), lambda t, rb, ex, ct, ok: (rb[t], 0)),
            pl.BlockSpec((None, d_ff, d), lambda t, rb, ex, ct, ok: (ex[t], 0, 0)),
            pl.BlockSpec((None, 1, d), lambda t, rb, ex, ct, ok: (ex[t], 0, 0)),
        ],
        out_specs=pl.BlockSpec((MOE_TM * pitch, LANES), lambda t, rb, ex, ct, ok: (rb[t], 0)),
    )
    return pl.pallas_call(
        _moe_down_kernel,
        grid_spec=grid_spec,
        out_shape=jax.ShapeDtypeStruct((n_rows * pitch, LANES), F32),
        compiler_params=_cparams(("arbitrary",)),
        name="moe_down",
    )(rb, ex, ct, ok, act, w_down, b_down)


def _work_items(blk_start, blk_count, n_blocks_max, n_col_tiles):
    n_items = n_blocks_max * n_col_tiles
    blk_end = jnp.cumsum(blk_count)
    item_end = blk_end * n_col_tiles
    n_used = blk_end[-1]
    total = item_end[-1]
    idx = jnp.arange(n_items, dtype=jnp.int32)
    t = jnp.minimum(idx, total - 1)
    ex = jnp.sum(item_end[None, :] <= t[:, None], axis=1).astype(jnp.int32)
    local = t - blk_start[ex] * n_col_tiles
    cnt = jnp.maximum(blk_count[ex], 1)
    tail = idx - total
    n_unused = jnp.maximum(n_blocks_max - n_used, 1)
    is_tail = idx >= total
    ct = jnp.where(is_tail, tail // n_unused, local // cnt)
    rb = jnp.where(is_tail, n_used + tail % n_unused, blk_start[ex] + local % cnt)
    ok = jnp.logical_not(is_tail).astype(jnp.int32)
    return rb.astype(jnp.int32), ex, ct.astype(jnp.int32), ok


def _combine_kernel(pos_ref, y_hbm, x2_ref, gate_ref, g_ref, o_ref, buf, sem, *, tb, pitch):
    def issue(r, carry):
        src = y_hbm.at[pl.ds(pl.multiple_of(pos_ref[0, r] * pitch, pitch), pitch), :]
        dst = buf.at[pl.ds(pl.multiple_of(r * pitch, pitch), pitch), :]
        pltpu.make_async_copy(src, dst, sem.at[0]).start()
        return carry

    lax.fori_loop(0, TOP_K * tb, issue, 0, unroll=8)
    pltpu.make_async_copy(y_hbm.at[pl.ds(0, TOP_K * tb * pitch), :], buf, sem.at[0]).wait()
    gates = gate_ref[...]
    ssq = jnp.zeros((tb, 1), F32)
    for c in range(pitch):
        cs = slice(c * LANES, (c + 1) * LANES)
        z = x2_ref[:, cs]
        for k in range(TOP_K):
            z = z + gates[:, k:k + 1] * _rowmajor_chunk(buf, c, tb, pitch, first=k * tb)
        o_ref[:, cs] = z
        ssq = ssq + jnp.sum(z * z, axis=-1, keepdims=True)
    o_ref[...] = o_ref[...] * lax.rsqrt(ssq * (1.0 / (pitch * LANES)) + EPS) * g_ref[...]


def _combine(y_rows_rm, pos, gates, x2, g_final, tb=128):
    n, d = x2.shape
    nb = n // tb
    pitch = d // LANES
    pos_blk = pos.reshape(nb, tb, TOP_K).transpose(0, 2, 1).reshape(nb, 1, TOP_K * tb)
    return pl.pallas_call(
        functools.partial(_combine_kernel, tb=tb, pitch=pitch),
        grid=(nb,),
        in_specs=[
            pl.BlockSpec((None, 1, TOP_K * tb), lambda i: (i, 0, 0), memory_space=pltpu.SMEM),
            pl.BlockSpec(memory_space=pl.ANY),
            pl.BlockSpec((tb, d), lambda i: (i, 0)),
            pl.BlockSpec((tb, LANES), lambda i: (i, 0)),
            pl.BlockSpec((1, d), lambda i: (0, 0)),
        ],
        out_specs=pl.BlockSpec((tb, d), lambda i: (i, 0)),
        out_shape=jax.ShapeDtypeStruct((n, d), F32),
        scratch_shapes=[pltpu.VMEM((TOP_K * tb * pitch, LANES), F32), pltpu.SemaphoreType.DMA((1,))],
        compiler_params=_cparams(("arbitrary",)),
        name="combine",
    )(pos_blk, y_rows_rm, x2, gates, g_final)


def _routing_tables(idx, rank, cnt):
    counts = cnt[0, :N_EXPERTS]
    blk_count = (counts + MOE_TM - 1) // MOE_TM
    blk_end = jnp.cumsum(blk_count)
    blk_start = blk_end - blk_count
    hot = idx[:, :TOP_K, None] == jnp.arange(N_EXPERTS, dtype=jnp.int32)
    pos = jnp.sum(jnp.where(hot, blk_start * MOE_TM, 0), axis=-1) + rank[:, :TOP_K]
    return pos.astype(jnp.int32), blk_start.astype(jnp.int32), blk_count.astype(jnp.int32)


def kernel(x, mem, g_attn_norm, g_mem_norm, w_in, w_gla_a2, b_gla_a, g_gla_out, b_fox_f, g_fox_out,
           w_mem_kv, g_mem_out, w_out, g_ffn_norm, w_router, b_router, w_moe_up, b_moe_up,
           w_moe_down, b_moe_down, g_final):
    batch, seq, d = x.shape
    n_mem = mem.shape[1]
    n = batch * seq
    depth = w_in.shape[0]
    assert depth == 1, "the combine kernel applies the final norm, so exactly one layer is supported"
    kw = GLA_HEADS * GLA_DK
    vw = GLA_HEADS * GLA_DV
    fw = FOX_HEADS * FOX_DH
    mw = MEM_HEADS * MEM_DH
    o_q, o_k, o_v, o_g = 0, kw, 2 * kw, 2 * kw + vw
    o_a = o_g + vw
    o_fq = o_a + GLA_LOWRANK
    o_fk, o_fv = o_fq + fw, o_fq + 2 * fw
    o_ff = o_fq + 3 * fw
    o_mq = o_ff + FOX_HEADS
    f_lane = GLA_LOWRANK

    xf = x.reshape(n, d)
    for l in range(depth):
        wi = w_in[l]
        fox_q_scale = FOX_DH ** -0.5 * LOG2E
        wa = jnp.concatenate([wi[:, o_q:o_a], wi[:, o_fq:o_fk] * fox_q_scale, wi[:, o_fk:o_ff],
                              wi[:, o_mq:o_mq + mw]], axis=1).astype(BF16)
        wb = jnp.zeros((d, LANES), F32).at[:, :GLA_LOWRANK].set(wi[:, o_a:o_fq])
        wb = wb.at[:, f_lane:f_lane + FOX_HEADS].set(wi[:, o_ff:o_mq]).astype(BF16)
        proj, small = _in_proj(xf, g_attn_norm[l].reshape(1, d), wa, wb)

        wa2p = jnp.zeros((LANES, kw), F32).at[:GLA_LOWRANK].set(w_gla_a2[l]).astype(BF16)
        bfv = jnp.zeros((1, LANES), F32).at[0, f_lane:f_lane + FOX_HEADS].set(b_fox_f[l])
        gla, f_cum = _gla(proj, small, wa2p, b_gla_a[l].reshape(1, kw), bfv,
                          g_gla_out[l].reshape(1, vw), batch, seq)

        f_t = f_cum.reshape(batch, seq, LANES)[:, :, f_lane:f_lane + FOX_HEADS].transpose(0, 2, 1)
        f_t = jnp.concatenate([f_t, jnp.zeros_like(f_t)], axis=1)
        fox = _fox(proj, f_t, g_fox_out[l].reshape(1, fw), batch, seq)

        kv = _mem_kv(mem.reshape(batch * n_mem, d), g_mem_norm[l].reshape(1, d), w_mem_kv[l].astype(BF16))
        memo = _mem_attn(proj, kv, g_mem_out[l].reshape(1, mw), batch, seq, n_mem)

        wr = jnp.zeros((d, LANES), F32).at[:, :N_EXPERTS].set(w_router[l])
        wr_hi = wr.astype(BF16)
        wr = jnp.concatenate([wr_hi, (wr - wr_hi.astype(F32)).astype(BF16)], axis=1)
        br =jnp.zeros((1, LANES), F32).at[0, :N_EXPERTS].set(b_router[l])
        x2, xn, logits = _out_proj(xf, gla, fox, memo, w_out[l].astype(BF16),
                                   g_ffn_norm[l].reshape(1, d), wr, br)

        idx, gates, rank, cnt = _route(logits)
        pos, blk_start, blk_count = _routing_tables(idx, rank, cnt)
        n_blocks_max = -(-(n * TOP_K + N_EXPERTS * (MOE_TM - 1)) // MOE_TM)
        blk_end = blk_start + blk_count
        x_rows = _dispatch(xn, pos, blk_end - 1, (blk_count > 0).astype(jnp.int32), blk_end[-1:],
                           n_blocks_max, d)
        up_tn = 1024
        d_ff = w_moe_up.shape[3] // 2
        act = _moe_up(x_rows, w_moe_up[l].astype(BF16), b_moe_up[l].reshape(N_EXPERTS, 1, 2 * d_ff),
                      _work_items(blk_start, blk_count, n_blocks_max, d_ff // up_tn), up_tn)
        y_rows = _moe_down(act, w_moe_down[l].astype(BF16), b_moe_down[l].reshape(N_EXPERTS, 1, d),
                           _work_items(blk_start, blk_count, n_blocks_max, 1))
        xf = _combine(y_rows, pos, gates, x2, g_final.reshape(1, d))
    return xf.reshape(batch, seq, d)
```

```python
import functools

import jax
import jax.numpy as jnp
from jax import lax
from jax.experimental import pallas as pl
from jax.experimental.pallas import tpu as pltpu

EPS = 1e-5
CHUNK = 64
GLA_HEADS = 4
GLA_DK = 128
GLA_DV = 256
GLA_LOWRANK = 16
GLA_TAU = 16.0
FOX_HEADS = 4
FOX_DH = 128
MEM_HEADS = 4
MEM_DH = 128
N_EXPERTS = 32
TOP_K = 4
SWIGLU_LIMIT = 7.0
SWIGLU_ALPHA = 1.702
LANES = 128
MOE_TM = 256
VMEM_LIMIT = 56 * 1024 * 1024

F32 = jnp.float32
BF16 = jnp.bfloat16


def _cparams(sem, vmem=VMEM_LIMIT):
    return pltpu.CompilerParams(dimension_semantics=sem, vmem_limit_bytes=vmem)


def _log_sigmoid(x):
    return jnp.minimum(x, 0.0) - jnp.log1p(jnp.exp(-jnp.abs(x)))


def _rms(x, g):
    return x * lax.rsqrt(jnp.mean(x * x, axis=-1, keepdims=True) + EPS) * g


def _dot(a, b, **kw):
    return jnp.dot(a, b, preferred_element_type=F32, **kw)


def _dot_nt(a, b):
    return lax.dot_general(a, b, (((1,), (1,)), ((), ())), preferred_element_type=F32)


def _dot_tn(a, b):
    return lax.dot_general(a, b, (((0,), (0,)), ((), ())), preferred_element_type=F32)


def _rowmajor_chunk(ref, c, rows, pitch, first=0):
    return ref[pl.ds(first * pitch + c, rows, stride=pitch), :]


def _store_rowmajor(ref, x):
    rows, d = x.shape
    pitch = d // LANES
    for c in range(pitch):
        ref[pl.ds(c, rows, stride=pitch), :] = x[:, c * LANES:(c + 1) * LANES]


def _in_proj_kernel(x_ref, g_ref, wa_ref, wb_ref, proj_ref, small_ref, h_scr):
    @pl.when(pl.program_id(1) == 0)
    def _():
        hb = _rms(x_ref[...], g_ref[...]).astype(BF16)
        h_scr[...] = hb
        small_ref[...] = _dot(hb, wb_ref[...])

    proj_ref[...] = _dot(h_scr[...], wa_ref[...]).astype(proj_ref.dtype)


def _in_proj(x2d, g, wa, wb, tm=1024, tn=1280):
    n, d = x2d.shape
    na = wa.shape[1]
    return pl.pallas_call(
        _in_proj_kernel,
        grid=(n // tm, na // tn),
        in_specs=[
            pl.BlockSpec((tm, d), lambda i, j: (i, 0)),
            pl.BlockSpec((1, d), lambda i, j: (0, 0)),
            pl.BlockSpec((d, tn), lambda i, j: (0, j)),
            pl.BlockSpec((d, LANES), lambda i, j: (0, 0)),
        ],
        out_specs=[
            pl.BlockSpec((tm, tn), lambda i, j: (i, j)),
            pl.BlockSpec((tm, LANES), lambda i, j: (i, 0)),
        ],
        out_shape=[
            jax.ShapeDtypeStruct((n, na), BF16),
            jax.ShapeDtypeStruct((n, LANES), F32),
        ],
        scratch_shapes=[pltpu.VMEM((tm, d), BF16)],
        compiler_params=_cparams(("parallel", "arbitrary")),
        name="in_proj",
    )(x2d, g, wa, wb)


def _gla_kernel(q_ref, k_ref, v_ref, gate_ref, small_ref, wa2_ref, ba_ref, bf_ref, gout_ref,
                o_ref, f_ref, state_scr, fcar_scr, la_scr, lf_scr, *, n_chunks):
    @pl.when(pl.program_id(1) == 0)
    def _():
        state_scr[...] = jnp.zeros_like(state_scr)
        fcar_scr[...] = jnp.zeros_like(fcar_scr)

    small = small_ref[...]
    la_scr[...] = _log_sigmoid(_dot(small.astype(BF16), wa2_ref[...]) + ba_ref[...]) * (1.0 / GLA_TAU)
    lf_scr[...] = _log_sigmoid(small + bf_ref[...])
    row = lax.broadcasted_iota(jnp.int32, (CHUNK, CHUNK), 0)
    col = lax.broadcasted_iota(jnp.int32, (CHUNK, CHUNK), 1)
    tri = (col <= row).astype(F32)
    scale = GLA_DK ** -0.5

    def chunk_body(c, carry):
        r = pl.ds(pl.multiple_of(c * CHUNK, CHUNK), CHUNK)
        b = _dot(tri, la_scr[r, :], precision=lax.Precision.HIGHEST)
        b_end = b[CHUNK - 1:CHUNK, :]
        k_dec = k_ref[r, :].astype(F32) * jnp.exp(b_end - b)
        decay = jnp.exp(b_end)
        f_cum = _dot(tri, lf_scr[r, :], precision=lax.Precision.HIGHEST) + fcar_scr[...]
        f_ref[r, :] = f_cum
        fcar_scr[...] = f_cum[CHUNK - 1:CHUNK, :]
        for h in range(GLA_HEADS):
            ks = slice(h * GLA_DK, (h + 1) * GLA_DK)
            vs = slice(h * GLA_DV, (h + 1) * GLA_DV)
            st = state_scr[h] * decay[:, ks] + _dot_tn(v_ref[r, vs], k_dec[:, ks].astype(BF16))
            state_scr[h] = st
            o = _dot_nt(q_ref[r, ks], st.astype(BF16)) * scale
            gt = gate_ref[r, vs].astype(F32)
            o_ref[r, vs] = (_rms(o, gout_ref[:, vs]) * (gt * jax.nn.sigmoid(gt))).astype(o_ref.dtype)
        return carry

    lax.fori_loop(0, n_chunks, chunk_body, 0)


def _gla(proj, small, wa2p, ba, bfv, gout, batch, seq, ts=512):
    n = proj.shape[0]
    nsb = seq // ts
    kw = GLA_HEADS * GLA_DK
    vw = GLA_HEADS * GLA_DV
    row = lambda b, s: b * nsb + s
    return pl.pallas_call(
        functools.partial(_gla_kernel, n_chunks=ts // CHUNK),
        grid=(batch, nsb),
        in_specs=[
            pl.BlockSpec((ts, kw), lambda b, s: (row(b, s), 0)),
            pl.BlockSpec((ts, kw), lambda b, s: (row(b, s), 1)),
            pl.BlockSpec((ts, vw), lambda b, s: (row(b, s), 1)),
            pl.BlockSpec((ts, vw), lambda b, s: (row(b, s), 2)),
            pl.BlockSpec((ts, LANES), lambda b, s: (row(b, s), 0)),
            pl.BlockSpec((LANES, kw), lambda b, s: (0, 0)),
            pl.BlockSpec((1, kw), lambda b, s: (0, 0)),
            pl.BlockSpec((1, LANES), lambda b, s: (0, 0)),
            pl.BlockSpec((1, vw), lambda b, s: (0, 0)),
        ],
        out_specs=[
            pl.BlockSpec((ts, vw), lambda b, s: (row(b, s), 0)),
            pl.BlockSpec((ts, LANES), lambda b, s: (row(b, s), 0)),
        ],
        out_shape=[
            jax.ShapeDtypeStruct((n, vw), BF16),
            jax.ShapeDtypeStruct((n, LANES), F32),
        ],
        scratch_shapes=[
            pltpu.VMEM((GLA_HEADS, GLA_DV, GLA_DK), F32),
            pltpu.VMEM((1, LANES), F32),
            pltpu.VMEM((ts, kw), F32),
            pltpu.VMEM((ts, LANES), F32),
        ],
        compiler_params=_cparams(("parallel", "arbitrary")),
        name="gla",
    )(proj, proj, proj, proj, small, wa2p, ba, bfv, gout)


LOG2E = 1.4426950408889634
def _fox_kernel(q_ref, k_ref, v_ref, fk_ref, g_ref, o_ref, m_scr, l_scr, acc_scr, *, tq, tk):
    i = pl.program_id(1)
    j = pl.program_id(2)

    @pl.when(j == 0)
    def _():
        m_scr[...] = jnp.full_like(m_scr, -jnp.inf)
        l_scr[...] = jnp.zeros_like(l_scr)
        acc_scr[...] = jnp.zeros_like(acc_scr)

    def update(on_diagonal):
        for h in range(FOX_HEADS):
            hs = slice(h * FOX_DH, (h + 1) * FOX_DH)
            s = _dot_nt(q_ref[:, hs], k_ref[:, hs]) - fk_ref[h:h + 1, :] * LOG2E
            if on_diagonal:
                row = lax.broadcasted_iota(jnp.int32, (tq, tk), 0)
                col = lax.broadcasted_iota(jnp.int32, (tq, tk), 1)
                s = jnp.where(col <= row, s, -jnp.inf)
            m_prev = m_scr[h]
            m_new = jnp.maximum(m_prev, jnp.max(s, axis=-1, keepdims=True))
            alpha = jnp.exp2(m_prev - m_new)
            p = jnp.exp2(s - m_new)
            l_scr[h] = alpha * l_scr[h] + jnp.sum(p, axis=-1, keepdims=True)
            acc_scr[h] = alpha * acc_scr[h] + _dot(p.astype(BF16), v_ref[:, hs])
            m_scr[h] = m_new

    @pl.when(j < i)
    def _():
        update(False)

    @pl.when(j == i)
    def _():
        update(True)
        for h in range(FOX_HEADS):
            hs = slice(h * FOX_DH, (h + 1) * FOX_DH)
            o = acc_scr[h] / l_scr[h]
            o_ref[:, hs] = _rms(o, g_ref[:, hs]).astype(o_ref.dtype)


def _fox(proj, f_t, g_fox, batch, seq, tq=512, tk=512):
    assert tq == tk, "the diagonal-block mask assumes square blocks"
    n = proj.shape[0]
    w = FOX_HEADS * FOX_DH
    nq, nk = seq // tq, seq // tk
    qcol, kcol, vcol = 3072 // w, 3584 // w, 4096 // w
    return pl.pallas_call(
        functools.partial(_fox_kernel, tq=tq, tk=tk),
        grid=(batch, nq, nk),
        in_specs=[
            pl.BlockSpec((tq, w), lambda b, i, j: (b * nq + i, qcol)),
            pl.BlockSpec((tk, w), lambda b, i, j: (b * nk + jnp.minimum(j, i), kcol)),
            pl.BlockSpec((tk, w), lambda b, i, j: (b * nk + jnp.minimum(j, i), vcol)),
            pl.BlockSpec((None, 8, tk), lambda b, i, j: (b, 0, jnp.minimum(j, i))),
            pl.BlockSpec((1, w), lambda b, i, j: (0, 0)),
        ],
        out_specs=pl.BlockSpec((tq, w), lambda b, i, j: (b * nq + i, 0)),
        out_shape=jax.ShapeDtypeStruct((n, w), BF16),
        scratch_shapes=[
            pltpu.VMEM((FOX_HEADS, tq, 1), F32),
            pltpu.VMEM((FOX_HEADS, tq, 1), F32),
            pltpu.VMEM((FOX_HEADS, tq, FOX_DH), F32),
        ],
        compiler_params=_cparams(("parallel", "parallel", "arbitrary")),
        name="fox",
    )(proj, proj, proj, f_t, g_fox)


def _mem_kv_kernel(m_ref, g_ref, w_ref, kv_ref):
    kv_ref[...] = _dot(_rms(m_ref[...], g_ref[...]).astype(BF16), w_ref[...]).astype(kv_ref.dtype)


def _mem_kv(mem2d, g, w, tm=256):
    n, d = mem2d.shape
    nw = w.shape[1]
    return pl.pallas_call(
        _mem_kv_kernel,
        grid=(n // tm,),
        in_specs=[
            pl.BlockSpec((tm, d), lambda i: (i, 0)),
            pl.BlockSpec((1, d), lambda i: (0, 0)),
            pl.BlockSpec((d, nw), lambda i: (0, 0)),
        ],
        out_specs=pl.BlockSpec((tm, nw), lambda i: (i, 0)),
        out_shape=jax.ShapeDtypeStruct((n, nw), BF16),
        compiler_params=_cparams(("parallel",)),
        name="mem_kv",
    )(mem2d, g, w)


def _mem_attn_kernel(q_ref, k_ref, v_ref, g_ref, o_ref):
    scale = MEM_DH ** -0.5
    for h in range(MEM_HEADS):
        hs = slice(h * MEM_DH, (h + 1) * MEM_DH)
        s = _dot_nt(q_ref[:, hs], k_ref[:, hs]) * scale
        p = jnp.exp(s - jnp.max(s, axis=-1, keepdims=True))
        l = jnp.sum(p, axis=-1, keepdims=True)
        o = _dot((p / l).astype(BF16), v_ref[:, hs])
        o_ref[:, hs] = _rms(o, g_ref[:, hs]).astype(o_ref.dtype)


def _mem_attn(proj, kv, g_mem_out, batch, seq, n_mem, tq=1024):
    n = proj.shape[0]
    w = MEM_HEADS * MEM_DH
    nq = seq // tq
    qcol = 4608 // w
    return pl.pallas_call(
        _mem_attn_kernel,
        grid=(n // tq,),
        in_specs=[
            pl.BlockSpec((tq, w), lambda i: (i, qcol)),
            pl.BlockSpec((n_mem, w), lambda i: (i // nq, 0)),
            pl.BlockSpec((n_mem, w), lambda i: (i // nq, 1)),
            pl.BlockSpec((1, w), lambda i: (0, 0)),
        ],
        out_specs=pl.BlockSpec((tq, w), lambda i: (i, 0)),
        out_shape=jax.ShapeDtypeStruct((n, w), BF16),
        compiler_params=_cparams(("parallel",)),
        name="mem_attn",
    )(proj, kv, kv, g_mem_out)


def _out_proj_kernel(x_ref, gla_ref, fox_ref, mem_ref, w1_ref, w2_ref, w3_ref, g_ref, wr_ref, br_ref,
                     x2_ref, xn_ref, logit_ref):
    x2 = (x_ref[...] + _dot(gla_ref[...], w1_ref[...]) + _dot(fox_ref[...], w2_ref[...])
          + _dot(mem_ref[...], w3_ref[...]))
    x2_ref[...] = x2
    xn = _rms(x2, g_ref[...])
    _store_rowmajor(xn_ref, xn)
    xh = xn.astype(BF16)
    xl = (xn - xh.astype(F32)).astype(BF16)
    hi = _dot(xh, wr_ref[...])
    logit_ref[...] = hi[:, :LANES] + hi[:, LANES:] + _dot(xl, wr_ref[:, :LANES]) + br_ref[...]


def _out_proj(x2d, gla, fox, memo, w_out, g_ffn, wr, br, tm=256):
    n, d = x2d.shape
    w1, w2 = gla.shape[1], fox.shape[1]
    const = lambda i: (0, 0)
    return pl.pallas_call(
        _out_proj_kernel,
        grid=(n // tm,),
        in_specs=[
            pl.BlockSpec((tm, d), lambda i: (i, 0)),
            pl.BlockSpec((tm, w1), lambda i: (i, 0)),
            pl.BlockSpec((tm, w2), lambda i: (i, 0)),
            pl.BlockSpec((tm, w2), lambda i: (i, 0)),
            pl.BlockSpec((w1, d), lambda i: (0, 0)),
            pl.BlockSpec((w2, d), lambda i: (w1 // w2, 0)),
            pl.BlockSpec((w2, d), lambda i: (w1 // w2 + 1, 0)),
            pl.BlockSpec((1, d), const),
            pl.BlockSpec((d, 2 * LANES), const),
            pl.BlockSpec((1, LANES), const),
        ],
        out_specs=[
            pl.BlockSpec((tm, d), lambda i: (i, 0)),
            pl.BlockSpec((tm * (d // LANES), LANES), lambda i: (i, 0)),
            pl.BlockSpec((tm, LANES), lambda i: (i, 0)),
        ],
        out_shape=[
            jax.ShapeDtypeStruct((n, d), F32),
            jax.ShapeDtypeStruct((n * (d // LANES), LANES), F32),
            jax.ShapeDtypeStruct((n, LANES), F32),
        ],
        compiler_params=_cparams(("parallel",)),
        name="out_proj",
    )(x2d, gla, fox, memo, w_out, w_out, w_out, g_ffn, wr, br)


def _route_kernel(l_ref, idx_ref, gate_ref, rank_ref, cnt_ref, carry_scr, *, tb):
    @pl.when(pl.program_id(0) == 0)
    def _():
        carry_scr[...] = jnp.zeros_like(carry_scr)

    lane = lax.broadcasted_iota(jnp.int32, (tb, LANES), 1)
    logit = jnp.where(lane < N_EXPERTS, l_ref[...], -jnp.inf)
    vals, hots = [], []
    idx_out = jnp.zeros((tb, LANES), jnp.int32)
    for k in range(TOP_K):
        m = jnp.max(logit, axis=-1, keepdims=True)
        ik = jnp.min(jnp.where(logit == m, lane, LANES), axis=-1, keepdims=True)
        hot = lane == ik
        logit = jnp.where(hot, -jnp.inf, logit)
        vals.append(m)
        hots.append(hot)
        idx_out = jnp.where(lane == k, ik, idx_out)
    idx_ref[...] = idx_out

    e = [jnp.exp(v - vals[0]) for v in vals]
    den = e[0] + e[1] + e[2] + e[3]
    gate_out = jnp.zeros((tb, LANES), F32)
    for k in range(TOP_K):
        gate_out = jnp.where(lane == k, e[k] / den, gate_out)
    gate_ref[...] = gate_out

    member = jnp.zeros((tb, LANES), F32)
    for hot in hots:
        member = member + hot.astype(F32)
    row = lax.broadcasted_iota(jnp.int32, (tb, tb), 0)
    col = lax.broadcasted_iota(jnp.int32, (tb, tb), 1)
    before = (col < row).astype(BF16)
    rank = _dot(before, member.astype(BF16)) + carry_scr[...]
    rank_out = jnp.zeros((tb, LANES), F32)
    for k in range(TOP_K):
        rk = jnp.sum(jnp.where(hots[k], rank, 0.0), axis=-1, keepdims=True)
        rank_out = jnp.where(lane == k, rk, rank_out)
    rank_ref[...] = rank_out.astype(jnp.int32)
    total = carry_scr[...] + jnp.sum(member, axis=0, keepdims=True)
    carry_scr[...] = total
    cnt_ref[...] = total.astype(jnp.int32)


def _route(logits, tb=512):
    n = logits.shape[0]
    blk = pl.BlockSpec((tb, LANES), lambda i: (i, 0))
    return pl.pallas_call(
        functools.partial(_route_kernel, tb=tb),
        grid=(n // tb,),
        in_specs=[blk],
        out_specs=[blk, blk, blk, pl.BlockSpec((1, LANES), lambda i: (0, 0))],
        out_shape=[
            jax.ShapeDtypeStruct((n, LANES), jnp.int32),
            jax.ShapeDtypeStruct((n, LANES), F32),
            jax.ShapeDtypeStruct((n, LANES), jnp.int32),
            jax.ShapeDtypeStruct((1, LANES), jnp.int32),
        ],
        scratch_shapes=[pltpu.VMEM((1, LANES), F32)],
        compiler_params=_cparams(("arbitrary",)),
        name="route",
    )(logits)


def _dispatch_kernel(zblk_ref, zok_ref, used_ref, pos_ref, xn_ref, xr_hbm, zero_scr, sem,
                     *, tb, pitch, n_blocks_max):
    blk_rows = MOE_TM * pitch

    def zero_copy(b):
        dst = xr_hbm.at[pl.ds(pl.multiple_of(b * blk_rows, blk_rows), blk_rows), :]
        return pltpu.make_async_copy(zero_scr, dst, sem.at[1])

    @pl.when(pl.program_id(0) == 0)
    def _():
        zero_scr[...] = jnp.zeros_like(zero_scr)

        def per_expert(action):
            def body(e, carry):
                @pl.when(zok_ref[e] == 1)
                def _():
                    action(zero_copy(zblk_ref[e]))
                return carry
            lax.fori_loop(0, N_EXPERTS, body, 0)

        def per_tail(action):
            def body(b, carry):
                action(zero_copy(b))
                return carry
            lax.fori_loop(used_ref[0], n_blocks_max, body, 0)

        per_expert(lambda c: c.start())
        per_tail(lambda c: c.start())
        per_expert(lambda c: c.wait())
        per_tail(lambda c: c.wait())

    def issue(t, carry):
        src = xn_ref.at[pl.ds(pl.multiple_of(t * pitch, pitch), pitch), :]
        for k in range(TOP_K):
            p = pos_ref[0, t * TOP_K + k]
            dst = xr_hbm.at[pl.ds(pl.multiple_of(p * pitch, pitch), pitch), :]
            pltpu.make_async_copy(src, dst, sem.at[0]).start()
        return carry

    lax.fori_loop(0, tb, issue, 0, unroll=4)
    for k in range(TOP_K):
        pltpu.make_async_copy(xn_ref, xr_hbm.at[pl.ds(0, tb * pitch), :], sem.at[0]).wait()


def _dispatch(xn_rm, pos, zblk, zok, n_used, n_blocks_max, d, tb=256):
    pitch = d // LANES
    n = xn_rm.shape[0] // pitch
    nb = n // tb
    grid_spec = pltpu.PrefetchScalarGridSpec(
        num_scalar_prefetch=3,
        grid=(nb,),
        in_specs=[
            pl.BlockSpec((None, 1, TOP_K * tb), lambda i, *_: (i, 0, 0), memory_space=pltpu.SMEM),
            pl.BlockSpec((tb * pitch, LANES), lambda i, *_: (i, 0)),
        ],
        out_specs=pl.BlockSpec(memory_space=pl.ANY),
        scratch_shapes=[pltpu.VMEM((MOE_TM * pitch, LANES), F32), pltpu.SemaphoreType.DMA((2,))],
    )
    return pl.pallas_call(
        functools.partial(_dispatch_kernel, tb=tb, pitch=pitch, n_blocks_max=n_blocks_max),
        grid_spec=grid_spec,
        out_shape=jax.ShapeDtypeStruct((n_blocks_max * MOE_TM * pitch, LANES), F32),
        compiler_params=_cparams(("arbitrary",)),
        name="dispatch",
    )(zblk, zok, n_used, pos.reshape(nb, 1, TOP_K * tb), xn_rm)


def _cast_rows(src, dst, rows_per=256):
    def body(i, carry):
        r = pl.ds(pl.multiple_of(i * rows_per, rows_per), rows_per)
        dst[r, :] = src[r, :].astype(dst.dtype)
        return carry

    lax.fori_loop(0, src.shape[0] // rows_per, body, 0)


def _stream_row_blocks(cnt, in_copy, out_copy, compute):
    @pl.when(cnt > 0)
    def _():
        in_copy(0, 0).start()

    def body(b, carry):
        slot = b % 2

        @pl.when(b + 1 < cnt)
        def _():
            in_copy(b + 1, 1 - slot).start()

        in_copy(b, slot).wait()

        @pl.when(b >= 2)
        def _():
            out_copy(b - 2, slot).wait()

        compute(slot)
        out_copy(b, slot).start()
        return carry

    lax.fori_loop(0, cnt, body, 0)

    @pl.when(cnt >= 2)
    def _():
        out_copy(cnt - 2, cnt % 2).wait()

    @pl.when(cnt >= 1)
    def _():
        out_copy(cnt - 1, (cnt - 1) % 2).wait()


def _zero_fill_blocks(first, last, zero_src, dst_copy):
    def start(b, carry):
        dst_copy(b).start()
        return carry

    def wait(b, carry):
        dst_copy(b).wait()
        return carry

    zero_src[...] = jnp.zeros_like(zero_src)
    lax.fori_loop(first, last, start, 0)
    lax.fori_loop(first, last, wait, 0)


def _moe_up_kernel(bs_ref, bc_ref, used_ref, x_hbm, w_hbm, bg_ref, bu_ref, act_hbm,
                   wf_scr, wb_scr, xbuf, lhs_scr, obuf, sem_w, sem_x, sem_o,
                   *, nt, tn, d_ff, pitch, n_blocks_max):
    g = pl.program_id(0)
    ng = pl.num_programs(0)
    e = g // nt
    col = pl.multiple_of((g % nt) * tn, tn)
    blk_rows = MOE_TM * pitch

    def w_copies(step):
        ee = step // nt
        cc = pl.multiple_of((step % nt) * tn, tn)
        return (pltpu.make_async_copy(w_hbm.at[ee, :, pl.ds(cc, tn)], wf_scr.at[0], sem_w.at[0]),
                pltpu.make_async_copy(w_hbm.at[ee, :, pl.ds(d_ff + cc, tn)], wf_scr.at[1], sem_w.at[1]))

    @pl.when(g == 0)
    def _():
        for c in w_copies(0):
            c.start()

    for half, c in enumerate(w_copies(g)):
        c.wait()
        _cast_rows(wf_scr.at[half], wb_scr.at[half])

    @pl.when(g + 1 < ng)
    def _():
        for c in w_copies(g + 1):
            c.start()

    start = bs_ref[e]

    def x_copy(b, slot):
        r0 = pl.multiple_of((start + b) * blk_rows, blk_rows)
        return pltpu.make_async_copy(x_hbm.at[pl.ds(r0, blk_rows), :], xbuf.at[slot], sem_x.at[slot])

    def o_copy(b, slot):
        r0 = pl.multiple_of((start + b) * MOE_TM, MOE_TM)
        return pltpu.make_async_copy(obuf.at[slot], act_hbm.at[pl.ds(r0, MOE_TM), pl.ds(col, tn)],
                                     sem_o.at[slot])

    def compute(slot):
        for c in range(pitch):
            lhs_scr[:, c * LANES:(c + 1) * LANES] = (
                xbuf[slot, pl.ds(c, MOE_TM, stride=pitch), :].astype(BF16))
        x = lhs_scr[...]
        gate = jnp.minimum(_dot(x, wb_scr[0]) + bg_ref[...], SWIGLU_LIMIT)
        up = jnp.clip(_dot(x, wb_scr[1]) + bu_ref[...], -SWIGLU_LIMIT, SWIGLU_LIMIT)
        obuf[slot] = (gate * jax.nn.sigmoid(SWIGLU_ALPHA * gate) * (up + 1.0)).astype(obuf.dtype)

    _stream_row_blocks(bc_ref[e], x_copy, o_copy, compute)

    @pl.when(e == N_EXPERTS - 1)
    def _():
        def tail_copy(b):
            r0 = pl.multiple_of(b * MOE_TM, MOE_TM)
            return pltpu.make_async_copy(obuf.at[0], act_hbm.at[pl.ds(r0, MOE_TM), pl.ds(col, tn)],
                                         sem_o.at[0])
        _zero_fill_blocks(used_ref[0], n_blocks_max, obuf.at[0], tail_copy)


def _moe_up(x_rows_rm, w_up, b_up, blk_start, blk_count, n_used, n_blocks_max, tn=1024):
    n_exp, d, two_ff = w_up.shape
    d_ff = two_ff // 2
    pitch = d // LANES
    nt = d_ff // tn
    up_off = d_ff // tn
    grid_spec = pltpu.PrefetchScalarGridSpec(
        num_scalar_prefetch=3,
        grid=(n_exp * nt,),
        in_specs=[
            pl.BlockSpec(memory_space=pl.ANY),
            pl.BlockSpec(memory_space=pl.ANY),
            pl.BlockSpec((None, 1, tn), lambda g, *_: (g // nt, 0, g % nt)),
            pl.BlockSpec((None, 1, tn), lambda g, *_: (g // nt, 0, g % nt + up_off)),
        ],
        out_specs=pl.BlockSpec(memory_space=pl.ANY),
        scratch_shapes=[
            pltpu.VMEM((2, d, tn), F32),
            pltpu.VMEM((2, d, tn), BF16),
            pltpu.VMEM((2, MOE_TM * pitch, LANES), F32),
            pltpu.VMEM((MOE_TM, d), BF16),
            pltpu.VMEM((2, MOE_TM, tn), BF16),
            pltpu.SemaphoreType.DMA((2,)),
            pltpu.SemaphoreType.DMA((2,)),
            pltpu.SemaphoreType.DMA((2,)),
        ],
    )
    return pl.pallas_call(
        functools.partial(_moe_up_kernel, nt=nt, tn=tn, d_ff=d_ff, pitch=pitch, n_blocks_max=n_blocks_max),
        grid_spec=grid_spec,
        out_shape=jax.ShapeDtypeStruct((n_blocks_max * MOE_TM, d_ff), BF16),
        compiler_params=_cparams(("arbitrary",)),
        name="moe_up",
    )(blk_start, blk_count, n_used, x_rows_rm, w_up, b_up, b_up)


def _moe_down_kernel(bs_ref, bc_ref, used_ref, a_hbm, w_hbm, b_ref, y_hbm,
                     wf_scr, wb_scr, abuf, obuf, sem_w, sem_a, sem_o, *, pitch, n_blocks_max):
    e = pl.program_id(0)
    blk_rows = MOE_TM * pitch

    def w_copy(ee):
        return pltpu.make_async_copy(w_hbm.at[ee], wf_scr, sem_w.at[0])

    @pl.when(e == 0)
    def _():
        w_copy(0).start()

    w_copy(e).wait()
    _cast_rows(wf_scr, wb_scr)

    @pl.when(e + 1 < pl.num_programs(0))
    def _():
        w_copy(e + 1).start()

    start = bs_ref[e]

    def a_copy(b, slot):
        r0 = pl.multiple_of((start + b) * MOE_TM, MOE_TM)
        return pltpu.make_async_copy(a_hbm.at[pl.ds(r0, MOE_TM), :], abuf.at[slot], sem_a.at[slot])

    def o_copy(b, slot):
        r0 = pl.multiple_of((start + b) * blk_rows, blk_rows)
        return pltpu.make_async_copy(obuf.at[slot], y_hbm.at[pl.ds(r0, blk_rows), :], sem_o.at[slot])

    def compute(slot):
        y = _dot(abuf[slot], wb_scr[...]) + b_ref[...]
        for c in range(pitch):
            obuf[slot, pl.ds(c, MOE_TM, stride=pitch), :] = y[:, c * LANES:(c + 1) * LANES]

    _stream_row_blocks(bc_ref[e], a_copy, o_copy, compute)

    @pl.when(e == N_EXPERTS - 1)
    def _():
        def tail_copy(b):
            r0 = pl.multiple_of(b * blk_rows, blk_rows)
            return pltpu.make_async_copy(obuf.at[0], y_hbm.at[pl.ds(r0, blk_rows), :], sem_o.at[0])
        _zero_fill_blocks(used_ref[0], n_blocks_max, obuf.at[0], tail_copy)


def _moe_down(act, w_down, b_down, blk_start, blk_count, n_used, n_blocks_max):
    n_rows, d_ff = act.shape
    n_exp, _, d = w_down.shape
    pitch = d // LANES
    grid_spec = pltpu.PrefetchScalarGridSpec(
        num_scalar_prefetch=3,
        grid=(n_exp,),
        in_specs=[
            pl.BlockSpec(memory_space=pl.ANY),
            pl.BlockSpec(memory_space=pl.ANY),
            pl.BlockSpec((None, 1, d), lambda e, *_: (e, 0, 0)),
        ],
        out_specs=pl.BlockSpec(memory_space=pl.ANY),
        scratch_shapes=[
            pltpu.VMEM((d_ff, d), F32),
            pltpu.VMEM((d_ff, d), BF16),
            pltpu.VMEM((2, MOE_TM, d_ff), BF16),
            pltpu.VMEM((2, MOE_TM * pitch, LANES), F32),
            pltpu.SemaphoreType.DMA((1,)),
            pltpu.SemaphoreType.DMA((2,)),
            pltpu.SemaphoreType.DMA((2,)),
        ],
    )
    return pl.pallas_call(
        functools.partial(_moe_down_kernel, pitch=pitch, n_blocks_max=n_blocks_max),
        grid_spec=grid_spec,
        out_shape=jax.ShapeDtypeStruct((n_rows * pitch, LANES), F32),
        compiler_params=_cparams(("arbitrary",)),
        name="moe_down",
    )(blk_start, blk_count, n_used, act, w_down, b_down)


def _combine_kernel(pos_ref, y_hbm, x2_ref, gate_ref, g_ref, o_ref, buf, sem, *, tb, pitch):
    def issue(r, carry):
        src = y_hbm.at[pl.ds(pl.multiple_of(pos_ref[0, r] * pitch, pitch), pitch), :]
        dst = buf.at[pl.ds(pl.multiple_of(r * pitch, pitch), pitch), :]
        pltpu.make_async_copy(src, dst, sem.at[0]).start()
        return carry

    lax.fori_loop(0, TOP_K * tb, issue, 0, unroll=8)
    pltpu.make_async_copy(y_hbm.at[pl.ds(0, TOP_K * tb * pitch), :], buf, sem.at[0]).wait()
    gates = gate_ref[...]
    ssq = jnp.zeros((tb, 1), F32)
    for c in range(pitch):
        cs = slice(c * LANES, (c + 1) * LANES)
        z = x2_ref[:, cs]
        for k in range(TOP_K):
            z = z + gates[:, k:k + 1] * _rowmajor_chunk(buf, c, tb, pitch, first=k * tb)
        o_ref[:, cs] = z
        ssq = ssq + jnp.sum(z * z, axis=-1, keepdims=True)
    o_ref[...] = o_ref[...] * lax.rsqrt(ssq * (1.0 / (pitch * LANES)) + EPS) * g_ref[...]


def _combine(y_rows_rm, pos, gates, x2, g_final, tb=128):
    n, d = x2.shape
    nb = n // tb
    pitch = d // LANES
    pos_blk = pos.reshape(nb, tb, TOP_K).transpose(0, 2, 1).reshape(nb, 1, TOP_K * tb)
    return pl.pallas_call(
        functools.partial(_combine_kernel, tb=tb, pitch=pitch),
        grid=(nb,),
        in_specs=[
            pl.BlockSpec((None, 1, TOP_K * tb), lambda i: (i, 0, 0), memory_space=pltpu.SMEM),
            pl.BlockSpec(memory_space=pl.ANY),
            pl.BlockSpec((tb, d), lambda i: (i, 0)),
            pl.BlockSpec((tb, LANES), lambda i: (i, 0)),
            pl.BlockSpec((1, d), lambda i: (0, 0)),
        ],
        out_specs=pl.BlockSpec((tb, d), lambda i: (i, 0)),
        out_shape=jax.ShapeDtypeStruct((n, d), F32),
        scratch_shapes=[pltpu.VMEM((TOP_K * tb * pitch, LANES), F32), pltpu.SemaphoreType.DMA((1,))],
        compiler_params=_cparams(("arbitrary",)),
        name="combine",
    )(pos_blk, y_rows_rm, x2, gates, g_final)


def _routing_tables(idx, rank, cnt):
    counts = cnt[0, :N_EXPERTS]
    blk_count = (counts + MOE_TM - 1) // MOE_TM
    blk_end = jnp.cumsum(blk_count)
    blk_start = blk_end - blk_count
    hot = idx[:, :TOP_K, None] == jnp.arange(N_EXPERTS, dtype=jnp.int32)
    pos = jnp.sum(jnp.where(hot, blk_start * MOE_TM, 0), axis=-1) + rank[:, :TOP_K]
    return pos.astype(jnp.int32), blk_start.astype(jnp.int32), blk_count.astype(jnp.int32)


def kernel(x, mem, g_attn_norm, g_mem_norm, w_in, w_gla_a2, b_gla_a, g_gla_out, b_fox_f, g_fox_out,
           w_mem_kv, g_mem_out, w_out, g_ffn_norm, w_router, b_router, w_moe_up, b_moe_up,
           w_moe_down, b_moe_down, g_final):
    batch, seq, d = x.shape
    n_mem = mem.shape[1]
    n = batch * seq
    depth = w_in.shape[0]
    assert depth == 1, "the combine kernel applies the final norm, so exactly one layer is supported"
    kw = GLA_HEADS * GLA_DK
    vw = GLA_HEADS * GLA_DV
    fw = FOX_HEADS * FOX_DH
    mw = MEM_HEADS * MEM_DH
    o_q, o_k, o_v, o_g = 0, kw, 2 * kw, 2 * kw + vw
    o_a = o_g + vw
    o_fq = o_a + GLA_LOWRANK
    o_fk, o_fv = o_fq + fw, o_fq + 2 * fw
    o_ff = o_fq + 3 * fw
    o_mq = o_ff + FOX_HEADS
    f_lane = GLA_LOWRANK

    xf = x.reshape(n, d)
    for l in range(depth):
        wi = w_in[l]
        fox_q_scale = FOX_DH ** -0.5 * LOG2E
        wa = jnp.concatenate([wi[:, o_q:o_a], wi[:, o_fq:o_fk] * fox_q_scale, wi[:, o_fk:o_ff],
                              wi[:, o_mq:o_mq + mw]], axis=1).astype(BF16)
        wb = jnp.zeros((d, LANES), F32).at[:, :GLA_LOWRANK].set(wi[:, o_a:o_fq])
        wb = wb.at[:, f_lane:f_lane + FOX_HEADS].set(wi[:, o_ff:o_mq]).astype(BF16)
        proj, small = _in_proj(xf, g_attn_norm[l].reshape(1, d), wa, wb)

        wa2p = jnp.zeros((LANES, kw), F32).at[:GLA_LOWRANK].set(w_gla_a2[l]).astype(BF16)
        bfv = jnp.zeros((1, LANES), F32).at[0, f_lane:f_lane + FOX_HEADS].set(b_fox_f[l])
        gla, f_cum = _gla(proj, small, wa2p, b_gla_a[l].reshape(1, kw), bfv,
                          g_gla_out[l].reshape(1, vw), batch, seq)

        f_t = f_cum.reshape(batch, seq, LANES)[:, :, f_lane:f_lane + FOX_HEADS].transpose(0, 2, 1)
        f_t = jnp.concatenate([f_t, jnp.zeros_like(f_t)], axis=1)
        fox = _fox(proj, f_t, g_fox_out[l].reshape(1, fw), batch, seq)

        kv = _mem_kv(mem.reshape(batch * n_mem, d), g_mem_norm[l].reshape(1, d), w_mem_kv[l].astype(BF16))
        memo = _mem_attn(proj, kv, g_mem_out[l].reshape(1, mw), batch, seq, n_mem)

        wr = jnp.zeros((d, LANES), F32).at[:, :N_EXPERTS].set(w_router[l])
        wr_hi = wr.astype(BF16)
        wr = jnp.concatenate([wr_hi, (wr - wr_hi.astype(F32)).astype(BF16)], axis=1)
        br =jnp.zeros((1, LANES), F32).at[0, :N_EXPERTS].set(b_router[l])
        x2, xn, logits = _out_proj(xf, gla, fox, memo, w_out[l].astype(BF16),
                                   g_ffn_norm[l].reshape(1, d), wr, br)

        idx, gates, rank, cnt = _route(logits)
        pos, blk_start, blk_count = _routing_tables(idx, rank, cnt)
        n_blocks_max = -(-(n * TOP_K + N_EXPERTS * (MOE_TM - 1)) // MOE_TM)
        blk_end = blk_start + blk_count
        n_used = blk_end[-1:]
        x_rows = _dispatch(xn, pos, blk_end - 1, (blk_count > 0).astype(jnp.int32), n_used,
                           n_blocks_max, d)
        d_ff = w_moe_up.shape[3] // 2
        act = _moe_up(x_rows, w_moe_up[l], b_moe_up[l].reshape(N_EXPERTS, 1, 2 * d_ff),
                      blk_start, blk_count, n_used, n_blocks_max)
        y_rows = _moe_down(act, w_moe_down[l], b_moe_down[l].reshape(N_EXPERTS, 1, d),
                           blk_start, blk_count, n_used, n_blocks_max)
        xf = _combine(y_rows, pos, gates, x2, g_final.reshape(1, d))
    return xf.reshape(batch, seq, d)
```

```python
import functools

import jax
import jax.numpy as jnp
from jax import lax
from jax.experimental import pallas as pl
from jax.experimental.pallas import tpu as pltpu

EPS = 1e-5
CHUNK = 64
GLA_HEADS = 4
GLA_DK = 128
GLA_DV = 256
GLA_LOWRANK = 16
GLA_TAU = 16.0
FOX_HEADS = 4
FOX_DH = 128
MEM_HEADS = 4
MEM_DH = 128
N_EXPERTS = 32
TOP_K = 4
SWIGLU_LIMIT = 7.0
SWIGLU_ALPHA = 1.702
LANES = 128
MOE_TM = 256
VMEM_LIMIT = 56 * 1024 * 1024

F32 = jnp.float32
BF16 = jnp.bfloat16


def _cparams(sem, vmem=VMEM_LIMIT):
    return pltpu.CompilerParams(dimension_semantics=sem, vmem_limit_bytes=vmem)


def _log_sigmoid(x):
    return jnp.minimum(x, 0.0) - jnp.log1p(jnp.exp(-jnp.abs(x)))


def _rms(x, g):
    return x * lax.rsqrt(jnp.mean(x * x, axis=-1, keepdims=True) + EPS) * g


def _dot(a, b, **kw):
    return jnp.dot(a, b, preferred_element_type=F32, **kw)


def _dot_nt(a, b):
    return lax.dot_general(a, b, (((1,), (1,)), ((), ())), preferred_element_type=F32)


def _dot_tn(a, b):
    return lax.dot_general(a, b, (((0,), (0,)), ((), ())), preferred_element_type=F32)


def _rowmajor_chunk(ref, c, rows, pitch, first=0):
    return ref[pl.ds(first * pitch + c, rows, stride=pitch), :]


def _store_rowmajor(ref, x):
    rows, d = x.shape
    pitch = d // LANES
    for c in range(pitch):
        ref[pl.ds(c, rows, stride=pitch), :] = x[:, c * LANES:(c + 1) * LANES]


def _in_proj_kernel(x_ref, g_ref, wa_ref, wb_ref, proj_ref, small_ref, h_scr):
    @pl.when(pl.program_id(1) == 0)
    def _():
        hb = _rms(x_ref[...], g_ref[...]).astype(BF16)
        h_scr[...] = hb
        small_ref[...] = _dot(hb, wb_ref[...])

    proj_ref[...] = _dot(h_scr[...], wa_ref[...]).astype(proj_ref.dtype)


def _in_proj(x2d, g, wa, wb, tm=1024, tn=1280):
    n, d = x2d.shape
    na = wa.shape[1]
    return pl.pallas_call(
        _in_proj_kernel,
        grid=(n // tm, na // tn),
        in_specs=[
            pl.BlockSpec((tm, d), lambda i, j: (i, 0)),
            pl.BlockSpec((1, d), lambda i, j: (0, 0)),
            pl.BlockSpec((d, tn), lambda i, j: (0, j)),
            pl.BlockSpec((d, LANES), lambda i, j: (0, 0)),
        ],
        out_specs=[
            pl.BlockSpec((tm, tn), lambda i, j: (i, j)),
            pl.BlockSpec((tm, LANES), lambda i, j: (i, 0)),
        ],
        out_shape=[
            jax.ShapeDtypeStruct((n, na), BF16),
            jax.ShapeDtypeStruct((n, LANES), F32),
        ],
        scratch_shapes=[pltpu.VMEM((tm, d), BF16)],
        compiler_params=_cparams(("parallel", "arbitrary")),
        name="in_proj",
    )(x2d, g, wa, wb)


def _gla_kernel(q_ref, k_ref, v_ref, gate_ref, small_ref, wa2_ref, ba_ref, bf_ref, gout_ref,
                o_ref, f_ref, state_scr, fcar_scr, la_scr, lf_scr, *, n_chunks):
    @pl.when(pl.program_id(1) == 0)
    def _():
        state_scr[...] = jnp.zeros_like(state_scr)
        fcar_scr[...] = jnp.zeros_like(fcar_scr)

    small = small_ref[...]
    la_scr[...] = _log_sigmoid(_dot(small.astype(BF16), wa2_ref[...]) + ba_ref[...]) * (1.0 / GLA_TAU)
    lf_scr[...] = _log_sigmoid(small + bf_ref[...])
    row = lax.broadcasted_iota(jnp.int32, (CHUNK, CHUNK), 0)
    col = lax.broadcasted_iota(jnp.int32, (CHUNK, CHUNK), 1)
    tri = (col <= row).astype(F32)
    scale = GLA_DK ** -0.5

    def chunk_body(c, carry):
        r = pl.ds(pl.multiple_of(c * CHUNK, CHUNK), CHUNK)
        b = _dot(tri, la_scr[r, :], precision=lax.Precision.HIGHEST)
        b_end = b[CHUNK - 1:CHUNK, :]
        k_dec = k_ref[r, :].astype(F32) * jnp.exp(b_end - b)
        decay = jnp.exp(b_end)
        f_cum = _dot(tri, lf_scr[r, :], precision=lax.Precision.HIGHEST) + fcar_scr[...]
        f_ref[r, :] = f_cum
        fcar_scr[...] = f_cum[CHUNK - 1:CHUNK, :]
        for h in range(GLA_HEADS):
            ks = slice(h * GLA_DK, (h + 1) * GLA_DK)
            vs = slice(h * GLA_DV, (h + 1) * GLA_DV)
            st = state_scr[h] * decay[:, ks] + _dot_tn(v_ref[r, vs], k_dec[:, ks].astype(BF16))
            state_scr[h] = st
            o = _dot_nt(q_ref[r, ks], st.astype(BF16)) * scale
            gt = gate_ref[r, vs].astype(F32)
            o_ref[r, vs] = (_rms(o, gout_ref[:, vs]) * (gt * jax.nn.sigmoid(gt))).astype(o_ref.dtype)
        return carry

    lax.fori_loop(0, n_chunks, chunk_body, 0)


def _gla(proj, small, wa2p, ba, bfv, gout, batch, seq, ts=512):
    n = proj.shape[0]
    nsb = seq // ts
    kw = GLA_HEADS * GLA_DK
    vw = GLA_HEADS * GLA_DV
    row = lambda b, s: b * nsb + s
    return pl.pallas_call(
        functools.partial(_gla_kernel, n_chunks=ts // CHUNK),
        grid=(batch, nsb),
        in_specs=[
            pl.BlockSpec((ts, kw), lambda b, s: (row(b, s), 0)),
            pl.BlockSpec((ts, kw), lambda b, s: (row(b, s), 1)),
            pl.BlockSpec((ts, vw), lambda b, s: (row(b, s), 1)),
            pl.BlockSpec((ts, vw), lambda b, s: (row(b, s), 2)),
            pl.BlockSpec((ts, LANES), lambda b, s: (row(b, s), 0)),
            pl.BlockSpec((LANES, kw), lambda b, s: (0, 0)),
            pl.BlockSpec((1, kw), lambda b, s: (0, 0)),
            pl.BlockSpec((1, LANES), lambda b, s: (0, 0)),
            pl.BlockSpec((1, vw), lambda b, s: (0, 0)),
        ],
        out_specs=[
            pl.BlockSpec((ts, vw), lambda b, s: (row(b, s), 0)),
            pl.BlockSpec((ts, LANES), lambda b, s: (row(b, s), 0)),
        ],
        out_shape=[
            jax.ShapeDtypeStruct((n, vw), BF16),
            jax.ShapeDtypeStruct((n, LANES), F32),
        ],
        scratch_shapes=[
            pltpu.VMEM((GLA_HEADS, GLA_DV, GLA_DK), F32),
            pltpu.VMEM((1, LANES), F32),
            pltpu.VMEM((ts, kw), F32),
            pltpu.VMEM((ts, LANES), F32),
        ],
        compiler_params=_cparams(("parallel", "arbitrary")),
        name="gla",
    )(proj, proj, proj, proj, small, wa2p, ba, bfv, gout)


LOG2E = 1.4426950408889634
def _fox_kernel(q_ref, k_ref, v_ref, fk_ref, g_ref, o_ref, m_scr, l_scr, acc_scr, *, tq, tk):
    i = pl.program_id(1)
    j = pl.program_id(2)

    @pl.when(j == 0)
    def _():
        m_scr[...] = jnp.full_like(m_scr, -jnp.inf)
        l_scr[...] = jnp.zeros_like(l_scr)
        acc_scr[...] = jnp.zeros_like(acc_scr)

    def update(on_diagonal):
        for h in range(FOX_HEADS):
            hs = slice(h * FOX_DH, (h + 1) * FOX_DH)
            s = _dot_nt(q_ref[:, hs], k_ref[:, hs]) - fk_ref[h:h + 1, :] * LOG2E
            if on_diagonal:
                row = lax.broadcasted_iota(jnp.int32, (tq, tk), 0)
                col = lax.broadcasted_iota(jnp.int32, (tq, tk), 1)
                s = jnp.where(col <= row, s, -jnp.inf)
            m_prev = m_scr[h]
            m_new = jnp.maximum(m_prev, jnp.max(s, axis=-1, keepdims=True))
            alpha = jnp.exp2(m_prev - m_new)
            p = jnp.exp2(s - m_new)
            l_scr[h] = alpha * l_scr[h] + jnp.sum(p, axis=-1, keepdims=True)
            acc_scr[h] = alpha * acc_scr[h] + _dot(p.astype(BF16), v_ref[:, hs])
            m_scr[h] = m_new

    @pl.when(j < i)
    def _():
        update(False)

    @pl.when(j == i)
    def _():
        update(True)
        for h in range(FOX_HEADS):
            hs = slice(h * FOX_DH, (h + 1) * FOX_DH)
            o = acc_scr[h] / l_scr[h]
            o_ref[:, hs] = _rms(o, g_ref[:, hs]).astype(o_ref.dtype)


def _fox(proj, f_t, g_fox, batch, seq, tq=512, tk=512):
    assert tq == tk, "the diagonal-block mask assumes square blocks"
    n = proj.shape[0]
    w = FOX_HEADS * FOX_DH
    nq, nk = seq // tq, seq // tk
    qcol, kcol, vcol = 3072 // w, 3584 // w, 4096 // w
    return pl.pallas_call(
        functools.partial(_fox_kernel, tq=tq, tk=tk),
        grid=(batch, nq, nk),
        in_specs=[
            pl.BlockSpec((tq, w), lambda b, i, j: (b * nq + i, qcol)),
            pl.BlockSpec((tk, w), lambda b, i, j: (b * nk + jnp.minimum(j, i), kcol)),
            pl.BlockSpec((tk, w), lambda b, i, j: (b * nk + jnp.minimum(j, i), vcol)),
            pl.BlockSpec((None, 8, tk), lambda b, i, j: (b, 0, jnp.minimum(j, i))),
            pl.BlockSpec((1, w), lambda b, i, j: (0, 0)),
        ],
        out_specs=pl.BlockSpec((tq, w), lambda b, i, j: (b * nq + i, 0)),
        out_shape=jax.ShapeDtypeStruct((n, w), BF16),
        scratch_shapes=[
            pltpu.VMEM((FOX_HEADS, tq, 1), F32),
            pltpu.VMEM((FOX_HEADS, tq, 1), F32),
            pltpu.VMEM((FOX_HEADS, tq, FOX_DH), F32),
        ],
        compiler_params=_cparams(("parallel", "parallel", "arbitrary")),
        name="fox",
    )(proj, proj, proj, f_t, g_fox)


def _mem_kv_kernel(m_ref, g_ref, w_ref, kv_ref):
    kv_ref[...] = _dot(_rms(m_ref[...], g_ref[...]).astype(BF16), w_ref[...]).astype(kv_ref.dtype)


def _mem_kv(mem2d, g, w, tm=256):
    n, d = mem2d.shape
    nw = w.shape[1]
    return pl.pallas_call(
        _mem_kv_kernel,
        grid=(n // tm,),
        in_specs=[
            pl.BlockSpec((tm, d), lambda i: (i, 0)),
            pl.BlockSpec((1, d), lambda i: (0, 0)),
            pl.BlockSpec((d, nw), lambda i: (0, 0)),
        ],
        out_specs=pl.BlockSpec((tm, nw), lambda i: (i, 0)),
        out_shape=jax.ShapeDtypeStruct((n, nw), BF16),
        compiler_params=_cparams(("parallel",)),
        name="mem_kv",
    )(mem2d, g, w)


def _mem_attn_kernel(q_ref, k_ref, v_ref, g_ref, o_ref):
    scale = MEM_DH ** -0.5
    for h in range(MEM_HEADS):
        hs = slice(h * MEM_DH, (h + 1) * MEM_DH)
        s = _dot_nt(q_ref[:, hs], k_ref[:, hs]) * scale
        p = jnp.exp(s - jnp.max(s, axis=-1, keepdims=True))
        l = jnp.sum(p, axis=-1, keepdims=True)
        o = _dot((p / l).astype(BF16), v_ref[:, hs])
        o_ref[:, hs] = _rms(o, g_ref[:, hs]).astype(o_ref.dtype)


def _mem_attn(proj, kv, g_mem_out, batch, seq, n_mem, tq=1024):
    n = proj.shape[0]
    w = MEM_HEADS * MEM_DH
    nq = seq // tq
    qcol = 4608 // w
    return pl.pallas_call(
        _mem_attn_kernel,
        grid=(n // tq,),
        in_specs=[
            pl.BlockSpec((tq, w), lambda i: (i, qcol)),
            pl.BlockSpec((n_mem, w), lambda i: (i // nq, 0)),
            pl.BlockSpec((n_mem, w), lambda i: (i // nq, 1)),
            pl.BlockSpec((1, w), lambda i: (0, 0)),
        ],
        out_specs=pl.BlockSpec((tq, w), lambda i: (i, 0)),
        out_shape=jax.ShapeDtypeStruct((n, w), BF16),
        compiler_params=_cparams(("parallel",)),
        name="mem_attn",
    )(proj, kv, kv, g_mem_out)


def _out_proj_kernel(x_ref, gla_ref, fox_ref, mem_ref, w1_ref, w2_ref, w3_ref, g_ref, wr_ref, br_ref,
                     x2_ref, xn_ref, logit_ref):
    x2 = (x_ref[...] + _dot(gla_ref[...], w1_ref[...]) + _dot(fox_ref[...], w2_ref[...])
          + _dot(mem_ref[...], w3_ref[...]))
    x2_ref[...] = x2
    xn = _rms(x2, g_ref[...])
    _store_rowmajor(xn_ref, xn)
    xh = xn.astype(BF16)
    xl = (xn - xh.astype(F32)).astype(BF16)
    hi = _dot(xh, wr_ref[...])
    logit_ref[...] = hi[:, :LANES] + hi[:, LANES:] + _dot(xl, wr_ref[:, :LANES]) + br_ref[...]


def _out_proj(x2d, gla, fox, memo, w_out, g_ffn, wr, br, tm=256):
    n, d = x2d.shape
    w1, w2 = gla.shape[1], fox.shape[1]
    const = lambda i: (0, 0)
    return pl.pallas_call(
        _out_proj_kernel,
        grid=(n // tm,),
        in_specs=[
            pl.BlockSpec((tm, d), lambda i: (i, 0)),
            pl.BlockSpec((tm, w1), lambda i: (i, 0)),
            pl.BlockSpec((tm, w2), lambda i: (i, 0)),
            pl.BlockSpec((tm, w2), lambda i: (i, 0)),
            pl.BlockSpec((w1, d), lambda i: (0, 0)),
            pl.BlockSpec((w2, d), lambda i: (w1 // w2, 0)),
            pl.BlockSpec((w2, d), lambda i: (w1 // w2 + 1, 0)),
            pl.BlockSpec((1, d), const),
            pl.BlockSpec((d, 2 * LANES), const),
            pl.BlockSpec((1, LANES), const),
        ],
        out_specs=[
            pl.BlockSpec((tm, d), lambda i: (i, 0)),
            pl.BlockSpec((tm * (d // LANES), LANES), lambda i: (i, 0)),
            pl.BlockSpec((tm, LANES), lambda i: (i, 0)),
        ],
        out_shape=[
            jax.ShapeDtypeStruct((n, d), F32),
            jax.ShapeDtypeStruct((n * (d // LANES), LANES), F32),
            jax.ShapeDtypeStruct((n, LANES), F32),
        ],
        compiler_params=_cparams(("parallel",)),
        name="out_proj",
    )(x2d, gla, fox, memo, w_out, w_out, w_out, g_ffn, wr, br)


def _route_kernel(l_ref, idx_ref, gate_ref, rank_ref, cnt_ref, carry_scr, *, tb):
    @pl.when(pl.program_id(0) == 0)
    def _():
        carry_scr[...] = jnp.zeros_like(carry_scr)

    lane = lax.broadcasted_iota(jnp.int32, (tb, LANES), 1)
    logit = jnp.where(lane < N_EXPERTS, l_ref[...], -jnp.inf)
    vals, hots = [], []
    idx_out = jnp.zeros((tb, LANES), jnp.int32)
    for k in range(TOP_K):
        m = jnp.max(logit, axis=-1, keepdims=True)
        ik = jnp.min(jnp.where(logit == m, lane, LANES), axis=-1, keepdims=True)
        hot = lane == ik
        logit = jnp.where(hot, -jnp.inf, logit)
        vals.append(m)
        hots.append(hot)
        idx_out = jnp.where(lane == k, ik, idx_out)
    idx_ref[...] = idx_out

    e = [jnp.exp(v - vals[0]) for v in vals]
    den = e[0] + e[1] + e[2] + e[3]
    gate_out = jnp.zeros((tb, LANES), F32)
    for k in range(TOP_K):
        gate_out = jnp.where(lane == k, e[k] / den, gate_out)
    gate_ref[...] = gate_out

    member = jnp.zeros((tb, LANES), F32)
    for hot in hots:
        member = member + hot.astype(F32)
    row = lax.broadcasted_iota(jnp.int32, (tb, tb), 0)
    col = lax.broadcasted_iota(jnp.int32, (tb, tb), 1)
    before = (col < row).astype(BF16)
    rank = _dot(before, member.astype(BF16)) + carry_scr[...]
    rank_out = jnp.zeros((tb, LANES), F32)
    for k in range(TOP_K):
        rk = jnp.sum(jnp.where(hots[k], rank, 0.0), axis=-1, keepdims=True)
        rank_out = jnp.where(lane == k, rk, rank_out)
    rank_ref[...] = rank_out.astype(jnp.int32)
    total = carry_scr[...] + jnp.sum(member, axis=0, keepdims=True)
    carry_scr[...] = total
    cnt_ref[...] = total.astype(jnp.int32)


def _route(logits, tb=512):
    n = logits.shape[0]
    blk = pl.BlockSpec((tb, LANES), lambda i: (i, 0))
    return pl.pallas_call(
        functools.partial(_route_kernel, tb=tb),
        grid=(n // tb,),
        in_specs=[blk],
        out_specs=[blk, blk, blk, pl.BlockSpec((1, LANES), lambda i: (0, 0))],
        out_shape=[
            jax.ShapeDtypeStruct((n, LANES), jnp.int32),
            jax.ShapeDtypeStruct((n, LANES), F32),
            jax.ShapeDtypeStruct((n, LANES), jnp.int32),
            jax.ShapeDtypeStruct((1, LANES), jnp.int32),
        ],
        scratch_shapes=[pltpu.VMEM((1, LANES), F32)],
        compiler_params=_cparams(("arbitrary",)),
        name="route",
    )(logits)


def _dispatch_kernel(zblk_ref, zok_ref, used_ref, pos_ref, xn_ref, xr_hbm, zero_scr, sem,
                     *, tb, pitch, n_blocks_max):
    blk_rows = MOE_TM * pitch

    def zero_copy(b):
        dst = xr_hbm.at[pl.ds(pl.multiple_of(b * blk_rows, blk_rows), blk_rows), :]
        return pltpu.make_async_copy(zero_scr, dst, sem.at[1])

    @pl.when(pl.program_id(0) == 0)
    def _():
        zero_scr[...] = jnp.zeros_like(zero_scr)

        def per_expert(action):
            def body(e, carry):
                @pl.when(zok_ref[e] == 1)
                def _():
                    action(zero_copy(zblk_ref[e]))
                return carry
            lax.fori_loop(0, N_EXPERTS, body, 0)

        def per_tail(action):
            def body(b, carry):
                action(zero_copy(b))
                return carry
            lax.fori_loop(used_ref[0], n_blocks_max, body, 0)

        per_expert(lambda c: c.start())
        per_tail(lambda c: c.start())
        per_expert(lambda c: c.wait())
        per_tail(lambda c: c.wait())

    def issue(t, carry):
        src = xn_ref.at[pl.ds(pl.multiple_of(t * pitch, pitch), pitch), :]
        for k in range(TOP_K):
            p = pos_ref[0, t * TOP_K + k]
            dst = xr_hbm.at[pl.ds(pl.multiple_of(p * pitch, pitch), pitch), :]
            pltpu.make_async_copy(src, dst, sem.at[0]).start(priority=k % 2)
        return carry

    lax.fori_loop(0, tb, issue, 0, unroll=4)
    for k in range(TOP_K):
        pltpu.make_async_copy(xn_ref, xr_hbm.at[pl.ds(0, tb * pitch), :], sem.at[0]).wait()


def _dispatch(xn_rm, pos, zblk, zok, n_used, n_blocks_max, d, tb=256):
    pitch = d // LANES
    n = xn_rm.shape[0] // pitch
    nb = n // tb
    grid_spec = pltpu.PrefetchScalarGridSpec(
        num_scalar_prefetch=3,
        grid=(nb,),
        in_specs=[
            pl.BlockSpec((None, 1, TOP_K * tb), lambda i, *_: (i, 0, 0), memory_space=pltpu.SMEM),
            pl.BlockSpec((tb * pitch, LANES), lambda i, *_: (i, 0)),
        ],
        out_specs=pl.BlockSpec(memory_space=pl.ANY),
        scratch_shapes=[pltpu.VMEM((MOE_TM * pitch, LANES), F32), pltpu.SemaphoreType.DMA((2,))],
    )
    return pl.pallas_call(
        functools.partial(_dispatch_kernel, tb=tb, pitch=pitch, n_blocks_max=n_blocks_max),
        grid_spec=grid_spec,
        out_shape=jax.ShapeDtypeStruct((n_blocks_max * MOE_TM * pitch, LANES), F32),
        compiler_params=_cparams(("arbitrary",)),
        name="dispatch",
    )(zblk, zok, n_used, pos.reshape(nb, 1, TOP_K * tb), xn_rm)


def _cast_rows(src, dst, rows_per=256):
    def body(i, carry):
        r = pl.ds(pl.multiple_of(i * rows_per, rows_per), rows_per)
        dst[r, :] = src[r, :].astype(dst.dtype)
        return carry

    lax.fori_loop(0, src.shape[0] // rows_per, body, 0)


def _stream_row_blocks(cnt, in_copy, out_copy, compute):
    @pl.when(cnt > 0)
    def _():
        in_copy(0, 0).start()

    def body(b, carry):
        slot = b % 2

        @pl.when(b + 1 < cnt)
        def _():
            in_copy(b + 1, 1 - slot).start()

        in_copy(b, slot).wait()

        @pl.when(b >= 2)
        def _():
            out_copy(b - 2, slot).wait()

        compute(slot)
        out_copy(b, slot).start()
        return carry

    lax.fori_loop(0, cnt, body, 0)

    @pl.when(cnt >= 2)
    def _():
        out_copy(cnt - 2, cnt % 2).wait()

    @pl.when(cnt >= 1)
    def _():
        out_copy(cnt - 1, (cnt - 1) % 2).wait()


def _zero_fill_blocks(first, last, zero_src, dst_copy):
    def start(b, carry):
        dst_copy(b).start()
        return carry

    def wait(b, carry):
        dst_copy(b).wait()
        return carry

    zero_src[...] = jnp.zeros_like(zero_src)
    lax.fori_loop(first, last, start, 0)
    lax.fori_loop(first, last, wait, 0)


def _moe_up_kernel(bs_ref, bc_ref, used_ref, x_hbm, w_hbm, bg_ref, bu_ref, act_hbm,
                   wf_scr, wb_scr, xbuf, lhs_scr, obuf, sem_w, sem_x, sem_o,
                   *, nt, tn, d_ff, pitch, n_blocks_max):
    g = pl.program_id(0)
    ng = pl.num_programs(0)
    e = g // nt
    col = pl.multiple_of((g % nt) * tn, tn)
    blk_rows = MOE_TM * pitch

    def w_copies(step):
        ee = step // nt
        cc = pl.multiple_of((step % nt) * tn, tn)
        return (pltpu.make_async_copy(w_hbm.at[ee, :, pl.ds(cc, tn)], wf_scr.at[0], sem_w.at[0]),
                pltpu.make_async_copy(w_hbm.at[ee, :, pl.ds(d_ff + cc, tn)], wf_scr.at[1], sem_w.at[1]))

    @pl.when(g == 0)
    def _():
        for c in w_copies(0):
            c.start(priority=1)

    for half, c in enumerate(w_copies(g)):
        c.wait()
        _cast_rows(wf_scr.at[half], wb_scr.at[half])

    @pl.when(g + 1 < ng)
    def _():
        for c in w_copies(g + 1):
            c.start(priority=1)

    start = bs_ref[e]

    def x_copy(b, slot):
        r0 = pl.multiple_of((start + b) * blk_rows, blk_rows)
        return pltpu.make_async_copy(x_hbm.at[pl.ds(r0, blk_rows), :], xbuf.at[slot], sem_x.at[slot])

    def o_copy(b, slot):
        r0 = pl.multiple_of((start + b) * MOE_TM, MOE_TM)
        return pltpu.make_async_copy(obuf.at[slot], act_hbm.at[pl.ds(r0, MOE_TM), pl.ds(col, tn)],
                                     sem_o.at[slot])

    def compute(slot):
        for c in range(pitch):
            lhs_scr[:, c * LANES:(c + 1) * LANES] = (
                xbuf[slot, pl.ds(c, MOE_TM, stride=pitch), :].astype(BF16))
        x = lhs_scr[...]
        gate = jnp.minimum(_dot(x, wb_scr[0]) + bg_ref[...], SWIGLU_LIMIT)
        up = jnp.clip(_dot(x, wb_scr[1]) + bu_ref[...], -SWIGLU_LIMIT, SWIGLU_LIMIT)
        obuf[slot] = (gate * jax.nn.sigmoid(SWIGLU_ALPHA * gate) * (up + 1.0)).astype(obuf.dtype)

    _stream_row_blocks(bc_ref[e], x_copy, o_copy, compute)

    @pl.when(e == N_EXPERTS - 1)
    def _():
        def tail_copy(b):
            r0 = pl.multiple_of(b * MOE_TM, MOE_TM)
            return pltpu.make_async_copy(obuf.at[0], act_hbm.at[pl.ds(r0, MOE_TM), pl.ds(col, tn)],
                                         sem_o.at[0])
        _zero_fill_blocks(used_ref[0], n_blocks_max, obuf.at[0], tail_copy)


def _moe_up(x_rows_rm, w_up, b_up, blk_start, blk_count, n_used, n_blocks_max, tn=1024):
    n_exp, d, two_ff = w_up.shape
    d_ff = two_ff // 2
    pitch = d // LANES
    nt = d_ff // tn
    up_off = d_ff // tn
    grid_spec = pltpu.PrefetchScalarGridSpec(
        num_scalar_prefetch=3,
        grid=(n_exp * nt,),
        in_specs=[
            pl.BlockSpec(memory_space=pl.ANY),
            pl.BlockSpec(memory_space=pl.ANY),
            pl.BlockSpec((None, 1, tn), lambda g, *_: (g // nt, 0, g % nt)),
            pl.BlockSpec((None, 1, tn), lambda g, *_: (g // nt, 0, g % nt + up_off)),
        ],
        out_specs=pl.BlockSpec(memory_space=pl.ANY),
        scratch_shapes=[
            pltpu.VMEM((2, d, tn), F32),
            pltpu.VMEM((2, d, tn), BF16),
            pltpu.VMEM((2, MOE_TM * pitch, LANES), F32),
            pltpu.VMEM((MOE_TM, d), BF16),
            pltpu.VMEM((2, MOE_TM, tn), BF16),
            pltpu.SemaphoreType.DMA((2,)),
            pltpu.SemaphoreType.DMA((2,)),
            pltpu.SemaphoreType.DMA((2,)),
        ],
    )
    return pl.pallas_call(
        functools.partial(_moe_up_kernel, nt=nt, tn=tn, d_ff=d_ff, pitch=pitch, n_blocks_max=n_blocks_max),
        grid_spec=grid_spec,
        out_shape=jax.ShapeDtypeStruct((n_blocks_max * MOE_TM, d_ff), BF16),
        compiler_params=_cparams(("arbitrary",)),
        name="moe_up",
    )(blk_start, blk_count, n_used, x_rows_rm, w_up, b_up, b_up)


def _moe_down_kernel(bs_ref, bc_ref, used_ref, a_hbm, w_hbm, b_ref, y_hbm,
                     wf_scr, wb_scr, abuf, obuf, sem_w, sem_a, sem_o, *, pitch, n_blocks_max):
    e = pl.program_id(0)
    blk_rows = MOE_TM * pitch

    def w_copy(ee):
        return pltpu.make_async_copy(w_hbm.at[ee], wf_scr, sem_w.at[0])

    @pl.when(e == 0)
    def _():
        w_copy(0).start(priority=1)

    w_copy(e).wait()
    _cast_rows(wf_scr, wb_scr)

    @pl.when(e + 1 < pl.num_programs(0))
    def _():
        w_copy(e + 1).start(priority=1)

    start = bs_ref[e]

    def a_copy(b, slot):
        r0 = pl.multiple_of((start + b) * MOE_TM, MOE_TM)
        return pltpu.make_async_copy(a_hbm.at[pl.ds(r0, MOE_TM), :], abuf.at[slot], sem_a.at[slot])

    def o_copy(b, slot):
        r0 = pl.multiple_of((start + b) * blk_rows, blk_rows)
        return pltpu.make_async_copy(obuf.at[slot], y_hbm.at[pl.ds(r0, blk_rows), :], sem_o.at[slot])

    def compute(slot):
        y = _dot(abuf[slot], wb_scr[...]) + b_ref[...]
        for c in range(pitch):
            obuf[slot, pl.ds(c, MOE_TM, stride=pitch), :] = y[:, c * LANES:(c + 1) * LANES]

    _stream_row_blocks(bc_ref[e], a_copy, o_copy, compute)

    @pl.when(e == N_EXPERTS - 1)
    def _():
        def tail_copy(b):
            r0 = pl.multiple_of(b * blk_rows, blk_rows)
            return pltpu.make_async_copy(obuf.at[0], y_hbm.at[pl.ds(r0, blk_rows), :], sem_o.at[0])
        _zero_fill_blocks(used_ref[0], n_blocks_max, obuf.at[0], tail_copy)


def _moe_down(act, w_down, b_down, blk_start, blk_count, n_used, n_blocks_max):
    n_rows, d_ff = act.shape
    n_exp, _, d = w_down.shape
    pitch = d // LANES
    grid_spec = pltpu.PrefetchScalarGridSpec(
        num_scalar_prefetch=3,
        grid=(n_exp,),
        in_specs=[
            pl.BlockSpec(memory_space=pl.ANY),
            pl.BlockSpec(memory_space=pl.ANY),
            pl.BlockSpec((None, 1, d), lambda e, *_: (e, 0, 0)),
        ],
        out_specs=pl.BlockSpec(memory_space=pl.ANY),
        scratch_shapes=[
            pltpu.VMEM((d_ff, d), F32),
            pltpu.VMEM((d_ff, d), BF16),
            pltpu.VMEM((2, MOE_TM, d_ff), BF16),
            pltpu.VMEM((2, MOE_TM * pitch, LANES), F32),
            pltpu.SemaphoreType.DMA((1,)),
            pltpu.SemaphoreType.DMA((2,)),
            pltpu.SemaphoreType.DMA((2,)),
        ],
    )
    return pl.pallas_call(
        functools.partial(_moe_down_kernel, pitch=pitch, n_blocks_max=n_blocks_max),
        grid_spec=grid_spec,
        out_shape=jax.ShapeDtypeStruct((n_rows * pitch, LANES), F32),
        compiler_params=_cparams(("arbitrary",)),
        name="moe_down",
    )(blk_start, blk_count, n_used, act, w_down, b_down)


COMBINE_PITCH = 24


def _combine_kernel(pos0_ref, posn_ref, y_hbm, x2_ref, gate_ref, g_ref, o_ref, buf, sem, *, tb, pitch):
    i = pl.program_id(0)
    n_rows = TOP_K * tb

    def issue_block(p_ref, slot):
        def issue(j, carry):
            for half in range(2):
                r = 2 * j + half
                src = y_hbm.at[pl.ds(pl.multiple_of(p_ref[0, r] * pitch, pitch), pitch), :]
                d0 = pl.multiple_of((slot * n_rows + r) * COMBINE_PITCH, 8)
                pltpu.make_async_copy(src, buf.at[pl.ds(d0, pitch), :], sem.at[slot]).start(priority=half)
            return carry
        lax.fori_loop(0, n_rows // 2, issue, 0, unroll=4)

    @pl.when(i == 0)
    def _():
        issue_block(pos0_ref, 0)

    @pl.when(i + 1 < pl.num_programs(0))
    def _():
        issue_block(posn_ref, (i + 1) % 2)

    slot = i % 2
    base = pl.multiple_of(slot * (n_rows * COMBINE_PITCH), 8)
    pltpu.make_async_copy(y_hbm.at[pl.ds(0, n_rows * pitch), :], buf.at[pl.ds(base, n_rows * pitch), :],
                          sem.at[slot]).wait()
    gates = gate_ref[...]
    ssq = jnp.zeros((tb, 1), F32)
    for c in range(pitch):
        cs = slice(c * LANES, (c + 1) * LANES)
        z = x2_ref[:, cs]
        for k in range(TOP_K):
            rows = buf[pl.ds(base + k * tb * COMBINE_PITCH + c, tb, stride=COMBINE_PITCH), :]
            z = z + gates[:, k:k + 1] * rows
        o_ref[:, cs] = z
        ssq = ssq + jnp.sum(z * z, axis=-1, keepdims=True)
    o_ref[...] = o_ref[...] * lax.rsqrt(ssq * (1.0 / (pitch * LANES)) + EPS) * g_ref[...]


def _combine(y_rows_rm, pos, gates, x2, g_final, tb=128):
    n, d = x2.shape
    nb = n // tb
    pitch = d // LANES
    pos_blk = pos.reshape(nb, tb, TOP_K).transpose(0, 2, 1).reshape(nb, 1, TOP_K * tb)
    pos_spec = lambda imap: pl.BlockSpec((None, 1, TOP_K * tb), imap, memory_space=pltpu.SMEM)
    return pl.pallas_call(
        functools.partial(_combine_kernel, tb=tb, pitch=pitch),
        grid=(nb,),
        in_specs=[
            pos_spec(lambda i: (0, 0, 0)),
            pos_spec(lambda i: (jnp.minimum(i + 1, nb - 1), 0, 0)),
            pl.BlockSpec(memory_space=pl.ANY),
            pl.BlockSpec((tb, d), lambda i: (i, 0)),
            pl.BlockSpec((tb, LANES), lambda i: (i, 0)),
            pl.BlockSpec((1, d), lambda i: (0, 0)),
        ],
        out_specs=pl.BlockSpec((tb, d), lambda i: (i, 0)),
        out_shape=jax.ShapeDtypeStruct((n, d), F32),
        scratch_shapes=[pltpu.VMEM((2 * TOP_K * tb * COMBINE_PITCH, LANES), F32),
                        pltpu.SemaphoreType.DMA((2,))],
        compiler_params=_cparams(("arbitrary",)),
        name="combine",
    )(pos_blk, pos_blk, y_rows_rm, x2, gates, g_final)


def _routing_tables(idx, rank, cnt):
    counts = cnt[0, :N_EXPERTS]
    blk_count = (counts + MOE_TM - 1) // MOE_TM
    blk_end = jnp.cumsum(blk_count)
    blk_start = blk_end - blk_count
    hot = idx[:, :TOP_K, None] == jnp.arange(N_EXPERTS, dtype=jnp.int32)
    pos = jnp.sum(jnp.where(hot, blk_start * MOE_TM, 0), axis=-1) + rank[:, :TOP_K]
    return pos.astype(jnp.int32), blk_start.astype(jnp.int32), blk_count.astype(jnp.int32)


def kernel(x, mem, g_attn_norm, g_mem_norm, w_in, w_gla_a2, b_gla_a, g_gla_out, b_fox_f, g_fox_out,
           w_mem_kv, g_mem_out, w_out, g_ffn_norm, w_router, b_router, w_moe_up, b_moe_up,
           w_moe_down, b_moe_down, g_final):
    batch, seq, d = x.shape
    n_mem = mem.shape[1]
    n = batch * seq
    depth = w_in.shape[0]
    assert depth == 1, "the combine kernel applies the final norm, so exactly one layer is supported"
    kw = GLA_HEADS * GLA_DK
    vw = GLA_HEADS * GLA_DV
    fw = FOX_HEADS * FOX_DH
    mw = MEM_HEADS * MEM_DH
    o_q, o_k, o_v, o_g = 0, kw, 2 * kw, 2 * kw + vw
    o_a = o_g + vw
    o_fq = o_a + GLA_LOWRANK
    o_fk, o_fv = o_fq + fw, o_fq + 2 * fw
    o_ff = o_fq + 3 * fw
    o_mq = o_ff + FOX_HEADS
    f_lane = GLA_LOWRANK

    xf = x.reshape(n, d)
    for l in range(depth):
        wi = w_in[l]
        fox_q_scale = FOX_DH ** -0.5 * LOG2E
        wa = jnp.concatenate([wi[:, o_q:o_a], wi[:, o_fq:o_fk] * fox_q_scale, wi[:, o_fk:o_ff],
                              wi[:, o_mq:o_mq + mw]], axis=1).astype(BF16)
        wb = jnp.zeros((d, LANES), F32).at[:, :GLA_LOWRANK].set(wi[:, o_a:o_fq])
        wb = wb.at[:, f_lane:f_lane + FOX_HEADS].set(wi[:, o_ff:o_mq]).astype(BF16)
        proj, small = _in_proj(xf, g_attn_norm[l].reshape(1, d), wa, wb)

        wa2p = jnp.zeros((LANES, kw), F32).at[:GLA_LOWRANK].set(w_gla_a2[l]).astype(BF16)
        bfv = jnp.zeros((1, LANES), F32).at[0, f_lane:f_lane + FOX_HEADS].set(b_fox_f[l])
        gla, f_cum = _gla(proj, small, wa2p, b_gla_a[l].reshape(1, kw), bfv,
                          g_gla_out[l].reshape(1, vw), batch, seq)

        f_t = f_cum.reshape(batch, seq, LANES)[:, :, f_lane:f_lane + FOX_HEADS].transpose(0, 2, 1)
        f_t = jnp.concatenate([f_t, jnp.zeros_like(f_t)], axis=1)
        fox = _fox(proj, f_t, g_fox_out[l].reshape(1, fw), batch, seq)

        kv = _mem_kv(mem.reshape(batch * n_mem, d), g_mem_norm[l].reshape(1, d), w_mem_kv[l].astype(BF16))
        memo = _mem_attn(proj, kv, g_mem_out[l].reshape(1, mw), batch, seq, n_mem)

        wr = jnp.zeros((d, LANES), F32).at[:, :N_EXPERTS].set(w_router[l])
        wr_hi = wr.astype(BF16)
        wr = jnp.concatenate([wr_hi, (wr - wr_hi.astype(F32)).astype(BF16)], axis=1)
        br =jnp.zeros((1, LANES), F32).at[0, :N_EXPERTS].set(b_router[l])
        x2, xn, logits = _out_proj(xf, gla, fox, memo, w_out[l].astype(BF16),
                                   g_ffn_norm[l].reshape(1, d), wr, br)

        idx, gates, rank, cnt = _route(logits)
        pos, blk_start, blk_count = _routing_tables(idx, rank, cnt)
        n_blocks_max = -(-(n * TOP_K + N_EXPERTS * (MOE_TM - 1)) // MOE_TM)
        blk_end = blk_start + blk_count
        n_used = blk_end[-1:]
        x_rows = _dispatch(xn, pos, blk_end - 1, (blk_count > 0).astype(jnp.int32), n_used,
                           n_blocks_max, d)
        d_ff = w_moe_up.shape[3] // 2
        act = _moe_up(x_rows, w_moe_up[l], b_moe_up[l].reshape(N_EXPERTS, 1, 2 * d_ff),
                      blk_start, blk_count, n_used, n_blocks_max)
        y_rows = _moe_down(act, w_moe_down[l], b_moe_down[l].reshape(N_EXPERTS, 1, d),
                           blk_start, blk_count, n_used, n_blocks_max)
        xf = _combine(y_rows, pos, gates, x2, g_final.reshape(1, d))
    return xf.reshape(batch, seq, d)
```

```python
import functools

import jax
import jax.numpy as jnp
from jax import lax
from jax.experimental import pallas as pl
from jax.experimental.pallas import tpu as pltpu

EPS = 1e-5
CHUNK = 64
GLA_HEADS = 4
GLA_DK = 128
GLA_DV = 256
GLA_LOWRANK = 16
GLA_TAU = 16.0
FOX_HEADS = 4
FOX_DH = 128
MEM_HEADS = 4
MEM_DH = 128
N_EXPERTS = 32
TOP_K = 4
SWIGLU_LIMIT = 7.0
SWIGLU_ALPHA = 1.702
LANES = 128
MOE_TM = 512
VMEM_LIMIT = 56 * 1024 * 1024

F32 = jnp.float32
BF16 = jnp.bfloat16


def _cparams(sem, vmem=VMEM_LIMIT):
    return pltpu.CompilerParams(dimension_semantics=sem, vmem_limit_bytes=vmem)


def _log_sigmoid(x):
    return jnp.minimum(x, 0.0) - jnp.log1p(jnp.exp(-jnp.abs(x)))


def _rms(x, g):
    return x * lax.rsqrt(jnp.mean(x * x, axis=-1, keepdims=True) + EPS) * g


def _dot(a, b, **kw):
    return jnp.dot(a, b, preferred_element_type=F32, **kw)


def _dot_nt(a, b):
    return lax.dot_general(a, b, (((1,), (1,)), ((), ())), preferred_element_type=F32)


def _dot_tn(a, b):
    return lax.dot_general(a, b, (((0,), (0,)), ((), ())), preferred_element_type=F32)


def _rowmajor_chunk(ref, c, rows, pitch, first=0):
    return ref[pl.ds(first * pitch + c, rows, stride=pitch), :]


def _store_rowmajor(ref, x):
    rows, d = x.shape
    pitch = d // LANES
    for c in range(pitch):
        ref[pl.ds(c, rows, stride=pitch), :] = x[:, c * LANES:(c + 1) * LANES]


def _in_proj_kernel(x_ref, g_ref, wa_ref, wb_ref, proj_ref, small_ref, h_scr):
    @pl.when(pl.program_id(1) == 0)
    def _():
        hb = _rms(x_ref[...], g_ref[...]).astype(BF16)
        h_scr[...] = hb
        small_ref[...] = _dot(hb, wb_ref[...])

    proj_ref[...] = _dot(h_scr[...], wa_ref[...]).astype(proj_ref.dtype)


def _in_proj(x2d, g, wa, wb, tm=1024, tn=1280):
    n, d = x2d.shape
    na = wa.shape[1]
    return pl.pallas_call(
        _in_proj_kernel,
        grid=(n // tm, na // tn),
        in_specs=[
            pl.BlockSpec((tm, d), lambda i, j: (i, 0)),
            pl.BlockSpec((1, d), lambda i, j: (0, 0)),
            pl.BlockSpec((d, tn), lambda i, j: (0, j)),
            pl.BlockSpec((d, LANES), lambda i, j: (0, 0)),
        ],
        out_specs=[
            pl.BlockSpec((tm, tn), lambda i, j: (i, j)),
            pl.BlockSpec((tm, LANES), lambda i, j: (i, 0)),
        ],
        out_shape=[
            jax.ShapeDtypeStruct((n, na), BF16),
            jax.ShapeDtypeStruct((n, LANES), F32),
        ],
        scratch_shapes=[pltpu.VMEM((tm, d), BF16)],
        compiler_params=_cparams(("parallel", "arbitrary")),
        name="in_proj",
    )(x2d, g, wa, wb)


def _gla_kernel(q_ref, k_ref, v_ref, gate_ref, small_ref, wa2_ref, ba_ref, bf_ref, gout_ref,
                o_ref, f_ref, state_scr, fcar_scr, la_scr, lf_scr, *, n_chunks):
    @pl.when(pl.program_id(1) == 0)
    def _():
        state_scr[...] = jnp.zeros_like(state_scr)
        fcar_scr[...] = jnp.zeros_like(fcar_scr)

    small = small_ref[...]
    la_scr[...] = _log_sigmoid(_dot(small.astype(BF16), wa2_ref[...]) + ba_ref[...]) * (1.0 / GLA_TAU)
    lf_scr[...] = _log_sigmoid(small + bf_ref[...])
    row = lax.broadcasted_iota(jnp.int32, (CHUNK, CHUNK), 0)
    col = lax.broadcasted_iota(jnp.int32, (CHUNK, CHUNK), 1)
    tri = (col <= row).astype(F32)
    scale = GLA_DK ** -0.5

    def chunk_body(c, carry):
        r = pl.ds(pl.multiple_of(c * CHUNK, CHUNK), CHUNK)
        b = _dot(tri, la_scr[r, :], precision=lax.Precision.HIGHEST)
        b_end = b[CHUNK - 1:CHUNK, :]
        k_dec = k_ref[r, :].astype(F32) * jnp.exp(b_end - b)
        decay = jnp.exp(b_end)
        f_cum = _dot(tri, lf_scr[r, :], precision=lax.Precision.HIGHEST) + fcar_scr[...]
        f_ref[r, :] = f_cum
        fcar_scr[...] = f_cum[CHUNK - 1:CHUNK, :]
        for h in range(GLA_HEADS):
            ks = slice(h * GLA_DK, (h + 1) * GLA_DK)
            vs = slice(h * GLA_DV, (h + 1) * GLA_DV)
            st = state_scr[h] * decay[:, ks] + _dot_tn(v_ref[r, vs], k_dec[:, ks].astype(BF16))
            state_scr[h] = st
            o = _dot_nt(q_ref[r, ks], st.astype(BF16)) * scale
            gt = gate_ref[r, vs].astype(F32)
            o_ref[r, vs] = (_rms(o, gout_ref[:, vs]) * (gt * jax.nn.sigmoid(gt))).astype(o_ref.dtype)
        return carry

    lax.fori_loop(0, n_chunks, chunk_body, 0)


def _gla(proj, small, wa2p, ba, bfv, gout, batch, seq, ts=512):
    n = proj.shape[0]
    nsb = seq // ts
    kw = GLA_HEADS * GLA_DK
    vw = GLA_HEADS * GLA_DV
    row = lambda b, s: b * nsb + s
    return pl.pallas_call(
        functools.partial(_gla_kernel, n_chunks=ts // CHUNK),
        grid=(batch, nsb),
        in_specs=[
            pl.BlockSpec((ts, kw), lambda b, s: (row(b, s), 0)),
            pl.BlockSpec((ts, kw), lambda b, s: (row(b, s), 1)),
            pl.BlockSpec((ts, vw), lambda b, s: (row(b, s), 1)),
            pl.BlockSpec((ts, vw), lambda b, s: (row(b, s), 2)),
            pl.BlockSpec((ts, LANES), lambda b, s: (row(b, s), 0)),
            pl.BlockSpec((LANES, kw), lambda b, s: (0, 0)),
            pl.BlockSpec((1, kw), lambda b, s: (0, 0)),
            pl.BlockSpec((1, LANES), lambda b, s: (0, 0)),
            pl.BlockSpec((1, vw), lambda b, s: (0, 0)),
        ],
        out_specs=[
            pl.BlockSpec((ts, vw), lambda b, s: (row(b, s), 0)),
            pl.BlockSpec((ts, LANES), lambda b, s: (row(b, s), 0)),
        ],
        out_shape=[
            jax.ShapeDtypeStruct((n, vw), BF16),
            jax.ShapeDtypeStruct((n, LANES), F32),
        ],
        scratch_shapes=[
            pltpu.VMEM((GLA_HEADS, GLA_DV, GLA_DK), F32),
            pltpu.VMEM((1, LANES), F32),
            pltpu.VMEM((ts, kw), F32),
            pltpu.VMEM((ts, LANES), F32),
        ],
        compiler_params=_cparams(("parallel", "arbitrary")),
        name="gla",
    )(proj, proj, proj, proj, small, wa2p, ba, bfv, gout)


LOG2E = 1.4426950408889634
def _fox_kernel(q_ref, k_ref, v_ref, fk_ref, g_ref, o_ref, m_scr, l_scr, acc_scr, *, tq, tk):
    i = pl.program_id(1)
    j = pl.program_id(2)

    @pl.when(j == 0)
    def _():
        m_scr[...] = jnp.full_like(m_scr, -jnp.inf)
        l_scr[...] = jnp.zeros_like(l_scr)
        acc_scr[...] = jnp.zeros_like(acc_scr)

    heads = range(FOX_HEADS)
    hs = [slice(h * FOX_DH, (h + 1) * FOX_DH) for h in heads]

    def update(on_diagonal):
        m_prev = [m_scr[h] for h in heads]
        l_prev = [l_scr[h] for h in heads]
        acc_prev = [acc_scr[h] for h in heads]
        s = [_dot_nt(q_ref[:, hs[h]], k_ref[:, hs[h]]) - fk_ref[h:h + 1, :] * LOG2E for h in heads]
        if on_diagonal:
            row = lax.broadcasted_iota(jnp.int32, (tq, tk), 0)
            col = lax.broadcasted_iota(jnp.int32, (tq, tk), 1)
            s = [jnp.where(col <= row, sh, -jnp.inf) for sh in s]
        m_new = [jnp.maximum(m_prev[h], jnp.max(s[h], axis=-1, keepdims=True)) for h in heads]
        alpha = [jnp.exp2(m_prev[h] - m_new[h]) for h in heads]
        p = [jnp.exp2(s[h] - m_new[h]) for h in heads]
        l_new = [alpha[h] * l_prev[h] + jnp.sum(p[h], axis=-1, keepdims=True) for h in heads]
        acc_new = [alpha[h] * acc_prev[h] + _dot(p[h].astype(BF16), v_ref[:, hs[h]]) for h in heads]
        return m_new, l_new, acc_new

    @pl.when(j < i)
    def _():
        m_new, l_new, acc_new = update(False)
        for h in heads:
            m_scr[h] = m_new[h]
            l_scr[h] = l_new[h]
            acc_scr[h] = acc_new[h]

    @pl.when(j == i)
    def _():
        _, l_new, acc_new = update(True)
        for h in heads:
            o_ref[:, hs[h]] = _rms(acc_new[h] / l_new[h], g_ref[:, hs[h]]).astype(o_ref.dtype)


def _fox(proj, f_t, g_fox, batch, seq, tq=512, tk=512):
    assert tq == tk, "the diagonal-block mask assumes square blocks"
    n = proj.shape[0]
    w = FOX_HEADS * FOX_DH
    nq, nk = seq // tq, seq // tk
    qcol, kcol, vcol = 3072 // w, 3584 // w, 4096 // w
    return pl.pallas_call(
        functools.partial(_fox_kernel, tq=tq, tk=tk),
        grid=(batch, nq, nk),
        in_specs=[
            pl.BlockSpec((tq, w), lambda b, i, j: (b * nq + i, qcol)),
            pl.BlockSpec((tk, w), lambda b, i, j: (b * nk + jnp.minimum(j, i), kcol)),
            pl.BlockSpec((tk, w), lambda b, i, j: (b * nk + jnp.minimum(j, i), vcol)),
            pl.BlockSpec((None, 8, tk), lambda b, i, j: (b, 0, jnp.minimum(j, i))),
            pl.BlockSpec((1, w), lambda b, i, j: (0, 0)),
        ],
        out_specs=pl.BlockSpec((tq, w), lambda b, i, j: (b * nq + i, 0)),
        out_shape=jax.ShapeDtypeStruct((n, w), BF16),
        scratch_shapes=[
            pltpu.VMEM((FOX_HEADS, tq, 1), F32),
            pltpu.VMEM((FOX_HEADS, tq, 1), F32),
            pltpu.VMEM((FOX_HEADS, tq, FOX_DH), F32),
        ],
        compiler_params=_cparams(("parallel", "parallel", "arbitrary")),
        name="fox",
    )(proj, proj, proj, f_t, g_fox)


def _mem_kv_kernel(m_ref, g_ref, w_ref, kv_ref):
    kv_ref[...] = _dot(_rms(m_ref[...], g_ref[...]).astype(BF16), w_ref[...]).astype(kv_ref.dtype)


def _mem_kv(mem2d, g, w, tm=256):
    n, d = mem2d.shape
    nw = w.shape[1]
    return pl.pallas_call(
        _mem_kv_kernel,
        grid=(n // tm,),
        in_specs=[
            pl.BlockSpec((tm, d), lambda i: (i, 0)),
            pl.BlockSpec((1, d), lambda i: (0, 0)),
            pl.BlockSpec((d, nw), lambda i: (0, 0)),
        ],
        out_specs=pl.BlockSpec((tm, nw), lambda i: (i, 0)),
        out_shape=jax.ShapeDtypeStruct((n, nw), BF16),
        compiler_params=_cparams(("parallel",)),
        name="mem_kv",
    )(mem2d, g, w)


def _mem_attn_kernel(q_ref, k_ref, v_ref, g_ref, o_ref):
    scale = MEM_DH ** -0.5
    for h in range(MEM_HEADS):
        hs = slice(h * MEM_DH, (h + 1) * MEM_DH)
        s = _dot_nt(q_ref[:, hs], k_ref[:, hs]) * scale
        p = jnp.exp(s - jnp.max(s, axis=-1, keepdims=True))
        l = jnp.sum(p, axis=-1, keepdims=True)
        o = _dot((p / l).astype(BF16), v_ref[:, hs])
        o_ref[:, hs] = _rms(o, g_ref[:, hs]).astype(o_ref.dtype)


def _mem_attn(proj, kv, g_mem_out, batch, seq, n_mem, tq=1024):
    n = proj.shape[0]
    w = MEM_HEADS * MEM_DH
    nq = seq // tq
    qcol = 4608 // w
    return pl.pallas_call(
        _mem_attn_kernel,
        grid=(n // tq,),
        in_specs=[
            pl.BlockSpec((tq, w), lambda i: (i, qcol)),
            pl.BlockSpec((n_mem, w), lambda i: (i // nq, 0)),
            pl.BlockSpec((n_mem, w), lambda i: (i // nq, 1)),
            pl.BlockSpec((1, w), lambda i: (0, 0)),
        ],
        out_specs=pl.BlockSpec((tq, w), lambda i: (i, 0)),
        out_shape=jax.ShapeDtypeStruct((n, w), BF16),
        compiler_params=_cparams(("parallel",)),
        name="mem_attn",
    )(proj, kv, kv, g_mem_out)


def _out_proj_kernel(x_ref, gla_ref, fox_ref, mem_ref, w1_ref, w2_ref, w3_ref, g_ref, wr_ref, br_ref,
                     x2_ref, xn_ref, logit_ref):
    x2 = (x_ref[...] + _dot(gla_ref[...], w1_ref[...]) + _dot(fox_ref[...], w2_ref[...])
          + _dot(mem_ref[...], w3_ref[...]))
    x2_ref[...] = x2
    xn = _rms(x2, g_ref[...])
    _store_rowmajor(xn_ref, xn)
    xh = xn.astype(BF16)
    xl = (xn - xh.astype(F32)).astype(BF16)
    hi = _dot(xh, wr_ref[...])
    logit_ref[...] = hi[:, :LANES] + hi[:, LANES:] + _dot(xl, wr_ref[:, :LANES]) + br_ref[...]


def _out_proj(x2d, gla, fox, memo, w_out, g_ffn, wr, br, tm=256):
    n, d = x2d.shape
    w1, w2 = gla.shape[1], fox.shape[1]
    const = lambda i: (0, 0)
    return pl.pallas_call(
        _out_proj_kernel,
        grid=(n // tm,),
        in_specs=[
            pl.BlockSpec((tm, d), lambda i: (i, 0)),
            pl.BlockSpec((tm, w1), lambda i: (i, 0)),
            pl.BlockSpec((tm, w2), lambda i: (i, 0)),
            pl.BlockSpec((tm, w2), lambda i: (i, 0)),
            pl.BlockSpec((w1, d), lambda i: (0, 0)),
            pl.BlockSpec((w2, d), lambda i: (w1 // w2, 0)),
            pl.BlockSpec((w2, d), lambda i: (w1 // w2 + 1, 0)),
            pl.BlockSpec((1, d), const),
            pl.BlockSpec((d, 2 * LANES), const),
            pl.BlockSpec((1, LANES), const),
        ],
        out_specs=[
            pl.BlockSpec((tm, d), lambda i: (i, 0)),
            pl.BlockSpec((tm * (d // LANES), LANES), lambda i: (i, 0)),
            pl.BlockSpec((tm, LANES), lambda i: (i, 0)),
        ],
        out_shape=[
            jax.ShapeDtypeStruct((n, d), F32),
            jax.ShapeDtypeStruct((n * (d // LANES), LANES), F32),
            jax.ShapeDtypeStruct((n, LANES), F32),
        ],
        compiler_params=_cparams(("parallel",)),
        name="out_proj",
    )(x2d, gla, fox, memo, w_out, w_out, w_out, g_ffn, wr, br)


def _route_kernel(l_ref, idx_ref, gate_ref, rank_ref, cnt_ref, carry_scr, *, tb):
    @pl.when(pl.program_id(0) == 0)
    def _():
        carry_scr[...] = jnp.zeros_like(carry_scr)

    lane = lax.broadcasted_iota(jnp.int32, (tb, LANES), 1)
    logit = jnp.where(lane < N_EXPERTS, l_ref[...], -jnp.inf)
    vals, hots = [], []
    idx_out = jnp.zeros((tb, LANES), jnp.int32)
    for k in range(TOP_K):
        m = jnp.max(logit, axis=-1, keepdims=True)
        ik = jnp.min(jnp.where(logit == m, lane, LANES), axis=-1, keepdims=True)
        hot = lane == ik
        logit = jnp.where(hot, -jnp.inf, logit)
        vals.append(m)
        hots.append(hot)
        idx_out = jnp.where(lane == k, ik, idx_out)
    idx_ref[...] = idx_out

    e = [jnp.exp(v - vals[0]) for v in vals]
    den = e[0] + e[1] + e[2] + e[3]
    gate_out = jnp.zeros((tb, LANES), F32)
    for k in range(TOP_K):
        gate_out = jnp.where(lane == k, e[k] / den, gate_out)
    gate_ref[...] = gate_out

    member = jnp.zeros((tb, LANES), F32)
    for hot in hots:
        member = member + hot.astype(F32)
    row = lax.broadcasted_iota(jnp.int32, (tb, tb), 0)
    col = lax.broadcasted_iota(jnp.int32, (tb, tb), 1)
    before = (col < row).astype(BF16)
    rank = _dot(before, member.astype(BF16)) + carry_scr[...]
    rank_out = jnp.zeros((tb, LANES), F32)
    for k in range(TOP_K):
        rk = jnp.sum(jnp.where(hots[k], rank, 0.0), axis=-1, keepdims=True)
        rank_out = jnp.where(lane == k, rk, rank_out)
    rank_ref[...] = rank_out.astype(jnp.int32)
    total = carry_scr[...] + jnp.sum(member, axis=0, keepdims=True)
    carry_scr[...] = total
    cnt_ref[...] = total.astype(jnp.int32)


def _route(logits, tb=512):
    n = logits.shape[0]
    blk = pl.BlockSpec((tb, LANES), lambda i: (i, 0))
    return pl.pallas_call(
        functools.partial(_route_kernel, tb=tb),
        grid=(n // tb,),
        in_specs=[blk],
        out_specs=[blk, blk, blk, pl.BlockSpec((1, LANES), lambda i: (0, 0))],
        out_shape=[
            jax.ShapeDtypeStruct((n, LANES), jnp.int32),
            jax.ShapeDtypeStruct((n, LANES), F32),
            jax.ShapeDtypeStruct((n, LANES), jnp.int32),
            jax.ShapeDtypeStruct((1, LANES), jnp.int32),
        ],
        scratch_shapes=[pltpu.VMEM((1, LANES), F32)],
        compiler_params=_cparams(("arbitrary",)),
        name="route",
    )(logits)


def _dispatch_kernel(zblk_ref, zok_ref, used_ref, pos_ref, xn_ref, xr_hbm, zero_scr, sem,
                     *, tb, pitch, n_blocks_max):
    blk_rows = MOE_TM * pitch

    def zero_copy(b):
        dst = xr_hbm.at[pl.ds(pl.multiple_of(b * blk_rows, blk_rows), blk_rows), :]
        return pltpu.make_async_copy(zero_scr, dst, sem.at[1])

    @pl.when(pl.program_id(0) == 0)
    def _():
        zero_scr[...] = jnp.zeros_like(zero_scr)

        def per_expert(action):
            def body(e, carry):
                @pl.when(zok_ref[e] == 1)
                def _():
                    action(zero_copy(zblk_ref[e]))
                return carry
            lax.fori_loop(0, N_EXPERTS, body, 0)

        def per_tail(action):
            def body(b, carry):
                action(zero_copy(b))
                return carry
            lax.fori_loop(used_ref[0], n_blocks_max, body, 0)

        per_expert(lambda c: c.start())
        per_tail(lambda c: c.start())
        per_expert(lambda c: c.wait())
        per_tail(lambda c: c.wait())

    def issue(t, carry):
        src = xn_ref.at[pl.ds(pl.multiple_of(t * pitch, pitch), pitch), :]
        for k in range(TOP_K):
            p = pos_ref[0, t * TOP_K + k]
            dst = xr_hbm.at[pl.ds(pl.multiple_of(p * pitch, pitch), pitch), :]
            pltpu.make_async_copy(src, dst, sem.at[0]).start(priority=k % 2)
        return carry

    lax.fori_loop(0, tb, issue, 0, unroll=4)
    for k in range(TOP_K):
        pltpu.make_async_copy(xn_ref, xr_hbm.at[pl.ds(0, tb * pitch), :], sem.at[0]).wait()


def _dispatch(xn_rm, pos, zblk, zok, n_used, n_blocks_max, d, tb=256):
    pitch = d // LANES
    n = xn_rm.shape[0] // pitch
    nb = n // tb
    grid_spec = pltpu.PrefetchScalarGridSpec(
        num_scalar_prefetch=3,
        grid=(nb,),
        in_specs=[
            pl.BlockSpec((None, 1, TOP_K * tb), lambda i, *_: (i, 0, 0), memory_space=pltpu.SMEM),
            pl.BlockSpec((tb * pitch, LANES), lambda i, *_: (i, 0)),
        ],
        out_specs=pl.BlockSpec(memory_space=pl.ANY),
        scratch_shapes=[pltpu.VMEM((MOE_TM * pitch, LANES), F32), pltpu.SemaphoreType.DMA((2,))],
    )
    return pl.pallas_call(
        functools.partial(_dispatch_kernel, tb=tb, pitch=pitch, n_blocks_max=n_blocks_max),
        grid_spec=grid_spec,
        out_shape=jax.ShapeDtypeStruct((n_blocks_max * MOE_TM * pitch, LANES), F32),
        compiler_params=_cparams(("arbitrary",)),
        name="dispatch",
    )(zblk, zok, n_used, pos.reshape(nb, 1, TOP_K * tb), xn_rm)


def _cast_rows(src, dst, rows_per=256):
    def body(i, carry):
        r = pl.ds(pl.multiple_of(i * rows_per, rows_per), rows_per)
        dst[r, :] = src[r, :].astype(dst.dtype)
        return carry

    lax.fori_loop(0, src.shape[0] // rows_per, body, 0)


def _stream_row_blocks(cnt, in_copy, out_copy, compute):
    @pl.when(cnt > 0)
    def _():
        in_copy(0, 0).start()

    def body(b, carry):
        slot = b % 2

        @pl.when(b + 1 < cnt)
        def _():
            in_copy(b + 1, 1 - slot).start()

        in_copy(b, slot).wait()

        @pl.when(b >= 2)
        def _():
            out_copy(b - 2, slot).wait()

        compute(slot)
        out_copy(b, slot).start()
        return carry

    lax.fori_loop(0, cnt, body, 0)

    @pl.when(cnt >= 2)
    def _():
        out_copy(cnt - 2, cnt % 2).wait()

    @pl.when(cnt >= 1)
    def _():
        out_copy(cnt - 1, (cnt - 1) % 2).wait()


def _zero_fill_blocks(first, last, zero_src, dst_copy):
    def start(b, carry):
        dst_copy(b).start()
        return carry

    def wait(b, carry):
        dst_copy(b).wait()
        return carry

    zero_src[...] = jnp.zeros_like(zero_src)
    lax.fori_loop(first, last, start, 0)
    lax.fori_loop(first, last, wait, 0)


def _moe_up_kernel(bs_ref, bc_ref, used_ref, x_hbm, w_hbm, bg_ref, bu_ref, act_hbm,
                   wf_scr, wb_scr, xbuf, lhs_scr, obuf, sem_w, sem_x, sem_o,
                   *, nt, tn, d_ff, pitch, n_blocks_max):
    g = pl.program_id(0)
    ng = pl.num_programs(0)
    e = g // nt
    col = pl.multiple_of((g % nt) * tn, tn)
    blk_rows = MOE_TM * pitch

    def w_copies(step):
        ee = step // nt
        cc = pl.multiple_of((step % nt) * tn, tn)
        return (pltpu.make_async_copy(w_hbm.at[ee, :, pl.ds(cc, tn)], wf_scr.at[0], sem_w.at[0]),
                pltpu.make_async_copy(w_hbm.at[ee, :, pl.ds(d_ff + cc, tn)], wf_scr.at[1], sem_w.at[1]))

    @pl.when(g == 0)
    def _():
        for c in w_copies(0):
            c.start(priority=1)

    for half, c in enumerate(w_copies(g)):
        c.wait()
        _cast_rows(wf_scr.at[half], wb_scr.at[half])

    @pl.when(g + 1 < ng)
    def _():
        for c in w_copies(g + 1):
            c.start(priority=1)

    start = bs_ref[e]

    def x_copy(b, slot):
        r0 = pl.multiple_of((start + b) * blk_rows, blk_rows)
        return pltpu.make_async_copy(x_hbm.at[pl.ds(r0, blk_rows), :], xbuf.at[slot], sem_x.at[slot])

    def o_copy(b, slot):
        r0 = pl.multiple_of((start + b) * MOE_TM, MOE_TM)
        return pltpu.make_async_copy(obuf.at[slot], act_hbm.at[pl.ds(r0, MOE_TM), pl.ds(col, tn)],
                                     sem_o.at[slot])

    def compute(slot):
        for c in range(pitch):
            lhs_scr[:, c * LANES:(c + 1) * LANES] = (
                xbuf[slot, pl.ds(c, MOE_TM, stride=pitch), :].astype(BF16))
        x = lhs_scr[...]
        gate = jnp.minimum(_dot(x, wb_scr[0]) + bg_ref[...], SWIGLU_LIMIT)
        up = jnp.clip(_dot(x, wb_scr[1]) + bu_ref[...], -SWIGLU_LIMIT, SWIGLU_LIMIT)
        obuf[slot] = (gate * jax.nn.sigmoid(SWIGLU_ALPHA * gate) * (up + 1.0)).astype(obuf.dtype)

    _stream_row_blocks(bc_ref[e], x_copy, o_copy, compute)

    @pl.when(e == N_EXPERTS - 1)
    def _():
        def tail_copy(b):
            r0 = pl.multiple_of(b * MOE_TM, MOE_TM)
            return pltpu.make_async_copy(obuf.at[0], act_hbm.at[pl.ds(r0, MOE_TM), pl.ds(col, tn)],
                                         sem_o.at[0])
        _zero_fill_blocks(used_ref[0], n_blocks_max, obuf.at[0], tail_copy)


def _moe_up(x_rows_rm, w_up, b_up, blk_start, blk_count, n_used, n_blocks_max, tn=1024):
    n_exp, d, two_ff = w_up.shape
    d_ff = two_ff // 2
    pitch = d // LANES
    nt = d_ff // tn
    up_off = d_ff // tn
    grid_spec = pltpu.PrefetchScalarGridSpec(
        num_scalar_prefetch=3,
        grid=(n_exp * nt,),
        in_specs=[
            pl.BlockSpec(memory_space=pl.ANY),
            pl.BlockSpec(memory_space=pl.ANY),
            pl.BlockSpec((None, 1, tn), lambda g, *_: (g // nt, 0, g % nt)),
            pl.BlockSpec((None, 1, tn), lambda g, *_: (g // nt, 0, g % nt + up_off)),
        ],
        out_specs=pl.BlockSpec(memory_space=pl.ANY),
        scratch_shapes=[
            pltpu.VMEM((2, d, tn), F32),
            pltpu.VMEM((2, d, tn), BF16),
            pltpu.VMEM((2, MOE_TM * pitch, LANES), F32),
            pltpu.VMEM((MOE_TM, d), BF16),
            pltpu.VMEM((2, MOE_TM, tn), BF16),
            pltpu.SemaphoreType.DMA((2,)),
            pltpu.SemaphoreType.DMA((2,)),
            pltpu.SemaphoreType.DMA((2,)),
        ],
    )
    return pl.pallas_call(
        functools.partial(_moe_up_kernel, nt=nt, tn=tn, d_ff=d_ff, pitch=pitch, n_blocks_max=n_blocks_max),
        grid_spec=grid_spec,
        out_shape=jax.ShapeDtypeStruct((n_blocks_max * MOE_TM, d_ff), BF16),
        compiler_params=_cparams(("arbitrary",)),
        name="moe_up",
    )(blk_start, blk_count, n_used, x_rows_rm, w_up, b_up, b_up)


def _moe_down_kernel(bs_ref, bc_ref, used_ref, a_hbm, w_hbm, b_ref, y_hbm,
                     wf_scr, wb_scr, abuf, obuf, sem_w, sem_a, sem_o, *, pitch, n_blocks_max):
    e = pl.program_id(0)
    blk_rows = MOE_TM * pitch

    def w_copy(ee):
        return pltpu.make_async_copy(w_hbm.at[ee], wf_scr, sem_w.at[0])

    @pl.when(e == 0)
    def _():
        w_copy(0).start(priority=1)

    w_copy(e).wait()
    _cast_rows(wf_scr, wb_scr)

    @pl.when(e + 1 < pl.num_programs(0))
    def _():
        w_copy(e + 1).start(priority=1)

    start = bs_ref[e]

    def a_copy(b, slot):
        r0 = pl.multiple_of((start + b) * MOE_TM, MOE_TM)
        return pltpu.make_async_copy(a_hbm.at[pl.ds(r0, MOE_TM), :], abuf.at[slot], sem_a.at[slot])

    def o_copy(b, slot):
        r0 = pl.multiple_of((start + b) * blk_rows, blk_rows)
        return pltpu.make_async_copy(obuf.at[slot], y_hbm.at[pl.ds(r0, blk_rows), :], sem_o.at[slot])

    def compute(slot):
        y = _dot(abuf[slot], wb_scr[...]) + b_ref[...]
        for c in range(pitch):
            obuf[slot, pl.ds(c, MOE_TM, stride=pitch), :] = y[:, c * LANES:(c + 1) * LANES]

    _stream_row_blocks(bc_ref[e], a_copy, o_copy, compute)

    @pl.when(e == N_EXPERTS - 1)
    def _():
        def tail_copy(b):
            r0 = pl.multiple_of(b * blk_rows, blk_rows)
            return pltpu.make_async_copy(obuf.at[0], y_hbm.at[pl.ds(r0, blk_rows), :], sem_o.at[0])
        _zero_fill_blocks(used_ref[0], n_blocks_max, obuf.at[0], tail_copy)


def _moe_down(act, w_down, b_down, blk_start, blk_count, n_used, n_blocks_max):
    n_rows, d_ff = act.shape
    n_exp, _, d = w_down.shape
    pitch = d // LANES
    grid_spec = pltpu.PrefetchScalarGridSpec(
        num_scalar_prefetch=3,
        grid=(n_exp,),
        in_specs=[
            pl.BlockSpec(memory_space=pl.ANY),
            pl.BlockSpec(memory_space=pl.ANY),
            pl.BlockSpec((None, 1, d), lambda e, *_: (e, 0, 0)),
        ],
        out_specs=pl.BlockSpec(memory_space=pl.ANY),
        scratch_shapes=[
            pltpu.VMEM((d_ff, d), F32),
            pltpu.VMEM((d_ff, d), BF16),
            pltpu.VMEM((2, MOE_TM, d_ff), BF16),
            pltpu.VMEM((2, MOE_TM * pitch, LANES), F32),
            pltpu.SemaphoreType.DMA((1,)),
            pltpu.SemaphoreType.DMA((2,)),
            pltpu.SemaphoreType.DMA((2,)),
        ],
    )
    return pl.pallas_call(
        functools.partial(_moe_down_kernel, pitch=pitch, n_blocks_max=n_blocks_max),
        grid_spec=grid_spec,
        out_shape=jax.ShapeDtypeStruct((n_rows * pitch, LANES), F32),
        compiler_params=_cparams(("arbitrary",)),
        name="moe_down",
    )(blk_start, blk_count, n_used, act, w_down, b_down)


COMBINE_PITCH = 24


def _combine_kernel(pos0_ref, posn_ref, y_hbm, x2_ref, gate_ref, g_ref, o_ref, buf, sem, *, tb, pitch):
    i = pl.program_id(0)
    n_rows = TOP_K * tb

    def issue_block(p_ref, slot):
        def issue(j, carry):
            for half in range(2):
                r = 2 * j + half
                src = y_hbm.at[pl.ds(pl.multiple_of(p_ref[0, r] * pitch, pitch), pitch), :]
                d0 = pl.multiple_of((slot * n_rows + r) * COMBINE_PITCH, 8)
                pltpu.make_async_copy(src, buf.at[pl.ds(d0, pitch), :], sem.at[slot]).start(priority=half)
            return carry
        lax.fori_loop(0, n_rows // 2, issue, 0, unroll=4)

    @pl.when(i == 0)
    def _():
        issue_block(pos0_ref, 0)

    @pl.when(i + 1 < pl.num_programs(0))
    def _():
        issue_block(posn_ref, (i + 1) % 2)

    slot = i % 2
    base = pl.multiple_of(slot * (n_rows * COMBINE_PITCH), 8)
    pltpu.make_async_copy(y_hbm.at[pl.ds(0, n_rows * pitch), :], buf.at[pl.ds(base, n_rows * pitch), :],
                          sem.at[slot]).wait()
    gates = gate_ref[...]
    ssq = jnp.zeros((tb, 1), F32)
    for c in range(pitch):
        cs = slice(c * LANES, (c + 1) * LANES)
        z = x2_ref[:, cs]
        for k in range(TOP_K):
            rows = buf[pl.ds(base + k * tb * COMBINE_PITCH + c, tb, stride=COMBINE_PITCH), :]
            z = z + gates[:, k:k + 1] * rows
        o_ref[:, cs] = z
        ssq = ssq + jnp.sum(z * z, axis=-1, keepdims=True)
    o_ref[...] = o_ref[...] * lax.rsqrt(ssq * (1.0 / (pitch * LANES)) + EPS) * g_ref[...]


def _combine(y_rows_rm, pos, gates, x2, g_final, tb=128):
    n, d = x2.shape
    nb = n // tb
    pitch = d // LANES
    pos_blk = pos.reshape(nb, tb, TOP_K).transpose(0, 2, 1).reshape(nb, 1, TOP_K * tb)
    pos_spec = lambda imap: pl.BlockSpec((None, 1, TOP_K * tb), imap, memory_space=pltpu.SMEM)
    return pl.pallas_call(
        functools.partial(_combine_kernel, tb=tb, pitch=pitch),
        grid=(nb,),
        in_specs=[
            pos_spec(lambda i: (0, 0, 0)),
            pos_spec(lambda i: (jnp.minimum(i + 1, nb - 1), 0, 0)),
            pl.BlockSpec(memory_space=pl.ANY),
            pl.BlockSpec((tb, d), lambda i: (i, 0)),
            pl.BlockSpec((tb, LANES), lambda i: (i, 0)),
            pl.BlockSpec((1, d), lambda i: (0, 0)),
        ],
        out_specs=pl.BlockSpec((tb, d), lambda i: (i, 0)),
        out_shape=jax.ShapeDtypeStruct((n, d), F32),
        scratch_shapes=[pltpu.VMEM((2 * TOP_K * tb * COMBINE_PITCH, LANES), F32),
                        pltpu.SemaphoreType.DMA((2,))],
        compiler_params=_cparams(("arbitrary",)),
        name="combine",
    )(pos_blk, pos_blk, y_rows_rm, x2, gates, g_final)


def _routing_tables(idx, rank, cnt):
    counts = cnt[0, :N_EXPERTS]
    blk_count = (counts + MOE_TM - 1) // MOE_TM
    blk_end = jnp.cumsum(blk_count)
    blk_start = blk_end - blk_count
    hot = idx[:, :TOP_K, None] == jnp.arange(N_EXPERTS, dtype=jnp.int32)
    pos = jnp.sum(jnp.where(hot, blk_start * MOE_TM, 0), axis=-1) + rank[:, :TOP_K]
    return pos.astype(jnp.int32), blk_start.astype(jnp.int32), blk_count.astype(jnp.int32)


def kernel(x, mem, g_attn_norm, g_mem_norm, w_in, w_gla_a2, b_gla_a, g_gla_out, b_fox_f, g_fox_out,
           w_mem_kv, g_mem_out, w_out, g_ffn_norm, w_router, b_router, w_moe_up, b_moe_up,
           w_moe_down, b_moe_down, g_final):
    batch, seq, d = x.shape
    n_mem = mem.shape[1]
    n = batch * seq
    depth = w_in.shape[0]
    assert depth == 1, "the combine kernel applies the final norm, so exactly one layer is supported"
    kw = GLA_HEADS * GLA_DK
    vw = GLA_HEADS * GLA_DV
    fw = FOX_HEADS * FOX_DH
    mw = MEM_HEADS * MEM_DH
    o_q, o_k, o_v, o_g = 0, kw, 2 * kw, 2 * kw + vw
    o_a = o_g + vw
    o_fq = o_a + GLA_LOWRANK
    o_fk, o_fv = o_fq + fw, o_fq + 2 * fw
    o_ff = o_fq + 3 * fw
    o_mq = o_ff + FOX_HEADS
    f_lane = GLA_LOWRANK

    xf = x.reshape(n, d)
    for l in range(depth):
        wi = w_in[l]
        fox_q_scale = FOX_DH ** -0.5 * LOG2E
        wa = jnp.concatenate([wi[:, o_q:o_a], wi[:, o_fq:o_fk] * fox_q_scale, wi[:, o_fk:o_ff],
                              wi[:, o_mq:o_mq + mw]], axis=1).astype(BF16)
        wb = jnp.zeros((d, LANES), F32).at[:, :GLA_LOWRANK].set(wi[:, o_a:o_fq])
        wb = wb.at[:, f_lane:f_lane + FOX_HEADS].set(wi[:, o_ff:o_mq]).astype(BF16)
        proj, small = _in_proj(xf, g_attn_norm[l].reshape(1, d), wa, wb)

        wa2p = jnp.zeros((LANES, kw), F32).at[:GLA_LOWRANK].set(w_gla_a2[l]).astype(BF16)
        bfv = jnp.zeros((1, LANES), F32).at[0, f_lane:f_lane + FOX_HEADS].set(b_fox_f[l])
        gla, f_cum = _gla(proj, small, wa2p, b_gla_a[l].reshape(1, kw), bfv,
                          g_gla_out[l].reshape(1, vw), batch, seq)

        f_t = f_cum.reshape(batch, seq, LANES)[:, :, f_lane:f_lane + FOX_HEADS].transpose(0, 2, 1)
        f_t = jnp.concatenate([f_t, jnp.zeros_like(f_t)], axis=1)
        fox = _fox(proj, f_t, g_fox_out[l].reshape(1, fw), batch, seq)

        kv = _mem_kv(mem.reshape(batch * n_mem, d), g_mem_norm[l].reshape(1, d), w_mem_kv[l].astype(BF16))
        memo = _mem_attn(proj, kv, g_mem_out[l].reshape(1, mw), batch, seq, n_mem)

        wr = jnp.zeros((d, LANES), F32).at[:, :N_EXPERTS].set(w_router[l])
        wr_hi = wr.astype(BF16)
        wr = jnp.concatenate([wr_hi, (wr - wr_hi.astype(F32)).astype(BF16)], axis=1)
        br =jnp.zeros((1, LANES), F32).at[0, :N_EXPERTS].set(b_router[l])
        x2, xn, logits = _out_proj(xf, gla, fox, memo, w_out[l].astype(BF16),
                                   g_ffn_norm[l].reshape(1, d), wr, br)

        idx, gates, rank, cnt = _route(logits)
        pos, blk_start, blk_count = _routing_tables(idx, rank, cnt)
        n_blocks_max = -(-(n * TOP_K + N_EXPERTS * (MOE_TM - 1)) // MOE_TM)
        blk_end = blk_start + blk_count
        n_used = blk_end[-1:]
        x_rows = _dispatch(xn, pos, blk_end - 1, (blk_count > 0).astype(jnp.int32), n_used,
                           n_blocks_max, d)
        d_ff = w_moe_up.shape[3] // 2
        act = _moe_up(x_rows, w_moe_up[l], b_moe_up[l].reshape(N_EXPERTS, 1, 2 * d_ff),
                      blk_start, blk_count, n_used, n_blocks_max)
        y_rows = _moe_down(act, w_moe_down[l], b_moe_down[l].reshape(N_EXPERTS, 1, d),
                           blk_start, blk_count, n_used, n_blocks_max)
        xf = _combine(y_rows, pos, gates, x2, g_final.reshape(1, d))
    return xf.reshape(batch, seq, d)
```

```python
import functools

import jax
import jax.numpy as jnp
from jax import lax
from jax.experimental import pallas as pl
from jax.experimental.pallas import tpu as pltpu

EPS = 1e-5
CHUNK = 64
GLA_HEADS = 4
GLA_DK = 128
GLA_DV = 256
GLA_LOWRANK = 16
GLA_TAU = 16.0
FOX_HEADS = 4
FOX_DH = 128
MEM_HEADS = 4
MEM_DH = 128
N_EXPERTS = 32
TOP_K = 4
SWIGLU_LIMIT = 7.0
SWIGLU_ALPHA = 1.702
LANES = 128
MOE_TM = 256
VMEM_LIMIT = 56 * 1024 * 1024

F32 = jnp.float32
BF16 = jnp.bfloat16


def _cparams(sem, vmem=VMEM_LIMIT):
    return pltpu.CompilerParams(dimension_semantics=sem, vmem_limit_bytes=vmem)


def _log_sigmoid(x):
    return jnp.minimum(x, 0.0) - jnp.log1p(jnp.exp(-jnp.abs(x)))


def _rms(x, g):
    return x * lax.rsqrt(jnp.mean(x * x, axis=-1, keepdims=True) + EPS) * g


def _dot(a, b, **kw):
    return jnp.dot(a, b, preferred_element_type=F32, **kw)


def _dot_nt(a, b):
    return lax.dot_general(a, b, (((1,), (1,)), ((), ())), preferred_element_type=F32)


def _dot_tn(a, b):
    return lax.dot_general(a, b, (((0,), (0,)), ((), ())), preferred_element_type=F32)


def _store_rowmajor(ref, x, base=0):
    rows, w = x.shape
    pitch = w // LANES
    for c in range(pitch):
        ref[pl.ds(base + c, rows, stride=pitch), :] = x[:, c * LANES:(c + 1) * LANES]


def _pack_bf16_pairs(x):
    half = x.shape[1] // 2
    bits = lambda v: lax.bitcast_convert_type(v.astype(BF16).astype(F32), jnp.uint32)
    return bits(x[:, half:]) | (bits(x[:, :half]) >> 16)


def _unpack_lo(w):
    return lax.bitcast_convert_type(w << 16, F32)


def _unpack_hi(w):
    return lax.bitcast_convert_type(w & jnp.uint32(0xFFFF0000), F32)


def _in_proj_kernel(x_ref, g_ref, wa_ref, wb_ref, proj_ref, small_ref, h_scr):
    @pl.when(pl.program_id(1) == 0)
    def _():
        hb = _rms(x_ref[...], g_ref[...]).astype(BF16)
        h_scr[...] = hb
        small_ref[...] = _dot(hb, wb_ref[...])

    proj_ref[...] = _dot(h_scr[...], wa_ref[...]).astype(proj_ref.dtype)


def _in_proj(x2d, g, wa, wb, tm=1024, tn=1280):
    n, d = x2d.shape
    na = wa.shape[1]
    return pl.pallas_call(
        _in_proj_kernel,
        grid=(n // tm, na // tn),
        in_specs=[
            pl.BlockSpec((tm, d), lambda i, j: (i, 0)),
            pl.BlockSpec((1, d), lambda i, j: (0, 0)),
            pl.BlockSpec((d, tn), lambda i, j: (0, j)),
            pl.BlockSpec((d, LANES), lambda i, j: (0, 0)),
        ],
        out_specs=[
            pl.BlockSpec((tm, tn), lambda i, j: (i, j)),
            pl.BlockSpec((tm, LANES), lambda i, j: (i, 0)),
        ],
        out_shape=[
            jax.ShapeDtypeStruct((n, na), BF16),
            jax.ShapeDtypeStruct((n, LANES), F32),
        ],
        scratch_shapes=[pltpu.VMEM((tm, d), BF16)],
        compiler_params=_cparams(("parallel", "arbitrary")),
        name="in_proj",
    )(x2d, g, wa, wb)


def _gla_kernel(q_ref, k_ref, v_ref, gate_ref, small_ref, wa2_ref, ba_ref, bf_ref, gout_ref,
                o_ref, f_ref, state_scr, fcar_scr, la_scr, lf_scr, *, n_chunks):
    @pl.when(pl.program_id(1) == 0)
    def _():
        state_scr[...] = jnp.zeros_like(state_scr)
        fcar_scr[...] = jnp.zeros_like(fcar_scr)

    small = small_ref[...]
    la_scr[...] = _log_sigmoid(_dot(small.astype(BF16), wa2_ref[...]) + ba_ref[...]) * (1.0 / GLA_TAU)
    lf_scr[...] = _log_sigmoid(small + bf_ref[...])
    row = lax.broadcasted_iota(jnp.int32, (CHUNK, CHUNK), 0)
    col = lax.broadcasted_iota(jnp.int32, (CHUNK, CHUNK), 1)
    tri = (col <= row).astype(F32)
    scale = GLA_DK ** -0.5

    def chunk_body(c, carry):
        r = pl.ds(pl.multiple_of(c * CHUNK, CHUNK), CHUNK)
        b = _dot(tri, la_scr[r, :], precision=lax.Precision.HIGHEST)
        b_end = b[CHUNK - 1:CHUNK, :]
        k_dec = k_ref[r, :].astype(F32) * jnp.exp(b_end - b)
        decay = jnp.exp(b_end)
        f_cum = _dot(tri, lf_scr[r, :], precision=lax.Precision.HIGHEST) + fcar_scr[...]
        f_ref[r, :] = f_cum
        fcar_scr[...] = f_cum[CHUNK - 1:CHUNK, :]
        for h in range(GLA_HEADS):
            ks = slice(h * GLA_DK, (h + 1) * GLA_DK)
            vs = slice(h * GLA_DV, (h + 1) * GLA_DV)
            st = state_scr[h] * decay[:, ks] + _dot_tn(v_ref[r, vs], k_dec[:, ks].astype(BF16))
            state_scr[h] = st
            o = _dot_nt(q_ref[r, ks], st.astype(BF16)) * scale
            gt = gate_ref[r, vs].astype(F32)
            o_ref[r, vs] = (_rms(o, gout_ref[:, vs]) * (gt * jax.nn.sigmoid(gt))).astype(o_ref.dtype)
        return carry

    lax.fori_loop(0, n_chunks, chunk_body, 0)


def _gla(proj, small, wa2p, ba, bfv, gout, batch, seq, ts=512):
    n = proj.shape[0]
    nsb = seq // ts
    kw = GLA_HEADS * GLA_DK
    vw = GLA_HEADS * GLA_DV
    row = lambda b, s: b * nsb + s
    return pl.pallas_call(
        functools.partial(_gla_kernel, n_chunks=ts // CHUNK),
        grid=(batch, nsb),
        in_specs=[
            pl.BlockSpec((ts, kw), lambda b, s: (row(b, s), 0)),
            pl.BlockSpec((ts, kw), lambda b, s: (row(b, s), 1)),
            pl.BlockSpec((ts, vw), lambda b, s: (row(b, s), 1)),
            pl.BlockSpec((ts, vw), lambda b, s: (row(b, s), 2)),
            pl.BlockSpec((ts, LANES), lambda b, s: (row(b, s), 0)),
            pl.BlockSpec((LANES, kw), lambda b, s: (0, 0)),
            pl.BlockSpec((1, kw), lambda b, s: (0, 0)),
            pl.BlockSpec((1, LANES), lambda b, s: (0, 0)),
            pl.BlockSpec((1, vw), lambda b, s: (0, 0)),
        ],
        out_specs=[
            pl.BlockSpec((ts, vw), lambda b, s: (row(b, s), 0)),
            pl.BlockSpec((ts, LANES), lambda b, s: (row(b, s), 0)),
        ],
        out_shape=[
            jax.ShapeDtypeStruct((n, vw), BF16),
            jax.ShapeDtypeStruct((n, LANES), F32),
        ],
        scratch_shapes=[
            pltpu.VMEM((GLA_HEADS, GLA_DV, GLA_DK), F32),
            pltpu.VMEM((1, LANES), F32),
            pltpu.VMEM((ts, kw), F32),
            pltpu.VMEM((ts, LANES), F32),
        ],
        compiler_params=_cparams(("parallel", "arbitrary")),
        name="gla",
    )(proj, proj, proj, proj, small, wa2p, ba, bfv, gout)


LOG2E = 1.4426950408889634
def _fox_kernel(q_ref, k_ref, v_ref, fk_ref, g_ref, o_ref, m_scr, l_scr, acc_scr, *, tq, tk):
    i = pl.program_id(1)
    j = pl.program_id(2)

    @pl.when(j == 0)
    def _():
        m_scr[...] = jnp.full_like(m_scr, -jnp.inf)
        l_scr[...] = jnp.zeros_like(l_scr)
        acc_scr[...] = jnp.zeros_like(acc_scr)

    heads = range(FOX_HEADS)
    hs = [slice(h * FOX_DH, (h + 1) * FOX_DH) for h in heads]

    def update(on_diagonal):
        m_prev = [m_scr[h] for h in heads]
        l_prev = [l_scr[h] for h in heads]
        acc_prev = [acc_scr[h] for h in heads]
        s = [_dot_nt(q_ref[:, hs[h]], k_ref[:, hs[h]]) - fk_ref[h:h + 1, :] * LOG2E for h in heads]
        if on_diagonal:
            row = lax.broadcasted_iota(jnp.int32, (tq, tk), 0)
            col = lax.broadcasted_iota(jnp.int32, (tq, tk), 1)
            s = [jnp.where(col <= row, sh, -jnp.inf) for sh in s]
        m_new = [jnp.maximum(m_prev[h], jnp.max(s[h], axis=-1, keepdims=True)) for h in heads]
        alpha = [jnp.exp2(m_prev[h] - m_new[h]) for h in heads]
        p = [jnp.exp2(s[h] - m_new[h]) for h in heads]
        l_new = [alpha[h] * l_prev[h] + jnp.sum(p[h], axis=-1, keepdims=True) for h in heads]
        acc_new = [alpha[h] * acc_prev[h] + _dot(p[h].astype(BF16), v_ref[:, hs[h]]) for h in heads]
        return m_new, l_new, acc_new

    @pl.when(j < i)
    def _():
        m_new, l_new, acc_new = update(False)
        for h in heads:
            m_scr[h] = m_new[h]
            l_scr[h] = l_new[h]
            acc_scr[h] = acc_new[h]

    @pl.when(j == i)
    def _():
        _, l_new, acc_new = update(True)
        for h in heads:
            o_ref[:, hs[h]] = _rms(acc_new[h] / l_new[h], g_ref[:, hs[h]]).astype(o_ref.dtype)


def _fox(proj, f_t, g_fox, batch, seq, tq=512, tk=512):
    assert tq == tk, "the diagonal-block mask assumes square blocks"
    n = proj.shape[0]
    w = FOX_HEADS * FOX_DH
    nq, nk = seq // tq, seq // tk
    qcol, kcol, vcol = 3072 // w, 3584 // w, 4096 // w
    return pl.pallas_call(
        functools.partial(_fox_kernel, tq=tq, tk=tk),
        grid=(batch, nq, nk),
        in_specs=[
            pl.BlockSpec((tq, w), lambda b, i, j: (b * nq + i, qcol)),
            pl.BlockSpec((tk, w), lambda b, i, j: (b * nk + jnp.minimum(j, i), kcol)),
            pl.BlockSpec((tk, w), lambda b, i, j: (b * nk + jnp.minimum(j, i), vcol)),
            pl.BlockSpec((None, 8, tk), lambda b, i, j: (b, 0, jnp.minimum(j, i))),
            pl.BlockSpec((1, w), lambda b, i, j: (0, 0)),
        ],
        out_specs=pl.BlockSpec((tq, w), lambda b, i, j: (b * nq + i, 0)),
        out_shape=jax.ShapeDtypeStruct((n, w), BF16),
        scratch_shapes=[
            pltpu.VMEM((FOX_HEADS, tq, 1), F32),
            pltpu.VMEM((FOX_HEADS, tq, 1), F32),
            pltpu.VMEM((FOX_HEADS, tq, FOX_DH), F32),
        ],
        compiler_params=_cparams(("parallel", "parallel", "arbitrary")),
        name="fox",
    )(proj, proj, proj, f_t, g_fox)


def _mem_kv_kernel(m_ref, g_ref, w_ref, kv_ref):
    kv_ref[...] = _dot(_rms(m_ref[...], g_ref[...]).astype(BF16), w_ref[...]).astype(kv_ref.dtype)


def _mem_kv(mem2d, g, w, tm=256):
    n, d = mem2d.shape
    nw = w.shape[1]
    return pl.pallas_call(
        _mem_kv_kernel,
        grid=(n // tm,),
        in_specs=[
            pl.BlockSpec((tm, d), lambda i: (i, 0)),
            pl.BlockSpec((1, d), lambda i: (0, 0)),
            pl.BlockSpec((d, nw), lambda i: (0, 0)),
        ],
        out_specs=pl.BlockSpec((tm, nw), lambda i: (i, 0)),
        out_shape=jax.ShapeDtypeStruct((n, nw), BF16),
        compiler_params=_cparams(("parallel",)),
        name="mem_kv",
    )(mem2d, g, w)


def _mem_attn_kernel(q_ref, k_ref, v_ref, g_ref, o_ref):
    scale = MEM_DH ** -0.5
    for h in range(MEM_HEADS):
        hs = slice(h * MEM_DH, (h + 1) * MEM_DH)
        s = _dot_nt(q_ref[:, hs], k_ref[:, hs]) * scale
        p = jnp.exp(s - jnp.max(s, axis=-1, keepdims=True))
        l = jnp.sum(p, axis=-1, keepdims=True)
        o = _dot((p / l).astype(BF16), v_ref[:, hs])
        o_ref[:, hs] = _rms(o, g_ref[:, hs]).astype(o_ref.dtype)


def _mem_attn(proj, kv, g_mem_out, batch, seq, n_mem, tq=1024):
    n = proj.shape[0]
    w = MEM_HEADS * MEM_DH
    nq = seq // tq
    qcol = 4608 // w
    return pl.pallas_call(
        _mem_attn_kernel,
        grid=(n // tq,),
        in_specs=[
            pl.BlockSpec((tq, w), lambda i: (i, qcol)),
            pl.BlockSpec((n_mem, w), lambda i: (i // nq, 0)),
            pl.BlockSpec((n_mem, w), lambda i: (i // nq, 1)),
            pl.BlockSpec((1, w), lambda i: (0, 0)),
        ],
        out_specs=pl.BlockSpec((tq, w), lambda i: (i, 0)),
        out_shape=jax.ShapeDtypeStruct((n, w), BF16),
        compiler_params=_cparams(("parallel",)),
        name="mem_attn",
    )(proj, kv, kv, g_mem_out)


def _out_proj_kernel(x_ref, gla_ref, fox_ref, mem_ref, w1_ref, w2_ref, w3_ref, g_ref, wr_ref, br_ref,
                     x2_ref, xn_ref, logit_ref):
    x2 = (x_ref[...] + _dot(gla_ref[...], w1_ref[...]) + _dot(fox_ref[...], w2_ref[...])
          + _dot(mem_ref[...], w3_ref[...]))
    x2_ref[...] = x2
    xn = _rms(x2, g_ref[...])
    _store_rowmajor(xn_ref, _pack_bf16_pairs(xn))
    xh = xn.astype(BF16)
    xl = (xn - xh.astype(F32)).astype(BF16)
    hi = _dot(xh, wr_ref[...])
    logit_ref[...] = hi[:, :LANES] + hi[:, LANES:] + _dot(xl, wr_ref[:, :LANES]) + br_ref[...]


def _out_proj(x2d, gla, fox, memo, w_out, g_ffn, wr, br, tm=256):
    n, d = x2d.shape
    w1, w2 = gla.shape[1], fox.shape[1]
    const = lambda i: (0, 0)
    return pl.pallas_call(
        _out_proj_kernel,
        grid=(n // tm,),
        in_specs=[
            pl.BlockSpec((tm, d), lambda i: (i, 0)),
            pl.BlockSpec((tm, w1), lambda i: (i, 0)),
            pl.BlockSpec((tm, w2), lambda i: (i, 0)),
            pl.BlockSpec((tm, w2), lambda i: (i, 0)),
            pl.BlockSpec((w1, d), lambda i: (0, 0)),
            pl.BlockSpec((w2, d), lambda i: (w1 // w2, 0)),
            pl.BlockSpec((w2, d), lambda i: (w1 // w2 + 1, 0)),
            pl.BlockSpec((1, d), const),
            pl.BlockSpec((d, 2 * LANES), const),
            pl.BlockSpec((1, LANES), const),
        ],
        out_specs=[
            pl.BlockSpec((tm, d), lambda i: (i, 0)),
            pl.BlockSpec((tm * (d // 2 // LANES), LANES), lambda i: (i, 0)),
            pl.BlockSpec((tm, LANES), lambda i: (i, 0)),
        ],
        out_shape=[
            jax.ShapeDtypeStruct((n, d), F32),
            jax.ShapeDtypeStruct((n * (d // 2 // LANES), LANES), jnp.uint32),
            jax.ShapeDtypeStruct((n, LANES), F32),
        ],
        compiler_params=_cparams(("parallel",)),
        name="out_proj",
    )(x2d, gla, fox, memo, w_out, w_out, w_out, g_ffn, wr, br)


def _route_kernel(l_ref, idx_ref, gate_ref, rank_ref, cnt_ref, carry_scr, *, tb):
    @pl.when(pl.program_id(0) == 0)
    def _():
        carry_scr[...] = jnp.zeros_like(carry_scr)

    lane = lax.broadcasted_iota(jnp.int32, (tb, LANES), 1)
    logit = jnp.where(lane < N_EXPERTS, l_ref[...], -jnp.inf)
    vals, hots = [], []
    idx_out = jnp.zeros((tb, LANES), jnp.int32)
    for k in range(TOP_K):
        m = jnp.max(logit, axis=-1, keepdims=True)
        ik = jnp.min(jnp.where(logit == m, lane, LANES), axis=-1, keepdims=True)
        hot = lane == ik
        logit = jnp.where(hot, -jnp.inf, logit)
        vals.append(m)
        hots.append(hot)
        idx_out = jnp.where(lane == k, ik, idx_out)
    idx_ref[...] = idx_out

    e = [jnp.exp(v - vals[0]) for v in vals]
    den = e[0] + e[1] + e[2] + e[3]
    gate_out = jnp.zeros((tb, LANES), F32)
    for k in range(TOP_K):
        gate_out = jnp.where(lane == k, e[k] / den, gate_out)
    gate_ref[...] = gate_out

    member = jnp.zeros((tb, LANES), F32)
    for hot in hots:
        member = member + hot.astype(F32)
    row = lax.broadcasted_iota(jnp.int32, (tb, tb), 0)
    col = lax.broadcasted_iota(jnp.int32, (tb, tb), 1)
    before = (col < row).astype(BF16)
    rank = _dot(before, member.astype(BF16)) + carry_scr[...]
    rank_out = jnp.zeros((tb, LANES), F32)
    for k in range(TOP_K):
        rk = jnp.sum(jnp.where(hots[k], rank, 0.0), axis=-1, keepdims=True)
        rank_out = jnp.where(lane == k, rk, rank_out)
    rank_ref[...] = rank_out.astype(jnp.int32)
    total = carry_scr[...] + jnp.sum(member, axis=0, keepdims=True)
    carry_scr[...] = total
    cnt_ref[...] = total.astype(jnp.int32)


def _route(logits, tb=512):
    n = logits.shape[0]
    blk = pl.BlockSpec((tb, LANES), lambda i: (i, 0))
    return pl.pallas_call(
        functools.partial(_route_kernel, tb=tb),
        grid=(n // tb,),
        in_specs=[blk],
        out_specs=[blk, blk, blk, pl.BlockSpec((1, LANES), lambda i: (0, 0))],
        out_shape=[
            jax.ShapeDtypeStruct((n, LANES), jnp.int32),
            jax.ShapeDtypeStruct((n, LANES), F32),
            jax.ShapeDtypeStruct((n, LANES), jnp.int32),
            jax.ShapeDtypeStruct((1, LANES), jnp.int32),
        ],
        scratch_shapes=[pltpu.VMEM((1, LANES), F32)],
        compiler_params=_cparams(("arbitrary",)),
        name="route",
    )(logits)


def _dispatch_kernel(zblk_ref, zok_ref, used_ref, pos_ref, xn_ref, xr_hbm, zero_scr, sem,
                     *, tb, pitch, n_blocks_max):
    blk_rows = MOE_TM * pitch

    def zero_copy(b):
        dst = xr_hbm.at[pl.ds(pl.multiple_of(b * blk_rows, blk_rows), blk_rows), :]
        return pltpu.make_async_copy(zero_scr, dst, sem.at[1])

    @pl.when(pl.program_id(0) == 0)
    def _():
        zero_scr[...] = jnp.zeros_like(zero_scr)

        def per_expert(action):
            def body(e, carry):
                @pl.when(zok_ref[e] == 1)
                def _():
                    action(zero_copy(zblk_ref[e]))
                return carry
            lax.fori_loop(0, N_EXPERTS, body, 0)

        def per_tail(action):
            def body(b, carry):
                action(zero_copy(b))
                return carry
            lax.fori_loop(used_ref[0], n_blocks_max, body, 0)

        per_expert(lambda c: c.start())
        per_tail(lambda c: c.start())
        per_expert(lambda c: c.wait())
        per_tail(lambda c: c.wait())

    def issue(t, carry):
        src = xn_ref.at[pl.ds(pl.multiple_of(t * pitch, pitch), pitch), :]
        for k in range(TOP_K):
            p = pos_ref[0, t * TOP_K + k]
            dst = xr_hbm.at[pl.ds(pl.multiple_of(p * pitch, pitch), pitch), :]
            pltpu.make_async_copy(src, dst, sem.at[0]).start(priority=k % 2)
        return carry

    lax.fori_loop(0, tb, issue, 0, unroll=4)
    for k in range(TOP_K):
        pltpu.make_async_copy(xn_ref, xr_hbm.at[pl.ds(0, tb * pitch), :], sem.at[0]).wait()


def _dispatch(xn_rm, pos, zblk, zok, n_used, n_blocks_max, pitch, tb=256):
    n = xn_rm.shape[0] // pitch
    nb = n // tb
    grid_spec = pltpu.PrefetchScalarGridSpec(
        num_scalar_prefetch=3,
        grid=(nb,),
        in_specs=[
            pl.BlockSpec((None, 1, TOP_K * tb), lambda i, *_: (i, 0, 0), memory_space=pltpu.SMEM),
            pl.BlockSpec((tb * pitch, LANES), lambda i, *_: (i, 0)),
        ],
        out_specs=pl.BlockSpec(memory_space=pl.ANY),
        scratch_shapes=[pltpu.VMEM((MOE_TM * pitch, LANES), xn_rm.dtype), pltpu.SemaphoreType.DMA((2,))],
    )
    return pl.pallas_call(
        functools.partial(_dispatch_kernel, tb=tb, pitch=pitch, n_blocks_max=n_blocks_max),
        grid_spec=grid_spec,
        out_shape=jax.ShapeDtypeStruct((n_blocks_max * MOE_TM * pitch, LANES), xn_rm.dtype),
        compiler_params=_cparams(("arbitrary",)),
        name="dispatch",
    )(zblk, zok, n_used, pos.reshape(nb, 1, TOP_K * tb), xn_rm)


def _cast_rows(src, dst, rows_per=256):
    def body(i, carry):
        r = pl.ds(pl.multiple_of(i * rows_per, rows_per), rows_per)
        dst[r, :] = src[r, :].astype(dst.dtype)
        return carry

    lax.fori_loop(0, src.shape[0] // rows_per, body, 0)


def _stream_row_blocks(cnt, in_copy, out_copy, compute):
    @pl.when(cnt > 0)
    def _():
        in_copy(0, 0).start()

    def body(b, carry):
        slot = b % 2

        @pl.when(b + 1 < cnt)
        def _():
            in_copy(b + 1, 1 - slot).start()

        in_copy(b, slot).wait()

        @pl.when(b >= 2)
        def _():
            out_copy(b - 2, slot).wait()

        compute(slot)
        out_copy(b, slot).start()
        return carry

    lax.fori_loop(0, cnt, body, 0)

    @pl.when(cnt >= 2)
    def _():
        out_copy(cnt - 2, cnt % 2).wait()

    @pl.when(cnt >= 1)
    def _():
        out_copy(cnt - 1, (cnt - 1) % 2).wait()


def _zero_fill_blocks(first, last, zero_src, dst_copy):
    def start(b, carry):
        dst_copy(b).start()
        return carry

    def wait(b, carry):
        dst_copy(b).wait()
        return carry

    zero_src[...] = jnp.zeros_like(zero_src)
    lax.fori_loop(first, last, start, 0)
    lax.fori_loop(first, last, wait, 0)


def _moe_up_kernel(bs_ref, bc_ref, used_ref, x_hbm, w_hbm, bg_ref, bu_ref, act_hbm,
                   wf_scr, wb_scr, xbuf, lhs_scr, obuf, sem_w, sem_x, sem_o,
                   *, nt, tn, d_ff, pitch, n_blocks_max):
    g = pl.program_id(0)
    ng = pl.num_programs(0)
    e = g // nt
    col = pl.multiple_of((g % nt) * tn, tn)
    blk_rows = MOE_TM * pitch

    def w_copies(step):
        ee = step // nt
        cc = pl.multiple_of((step % nt) * tn, tn)
        return (pltpu.make_async_copy(w_hbm.at[ee, :, pl.ds(cc, tn)], wf_scr.at[0], sem_w.at[0]),
                pltpu.make_async_copy(w_hbm.at[ee, :, pl.ds(d_ff + cc, tn)], wf_scr.at[1], sem_w.at[1]))

    @pl.when(g == 0)
    def _():
        for c in w_copies(0):
            c.start(priority=1)

    for half, c in enumerate(w_copies(g)):
        c.wait()
        _cast_rows(wf_scr.at[half], wb_scr.at[half])

    @pl.when(g + 1 < ng)
    def _():
        for c in w_copies(g + 1):
            c.start(priority=1)

    start = bs_ref[e]

    def x_copy(b, slot):
        r0 = pl.multiple_of((start + b) * blk_rows, blk_rows)
        return pltpu.make_async_copy(x_hbm.at[pl.ds(r0, blk_rows), :], xbuf.at[slot], sem_x.at[slot])

    def o_copy(b, slot):
        r0 = pl.multiple_of((start + b) * MOE_TM, MOE_TM)
        return pltpu.make_async_copy(obuf.at[slot], act_hbm.at[pl.ds(r0, MOE_TM), pl.ds(col, tn)],
                                     sem_o.at[slot])

    def compute(slot):
        half = pitch * LANES
        for c in range(pitch):
            w = xbuf[slot, pl.ds(c, MOE_TM, stride=pitch), :]
            lhs_scr[:, c * LANES:(c + 1) * LANES] = _unpack_lo(w).astype(BF16)
            lhs_scr[:, half + c * LANES:half + (c + 1) * LANES] = _unpack_hi(w).astype(BF16)
        x = lhs_scr[...]
        gate = jnp.minimum(_dot(x, wb_scr[0]) + bg_ref[...], SWIGLU_LIMIT)
        up = jnp.clip(_dot(x, wb_scr[1]) + bu_ref[...], -SWIGLU_LIMIT, SWIGLU_LIMIT)
        obuf[slot] = (gate * jax.nn.sigmoid(SWIGLU_ALPHA * gate) * (up + 1.0)).astype(obuf.dtype)

    _stream_row_blocks(bc_ref[e], x_copy, o_copy, compute)

    @pl.when(e == N_EXPERTS - 1)
    def _():
        def tail_copy(b):
            r0 = pl.multiple_of(b * MOE_TM, MOE_TM)
            return pltpu.make_async_copy(obuf.at[0], act_hbm.at[pl.ds(r0, MOE_TM), pl.ds(col, tn)],
                                         sem_o.at[0])
        _zero_fill_blocks(used_ref[0], n_blocks_max, obuf.at[0], tail_copy)


def _moe_up(x_rows_rm, w_up, b_up, blk_start, blk_count, n_used, n_blocks_max, tn=1024):
    n_exp, d, two_ff = w_up.shape
    d_ff = two_ff // 2
    pitch = d // 2 // LANES
    nt = d_ff // tn
    up_off = d_ff // tn
    grid_spec = pltpu.PrefetchScalarGridSpec(
        num_scalar_prefetch=3,
        grid=(n_exp * nt,),
        in_specs=[
            pl.BlockSpec(memory_space=pl.ANY),
            pl.BlockSpec(memory_space=pl.ANY),
            pl.BlockSpec((None, 1, tn), lambda g, *_: (g // nt, 0, g % nt)),
            pl.BlockSpec((None, 1, tn), lambda g, *_: (g // nt, 0, g % nt + up_off)),
        ],
        out_specs=pl.BlockSpec(memory_space=pl.ANY),
        scratch_shapes=[
            pltpu.VMEM((2, d, tn), F32),
            pltpu.VMEM((2, d, tn), BF16),
            pltpu.VMEM((2, MOE_TM * pitch, LANES), jnp.uint32),
            pltpu.VMEM((MOE_TM, d), BF16),
            pltpu.VMEM((2, MOE_TM, tn), BF16),
            pltpu.SemaphoreType.DMA((2,)),
            pltpu.SemaphoreType.DMA((2,)),
            pltpu.SemaphoreType.DMA((2,)),
        ],
    )
    return pl.pallas_call(
        functools.partial(_moe_up_kernel, nt=nt, tn=tn, d_ff=d_ff, pitch=pitch, n_blocks_max=n_blocks_max),
        grid_spec=grid_spec,
        out_shape=jax.ShapeDtypeStruct((n_blocks_max * MOE_TM, d_ff), BF16),
        compiler_params=_cparams(("arbitrary",)),
        name="moe_up",
    )(blk_start, blk_count, n_used, x_rows_rm, w_up, b_up, b_up)


def _moe_down_kernel(bs_ref, bc_ref, used_ref, a_hbm, w_hbm, b_ref, y_hbm,
                     wf_scr, wb_scr, abuf, obuf, sem_w, sem_a, sem_o, *, pitch, n_blocks_max):
    e = pl.program_id(0)
    blk_rows = MOE_TM * pitch

    def w_copy(ee):
        return pltpu.make_async_copy(w_hbm.at[ee], wf_scr, sem_w.at[0])

    @pl.when(e == 0)
    def _():
        w_copy(0).start(priority=1)

    w_copy(e).wait()
    _cast_rows(wf_scr, wb_scr)

    @pl.when(e + 1 < pl.num_programs(0))
    def _():
        w_copy(e + 1).start(priority=1)

    start = bs_ref[e]

    def a_copy(b, slot):
        r0 = pl.multiple_of((start + b) * MOE_TM, MOE_TM)
        return pltpu.make_async_copy(a_hbm.at[pl.ds(r0, MOE_TM), :], abuf.at[slot], sem_a.at[slot])

    def o_copy(b, slot):
        r0 = pl.multiple_of((start + b) * blk_rows, blk_rows)
        return pltpu.make_async_copy(obuf.at[slot], y_hbm.at[pl.ds(r0, blk_rows), :], sem_o.at[slot])

    def compute(slot):
        y = _dot(abuf[slot], wb_scr[...]) + b_ref[...]
        _store_rowmajor(obuf.at[slot], _pack_bf16_pairs(y))

    _stream_row_blocks(bc_ref[e], a_copy, o_copy, compute)

    @pl.when(e == N_EXPERTS - 1)
    def _():
        def tail_copy(b):
            r0 = pl.multiple_of(b * blk_rows, blk_rows)
            return pltpu.make_async_copy(obuf.at[0], y_hbm.at[pl.ds(r0, blk_rows), :], sem_o.at[0])
        _zero_fill_blocks(used_ref[0], n_blocks_max, obuf.at[0], tail_copy)


def _moe_down(act, w_down, b_down, blk_start, blk_count, n_used, n_blocks_max):
    n_rows, d_ff = act.shape
    n_exp, _, d = w_down.shape
    pitch = d // 2 // LANES
    grid_spec = pltpu.PrefetchScalarGridSpec(
        num_scalar_prefetch=3,
        grid=(n_exp,),
        in_specs=[
            pl.BlockSpec(memory_space=pl.ANY),
            pl.BlockSpec(memory_space=pl.ANY),
            pl.BlockSpec((None, 1, d), lambda e, *_: (e, 0, 0)),
        ],
        out_specs=pl.BlockSpec(memory_space=pl.ANY),
        scratch_shapes=[
            pltpu.VMEM((d_ff, d), F32),
            pltpu.VMEM((d_ff, d), BF16),
            pltpu.VMEM((2, MOE_TM, d_ff), BF16),
            pltpu.VMEM((2, MOE_TM * pitch, LANES), jnp.uint32),
            pltpu.SemaphoreType.DMA((1,)),
            pltpu.SemaphoreType.DMA((2,)),
            pltpu.SemaphoreType.DMA((2,)),
        ],
    )
    return pl.pallas_call(
        functools.partial(_moe_down_kernel, pitch=pitch, n_blocks_max=n_blocks_max),
        grid_spec=grid_spec,
        out_shape=jax.ShapeDtypeStruct((n_rows * pitch, LANES), jnp.uint32),
        compiler_params=_cparams(("arbitrary",)),
        name="moe_down",
    )(blk_start, blk_count, n_used, act, w_down, b_down)


def _combine_kernel(pos0_ref, posn_ref, y_hbm, x2_ref, gate_ref, g_ref, o_ref, buf, sem, *, tb, pitch):
    i = pl.program_id(0)
    n_rows = TOP_K * tb

    def issue_block(p_ref, slot):
        def issue(j, carry):
            for half in range(2):
                r = 2 * j + half
                src = y_hbm.at[pl.ds(pl.multiple_of(p_ref[0, r] * pitch, pitch), pitch), :]
                d0 = pl.multiple_of((slot * n_rows + r) * pitch, pitch)
                pltpu.make_async_copy(src, buf.at[pl.ds(d0, pitch), :], sem.at[slot]).start(priority=half)
            return carry
        lax.fori_loop(0, n_rows // 2, issue, 0, unroll=4)

    @pl.when(i == 0)
    def _():
        issue_block(pos0_ref, 0)

    @pl.when(i + 1 < pl.num_programs(0))
    def _():
        issue_block(posn_ref, (i + 1) % 2)

    slot = i % 2
    base = pl.multiple_of(slot * (n_rows * pitch), pitch)
    pltpu.make_async_copy(y_hbm.at[pl.ds(0, n_rows * pitch), :], buf.at[pl.ds(base, n_rows * pitch), :],
                          sem.at[slot]).wait()
    gates = gate_ref[...]
    half = pitch * LANES
    ssq = jnp.zeros((tb, 1), F32)
    for c in range(pitch):
        lo = slice(c * LANES, (c + 1) * LANES)
        hi = slice(half + c * LANES, half + (c + 1) * LANES)
        z_lo = x2_ref[:, lo]
        z_hi = x2_ref[:, hi]
        for k in range(TOP_K):
            w = buf[pl.ds(base + k * tb * pitch + c, tb, stride=pitch), :]
            z_lo = z_lo + gates[:, k:k + 1] * _unpack_lo(w)
            z_hi = z_hi + gates[:, k:k + 1] * _unpack_hi(w)
        o_ref[:, lo] = z_lo
        o_ref[:, hi] = z_hi
        ssq = ssq + jnp.sum(z_lo * z_lo, axis=-1, keepdims=True) + jnp.sum(z_hi * z_hi, axis=-1, keepdims=True)
    o_ref[...] = o_ref[...] * lax.rsqrt(ssq * (1.0 / (2 * half)) + EPS) * g_ref[...]


def _combine(y_rows_rm, pos, gates, x2, g_final, tb=256):
    n, d = x2.shape
    nb = n // tb
    pitch = d // 2 // LANES
    pos_blk = pos.reshape(nb, tb, TOP_K).transpose(0, 2, 1).reshape(nb, 1, TOP_K * tb)
    pos_spec = lambda imap: pl.BlockSpec((None, 1, TOP_K * tb), imap, memory_space=pltpu.SMEM)
    return pl.pallas_call(
        functools.partial(_combine_kernel, tb=tb, pitch=pitch),
        grid=(nb,),
        in_specs=[
            pos_spec(lambda i: (0, 0, 0)),
            pos_spec(lambda i: (jnp.minimum(i + 1, nb - 1), 0, 0)),
            pl.BlockSpec(memory_space=pl.ANY),
            pl.BlockSpec((tb, d), lambda i: (i, 0)),
            pl.BlockSpec((tb, LANES), lambda i: (i, 0)),
            pl.BlockSpec((1, d), lambda i: (0, 0)),
        ],
        out_specs=pl.BlockSpec((tb, d), lambda i: (i, 0)),
        out_shape=jax.ShapeDtypeStruct((n, d), F32),
        scratch_shapes=[pltpu.VMEM((2 * TOP_K * tb * pitch, LANES), jnp.uint32),
                        pltpu.SemaphoreType.DMA((2,))],
        compiler_params=_cparams(("arbitrary",)),
        name="combine",
    )(pos_blk, pos_blk, y_rows_rm, x2, gates, g_final)


def _routing_tables(idx, rank, cnt):
    counts = cnt[0, :N_EXPERTS]
    blk_count = (counts + MOE_TM - 1) // MOE_TM
    blk_end = jnp.cumsum(blk_count)
    blk_start = blk_end - blk_count
    hot = idx[:, :TOP_K, None] == jnp.arange(N_EXPERTS, dtype=jnp.int32)
    pos = jnp.sum(jnp.where(hot, blk_start * MOE_TM, 0), axis=-1) + rank[:, :TOP_K]
    return pos.astype(jnp.int32), blk_start.astype(jnp.int32), blk_count.astype(jnp.int32)


def kernel(x, mem, g_attn_norm, g_mem_norm, w_in, w_gla_a2, b_gla_a, g_gla_out, b_fox_f, g_fox_out,
           w_mem_kv, g_mem_out, w_out, g_ffn_norm, w_router, b_router, w_moe_up, b_moe_up,
           w_moe_down, b_moe_down, g_final):
    batch, seq, d = x.shape
    n_mem = mem.shape[1]
    n = batch * seq
    depth = w_in.shape[0]
    assert depth == 1, "the combine kernel applies the final norm, so exactly one layer is supported"
    kw = GLA_HEADS * GLA_DK
    vw = GLA_HEADS * GLA_DV
    fw = FOX_HEADS * FOX_DH
    mw = MEM_HEADS * MEM_DH
    o_q, o_k, o_v, o_g = 0, kw, 2 * kw, 2 * kw + vw
    o_a = o_g + vw
    o_fq = o_a + GLA_LOWRANK
    o_fk, o_fv = o_fq + fw, o_fq + 2 * fw
    o_ff = o_fq + 3 * fw
    o_mq = o_ff + FOX_HEADS
    f_lane = GLA_LOWRANK

    xf = x.reshape(n, d)
    for l in range(depth):
        wi = w_in[l]
        fox_q_scale = FOX_DH ** -0.5 * LOG2E
        wa = jnp.concatenate([wi[:, o_q:o_a], wi[:, o_fq:o_fk] * fox_q_scale, wi[:, o_fk:o_ff],
                              wi[:, o_mq:o_mq + mw]], axis=1).astype(BF16)
        wb = jnp.zeros((d, LANES), F32).at[:, :GLA_LOWRANK].set(wi[:, o_a:o_fq])
        wb = wb.at[:, f_lane:f_lane + FOX_HEADS].set(wi[:, o_ff:o_mq]).astype(BF16)
        proj, small = _in_proj(xf, g_attn_norm[l].reshape(1, d), wa, wb)

        wa2p = jnp.zeros((LANES, kw), F32).at[:GLA_LOWRANK].set(w_gla_a2[l]).astype(BF16)
        bfv = jnp.zeros((1, LANES), F32).at[0, f_lane:f_lane + FOX_HEADS].set(b_fox_f[l])
        gla, f_cum = _gla(proj, small, wa2p, b_gla_a[l].reshape(1, kw), bfv,
                          g_gla_out[l].reshape(1, vw), batch, seq)

        f_t = f_cum.reshape(batch, seq, LANES)[:, :, f_lane:f_lane + FOX_HEADS].transpose(0, 2, 1)
        f_t = jnp.concatenate([f_t, jnp.zeros_like(f_t)], axis=1)
        fox = _fox(proj, f_t, g_fox_out[l].reshape(1, fw), batch, seq)

        kv = _mem_kv(mem.reshape(batch * n_mem, d), g_mem_norm[l].reshape(1, d), w_mem_kv[l].astype(BF16))
        memo = _mem_attn(proj, kv, g_mem_out[l].reshape(1, mw), batch, seq, n_mem)

        wr = jnp.zeros((d, LANES), F32).at[:, :N_EXPERTS].set(w_router[l])
        wr_hi = wr.astype(BF16)
        wr = jnp.concatenate([wr_hi, (wr - wr_hi.astype(F32)).astype(BF16)], axis=1)
        br =jnp.zeros((1, LANES), F32).at[0, :N_EXPERTS].set(b_router[l])
        x2, xn, logits = _out_proj(xf, gla, fox, memo, w_out[l].astype(BF16),
                                   g_ffn_norm[l].reshape(1, d), wr, br)

        idx, gates, rank, cnt = _route(logits)
        pos, blk_start, blk_count = _routing_tables(idx, rank, cnt)
        n_blocks_max = -(-(n * TOP_K + N_EXPERTS * (MOE_TM - 1)) // MOE_TM)
        blk_end = blk_start + blk_count
        n_used = blk_end[-1:]
        x_rows = _dispatch(xn, pos, blk_end - 1, (blk_count > 0).astype(jnp.int32), n_used,
                           n_blocks_max, d // 2 // LANES)
        d_ff = w_moe_up.shape[3] // 2
        act = _moe_up(x_rows, w_moe_up[l], b_moe_up[l].reshape(N_EXPERTS, 1, 2 * d_ff),
                      blk_start, blk_count, n_used, n_blocks_max)
        y_rows = _moe_down(act, w_moe_down[l], b_moe_down[l].reshape(N_EXPERTS, 1, d),
                           blk_start, blk_count, n_used, n_blocks_max)
        xf = _combine(y_rows, pos, gates, x2, g_final.reshape(1, d))
    return xf.reshape(batch, seq, d)
```

```python
import functools

import jax
import jax.numpy as jnp
from jax import lax
from jax.experimental import pallas as pl
from jax.experimental.pallas import tpu as pltpu

EPS = 1e-5
CHUNK = 64
GLA_HEADS = 4
GLA_DK = 128
GLA_DV = 256
GLA_LOWRANK = 16
GLA_TAU = 16.0
FOX_HEADS = 4
FOX_DH = 128
MEM_HEADS = 4
MEM_DH = 128
N_EXPERTS = 32
TOP_K = 4
SWIGLU_LIMIT = 7.0
SWIGLU_ALPHA = 1.702
LANES = 128
MOE_TM = 256
MOE_NBUF = 3
VMEM_LIMIT = 56 * 1024 * 1024

F32 = jnp.float32
BF16 = jnp.bfloat16


def _cparams(sem, vmem=VMEM_LIMIT):
    return pltpu.CompilerParams(dimension_semantics=sem, vmem_limit_bytes=vmem)


def _log_sigmoid(x):
    return jnp.minimum(x, 0.0) - jnp.log1p(jnp.exp(-jnp.abs(x)))


def _rms(x, g):
    return x * lax.rsqrt(jnp.mean(x * x, axis=-1, keepdims=True) + EPS) * g


def _dot(a, b, **kw):
    return jnp.dot(a, b, preferred_element_type=F32, **kw)


def _dot_nt(a, b):
    return lax.dot_general(a, b, (((1,), (1,)), ((), ())), preferred_element_type=F32)


def _dot_tn(a, b):
    return lax.dot_general(a, b, (((0,), (0,)), ((), ())), preferred_element_type=F32)


def _store_rowmajor(ref, x, base=0):
    rows, w = x.shape
    pitch = w // LANES
    for c in range(pitch):
        ref[pl.ds(base + c, rows, stride=pitch), :] = x[:, c * LANES:(c + 1) * LANES]


def _pack_bf16_pairs(x):
    half = x.shape[1] // 2
    bits = lambda v: lax.bitcast_convert_type(v.astype(BF16).astype(F32), jnp.uint32)
    return bits(x[:, half:]) | (bits(x[:, :half]) >> 16)


def _unpack_lo(w):
    return lax.bitcast_convert_type(w << 16, F32)


def _unpack_hi(w):
    return lax.bitcast_convert_type(w & jnp.uint32(0xFFFF0000), F32)


def _in_proj_kernel(x_ref, g_ref, wa_ref, wb_ref, proj_ref, small_ref, h_scr):
    @pl.when(pl.program_id(1) == 0)
    def _():
        hb = _rms(x_ref[...], g_ref[...]).astype(BF16)
        h_scr[...] = hb
        small_ref[...] = _dot(hb, wb_ref[...])

    proj_ref[...] = _dot(h_scr[...], wa_ref[...]).astype(proj_ref.dtype)


def _in_proj(x2d, g, wa, wb, tm=1024, tn=1280):
    n, d = x2d.shape
    na = wa.shape[1]
    return pl.pallas_call(
        _in_proj_kernel,
        grid=(n // tm, na // tn),
        in_specs=[
            pl.BlockSpec((tm, d), lambda i, j: (i, 0)),
            pl.BlockSpec((1, d), lambda i, j: (0, 0)),
            pl.BlockSpec((d, tn), lambda i, j: (0, j)),
            pl.BlockSpec((d, LANES), lambda i, j: (0, 0)),
        ],
        out_specs=[
            pl.BlockSpec((tm, tn), lambda i, j: (i, j)),
            pl.BlockSpec((tm, LANES), lambda i, j: (i, 0)),
        ],
        out_shape=[
            jax.ShapeDtypeStruct((n, na), BF16),
            jax.ShapeDtypeStruct((n, LANES), F32),
        ],
        scratch_shapes=[pltpu.VMEM((tm, d), BF16)],
        compiler_params=_cparams(("parallel", "arbitrary")),
        name="in_proj",
    )(x2d, g, wa, wb)


def _gla_kernel(q_ref, k_ref, v_ref, gate_ref, small_ref, wa2_ref, ba_ref, bf_ref, gout_ref,
                o_ref, f_ref, state_scr, fcar_scr, la_scr, lf_scr, *, n_chunks):
    @pl.when(pl.program_id(1) == 0)
    def _():
        state_scr[...] = jnp.zeros_like(state_scr)
        fcar_scr[...] = jnp.zeros_like(fcar_scr)

    small = small_ref[...]
    la_scr[...] = _log_sigmoid(_dot(small.astype(BF16), wa2_ref[...]) + ba_ref[...]) * (1.0 / GLA_TAU)
    lf_scr[...] = _log_sigmoid(small + bf_ref[...])
    row = lax.broadcasted_iota(jnp.int32, (CHUNK, CHUNK), 0)
    col = lax.broadcasted_iota(jnp.int32, (CHUNK, CHUNK), 1)
    tri = (col <= row).astype(F32)
    scale = GLA_DK ** -0.5

    def chunk_body(c, carry):
        r = pl.ds(pl.multiple_of(c * CHUNK, CHUNK), CHUNK)
        b = _dot(tri, la_scr[r, :], precision=lax.Precision.HIGHEST)
        b_end = b[CHUNK - 1:CHUNK, :]
        k_dec = k_ref[r, :].astype(F32) * jnp.exp(b_end - b)
        decay = jnp.exp(b_end)
        f_cum = _dot(tri, lf_scr[r, :], precision=lax.Precision.HIGHEST) + fcar_scr[...]
        f_ref[r, :] = f_cum
        fcar_scr[...] = f_cum[CHUNK - 1:CHUNK, :]
        heads = range(GLA_HEADS)
        ks = [slice(h * GLA_DK, (h + 1) * GLA_DK) for h in heads]
        vs = [slice(h * GLA_DV, (h + 1) * GLA_DV) for h in heads]
        old = [state_scr[h] for h in heads]
        new = [old[h] * decay[:, ks[h]] + _dot_tn(v_ref[r, vs[h]], k_dec[:, ks[h]].astype(BF16))
               for h in heads]
        for h in heads:
            state_scr[h] = new[h]
        for h in heads:
            o = _dot_nt(q_ref[r, ks[h]], new[h].astype(BF16)) * scale
            gt = gate_ref[r, vs[h]].astype(F32)
            o_ref[r, vs[h]] = (_rms(o, gout_ref[:, vs[h]]) * (gt * jax.nn.sigmoid(gt))).astype(o_ref.dtype)
        return carry

    lax.fori_loop(0, n_chunks, chunk_body, 0, unroll=4)


def _gla(proj, small, wa2p, ba, bfv, gout, batch, seq, ts=512):
    n = proj.shape[0]
    nsb = seq // ts
    kw = GLA_HEADS * GLA_DK
    vw = GLA_HEADS * GLA_DV
    row = lambda b, s: b * nsb + s
    return pl.pallas_call(
        functools.partial(_gla_kernel, n_chunks=ts // CHUNK),
        grid=(batch, nsb),
        in_specs=[
            pl.BlockSpec((ts, kw), lambda b, s: (row(b, s), 0)),
            pl.BlockSpec((ts, kw), lambda b, s: (row(b, s), 1)),
            pl.BlockSpec((ts, vw), lambda b, s: (row(b, s), 1)),
            pl.BlockSpec((ts, vw), lambda b, s: (row(b, s), 2)),
            pl.BlockSpec((ts, LANES), lambda b, s: (row(b, s), 0)),
            pl.BlockSpec((LANES, kw), lambda b, s: (0, 0)),
            pl.BlockSpec((1, kw), lambda b, s: (0, 0)),
            pl.BlockSpec((1, LANES), lambda b, s: (0, 0)),
            pl.BlockSpec((1, vw), lambda b, s: (0, 0)),
        ],
        out_specs=[
            pl.BlockSpec((ts, vw), lambda b, s: (row(b, s), 0)),
            pl.BlockSpec((ts, LANES), lambda b, s: (row(b, s), 0)),
        ],
        out_shape=[
            jax.ShapeDtypeStruct((n, vw), BF16),
            jax.ShapeDtypeStruct((n, LANES), F32),
        ],
        scratch_shapes=[
            pltpu.VMEM((GLA_HEADS, GLA_DV, GLA_DK), F32),
            pltpu.VMEM((1, LANES), F32),
            pltpu.VMEM((ts, kw), F32),
            pltpu.VMEM((ts, LANES), F32),
        ],
        compiler_params=_cparams(("parallel", "arbitrary")),
        name="gla",
    )(proj, proj, proj, proj, small, wa2p, ba, bfv, gout)


LOG2E = 1.4426950408889634
def _fox_kernel(q_ref, k_ref, v_ref, fk_ref, g_ref, o_ref, m_scr, l_scr, acc_scr, *, tq, tk):
    i = pl.program_id(1)
    j = pl.program_id(2)

    @pl.when(j == 0)
    def _():
        m_scr[...] = jnp.full_like(m_scr, -jnp.inf)
        l_scr[...] = jnp.zeros_like(l_scr)
        acc_scr[...] = jnp.zeros_like(acc_scr)

    heads = range(FOX_HEADS)
    hs = [slice(h * FOX_DH, (h + 1) * FOX_DH) for h in heads]

    def update(on_diagonal):
        m_prev = [m_scr[h] for h in heads]
        l_prev = [l_scr[h] for h in heads]
        acc_prev = [acc_scr[h] for h in heads]
        s = [_dot_nt(q_ref[:, hs[h]], k_ref[:, hs[h]]) - fk_ref[h:h + 1, :] * LOG2E for h in heads]
        if on_diagonal:
            row = lax.broadcasted_iota(jnp.int32, (tq, tk), 0)
            col = lax.broadcasted_iota(jnp.int32, (tq, tk), 1)
            s = [jnp.where(col <= row, sh, -jnp.inf) for sh in s]
        m_new = [jnp.maximum(m_prev[h], jnp.max(s[h], axis=-1, keepdims=True)) for h in heads]
        alpha = [jnp.exp2(m_prev[h] - m_new[h]) for h in heads]
        p = [jnp.exp2(s[h] - m_new[h]) for h in heads]
        l_new = [alpha[h] * l_prev[h] + jnp.sum(p[h], axis=-1, keepdims=True) for h in heads]
        acc_new = [alpha[h] * acc_prev[h] + _dot(p[h].astype(BF16), v_ref[:, hs[h]]) for h in heads]
        return m_new, l_new, acc_new

    @pl.when(j < i)
    def _():
        m_new, l_new, acc_new = update(False)
        for h in heads:
            m_scr[h] = m_new[h]
            l_scr[h] = l_new[h]
            acc_scr[h] = acc_new[h]

    @pl.when(j == i)
    def _():
        _, l_new, acc_new = update(True)
        for h in heads:
            o_ref[:, hs[h]] = _rms(acc_new[h] / l_new[h], g_ref[:, hs[h]]).astype(o_ref.dtype)


def _fox(proj, f_t, g_fox, batch, seq, tq=512, tk=512):
    assert tq == tk, "the diagonal-block mask assumes square blocks"
    n = proj.shape[0]
    w = FOX_HEADS * FOX_DH
    nq, nk = seq // tq, seq // tk
    qcol, kcol, vcol = 3072 // w, 3584 // w, 4096 // w
    return pl.pallas_call(
        functools.partial(_fox_kernel, tq=tq, tk=tk),
        grid=(batch, nq, nk),
        in_specs=[
            pl.BlockSpec((tq, w), lambda b, i, j: (b * nq + i, qcol)),
            pl.BlockSpec((tk, w), lambda b, i, j: (b * nk + jnp.minimum(j, i), kcol)),
            pl.BlockSpec((tk, w), lambda b, i, j: (b * nk + jnp.minimum(j, i), vcol)),
            pl.BlockSpec((None, 8, tk), lambda b, i, j: (b, 0, jnp.minimum(j, i))),
            pl.BlockSpec((1, w), lambda b, i, j: (0, 0)),
        ],
        out_specs=pl.BlockSpec((tq, w), lambda b, i, j: (b * nq + i, 0)),
        out_shape=jax.ShapeDtypeStruct((n, w), BF16),
        scratch_shapes=[
            pltpu.VMEM((FOX_HEADS, tq, 1), F32),
            pltpu.VMEM((FOX_HEADS, tq, 1), F32),
            pltpu.VMEM((FOX_HEADS, tq, FOX_DH), F32),
        ],
        compiler_params=_cparams(("parallel", "parallel", "arbitrary")),
        name="fox",
    )(proj, proj, proj, f_t, g_fox)


def _mem_kv_kernel(m_ref, g_ref, w_ref, kv_ref):
    kv_ref[...] = _dot(_rms(m_ref[...], g_ref[...]).astype(BF16), w_ref[...]).astype(kv_ref.dtype)


def _mem_kv(mem2d, g, w, tm=256):
    n, d = mem2d.shape
    nw = w.shape[1]
    return pl.pallas_call(
        _mem_kv_kernel,
        grid=(n // tm,),
        in_specs=[
            pl.BlockSpec((tm, d), lambda i: (i, 0)),
            pl.BlockSpec((1, d), lambda i: (0, 0)),
            pl.BlockSpec((d, nw), lambda i: (0, 0)),
        ],
        out_specs=pl.BlockSpec((tm, nw), lambda i: (i, 0)),
        out_shape=jax.ShapeDtypeStruct((n, nw), BF16),
        compiler_params=_cparams(("parallel",)),
        name="mem_kv",
    )(mem2d, g, w)


def _mem_attn_kernel(q_ref, k_ref, v_ref, g_ref, o_ref):
    scale = MEM_DH ** -0.5
    for h in range(MEM_HEADS):
        hs = slice(h * MEM_DH, (h + 1) * MEM_DH)
        s = _dot_nt(q_ref[:, hs], k_ref[:, hs]) * scale
        p = jnp.exp(s - jnp.max(s, axis=-1, keepdims=True))
        l = jnp.sum(p, axis=-1, keepdims=True)
        o = _dot((p / l).astype(BF16), v_ref[:, hs])
        o_ref[:, hs] = _rms(o, g_ref[:, hs]).astype(o_ref.dtype)


def _mem_attn(proj, kv, g_mem_out, batch, seq, n_mem, tq=1024):
    n = proj.shape[0]
    w = MEM_HEADS * MEM_DH
    nq = seq // tq
    qcol = 4608 // w
    return pl.pallas_call(
        _mem_attn_kernel,
        grid=(n // tq,),
        in_specs=[
            pl.BlockSpec((tq, w), lambda i: (i, qcol)),
            pl.BlockSpec((n_mem, w), lambda i: (i // nq, 0)),
            pl.BlockSpec((n_mem, w), lambda i: (i // nq, 1)),
            pl.BlockSpec((1, w), lambda i: (0, 0)),
        ],
        out_specs=pl.BlockSpec((tq, w), lambda i: (i, 0)),
        out_shape=jax.ShapeDtypeStruct((n, w), BF16),
        compiler_params=_cparams(("parallel",)),
        name="mem_attn",
    )(proj, kv, kv, g_mem_out)


def _out_proj_kernel(x_ref, gla_ref, fox_ref, mem_ref, w1_ref, w2_ref, w3_ref, g_ref, wr_ref, br_ref,
                     x2_ref, xn_ref, logit_ref):
    x2 = (x_ref[...] + _dot(gla_ref[...], w1_ref[...]) + _dot(fox_ref[...], w2_ref[...])
          + _dot(mem_ref[...], w3_ref[...]))
    x2_ref[...] = x2
    xn = _rms(x2, g_ref[...])
    _store_rowmajor(xn_ref, _pack_bf16_pairs(xn))
    xh = xn.astype(BF16)
    xl = (xn - xh.astype(F32)).astype(BF16)
    hi = _dot(xh, wr_ref[...])
    logit_ref[...] = hi[:, :LANES] + hi[:, LANES:] + _dot(xl, wr_ref[:, :LANES]) + br_ref[...]


def _out_proj(x2d, gla, fox, memo, w_out, g_ffn, wr, br, tm=256):
    n, d = x2d.shape
    w1, w2 = gla.shape[1], fox.shape[1]
    const = lambda i: (0, 0)
    return pl.pallas_call(
        _out_proj_kernel,
        grid=(n // tm,),
        in_specs=[
            pl.BlockSpec((tm, d), lambda i: (i, 0)),
            pl.BlockSpec((tm, w1), lambda i: (i, 0)),
            pl.BlockSpec((tm, w2), lambda i: (i, 0)),
            pl.BlockSpec((tm, w2), lambda i: (i, 0)),
            pl.BlockSpec((w1, d), lambda i: (0, 0)),
            pl.BlockSpec((w2, d), lambda i: (w1 // w2, 0)),
            pl.BlockSpec((w2, d), lambda i: (w1 // w2 + 1, 0)),
            pl.BlockSpec((1, d), const),
            pl.BlockSpec((d, 2 * LANES), const),
            pl.BlockSpec((1, LANES), const),
        ],
        out_specs=[
            pl.BlockSpec((tm, d), lambda i: (i, 0)),
            pl.BlockSpec((tm * (d // 2 // LANES), LANES), lambda i: (i, 0)),
            pl.BlockSpec((tm, LANES), lambda i: (i, 0)),
        ],
        out_shape=[
            jax.ShapeDtypeStruct((n, d), F32),
            jax.ShapeDtypeStruct((n * (d // 2 // LANES), LANES), jnp.uint32),
            jax.ShapeDtypeStruct((n, LANES), F32),
        ],
        compiler_params=_cparams(("parallel",)),
        name="out_proj",
    )(x2d, gla, fox, memo, w_out, w_out, w_out, g_ffn, wr, br)


def _route_kernel(l_ref, idx_ref, gate_ref, rank_ref, cnt_ref, carry_scr, *, tb):
    @pl.when(pl.program_id(0) == 0)
    def _():
        carry_scr[...] = jnp.zeros_like(carry_scr)

    lane = lax.broadcasted_iota(jnp.int32, (tb, LANES), 1)
    logit = jnp.where(lane < N_EXPERTS, l_ref[...], -jnp.inf)
    vals, hots = [], []
    idx_out = jnp.zeros((tb, LANES), jnp.int32)
    for k in range(TOP_K):
        m = jnp.max(logit, axis=-1, keepdims=True)
        ik = jnp.min(jnp.where(logit == m, lane, LANES), axis=-1, keepdims=True)
        hot = lane == ik
        logit = jnp.where(hot, -jnp.inf, logit)
        vals.append(m)
        hots.append(hot)
        idx_out = jnp.where(lane == k, ik, idx_out)
    idx_ref[...] = idx_out

    e = [jnp.exp(v - vals[0]) for v in vals]
    den = e[0] + e[1] + e[2] + e[3]
    gate_out = jnp.zeros((tb, LANES), F32)
    for k in range(TOP_K):
        gate_out = jnp.where(lane == k, e[k] / den, gate_out)
    gate_ref[...] = gate_out

    member = jnp.zeros((tb, LANES), F32)
    for hot in hots:
        member = member + hot.astype(F32)
    row = lax.broadcasted_iota(jnp.int32, (tb, tb), 0)
    col = lax.broadcasted_iota(jnp.int32, (tb, tb), 1)
    before = (col < row).astype(BF16)
    rank = _dot(before, member.astype(BF16)) + carry_scr[...]
    rank_out = jnp.zeros((tb, LANES), F32)
    for k in range(TOP_K):
        rk = jnp.sum(jnp.where(hots[k], rank, 0.0), axis=-1, keepdims=True)
        rank_out = jnp.where(lane == k, rk, rank_out)
    rank_ref[...] = rank_out.astype(jnp.int32)
    total = carry_scr[...] + jnp.sum(member, axis=0, keepdims=True)
    carry_scr[...] = total
    cnt_ref[...] = total.astype(jnp.int32)


def _route(logits, tb=512):
    n = logits.shape[0]
    blk = pl.BlockSpec((tb, LANES), lambda i: (i, 0))
    return pl.pallas_call(
        functools.partial(_route_kernel, tb=tb),
        grid=(n // tb,),
        in_specs=[blk],
        out_specs=[blk, blk, blk, pl.BlockSpec((1, LANES), lambda i: (0, 0))],
        out_shape=[
            jax.ShapeDtypeStruct((n, LANES), jnp.int32),
            jax.ShapeDtypeStruct((n, LANES), F32),
            jax.ShapeDtypeStruct((n, LANES), jnp.int32),
            jax.ShapeDtypeStruct((1, LANES), jnp.int32),
        ],
        scratch_shapes=[pltpu.VMEM((1, LANES), F32)],
        compiler_params=_cparams(("arbitrary",)),
        name="route",
    )(logits)


def _dispatch_kernel(zblk_ref, zok_ref, used_ref, pos_ref, xn_ref, xr_hbm, zero_scr, sem,
                     *, tb, pitch, n_blocks_max):
    blk_rows = MOE_TM * pitch

    def zero_copy(b):
        dst = xr_hbm.at[pl.ds(pl.multiple_of(b * blk_rows, blk_rows), blk_rows), :]
        return pltpu.make_async_copy(zero_scr, dst, sem.at[1])

    @pl.when(pl.program_id(0) == 0)
    def _():
        zero_scr[...] = jnp.zeros_like(zero_scr)

        def per_expert(action):
            def body(e, carry):
                @pl.when(zok_ref[e] == 1)
                def _():
                    action(zero_copy(zblk_ref[e]))
                return carry
            lax.fori_loop(0, N_EXPERTS, body, 0)

        def per_tail(action):
            def body(b, carry):
                action(zero_copy(b))
                return carry
            lax.fori_loop(used_ref[0], n_blocks_max, body, 0)

        per_expert(lambda c: c.start())
        per_tail(lambda c: c.start())
        per_expert(lambda c: c.wait())
        per_tail(lambda c: c.wait())

    def issue(t, carry):
        src = xn_ref.at[pl.ds(pl.multiple_of(t * pitch, pitch), pitch), :]
        for k in range(TOP_K):
            p = pos_ref[0, t * TOP_K + k]
            dst = xr_hbm.at[pl.ds(pl.multiple_of(p * pitch, pitch), pitch), :]
            pltpu.make_async_copy(src, dst, sem.at[0]).start(priority=k % 2)
        return carry

    lax.fori_loop(0, tb, issue, 0, unroll=4)
    for k in range(TOP_K):
        pltpu.make_async_copy(xn_ref, xr_hbm.at[pl.ds(0, tb * pitch), :], sem.at[0]).wait()


def _dispatch(xn_rm, pos, zblk, zok, n_used, n_blocks_max, pitch, tb=256):
    n = xn_rm.shape[0] // pitch
    nb = n // tb
    grid_spec = pltpu.PrefetchScalarGridSpec(
        num_scalar_prefetch=3,
        grid=(nb,),
        in_specs=[
            pl.BlockSpec((None, 1, TOP_K * tb), lambda i, *_: (i, 0, 0), memory_space=pltpu.SMEM),
            pl.BlockSpec((tb * pitch, LANES), lambda i, *_: (i, 0)),
        ],
        out_specs=pl.BlockSpec(memory_space=pl.ANY),
        scratch_shapes=[pltpu.VMEM((MOE_TM * pitch, LANES), xn_rm.dtype), pltpu.SemaphoreType.DMA((2,))],
    )
    return pl.pallas_call(
        functools.partial(_dispatch_kernel, tb=tb, pitch=pitch, n_blocks_max=n_blocks_max),
        grid_spec=grid_spec,
        out_shape=jax.ShapeDtypeStruct((n_blocks_max * MOE_TM * pitch, LANES), xn_rm.dtype),
        compiler_params=_cparams(("arbitrary",)),
        name="dispatch",
    )(zblk, zok, n_used, pos.reshape(nb, 1, TOP_K * tb), xn_rm)


def _cast_rows(src, dst, rows_per=256):
    def body(i, carry):
        r = pl.ds(pl.multiple_of(i * rows_per, rows_per), rows_per)
        dst[r, :] = src[r, :].astype(dst.dtype)
        return carry

    lax.fori_loop(0, src.shape[0] // rows_per, body, 0)


def _stream_row_blocks(cnt, in_copy, out_copy, compute):
    for ahead in range(MOE_NBUF - 1):
        @pl.when(cnt > ahead)
        def _():
            in_copy(ahead, ahead).start()

    def body(b, carry):
        slot = b % MOE_NBUF

        @pl.when(b + MOE_NBUF - 1 < cnt)
        def _():
            in_copy(b + MOE_NBUF - 1, (b + MOE_NBUF - 1) % MOE_NBUF).start()

        in_copy(b, slot).wait()

        @pl.when(b >= MOE_NBUF)
        def _():
            out_copy(b - MOE_NBUF, slot).wait()

        compute(slot)
        out_copy(b, slot).start()
        return carry

    lax.fori_loop(0, cnt, body, 0)

    for back in range(MOE_NBUF, 0, -1):
        @pl.when(cnt >= back)
        def _():
            out_copy(cnt - back, (cnt - back) % MOE_NBUF).wait()


def _zero_fill_blocks(first, last, zero_src, dst_copy):
    def start(b, carry):
        dst_copy(b).start()
        return carry

    def wait(b, carry):
        dst_copy(b).wait()
        return carry

    zero_src[...] = jnp.zeros_like(zero_src)
    lax.fori_loop(first, last, start, 0)
    lax.fori_loop(first, last, wait, 0)


def _moe_up_kernel(bs_ref, bc_ref, used_ref, x_hbm, w_hbm, bg_ref, bu_ref, act_hbm,
                   wf_scr, wb_scr, xbuf, lhs_scr, obuf, sem_w, sem_x, sem_o,
                   *, nt, tn, d_ff, pitch, n_blocks_max):
    g = pl.program_id(0)
    ng = pl.num_programs(0)
    e = g // nt
    col = pl.multiple_of((g % nt) * tn, tn)
    blk_rows = MOE_TM * pitch

    def w_copies(step):
        ee = step // nt
        cc = pl.multiple_of((step % nt) * tn, tn)
        return (pltpu.make_async_copy(w_hbm.at[ee, :, pl.ds(cc, tn)], wf_scr.at[0], sem_w.at[0]),
                pltpu.make_async_copy(w_hbm.at[ee, :, pl.ds(d_ff + cc, tn)], wf_scr.at[1], sem_w.at[1]))

    @pl.when(g == 0)
    def _():
        for c in w_copies(0):
            c.start(priority=1)

    for half, c in enumerate(w_copies(g)):
        c.wait()
        _cast_rows(wf_scr.at[half], wb_scr.at[half])

    @pl.when(g + 1 < ng)
    def _():
        for c in w_copies(g + 1):
            c.start(priority=1)

    start = bs_ref[e]

    def x_copy(b, slot):
        r0 = pl.multiple_of((start + b) * blk_rows, blk_rows)
        return pltpu.make_async_copy(x_hbm.at[pl.ds(r0, blk_rows), :], xbuf.at[slot], sem_x.at[slot])

    def o_copy(b, slot):
        r0 = pl.multiple_of((start + b) * MOE_TM, MOE_TM)
        return pltpu.make_async_copy(obuf.at[slot], act_hbm.at[pl.ds(r0, MOE_TM), pl.ds(col, tn)],
                                     sem_o.at[slot])

    def compute(slot):
        half = pitch * LANES
        for c in range(pitch):
            w = xbuf[slot, pl.ds(c, MOE_TM, stride=pitch), :]
            lhs_scr[:, c * LANES:(c + 1) * LANES] = _unpack_lo(w).astype(BF16)
            lhs_scr[:, half + c * LANES:half + (c + 1) * LANES] = _unpack_hi(w).astype(BF16)
        x = lhs_scr[...]
        gate = jnp.minimum(_dot(x, wb_scr[0]) + bg_ref[...], SWIGLU_LIMIT)
        up = jnp.clip(_dot(x, wb_scr[1]) + bu_ref[...], -SWIGLU_LIMIT, SWIGLU_LIMIT)
        obuf[slot] = (gate * jax.nn.sigmoid(SWIGLU_ALPHA * gate) * (up + 1.0)).astype(obuf.dtype)

    _stream_row_blocks(bc_ref[e], x_copy, o_copy, compute)

    @pl.when(e == N_EXPERTS - 1)
    def _():
        def tail_copy(b):
            r0 = pl.multiple_of(b * MOE_TM, MOE_TM)
            return pltpu.make_async_copy(obuf.at[0], act_hbm.at[pl.ds(r0, MOE_TM), pl.ds(col, tn)],
                                         sem_o.at[0])
        _zero_fill_blocks(used_ref[0], n_blocks_max, obuf.at[0], tail_copy)


def _moe_up(x_rows_rm, w_up, b_up, blk_start, blk_count, n_used, n_blocks_max, tn=1024):
    n_exp, d, two_ff = w_up.shape
    d_ff = two_ff // 2
    pitch = d // 2 // LANES
    nt = d_ff // tn
    up_off = d_ff // tn
    grid_spec = pltpu.PrefetchScalarGridSpec(
        num_scalar_prefetch=3,
        grid=(n_exp * nt,),
        in_specs=[
            pl.BlockSpec(memory_space=pl.ANY),
            pl.BlockSpec(memory_space=pl.ANY),
            pl.BlockSpec((None, 1, tn), lambda g, *_: (g // nt, 0, g % nt)),
            pl.BlockSpec((None, 1, tn), lambda g, *_: (g // nt, 0, g % nt + up_off)),
        ],
        out_specs=pl.BlockSpec(memory_space=pl.ANY),
        scratch_shapes=[
            pltpu.VMEM((2, d, tn), F32),
            pltpu.VMEM((2, d, tn), BF16),
            pltpu.VMEM((MOE_NBUF, MOE_TM * pitch, LANES), jnp.uint32),
            pltpu.VMEM((MOE_TM, d), BF16),
            pltpu.VMEM((MOE_NBUF, MOE_TM, tn), BF16),
            pltpu.SemaphoreType.DMA((2,)),
            pltpu.SemaphoreType.DMA((MOE_NBUF,)),
            pltpu.SemaphoreType.DMA((MOE_NBUF,)),
        ],
    )
    return pl.pallas_call(
        functools.partial(_moe_up_kernel, nt=nt, tn=tn, d_ff=d_ff, pitch=pitch, n_blocks_max=n_blocks_max),
        grid_spec=grid_spec,
        out_shape=jax.ShapeDtypeStruct((n_blocks_max * MOE_TM, d_ff), BF16),
        compiler_params=_cparams(("arbitrary",)),
        name="moe_up",
    )(blk_start, blk_count, n_used, x_rows_rm, w_up, b_up, b_up)


def _moe_down_kernel(bs_ref, bc_ref, used_ref, a_hbm, w_hbm, b_ref, y_hbm,
                     wf_scr, wb_scr, abuf, obuf, sem_w, sem_a, sem_o, *, pitch, n_blocks_max):
    e = pl.program_id(0)
    blk_rows = MOE_TM * pitch

    def w_copy(ee):
        return pltpu.make_async_copy(w_hbm.at[ee], wf_scr, sem_w.at[0])

    @pl.when(e == 0)
    def _():
        w_copy(0).start(priority=1)

    w_copy(e).wait()
    _cast_rows(wf_scr, wb_scr)

    @pl.when(e + 1 < pl.num_programs(0))
    def _():
        w_copy(e + 1).start(priority=1)

    start = bs_ref[e]

    def a_copy(b, slot):
        r0 = pl.multiple_of((start + b) * MOE_TM, MOE_TM)
        return pltpu.make_async_copy(a_hbm.at[pl.ds(r0, MOE_TM), :], abuf.at[slot], sem_a.at[slot])

    def o_copy(b, slot):
        r0 = pl.multiple_of((start + b) * blk_rows, blk_rows)
        return pltpu.make_async_copy(obuf.at[slot], y_hbm.at[pl.ds(r0, blk_rows), :], sem_o.at[slot])

    def compute(slot):
        y = _dot(abuf[slot], wb_scr[...]) + b_ref[...]
        _store_rowmajor(obuf.at[slot], _pack_bf16_pairs(y))

    _stream_row_blocks(bc_ref[e], a_copy, o_copy, compute)

    @pl.when(e == N_EXPERTS - 1)
    def _():
        def tail_copy(b):
            r0 = pl.multiple_of(b * blk_rows, blk_rows)
            return pltpu.make_async_copy(obuf.at[0], y_hbm.at[pl.ds(r0, blk_rows), :], sem_o.at[0])
        _zero_fill_blocks(used_ref[0], n_blocks_max, obuf.at[0], tail_copy)


def _moe_down(act, w_down, b_down, blk_start, blk_count, n_used, n_blocks_max):
    n_rows, d_ff = act.shape
    n_exp, _, d = w_down.shape
    pitch = d // 2 // LANES
    grid_spec = pltpu.PrefetchScalarGridSpec(
        num_scalar_prefetch=3,
        grid=(n_exp,),
        in_specs=[
            pl.BlockSpec(memory_space=pl.ANY),
            pl.BlockSpec(memory_space=pl.ANY),
            pl.BlockSpec((None, 1, d), lambda e, *_: (e, 0, 0)),
        ],
        out_specs=pl.BlockSpec(memory_space=pl.ANY),
        scratch_shapes=[
            pltpu.VMEM((d_ff, d), F32),
            pltpu.VMEM((d_ff, d), BF16),
            pltpu.VMEM((MOE_NBUF, MOE_TM, d_ff), BF16),
            pltpu.VMEM((MOE_NBUF, MOE_TM * pitch, LANES), jnp.uint32),
            pltpu.SemaphoreType.DMA((1,)),
            pltpu.SemaphoreType.DMA((MOE_NBUF,)),
            pltpu.SemaphoreType.DMA((MOE_NBUF,)),
        ],
    )
    return pl.pallas_call(
        functools.partial(_moe_down_kernel, pitch=pitch, n_blocks_max=n_blocks_max),
        grid_spec=grid_spec,
        out_shape=jax.ShapeDtypeStruct((n_rows * pitch, LANES), jnp.uint32),
        compiler_params=_cparams(("arbitrary",)),
        name="moe_down",
    )(blk_start, blk_count, n_used, act, w_down, b_down)


def _combine_kernel(pos0_ref, posn_ref, y_hbm, x2_ref, gate_ref, g_ref, o_ref, buf, sem, *, tb, pitch):
    i = pl.program_id(0)
    n_rows = TOP_K * tb

    def issue_block(p_ref, slot):
        def issue(j, carry):
            for half in range(2):
                r = 2 * j + half
                src = y_hbm.at[pl.ds(pl.multiple_of(p_ref[0, r] * pitch, pitch), pitch), :]
                d0 = pl.multiple_of((slot * n_rows + r) * pitch, pitch)
                pltpu.make_async_copy(src, buf.at[pl.ds(d0, pitch), :], sem.at[slot]).start(priority=half)
            return carry
        lax.fori_loop(0, n_rows // 2, issue, 0, unroll=4)

    @pl.when(i == 0)
    def _():
        issue_block(pos0_ref, 0)

    @pl.when(i + 1 < pl.num_programs(0))
    def _():
        issue_block(posn_ref, (i + 1) % 2)

    slot = i % 2
    base = pl.multiple_of(slot * (n_rows * pitch), pitch)
    pltpu.make_async_copy(y_hbm.at[pl.ds(0, n_rows * pitch), :], buf.at[pl.ds(base, n_rows * pitch), :],
                          sem.at[slot]).wait()
    gates = gate_ref[...]
    half = pitch * LANES
    ssq = jnp.zeros((tb, 1), F32)
    for c in range(pitch):
        lo = slice(c * LANES, (c + 1) * LANES)
        hi = slice(half + c * LANES, half + (c + 1) * LANES)
        z_lo = x2_ref[:, lo]
        z_hi = x2_ref[:, hi]
        for k in range(TOP_K):
            w = buf[pl.ds(base + k * tb * pitch + c, tb, stride=pitch), :]
            z_lo = z_lo + gates[:, k:k + 1] * _unpack_lo(w)
            z_hi = z_hi + gates[:, k:k + 1] * _unpack_hi(w)
        o_ref[:, lo] = z_lo
        o_ref[:, hi] = z_hi
        ssq = ssq + jnp.sum(z_lo * z_lo, axis=-1, keepdims=True) + jnp.sum(z_hi * z_hi, axis=-1, keepdims=True)
    o_ref[...] = o_ref[...] * lax.rsqrt(ssq * (1.0 / (2 * half)) + EPS) * g_ref[...]


def _combine(y_rows_rm, pos, gates, x2, g_final, tb=256):
    n, d = x2.shape
    nb = n // tb
    pitch = d // 2 // LANES
    pos_blk = pos.reshape(nb, tb, TOP_K).transpose(0, 2, 1).reshape(nb, 1, TOP_K * tb)
    pos_spec = lambda imap: pl.BlockSpec((None, 1, TOP_K * tb), imap, memory_space=pltpu.SMEM)
    return pl.pallas_call(
        functools.partial(_combine_kernel, tb=tb, pitch=pitch),
        grid=(nb,),
        in_specs=[
            pos_spec(lambda i: (0, 0, 0)),
            pos_spec(lambda i: (jnp.minimum(i + 1, nb - 1), 0, 0)),
            pl.BlockSpec(memory_space=pl.ANY),
            pl.BlockSpec((tb, d), lambda i: (i, 0)),
            pl.BlockSpec((tb, LANES), lambda i: (i, 0)),
            pl.BlockSpec((1, d), lambda i: (0, 0)),
        ],
        out_specs=pl.BlockSpec((tb, d), lambda i: (i, 0)),
        out_shape=jax.ShapeDtypeStruct((n, d), F32),
        scratch_shapes=[pltpu.VMEM((2 * TOP_K * tb * pitch, LANES), jnp.uint32),
                        pltpu.SemaphoreType.DMA((2,))],
        compiler_params=_cparams(("arbitrary",)),
        name="combine",
    )(pos_blk, pos_blk, y_rows_rm, x2, gates, g_final)


def _routing_tables(idx, rank, cnt):
    counts = cnt[0, :N_EXPERTS]
    blk_count = (counts + MOE_TM - 1) // MOE_TM
    blk_end = jnp.cumsum(blk_count)
    blk_start = blk_end - blk_count
    hot = idx[:, :TOP_K, None] == jnp.arange(N_EXPERTS, dtype=jnp.int32)
    pos = jnp.sum(jnp.where(hot, blk_start * MOE_TM, 0), axis=-1) + rank[:, :TOP_K]
    return pos.astype(jnp.int32), blk_start.astype(jnp.int32), blk_count.astype(jnp.int32)


def kernel(x, mem, g_attn_norm, g_mem_norm, w_in, w_gla_a2, b_gla_a, g_gla_out, b_fox_f, g_fox_out,
           w_mem_kv, g_mem_out, w_out, g_ffn_norm, w_router, b_router, w_moe_up, b_moe_up,
           w_moe_down, b_moe_down, g_final):
    batch, seq, d = x.shape
    n_mem = mem.shape[1]
    n = batch * seq
    depth = w_in.shape[0]
    assert depth == 1, "the combine kernel applies the final norm, so exactly one layer is supported"
    kw = GLA_HEADS * GLA_DK
    vw = GLA_HEADS * GLA_DV
    fw = FOX_HEADS * FOX_DH
    mw = MEM_HEADS * MEM_DH
    o_q, o_k, o_v, o_g = 0, kw, 2 * kw, 2 * kw + vw
    o_a = o_g + vw
    o_fq = o_a + GLA_LOWRANK
    o_fk, o_fv = o_fq + fw, o_fq + 2 * fw
    o_ff = o_fq + 3 * fw
    o_mq = o_ff + FOX_HEADS
    f_lane = GLA_LOWRANK

    xf = x.reshape(n, d)
    for l in range(depth):
        wi = w_in[l]
        fox_q_scale = FOX_DH ** -0.5 * LOG2E
        wa = jnp.concatenate([wi[:, o_q:o_a], wi[:, o_fq:o_fk] * fox_q_scale, wi[:, o_fk:o_ff],
                              wi[:, o_mq:o_mq + mw]], axis=1).astype(BF16)
        wb = jnp.zeros((d, LANES), F32).at[:, :GLA_LOWRANK].set(wi[:, o_a:o_fq])
        wb = wb.at[:, f_lane:f_lane + FOX_HEADS].set(wi[:, o_ff:o_mq]).astype(BF16)
        proj, small = _in_proj(xf, g_attn_norm[l].reshape(1, d), wa, wb)

        wa2p = jnp.zeros((LANES, kw), F32).at[:GLA_LOWRANK].set(w_gla_a2[l]).astype(BF16)
        bfv = jnp.zeros((1, LANES), F32).at[0, f_lane:f_lane + FOX_HEADS].set(b_fox_f[l])
        gla, f_cum = _gla(proj, small, wa2p, b_gla_a[l].reshape(1, kw), bfv,
                          g_gla_out[l].reshape(1, vw), batch, seq)

        f_t = f_cum.reshape(batch, seq, LANES)[:, :, f_lane:f_lane + FOX_HEADS].transpose(0, 2, 1)
        f_t = jnp.concatenate([f_t, jnp.zeros_like(f_t)], axis=1)
        fox = _fox(proj, f_t, g_fox_out[l].reshape(1, fw), batch, seq)

        kv = _mem_kv(mem.reshape(batch * n_mem, d), g_mem_norm[l].reshape(1, d), w_mem_kv[l].astype(BF16))
        memo = _mem_attn(proj, kv, g_mem_out[l].reshape(1, mw), batch, seq, n_mem)

        wr = jnp.zeros((d, LANES), F32).at[:, :N_EXPERTS].set(w_router[l])
        wr_hi = wr.astype(BF16)
        wr = jnp.concatenate([wr_hi, (wr - wr_hi.astype(F32)).astype(BF16)], axis=1)
        br =jnp.zeros((1, LANES), F32).at[0, :N_EXPERTS].set(b_router[l])
        x2, xn, logits = _out_proj(xf, gla, fox, memo, w_out[l].astype(BF16),
                                   g_ffn_norm[l].reshape(1, d), wr, br)

        idx, gates, rank, cnt = _route(logits)
        pos, blk_start, blk_count = _routing_tables(idx, rank, cnt)
        n_blocks_max = -(-(n * TOP_K + N_EXPERTS * (MOE_TM - 1)) // MOE_TM)
        blk_end = blk_start + blk_count
        n_used = blk_end[-1:]
        x_rows = _dispatch(xn, pos, blk_end - 1, (blk_count > 0).astype(jnp.int32), n_used,
                           n_blocks_max, d // 2 // LANES)
        d_ff = w_moe_up.shape[3] // 2
        act = _moe_up(x_rows, w_moe_up[l], b_moe_up[l].reshape(N_EXPERTS, 1, 2 * d_ff),
                      blk_start, blk_count, n_used, n_blocks_max)
        y_rows = _moe_down(act, w_moe_down[l], b_moe_down[l].reshape(N_EXPERTS, 1, d),
                           blk_start, blk_count, n_used, n_blocks_max)
        xf = _combine(y_rows, pos, gates, x2, g_final.reshape(1, d))
    return xf.reshape(batch, seq, d)
```

```python
import functools

import jax
import jax.numpy as jnp
from jax import lax
from jax.experimental import pallas as pl
from jax.experimental.pallas import tpu as pltpu

EPS = 1e-5
CHUNK = 64
GLA_HEADS = 4
GLA_DK = 128
GLA_DV = 256
GLA_LOWRANK = 16
GLA_TAU = 16.0
FOX_HEADS = 4
FOX_DH = 128
MEM_HEADS = 4
MEM_DH = 128
N_EXPERTS = 32
TOP_K = 4
SWIGLU_LIMIT = 7.0
SWIGLU_ALPHA = 1.702
LANES = 128
MXU_COLS = 256
MOE_TM = 256
MOE_NBUF = 3
VMEM_LIMIT = 56 * 1024 * 1024

F32 = jnp.float32
BF16 = jnp.bfloat16


def _cparams(sem, vmem=VMEM_LIMIT):
    return pltpu.CompilerParams(dimension_semantics=sem, vmem_limit_bytes=vmem)


def _log_sigmoid(x):
    return jnp.minimum(x, 0.0) - jnp.log1p(jnp.exp(-jnp.abs(x)))


def _rms(x, g):
    return x * lax.rsqrt(jnp.mean(x * x, axis=-1, keepdims=True) + EPS) * g


def _dot(a, b, **kw):
    return jnp.dot(a, b, preferred_element_type=F32, **kw)


def _dot_nt(a, b):
    return lax.dot_general(a, b, (((1,), (1,)), ((), ())), preferred_element_type=F32)


def _dot_tn(a, b):
    return lax.dot_general(a, b, (((0,), (0,)), ((), ())), preferred_element_type=F32)


def _store_rowmajor(ref, x, base=0):
    rows, w = x.shape
    pitch = w // LANES
    for c in range(pitch):
        ref[pl.ds(base + c, rows, stride=pitch), :] = x[:, c * LANES:(c + 1) * LANES]


def _pack_bf16_pairs(x):
    half = x.shape[1] // 2
    bits = lambda v: lax.bitcast_convert_type(v.astype(BF16).astype(F32), jnp.uint32)
    return bits(x[:, half:]) | (bits(x[:, :half]) >> 16)


def _unpack_lo(w):
    return lax.bitcast_convert_type(w << 16, F32)


def _unpack_hi(w):
    return lax.bitcast_convert_type(w & jnp.uint32(0xFFFF0000), F32)


def _in_proj_kernel(x_ref, g_ref, wa_ref, wb_ref, proj_ref, small_ref, h_scr):
    @pl.when(pl.program_id(1) == 0)
    def _():
        hb = _rms(x_ref[...], g_ref[...]).astype(BF16)
        h_scr[...] = hb
        small_ref[...] = _dot(hb, wb_ref[...])

    proj_ref[...] = _dot(h_scr[...], wa_ref[...]).astype(proj_ref.dtype)


def _in_proj(x2d, g, wa, wb, tm=1024, tn=1280):
    n, d = x2d.shape
    na = wa.shape[1]
    return pl.pallas_call(
        _in_proj_kernel,
        grid=(n // tm, na // tn),
        in_specs=[
            pl.BlockSpec((tm, d), lambda i, j: (i, 0)),
            pl.BlockSpec((1, d), lambda i, j: (0, 0)),
            pl.BlockSpec((d, tn), lambda i, j: (0, j)),
            pl.BlockSpec((d, LANES), lambda i, j: (0, 0)),
        ],
        out_specs=[
            pl.BlockSpec((tm, tn), lambda i, j: (i, j)),
            pl.BlockSpec((tm, LANES), lambda i, j: (i, 0)),
        ],
        out_shape=[
            jax.ShapeDtypeStruct((n, na), BF16),
            jax.ShapeDtypeStruct((n, LANES), F32),
        ],
        scratch_shapes=[pltpu.VMEM((tm, d), BF16)],
        compiler_params=_cparams(("parallel", "arbitrary")),
        name="in_proj",
    )(x2d, g, wa, wb)


def _gla_kernel(q_ref, k_ref, v_ref, gate_ref, small_ref, wa2_ref, ba_ref, bf_ref, gout_ref,
                o_ref, f_ref, state_scr, fcar_scr, la_scr, lf_scr, *, n_chunks):
    @pl.when(pl.program_id(1) == 0)
    def _():
        state_scr[...] = jnp.zeros_like(state_scr)
        fcar_scr[...] = jnp.zeros_like(fcar_scr)

    small = small_ref[...]
    la_scr[...] = _log_sigmoid(_dot(small.astype(BF16), wa2_ref[...]) + ba_ref[...]) * (1.0 / GLA_TAU)
    lf_scr[...] = _log_sigmoid(small + bf_ref[...])
    row = lax.broadcasted_iota(jnp.int32, (CHUNK, CHUNK), 0)
    col = lax.broadcasted_iota(jnp.int32, (CHUNK, CHUNK), 1)
    tri = (col <= row).astype(F32)
    scale = GLA_DK ** -0.5

    def chunk_body(c, carry):
        r = pl.ds(pl.multiple_of(c * CHUNK, CHUNK), CHUNK)
        b = _dot(tri, la_scr[r, :], precision=lax.Precision.HIGHEST)
        b_end = b[CHUNK - 1:CHUNK, :]
        k_dec = k_ref[r, :].astype(F32) * jnp.exp(b_end - b)
        decay = jnp.exp(b_end)
        f_cum = _dot(tri, lf_scr[r, :], precision=lax.Precision.HIGHEST) + fcar_scr[...]
        f_ref[r, :] = f_cum
        fcar_scr[...] = f_cum[CHUNK - 1:CHUNK, :]
        heads = range(GLA_HEADS)
        ks = [slice(h * GLA_DK, (h + 1) * GLA_DK) for h in heads]
        vs = [slice(h * GLA_DV, (h + 1) * GLA_DV) for h in heads]
        old = [state_scr[h] for h in heads]
        new = [old[h] * decay[:, ks[h]] + _dot_tn(v_ref[r, vs[h]], k_dec[:, ks[h]].astype(BF16))
               for h in heads]
        for h in heads:
            state_scr[h] = new[h]
        for h in heads:
            o = _dot_nt(q_ref[r, ks[h]], new[h].astype(BF16)) * scale
            gt = gate_ref[r, vs[h]].astype(F32)
            o_ref[r, vs[h]] = (_rms(o, gout_ref[:, vs[h]]) * (gt * jax.nn.sigmoid(gt))).astype(o_ref.dtype)
        return carry

    lax.fori_loop(0, n_chunks, chunk_body, 0, unroll=4)


def _gla(proj, small, wa2p, ba, bfv, gout, batch, seq, ts=512):
    n = proj.shape[0]
    nsb = seq // ts
    kw = GLA_HEADS * GLA_DK
    vw = GLA_HEADS * GLA_DV
    row = lambda b, s: b * nsb + s
    return pl.pallas_call(
        functools.partial(_gla_kernel, n_chunks=ts // CHUNK),
        grid=(batch, nsb),
        in_specs=[
            pl.BlockSpec((ts, kw), lambda b, s: (row(b, s), 0)),
            pl.BlockSpec((ts, kw), lambda b, s: (row(b, s), 1)),
            pl.BlockSpec((ts, vw), lambda b, s: (row(b, s), 1)),
            pl.BlockSpec((ts, vw), lambda b, s: (row(b, s), 2)),
            pl.BlockSpec((ts, LANES), lambda b, s: (row(b, s), 0)),
            pl.BlockSpec((LANES, kw), lambda b, s: (0, 0)),
            pl.BlockSpec((1, kw), lambda b, s: (0, 0)),
            pl.BlockSpec((1, LANES), lambda b, s: (0, 0)),
            pl.BlockSpec((1, vw), lambda b, s: (0, 0)),
        ],
        out_specs=[
            pl.BlockSpec((ts, vw), lambda b, s: (row(b, s), 0)),
            pl.BlockSpec((ts, LANES), lambda b, s: (row(b, s), 0)),
        ],
        out_shape=[
            jax.ShapeDtypeStruct((n, vw), BF16),
            jax.ShapeDtypeStruct((n, LANES), F32),
        ],
        scratch_shapes=[
            pltpu.VMEM((GLA_HEADS, GLA_DV, GLA_DK), F32),
            pltpu.VMEM((1, LANES), F32),
            pltpu.VMEM((ts, kw), F32),
            pltpu.VMEM((ts, LANES), F32),
        ],
        compiler_params=_cparams(("parallel", "arbitrary")),
        name="gla",
    )(proj, proj, proj, proj, small, wa2p, ba, bfv, gout)


LOG2E = 1.4426950408889634
def _fox_kernel(q_ref, k_ref, v_ref, fk_ref, g_ref, o_ref, m_scr, l_scr, acc_scr, s_even, s_odd,
                *, tq, tk):
    i = pl.program_id(1)
    t = pl.program_id(2)

    @pl.when(t == 0)
    def _():
        m_scr[...] = jnp.full_like(m_scr, -jnp.inf)
        l_scr[...] = jnp.zeros_like(l_scr)
        acc_scr[...] = jnp.zeros_like(acc_scr)

    heads = range(FOX_HEADS)
    hs = [slice(h * FOX_DH, (h + 1) * FOX_DH) for h in heads]

    def score(s_out):
        for h in heads:
            s_out[h] = _dot_nt(q_ref[:, hs[h]], k_ref[:, hs[h]])

    def update(s_in, on_diagonal):
        m_prev = [m_scr[h] for h in heads]
        l_prev = [l_scr[h] for h in heads]
        acc_prev = [acc_scr[h] for h in heads]
        s = [s_in[h] - fk_ref[h:h + 1, :] * LOG2E for h in heads]
        if on_diagonal:
            row = lax.broadcasted_iota(jnp.int32, (tq, tk), 0)
            col = lax.broadcasted_iota(jnp.int32, (tq, tk), 1)
            s = [jnp.where(col <= row, sh, -jnp.inf) for sh in s]
        m_new = [jnp.maximum(m_prev[h], jnp.max(s[h], axis=-1, keepdims=True)) for h in heads]
        alpha = [jnp.exp2(m_prev[h] - m_new[h]) for h in heads]
        p = [jnp.exp2(s[h] - m_new[h]) for h in heads]
        l_new = [alpha[h] * l_prev[h] + jnp.sum(p[h], axis=-1, keepdims=True) for h in heads]
        acc_new = [alpha[h] * acc_prev[h] + _dot(p[h].astype(BF16), v_ref[:, hs[h]]) for h in heads]
        return m_new, l_new, acc_new

    def step(s_write, s_read):
        @pl.when(t == 0)
        def _():
            score(s_write)

        @pl.when(jnp.logical_and(t >= 1, t <= i))
        def _():
            m_new, l_new, acc_new = update(s_read, False)
            score(s_write)
            for h in heads:
                m_scr[h] = m_new[h]
                l_scr[h] = l_new[h]
                acc_scr[h] = acc_new[h]

        @pl.when(t == i + 1)
        def _():
            _, l_new, acc_new = update(s_read, True)
            for h in heads:
                o_ref[:, hs[h]] = _rms(acc_new[h] / l_new[h], g_ref[:, hs[h]]).astype(o_ref.dtype)

    @pl.when(t % 2 == 0)
    def _():
        step(s_even, s_odd)

    @pl.when(t % 2 == 1)
    def _():
        step(s_odd, s_even)


def _fox(proj, f_t, g_fox, batch, seq, tq=512, tk=512):
    assert tq == tk, "the diagonal-block mask assumes square blocks"
    n = proj.shape[0]
    w = FOX_HEADS * FOX_DH
    nq, nk = seq // tq, seq // tk
    qcol, kcol, vcol = 3072 // w, 3584 // w, 4096 // w
    return pl.pallas_call(
        functools.partial(_fox_kernel, tq=tq, tk=tk),
        grid=(batch, nq, nk + 1),
        in_specs=[
            pl.BlockSpec((tq, w), lambda b, i, t: (b * nq + i, qcol)),
            pl.BlockSpec((tk, w), lambda b, i, t: (b * nk + jnp.minimum(t, i), kcol)),
            pl.BlockSpec((tk, w), lambda b, i, t: (b * nk + jnp.clip(t - 1, 0, i), vcol)),
            pl.BlockSpec((None, 8, tk), lambda b, i, t: (b, 0, jnp.clip(t - 1, 0, i))),
            pl.BlockSpec((1, w), lambda b, i, t: (0, 0)),
        ],
        out_specs=pl.BlockSpec((tq, w), lambda b, i, t: (b * nq + i, 0)),
        out_shape=jax.ShapeDtypeStruct((n, w), BF16),
        scratch_shapes=[
            pltpu.VMEM((FOX_HEADS, tq, 1), F32),
            pltpu.VMEM((FOX_HEADS, tq, 1), F32),
            pltpu.VMEM((FOX_HEADS, tq, FOX_DH), F32),
            pltpu.VMEM((FOX_HEADS, tq, tk), F32),
            pltpu.VMEM((FOX_HEADS, tq, tk), F32),
        ],
        compiler_params=_cparams(("parallel", "parallel", "arbitrary")),
        name="fox",
    )(proj, proj, proj, f_t, g_fox)


def _mem_kv_kernel(m_ref, g_ref, w_ref, kv_ref):
    kv_ref[...] = _dot(_rms(m_ref[...], g_ref[...]).astype(BF16), w_ref[...]).astype(kv_ref.dtype)


def _mem_kv(mem2d, g, w, tm=256):
    n, d = mem2d.shape
    nw = w.shape[1]
    return pl.pallas_call(
        _mem_kv_kernel,
        grid=(n // tm,),
        in_specs=[
            pl.BlockSpec((tm, d), lambda i: (i, 0)),
            pl.BlockSpec((1, d), lambda i: (0, 0)),
            pl.BlockSpec((d, nw), lambda i: (0, 0)),
        ],
        out_specs=pl.BlockSpec((tm, nw), lambda i: (i, 0)),
        out_shape=jax.ShapeDtypeStruct((n, nw), BF16),
        compiler_params=_cparams(("parallel",)),
        name="mem_kv",
    )(mem2d, g, w)


def _mem_attn_kernel(q_ref, k_ref, v_ref, g_ref, o_ref):
    scale = MEM_DH ** -0.5
    for h in range(MEM_HEADS):
        hs = slice(h * MEM_DH, (h + 1) * MEM_DH)
        s = _dot_nt(q_ref[:, hs], k_ref[:, hs]) * scale
        p = jnp.exp(s - jnp.max(s, axis=-1, keepdims=True))
        l = jnp.sum(p, axis=-1, keepdims=True)
        o = _dot((p / l).astype(BF16), v_ref[:, hs])
        o_ref[:, hs] = _rms(o, g_ref[:, hs]).astype(o_ref.dtype)


def _mem_attn(proj, kv, g_mem_out, batch, seq, n_mem, tq=1024):
    n = proj.shape[0]
    w = MEM_HEADS * MEM_DH
    nq = seq // tq
    qcol = 4608 // w
    return pl.pallas_call(
        _mem_attn_kernel,
        grid=(n // tq,),
        in_specs=[
            pl.BlockSpec((tq, w), lambda i: (i, qcol)),
            pl.BlockSpec((n_mem, w), lambda i: (i // nq, 0)),
            pl.BlockSpec((n_mem, w), lambda i: (i // nq, 1)),
            pl.BlockSpec((1, w), lambda i: (0, 0)),
        ],
        out_specs=pl.BlockSpec((tq, w), lambda i: (i, 0)),
        out_shape=jax.ShapeDtypeStruct((n, w), BF16),
        compiler_params=_cparams(("parallel",)),
        name="mem_attn",
    )(proj, kv, kv, g_mem_out)


def _out_proj_kernel(x_ref, gla_ref, fox_ref, mem_ref, w1_ref, w2_ref, w3_ref, g_ref, wr_ref, br_ref,
                     x2_ref, xn_ref, logit_ref):
    x2 = (x_ref[...] + _dot(gla_ref[...], w1_ref[...]) + _dot(fox_ref[...], w2_ref[...])
          + _dot(mem_ref[...], w3_ref[...]))
    x2_ref[...] = x2
    xn = _rms(x2, g_ref[...])
    _store_rowmajor(xn_ref, _pack_bf16_pairs(xn))
    xh = xn.astype(BF16)
    xl = (xn - xh.astype(F32)).astype(BF16)
    hi = _dot(xh, wr_ref[...])
    logit_ref[...] = hi[:, :LANES] + hi[:, LANES:] + _dot(xl, wr_ref[:, :LANES]) + br_ref[...]


def _out_proj(x2d, gla, fox, memo, w_out, g_ffn, wr, br, tm=256):
    n, d = x2d.shape
    w1, w2 = gla.shape[1], fox.shape[1]
    const = lambda i: (0, 0)
    return pl.pallas_call(
        _out_proj_kernel,
        grid=(n // tm,),
        in_specs=[
            pl.BlockSpec((tm, d), lambda i: (i, 0)),
            pl.BlockSpec((tm, w1), lambda i: (i, 0)),
            pl.BlockSpec((tm, w2), lambda i: (i, 0)),
            pl.BlockSpec((tm, w2), lambda i: (i, 0)),
            pl.BlockSpec((w1, d), lambda i: (0, 0)),
            pl.BlockSpec((w2, d), lambda i: (w1 // w2, 0)),
            pl.BlockSpec((w2, d), lambda i: (w1 // w2 + 1, 0)),
            pl.BlockSpec((1, d), const),
            pl.BlockSpec((d, 2 * LANES), const),
            pl.BlockSpec((1, LANES), const),
        ],
        out_specs=[
            pl.BlockSpec((tm, d), lambda i: (i, 0)),
            pl.BlockSpec((tm * (d // 2 // LANES), LANES), lambda i: (i, 0)),
            pl.BlockSpec((tm, LANES), lambda i: (i, 0)),
        ],
        out_shape=[
            jax.ShapeDtypeStruct((n, d), F32),
            jax.ShapeDtypeStruct((n * (d // 2 // LANES), LANES), jnp.uint32),
            jax.ShapeDtypeStruct((n, LANES), F32),
        ],
        compiler_params=_cparams(("parallel",)),
        name="out_proj",
    )(x2d, gla, fox, memo, w_out, w_out, w_out, g_ffn, wr, br)


def _route_kernel(l_ref, idx_ref, gate_ref, rank_ref, cnt_ref, carry_scr, *, tb):
    @pl.when(pl.program_id(0) == 0)
    def _():
        carry_scr[...] = jnp.zeros_like(carry_scr)

    lane = lax.broadcasted_iota(jnp.int32, (tb, LANES), 1)
    logit = jnp.where(lane < N_EXPERTS, l_ref[...], -jnp.inf)
    vals, hots = [], []
    idx_out = jnp.zeros((tb, LANES), jnp.int32)
    for k in range(TOP_K):
        m = jnp.max(logit, axis=-1, keepdims=True)
        ik = jnp.min(jnp.where(logit == m, lane, LANES), axis=-1, keepdims=True)
        hot = lane == ik
        logit = jnp.where(hot, -jnp.inf, logit)
        vals.append(m)
        hots.append(hot)
        idx_out = jnp.where(lane == k, ik, idx_out)
    idx_ref[...] = idx_out

    e = [jnp.exp(v - vals[0]) for v in vals]
    den = e[0] + e[1] + e[2] + e[3]
    gate_out = jnp.zeros((tb, LANES), F32)
    for k in range(TOP_K):
        gate_out = jnp.where(lane == k, e[k] / den, gate_out)
    gate_ref[...] = gate_out

    member = jnp.zeros((tb, LANES), F32)
    for hot in hots:
        member = member + hot.astype(F32)
    row = lax.broadcasted_iota(jnp.int32, (tb, tb), 0)
    col = lax.broadcasted_iota(jnp.int32, (tb, tb), 1)
    before = (col < row).astype(BF16)
    rank = _dot(before, member.astype(BF16)) + carry_scr[...]
    rank_out = jnp.zeros((tb, LANES), F32)
    for k in range(TOP_K):
        rk = jnp.sum(jnp.where(hots[k], rank, 0.0), axis=-1, keepdims=True)
        rank_out = jnp.where(lane == k, rk, rank_out)
    rank_ref[...] = rank_out.astype(jnp.int32)
    total = carry_scr[...] + jnp.sum(member, axis=0, keepdims=True)
    carry_scr[...] = total
    cnt_ref[...] = total.astype(jnp.int32)


def _route(logits, tb=512):
    n = logits.shape[0]
    blk = pl.BlockSpec((tb, LANES), lambda i: (i, 0))
    return pl.pallas_call(
        functools.partial(_route_kernel, tb=tb),
        grid=(n // tb,),
        in_specs=[blk],
        out_specs=[blk, blk, blk, pl.BlockSpec((1, LANES), lambda i: (0, 0))],
        out_shape=[
            jax.ShapeDtypeStruct((n, LANES), jnp.int32),
            jax.ShapeDtypeStruct((n, LANES), F32),
            jax.ShapeDtypeStruct((n, LANES), jnp.int32),
            jax.ShapeDtypeStruct((1, LANES), jnp.int32),
        ],
        scratch_shapes=[pltpu.VMEM((1, LANES), F32)],
        compiler_params=_cparams(("arbitrary",)),
        name="route",
    )(logits)


def _dispatch_kernel(zblk_ref, zok_ref, used_ref, pos_ref, xn_ref, xr_hbm, zero_scr, sem,
                     *, tb, pitch, n_blocks_max):
    blk_rows = MOE_TM * pitch

    def zero_copy(b):
        dst = xr_hbm.at[pl.ds(pl.multiple_of(b * blk_rows, blk_rows), blk_rows), :]
        return pltpu.make_async_copy(zero_scr, dst, sem.at[1])

    @pl.when(pl.program_id(0) == 0)
    def _():
        zero_scr[...] = jnp.zeros_like(zero_scr)

        def per_expert(action):
            def body(e, carry):
                @pl.when(zok_ref[e] == 1)
                def _():
                    action(zero_copy(zblk_ref[e]))
                return carry
            lax.fori_loop(0, N_EXPERTS, body, 0)

        def per_tail(action):
            def body(b, carry):
                action(zero_copy(b))
                return carry
            lax.fori_loop(used_ref[0], n_blocks_max, body, 0)

        per_expert(lambda c: c.start())
        per_tail(lambda c: c.start())
        per_expert(lambda c: c.wait())
        per_tail(lambda c: c.wait())

    def issue(t, carry):
        src = xn_ref.at[pl.ds(pl.multiple_of(t * pitch, pitch), pitch), :]
        for k in range(TOP_K):
            p = pos_ref[0, t * TOP_K + k]
            dst = xr_hbm.at[pl.ds(pl.multiple_of(p * pitch, pitch), pitch), :]
            pltpu.make_async_copy(src, dst, sem.at[0]).start(priority=k % 2)
        return carry

    lax.fori_loop(0, tb, issue, 0, unroll=4)
    for k in range(TOP_K):
        pltpu.make_async_copy(xn_ref, xr_hbm.at[pl.ds(0, tb * pitch), :], sem.at[0]).wait()


def _dispatch(xn_rm, pos, zblk, zok, n_used, n_blocks_max, pitch, tb=256):
    n = xn_rm.shape[0] // pitch
    nb = n // tb
    grid_spec = pltpu.PrefetchScalarGridSpec(
        num_scalar_prefetch=3,
        grid=(nb,),
        in_specs=[
            pl.BlockSpec((None, 1, TOP_K * tb), lambda i, *_: (i, 0, 0), memory_space=pltpu.SMEM),
            pl.BlockSpec((tb * pitch, LANES), lambda i, *_: (i, 0)),
        ],
        out_specs=pl.BlockSpec(memory_space=pl.ANY),
        scratch_shapes=[pltpu.VMEM((MOE_TM * pitch, LANES), xn_rm.dtype), pltpu.SemaphoreType.DMA((2,))],
    )
    return pl.pallas_call(
        functools.partial(_dispatch_kernel, tb=tb, pitch=pitch, n_blocks_max=n_blocks_max),
        grid_spec=grid_spec,
        out_shape=jax.ShapeDtypeStruct((n_blocks_max * MOE_TM * pitch, LANES), xn_rm.dtype),
        compiler_params=_cparams(("arbitrary",)),
        name="dispatch",
    )(zblk, zok, n_used, pos.reshape(nb, 1, TOP_K * tb), xn_rm)


def _cast_rows(src, dst, rows_per=256):
    def body(i, carry):
        r = pl.ds(pl.multiple_of(i * rows_per, rows_per), rows_per)
        dst[r, :] = src[r, :].astype(dst.dtype)
        return carry

    lax.fori_loop(0, src.shape[0] // rows_per, body, 0)


def _stream_row_blocks(cnt, in_copy, out_copy, compute):
    for ahead in range(MOE_NBUF - 1):
        @pl.when(cnt > ahead)
        def _():
            in_copy(ahead, ahead).start()

    def body(b, carry):
        slot = b % MOE_NBUF

        @pl.when(b + MOE_NBUF - 1 < cnt)
        def _():
            in_copy(b + MOE_NBUF - 1, (b + MOE_NBUF - 1) % MOE_NBUF).start()

        in_copy(b, slot).wait()

        @pl.when(b >= MOE_NBUF)
        def _():
            out_copy(b - MOE_NBUF, slot).wait()

        compute(slot)
        out_copy(b, slot).start()
        return carry

    lax.fori_loop(0, cnt, body, 0)

    for back in range(MOE_NBUF, 0, -1):
        @pl.when(cnt >= back)
        def _():
            out_copy(cnt - back, (cnt - back) % MOE_NBUF).wait()


def _zero_fill_blocks(first, last, zero_src, dst_copy):
    def start(b, carry):
        dst_copy(b).start()
        return carry

    def wait(b, carry):
        dst_copy(b).wait()
        return carry

    zero_src[...] = jnp.zeros_like(zero_src)
    lax.fori_loop(first, last, start, 0)
    lax.fori_loop(first, last, wait, 0)


def _moe_up_kernel(bs_ref, bc_ref, used_ref, x_hbm, w_hbm, bg_ref, bu_ref, act_hbm,
                   wf_scr, wb_scr, xbuf, lhs_scr, obuf, sem_w, sem_x, sem_o,
                   *, nt, tn, d_ff, pitch, n_blocks_max):
    g = pl.program_id(0)
    ng = pl.num_programs(0)
    e = g // nt
    col = pl.multiple_of((g % nt) * tn, tn)
    blk_rows = MOE_TM * pitch

    def w_copies(step):
        ee = step // nt
        cc = pl.multiple_of((step % nt) * tn, tn)
        return (pltpu.make_async_copy(w_hbm.at[ee, :, pl.ds(cc, tn)], wf_scr.at[0], sem_w.at[0]),
                pltpu.make_async_copy(w_hbm.at[ee, :, pl.ds(d_ff + cc, tn)], wf_scr.at[1], sem_w.at[1]))

    @pl.when(g == 0)
    def _():
        for c in w_copies(0):
            c.start(priority=1)

    for half, c in enumerate(w_copies(g)):
        c.wait()
        _cast_rows(wf_scr.at[half], wb_scr.at[half])

    @pl.when(g + 1 < ng)
    def _():
        for c in w_copies(g + 1):
            c.start(priority=1)

    start = bs_ref[e]

    def x_copy(b, slot):
        r0 = pl.multiple_of((start + b) * blk_rows, blk_rows)
        return pltpu.make_async_copy(x_hbm.at[pl.ds(r0, blk_rows), :], xbuf.at[slot], sem_x.at[slot])

    def o_copy(b, slot):
        r0 = pl.multiple_of((start + b) * MOE_TM, MOE_TM)
        return pltpu.make_async_copy(obuf.at[slot], act_hbm.at[pl.ds(r0, MOE_TM), pl.ds(col, tn)],
                                     sem_o.at[slot])

    def compute(slot):
        half = pitch * LANES
        for c in range(pitch):
            w = xbuf[slot, pl.ds(c, MOE_TM, stride=pitch), :]
            lhs_scr[:, c * LANES:(c + 1) * LANES] = _unpack_lo(w).astype(BF16)
            lhs_scr[:, half + c * LANES:half + (c + 1) * LANES] = _unpack_hi(w).astype(BF16)
        x = lhs_scr[...]
        for n in range(tn // MXU_COLS):
            ns = slice(n * MXU_COLS, (n + 1) * MXU_COLS)
            gate = jnp.minimum(_dot(x, wb_scr[0, :, ns]) + bg_ref[:, ns], SWIGLU_LIMIT)
            up = jnp.clip(_dot(x, wb_scr[1, :, ns]) + bu_ref[:, ns], -SWIGLU_LIMIT, SWIGLU_LIMIT)
            obuf[slot, :, ns] = (gate * jax.nn.sigmoid(SWIGLU_ALPHA * gate) * (up + 1.0)).astype(obuf.dtype)

    _stream_row_blocks(bc_ref[e], x_copy, o_copy, compute)

    @pl.when(e == N_EXPERTS - 1)
    def _():
        def tail_copy(b):
            r0 = pl.multiple_of(b * MOE_TM, MOE_TM)
            return pltpu.make_async_copy(obuf.at[0], act_hbm.at[pl.ds(r0, MOE_TM), pl.ds(col, tn)],
                                         sem_o.at[0])
        _zero_fill_blocks(used_ref[0], n_blocks_max, obuf.at[0], tail_copy)


def _moe_up(x_rows_rm, w_up, b_up, blk_start, blk_count, n_used, n_blocks_max, tn=1024):
    n_exp, d, two_ff = w_up.shape
    d_ff = two_ff // 2
    pitch = d // 2 // LANES
    nt = d_ff // tn
    up_off = d_ff // tn
    grid_spec = pltpu.PrefetchScalarGridSpec(
        num_scalar_prefetch=3,
        grid=(n_exp * nt,),
        in_specs=[
            pl.BlockSpec(memory_space=pl.ANY),
            pl.BlockSpec(memory_space=pl.ANY),
            pl.BlockSpec((None, 1, tn), lambda g, *_: (g // nt, 0, g % nt)),
            pl.BlockSpec((None, 1, tn), lambda g, *_: (g // nt, 0, g % nt + up_off)),
        ],
        out_specs=pl.BlockSpec(memory_space=pl.ANY),
        scratch_shapes=[
            pltpu.VMEM((2, d, tn), F32),
            pltpu.VMEM((2, d, tn), BF16),
            pltpu.VMEM((MOE_NBUF, MOE_TM * pitch, LANES), jnp.uint32),
            pltpu.VMEM((MOE_TM, d), BF16),
            pltpu.VMEM((MOE_NBUF, MOE_TM, tn), BF16),
            pltpu.SemaphoreType.DMA((2,)),
            pltpu.SemaphoreType.DMA((MOE_NBUF,)),
            pltpu.SemaphoreType.DMA((MOE_NBUF,)),
        ],
    )
    return pl.pallas_call(
        functools.partial(_moe_up_kernel, nt=nt, tn=tn, d_ff=d_ff, pitch=pitch, n_blocks_max=n_blocks_max),
        grid_spec=grid_spec,
        out_shape=jax.ShapeDtypeStruct((n_blocks_max * MOE_TM, d_ff), BF16),
        compiler_params=_cparams(("arbitrary",)),
        name="moe_up",
    )(blk_start, blk_count, n_used, x_rows_rm, w_up, b_up, b_up)


def _moe_down_kernel(bs_ref, bc_ref, used_ref, a_hbm, w_hbm, b_ref, y_hbm,
                     wf_scr, wb_scr, abuf, obuf, sem_w, sem_a, sem_o, *, pitch, n_blocks_max):
    e = pl.program_id(0)
    blk_rows = MOE_TM * pitch

    def w_copy(ee):
        return pltpu.make_async_copy(w_hbm.at[ee], wf_scr, sem_w.at[0])

    @pl.when(e == 0)
    def _():
        w_copy(0).start(priority=1)

    w_copy(e).wait()
    _cast_rows(wf_scr, wb_scr)

    @pl.when(e + 1 < pl.num_programs(0))
    def _():
        w_copy(e + 1).start(priority=1)

    start = bs_ref[e]

    def a_copy(b, slot):
        r0 = pl.multiple_of((start + b) * MOE_TM, MOE_TM)
        return pltpu.make_async_copy(a_hbm.at[pl.ds(r0, MOE_TM), :], abuf.at[slot], sem_a.at[slot])

    def o_copy(b, slot):
        r0 = pl.multiple_of((start + b) * blk_rows, blk_rows)
        return pltpu.make_async_copy(obuf.at[slot], y_hbm.at[pl.ds(r0, blk_rows), :], sem_o.at[slot])

    def compute(slot):
        y = _dot(abuf[slot], wb_scr[...]) + b_ref[...]
        _store_rowmajor(obuf.at[slot], _pack_bf16_pairs(y))

    _stream_row_blocks(bc_ref[e], a_copy, o_copy, compute)

    @pl.when(e == N_EXPERTS - 1)
    def _():
        def tail_copy(b):
            r0 = pl.multiple_of(b * blk_rows, blk_rows)
            return pltpu.make_async_copy(obuf.at[0], y_hbm.at[pl.ds(r0, blk_rows), :], sem_o.at[0])
        _zero_fill_blocks(used_ref[0], n_blocks_max, obuf.at[0], tail_copy)


def _moe_down(act, w_down, b_down, blk_start, blk_count, n_used, n_blocks_max):
    n_rows, d_ff = act.shape
    n_exp, _, d = w_down.shape
    pitch = d // 2 // LANES
    grid_spec = pltpu.PrefetchScalarGridSpec(
        num_scalar_prefetch=3,
        grid=(n_exp,),
        in_specs=[
            pl.BlockSpec(memory_space=pl.ANY),
            pl.BlockSpec(memory_space=pl.ANY),
            pl.BlockSpec((None, 1, d), lambda e, *_: (e, 0, 0)),
        ],
        out_specs=pl.BlockSpec(memory_space=pl.ANY),
        scratch_shapes=[
            pltpu.VMEM((d_ff, d), F32),
            pltpu.VMEM((d_ff, d), BF16),
            pltpu.VMEM((MOE_NBUF, MOE_TM, d_ff), BF16),
            pltpu.VMEM((MOE_NBUF, MOE_TM * pitch, LANES), jnp.uint32),
            pltpu.SemaphoreType.DMA((1,)),
            pltpu.SemaphoreType.DMA((MOE_NBUF,)),
            pltpu.SemaphoreType.DMA((MOE_NBUF,)),
        ],
    )
    return pl.pallas_call(
        functools.partial(_moe_down_kernel, pitch=pitch, n_blocks_max=n_blocks_max),
        grid_spec=grid_spec,
        out_shape=jax.ShapeDtypeStruct((n_rows * pitch, LANES), jnp.uint32),
        compiler_params=_cparams(("arbitrary",)),
        name="moe_down",
    )(blk_start, blk_count, n_used, act, w_down, b_down)


def _combine_kernel(pos0_ref, posn_ref, y_hbm, x2_ref, gate_ref, g_ref, o_ref, buf, sem, *, tb, pitch):
    i = pl.program_id(0)
    n_rows = TOP_K * tb

    def issue_block(p_ref, slot):
        def issue(j, carry):
            for half in range(2):
                r = 2 * j + half
                src = y_hbm.at[pl.ds(pl.multiple_of(p_ref[0, r] * pitch, pitch), pitch), :]
                d0 = pl.multiple_of((slot * n_rows + r) * pitch, pitch)
                pltpu.make_async_copy(src, buf.at[pl.ds(d0, pitch), :], sem.at[slot]).start(priority=half)
            return carry
        lax.fori_loop(0, n_rows // 2, issue, 0, unroll=4)

    @pl.when(i == 0)
    def _():
        issue_block(pos0_ref, 0)

    @pl.when(i + 1 < pl.num_programs(0))
    def _():
        issue_block(posn_ref, (i + 1) % 2)

    slot = i % 2
    base = pl.multiple_of(slot * (n_rows * pitch), pitch)
    pltpu.make_async_copy(y_hbm.at[pl.ds(0, n_rows * pitch), :], buf.at[pl.ds(base, n_rows * pitch), :],
                          sem.at[slot]).wait()
    gates = gate_ref[...]
    half = pitch * LANES
    ssq = jnp.zeros((tb, 1), F32)
    for c in range(pitch):
        lo = slice(c * LANES, (c + 1) * LANES)
        hi = slice(half + c * LANES, half + (c + 1) * LANES)
        z_lo = x2_ref[:, lo]
        z_hi = x2_ref[:, hi]
        for k in range(TOP_K):
            w = buf[pl.ds(base + k * tb * pitch + c, tb, stride=pitch), :]
            z_lo = z_lo + gates[:, k:k + 1] * _unpack_lo(w)
            z_hi = z_hi + gates[:, k:k + 1] * _unpack_hi(w)
        o_ref[:, lo] = z_lo
        o_ref[:, hi] = z_hi
        ssq = ssq + jnp.sum(z_lo * z_lo, axis=-1, keepdims=True) + jnp.sum(z_hi * z_hi, axis=-1, keepdims=True)
    o_ref[...] = o_ref[...] * lax.rsqrt(ssq * (1.0 / (2 * half)) + EPS) * g_ref[...]


def _combine(y_rows_rm, pos, gates, x2, g_final, tb=256):
    n, d = x2.shape
    nb = n // tb
    pitch = d // 2 // LANES
    pos_blk = pos.reshape(nb, tb, TOP_K).transpose(0, 2, 1).reshape(nb, 1, TOP_K * tb)
    pos_spec = lambda imap: pl.BlockSpec((None, 1, TOP_K * tb), imap, memory_space=pltpu.SMEM)
    return pl.pallas_call(
        functools.partial(_combine_kernel, tb=tb, pitch=pitch),
        grid=(nb,),
        in_specs=[
            pos_spec(lambda i: (0, 0, 0)),
            pos_spec(lambda i: (jnp.minimum(i + 1, nb - 1), 0, 0)),
            pl.BlockSpec(memory_space=pl.ANY),
            pl.BlockSpec((tb, d), lambda i: (i, 0)),
            pl.BlockSpec((tb, LANES), lambda i: (i, 0)),
            pl.BlockSpec((1, d), lambda i: (0, 0)),
        ],
        out_specs=pl.BlockSpec((tb, d), lambda i: (i, 0)),
        out_shape=jax.ShapeDtypeStruct((n, d), F32),
        scratch_shapes=[pltpu.VMEM((2 * TOP_K * tb * pitch, LANES), jnp.uint32),
                        pltpu.SemaphoreType.DMA((2,))],
        compiler_params=_cparams(("arbitrary",)),
        name="combine",
    )(pos_blk, pos_blk, y_rows_rm, x2, gates, g_final)


def _routing_tables(idx, rank, cnt):
    counts = cnt[0, :N_EXPERTS]
    blk_count = (counts + MOE_TM - 1) // MOE_TM
    blk_end = jnp.cumsum(blk_count)
    blk_start = blk_end - blk_count
    hot = idx[:, :TOP_K, None] == jnp.arange(N_EXPERTS, dtype=jnp.int32)
    pos = jnp.sum(jnp.where(hot, blk_start * MOE_TM, 0), axis=-1) + rank[:, :TOP_K]
    return pos.astype(jnp.int32), blk_start.astype(jnp.int32), blk_count.astype(jnp.int32)


def kernel(x, mem, g_attn_norm, g_mem_norm, w_in, w_gla_a2, b_gla_a, g_gla_out, b_fox_f, g_fox_out,
           w_mem_kv, g_mem_out, w_out, g_ffn_norm, w_router, b_router, w_moe_up, b_moe_up,
           w_moe_down, b_moe_down, g_final):
    batch, seq, d = x.shape
    n_mem = mem.shape[1]
    n = batch * seq
    depth = w_in.shape[0]
    assert depth == 1, "the combine kernel applies the final norm, so exactly one layer is supported"
    kw = GLA_HEADS * GLA_DK
    vw = GLA_HEADS * GLA_DV
    fw = FOX_HEADS * FOX_DH
    mw = MEM_HEADS * MEM_DH
    o_q, o_k, o_v, o_g = 0, kw, 2 * kw, 2 * kw + vw
    o_a = o_g + vw
    o_fq = o_a + GLA_LOWRANK
    o_fk, o_fv = o_fq + fw, o_fq + 2 * fw
    o_ff = o_fq + 3 * fw
    o_mq = o_ff + FOX_HEADS
    f_lane = GLA_LOWRANK

    xf = x.reshape(n, d)
    for l in range(depth):
        wi = w_in[l]
        fox_q_scale = FOX_DH ** -0.5 * LOG2E
        wa = jnp.concatenate([wi[:, o_q:o_a], wi[:, o_fq:o_fk] * fox_q_scale, wi[:, o_fk:o_ff],
                              wi[:, o_mq:o_mq + mw]], axis=1).astype(BF16)
        wb = jnp.zeros((d, LANES), F32).at[:, :GLA_LOWRANK].set(wi[:, o_a:o_fq])
        wb = wb.at[:, f_lane:f_lane + FOX_HEADS].set(wi[:, o_ff:o_mq]).astype(BF16)
        proj, small = _in_proj(xf, g_attn_norm[l].reshape(1, d), wa, wb)

        wa2p = jnp.zeros((LANES, kw), F32).at[:GLA_LOWRANK].set(w_gla_a2[l]).astype(BF16)
        bfv = jnp.zeros((1, LANES), F32).at[0, f_lane:f_lane + FOX_HEADS].set(b_fox_f[l])
        gla, f_cum = _gla(proj, small, wa2p, b_gla_a[l].reshape(1, kw), bfv,
                          g_gla_out[l].reshape(1, vw), batch, seq)

        f_t = f_cum.reshape(batch, seq, LANES)[:, :, f_lane:f_lane + FOX_HEADS].transpose(0, 2, 1)
        f_t = jnp.concatenate([f_t, jnp.zeros_like(f_t)], axis=1)
        fox = _fox(proj, f_t, g_fox_out[l].reshape(1, fw), batch, seq)

        kv = _mem_kv(mem.reshape(batch * n_mem, d), g_mem_norm[l].reshape(1, d), w_mem_kv[l].astype(BF16))
        memo = _mem_attn(proj, kv, g_mem_out[l].reshape(1, mw), batch, seq, n_mem)

        wr = jnp.zeros((d, LANES), F32).at[:, :N_EXPERTS].set(w_router[l])
        wr_hi = wr.astype(BF16)
        wr = jnp.concatenate([wr_hi, (wr - wr_hi.astype(F32)).astype(BF16)], axis=1)
        br =jnp.zeros((1, LANES), F32).at[0, :N_EXPERTS].set(b_router[l])
        x2, xn, logits = _out_proj(xf, gla, fox, memo, w_out[l].astype(BF16),
                                   g_ffn_norm[l].reshape(1, d), wr, br)

        idx, gates, rank, cnt = _route(logits)
        pos, blk_start, blk_count = _routing_tables(idx, rank, cnt)
        n_blocks_max = -(-(n * TOP_K + N_EXPERTS * (MOE_TM - 1)) // MOE_TM)
        blk_end = blk_start + blk_count
        n_used = blk_end[-1:]
        x_rows = _dispatch(xn, pos, blk_end - 1, (blk_count > 0).astype(jnp.int32), n_used,
                           n_blocks_max, d // 2 // LANES)
        d_ff = w_moe_up.shape[3] // 2
        act = _moe_up(x_rows, w_moe_up[l], b_moe_up[l].reshape(N_EXPERTS, 1, 2 * d_ff),
                      blk_start, blk_count, n_used, n_blocks_max)
        y_rows = _moe_down(act, w_moe_down[l], b_moe_down[l].reshape(N_EXPERTS, 1, d),
                           blk_start, blk_count, n_used, n_blocks_max)
        xf = _combine(y_rows, pos, gates, x2, g_final.reshape(1, d))
    return xf.reshape(batch, seq, d)
```

```python
import functools

import jax
import jax.numpy as jnp
from jax import lax
from jax.experimental import pallas as pl
from jax.experimental.pallas import tpu as pltpu

EPS = 1e-5
CHUNK = 64
GLA_HEADS = 4
GLA_DK = 128
GLA_DV = 256
GLA_LOWRANK = 16
GLA_TAU = 16.0
FOX_HEADS = 4
FOX_DH = 128
MEM_HEADS = 4
MEM_DH = 128
N_EXPERTS = 32
TOP_K = 4
SWIGLU_LIMIT = 7.0
SWIGLU_ALPHA = 1.702
LANES = 128
MXU_COLS = 256
MOE_TM = 256
MOE_NBUF = 2
MOE_UNIT = 2
VMEM_LIMIT = 56 * 1024 * 1024

F32 = jnp.float32
BF16 = jnp.bfloat16


def _cparams(sem, vmem=VMEM_LIMIT):
    return pltpu.CompilerParams(dimension_semantics=sem, vmem_limit_bytes=vmem)


def _log_sigmoid(x):
    return jnp.minimum(x, 0.0) - jnp.log1p(jnp.exp(-jnp.abs(x)))


def _rms(x, g):
    return x * lax.rsqrt(jnp.mean(x * x, axis=-1, keepdims=True) + EPS) * g


def _dot(a, b, **kw):
    return jnp.dot(a, b, preferred_element_type=F32, **kw)


def _dot_nt(a, b):
    return lax.dot_general(a, b, (((1,), (1,)), ((), ())), preferred_element_type=F32)


def _dot_tn(a, b):
    return lax.dot_general(a, b, (((0,), (0,)), ((), ())), preferred_element_type=F32)


def _store_rowmajor(ref, x, base=0):
    rows, w = x.shape
    pitch = w // LANES
    for c in range(pitch):
        ref[pl.ds(base + c, rows, stride=pitch), :] = x[:, c * LANES:(c + 1) * LANES]


def _pack_bf16_pairs(x):
    half = x.shape[1] // 2
    bits = lambda v: lax.bitcast_convert_type(v.astype(BF16).astype(F32), jnp.uint32)
    return bits(x[:, half:]) | (bits(x[:, :half]) >> 16)


def _unpack_lo(w):
    return lax.bitcast_convert_type(w << 16, F32)


def _unpack_hi(w):
    return lax.bitcast_convert_type(w & jnp.uint32(0xFFFF0000), F32)


def _in_proj_kernel(x_ref, g_ref, wa_ref, wb_ref, proj_ref, small_ref, h_scr):
    @pl.when(pl.program_id(1) == 0)
    def _():
        hb = _rms(x_ref[...], g_ref[...]).astype(BF16)
        h_scr[...] = hb
        small_ref[...] = _dot(hb, wb_ref[...])

    proj_ref[...] = _dot(h_scr[...], wa_ref[...]).astype(proj_ref.dtype)


def _in_proj(x2d, g, wa, wb, tm=1024, tn=1280):
    n, d = x2d.shape
    na = wa.shape[1]
    return pl.pallas_call(
        _in_proj_kernel,
        grid=(n // tm, na // tn),
        in_specs=[
            pl.BlockSpec((tm, d), lambda i, j: (i, 0)),
            pl.BlockSpec((1, d), lambda i, j: (0, 0)),
            pl.BlockSpec((d, tn), lambda i, j: (0, j)),
            pl.BlockSpec((d, LANES), lambda i, j: (0, 0)),
        ],
        out_specs=[
            pl.BlockSpec((tm, tn), lambda i, j: (i, j)),
            pl.BlockSpec((tm, LANES), lambda i, j: (i, 0)),
        ],
        out_shape=[
            jax.ShapeDtypeStruct((n, na), BF16),
            jax.ShapeDtypeStruct((n, LANES), F32),
        ],
        scratch_shapes=[pltpu.VMEM((tm, d), BF16)],
        compiler_params=_cparams(("parallel", "arbitrary")),
        name="in_proj",
    )(x2d, g, wa, wb)


def _gla_kernel(q_ref, k_ref, v_ref, gate_ref, small_ref, wa2_ref, ba_ref, bf_ref, gout_ref,
                o_ref, f_ref, state_scr, fcar_scr, la_scr, lf_scr, *, n_chunks):
    @pl.when(pl.program_id(1) == 0)
    def _():
        state_scr[...] = jnp.zeros_like(state_scr)
        fcar_scr[...] = jnp.zeros_like(fcar_scr)

    small = small_ref[...]
    la_scr[...] = _log_sigmoid(_dot(small.astype(BF16), wa2_ref[...]) + ba_ref[...]) * (1.0 / GLA_TAU)
    lf_scr[...] = _log_sigmoid(small + bf_ref[...])
    row = lax.broadcasted_iota(jnp.int32, (CHUNK, CHUNK), 0)
    col = lax.broadcasted_iota(jnp.int32, (CHUNK, CHUNK), 1)
    tri = (col <= row).astype(F32)
    scale = GLA_DK ** -0.5

    def chunk_body(c, carry):
        r = pl.ds(pl.multiple_of(c * CHUNK, CHUNK), CHUNK)
        b = _dot(tri, la_scr[r, :], precision=lax.Precision.HIGHEST)
        b_end = b[CHUNK - 1:CHUNK, :]
        k_dec = k_ref[r, :].astype(F32) * jnp.exp(b_end - b)
        decay = jnp.exp(b_end)
        f_cum = _dot(tri, lf_scr[r, :], precision=lax.Precision.HIGHEST) + fcar_scr[...]
        f_ref[r, :] = f_cum
        fcar_scr[...] = f_cum[CHUNK - 1:CHUNK, :]
        heads = range(GLA_HEADS)
        ks = [slice(h * GLA_DK, (h + 1) * GLA_DK) for h in heads]
        vs = [slice(h * GLA_DV, (h + 1) * GLA_DV) for h in heads]
        old = [state_scr[h] for h in heads]
        new = [old[h] * decay[:, ks[h]] + _dot_tn(v_ref[r, vs[h]], k_dec[:, ks[h]].astype(BF16))
               for h in heads]
        for h in heads:
            state_scr[h] = new[h]
        for h in heads:
            o = _dot_nt(q_ref[r, ks[h]], new[h].astype(BF16)) * scale
            gt = gate_ref[r, vs[h]].astype(F32)
            o_ref[r, vs[h]] = (_rms(o, gout_ref[:, vs[h]]) * (gt * jax.nn.sigmoid(gt))).astype(o_ref.dtype)
        return carry

    lax.fori_loop(0, n_chunks, chunk_body, 0, unroll=4)


def _gla(proj, small, wa2p, ba, bfv, gout, batch, seq, ts=512):
    n = proj.shape[0]
    nsb = seq // ts
    kw = GLA_HEADS * GLA_DK
    vw = GLA_HEADS * GLA_DV
    row = lambda b, s: b * nsb + s
    return pl.pallas_call(
        functools.partial(_gla_kernel, n_chunks=ts // CHUNK),
        grid=(batch, nsb),
        in_specs=[
            pl.BlockSpec((ts, kw), lambda b, s: (row(b, s), 0)),
            pl.BlockSpec((ts, kw), lambda b, s: (row(b, s), 1)),
            pl.BlockSpec((ts, vw), lambda b, s: (row(b, s), 1)),
            pl.BlockSpec((ts, vw), lambda b, s: (row(b, s), 2)),
            pl.BlockSpec((ts, LANES), lambda b, s: (row(b, s), 0)),
            pl.BlockSpec((LANES, kw), lambda b, s: (0, 0)),
            pl.BlockSpec((1, kw), lambda b, s: (0, 0)),
            pl.BlockSpec((1, LANES), lambda b, s: (0, 0)),
            pl.BlockSpec((1, vw), lambda b, s: (0, 0)),
        ],
        out_specs=[
            pl.BlockSpec((ts, vw), lambda b, s: (row(b, s), 0)),
            pl.BlockSpec((ts, LANES), lambda b, s: (row(b, s), 0)),
        ],
        out_shape=[
            jax.ShapeDtypeStruct((n, vw), BF16),
            jax.ShapeDtypeStruct((n, LANES), F32),
        ],
        scratch_shapes=[
            pltpu.VMEM((GLA_HEADS, GLA_DV, GLA_DK), F32),
            pltpu.VMEM((1, LANES), F32),
            pltpu.VMEM((ts, kw), F32),
            pltpu.VMEM((ts, LANES), F32),
        ],
        compiler_params=_cparams(("parallel", "arbitrary")),
        name="gla",
    )(proj, proj, proj, proj, small, wa2p, ba, bfv, gout)


LOG2E = 1.4426950408889634
def _fox_kernel(q_ref, k_ref, v_ref, fk_ref, g_ref, o_ref, m_scr, l_scr, acc_scr, s_even, s_odd,
                *, tq, tk):
    i = pl.program_id(1)
    t = pl.program_id(2)

    @pl.when(t == 0)
    def _():
        m_scr[...] = jnp.full_like(m_scr, -jnp.inf)
        l_scr[...] = jnp.zeros_like(l_scr)
        acc_scr[...] = jnp.zeros_like(acc_scr)

    heads = range(FOX_HEADS)
    hs = [slice(h * FOX_DH, (h + 1) * FOX_DH) for h in heads]

    def score(s_out):
        for h in heads:
            s_out[h] = _dot_nt(q_ref[:, hs[h]], k_ref[:, hs[h]])

    def update(s_in, on_diagonal):
        m_prev = [m_scr[h] for h in heads]
        l_prev = [l_scr[h] for h in heads]
        acc_prev = [acc_scr[h] for h in heads]
        s = [s_in[h] - fk_ref[h:h + 1, :] * LOG2E for h in heads]
        if on_diagonal:
            row = lax.broadcasted_iota(jnp.int32, (tq, tk), 0)
            col = lax.broadcasted_iota(jnp.int32, (tq, tk), 1)
            s = [jnp.where(col <= row, sh, -jnp.inf) for sh in s]
        m_new = [jnp.maximum(m_prev[h], jnp.max(s[h], axis=-1, keepdims=True)) for h in heads]
        alpha = [jnp.exp2(m_prev[h] - m_new[h]) for h in heads]
        p = [jnp.exp2(s[h] - m_new[h]) for h in heads]
        l_new = [alpha[h] * l_prev[h] + jnp.sum(p[h], axis=-1, keepdims=True) for h in heads]
        acc_new = [alpha[h] * acc_prev[h] + _dot(p[h].astype(BF16), v_ref[:, hs[h]]) for h in heads]
        return m_new, l_new, acc_new

    def step(s_write, s_read):
        @pl.when(t == 0)
        def _():
            score(s_write)

        @pl.when(jnp.logical_and(t >= 1, t <= i))
        def _():
            m_new, l_new, acc_new = update(s_read, False)
            score(s_write)
            for h in heads:
                m_scr[h] = m_new[h]
                l_scr[h] = l_new[h]
                acc_scr[h] = acc_new[h]

        @pl.when(t == i + 1)
        def _():
            _, l_new, acc_new = update(s_read, True)
            for h in heads:
                o_ref[:, hs[h]] = _rms(acc_new[h] / l_new[h], g_ref[:, hs[h]]).astype(o_ref.dtype)

    @pl.when(t % 2 == 0)
    def _():
        step(s_even, s_odd)

    @pl.when(t % 2 == 1)
    def _():
        step(s_odd, s_even)


def _fox(proj, f_t, g_fox, batch, seq, tq=512, tk=512):
    assert tq == tk, "the diagonal-block mask assumes square blocks"
    n = proj.shape[0]
    w = FOX_HEADS * FOX_DH
    nq, nk = seq // tq, seq // tk
    qcol, kcol, vcol = 3072 // w, 3584 // w, 4096 // w
    return pl.pallas_call(
        functools.partial(_fox_kernel, tq=tq, tk=tk),
        grid=(batch, nq, nk + 1),
        in_specs=[
            pl.BlockSpec((tq, w), lambda b, i, t: (b * nq + i, qcol)),
            pl.BlockSpec((tk, w), lambda b, i, t: (b * nk + jnp.minimum(t, i), kcol)),
            pl.BlockSpec((tk, w), lambda b, i, t: (b * nk + jnp.clip(t - 1, 0, i), vcol)),
            pl.BlockSpec((None, 8, tk), lambda b, i, t: (b, 0, jnp.clip(t - 1, 0, i))),
            pl.BlockSpec((1, w), lambda b, i, t: (0, 0)),
        ],
        out_specs=pl.BlockSpec((tq, w), lambda b, i, t: (b * nq + i, 0)),
        out_shape=jax.ShapeDtypeStruct((n, w), BF16),
        scratch_shapes=[
            pltpu.VMEM((FOX_HEADS, tq, 1), F32),
            pltpu.VMEM((FOX_HEADS, tq, 1), F32),
            pltpu.VMEM((FOX_HEADS, tq, FOX_DH), F32),
            pltpu.VMEM((FOX_HEADS, tq, tk), F32),
            pltpu.VMEM((FOX_HEADS, tq, tk), F32),
        ],
        compiler_params=_cparams(("parallel", "parallel", "arbitrary")),
        name="fox",
    )(proj, proj, proj, f_t, g_fox)


def _mem_kv_kernel(m_ref, g_ref, w_ref, kv_ref):
    kv_ref[...] = _dot(_rms(m_ref[...], g_ref[...]).astype(BF16), w_ref[...]).astype(kv_ref.dtype)


def _mem_kv(mem2d, g, w, tm=256):
    n, d = mem2d.shape
    nw = w.shape[1]
    return pl.pallas_call(
        _mem_kv_kernel,
        grid=(n // tm,),
        in_specs=[
            pl.BlockSpec((tm, d), lambda i: (i, 0)),
            pl.BlockSpec((1, d), lambda i: (0, 0)),
            pl.BlockSpec((d, nw), lambda i: (0, 0)),
        ],
        out_specs=pl.BlockSpec((tm, nw), lambda i: (i, 0)),
        out_shape=jax.ShapeDtypeStruct((n, nw), BF16),
        compiler_params=_cparams(("parallel",)),
        name="mem_kv",
    )(mem2d, g, w)


def _mem_attn_kernel(q_ref, k_ref, v_ref, g_ref, o_ref):
    scale = MEM_DH ** -0.5
    for h in range(MEM_HEADS):
        hs = slice(h * MEM_DH, (h + 1) * MEM_DH)
        s = _dot_nt(q_ref[:, hs], k_ref[:, hs]) * scale
        p = jnp.exp(s - jnp.max(s, axis=-1, keepdims=True))
        l = jnp.sum(p, axis=-1, keepdims=True)
        o = _dot((p / l).astype(BF16), v_ref[:, hs])
        o_ref[:, hs] = _rms(o, g_ref[:, hs]).astype(o_ref.dtype)


def _mem_attn(proj, kv, g_mem_out, batch, seq, n_mem, tq=1024):
    n = proj.shape[0]
    w = MEM_HEADS * MEM_DH
    nq = seq // tq
    qcol = 4608 // w
    return pl.pallas_call(
        _mem_attn_kernel,
        grid=(n // tq,),
        in_specs=[
            pl.BlockSpec((tq, w), lambda i: (i, qcol)),
            pl.BlockSpec((n_mem, w), lambda i: (i // nq, 0)),
            pl.BlockSpec((n_mem, w), lambda i: (i // nq, 1)),
            pl.BlockSpec((1, w), lambda i: (0, 0)),
        ],
        out_specs=pl.BlockSpec((tq, w), lambda i: (i, 0)),
        out_shape=jax.ShapeDtypeStruct((n, w), BF16),
        compiler_params=_cparams(("parallel",)),
        name="mem_attn",
    )(proj, kv, kv, g_mem_out)


def _out_proj_kernel(x_ref, gla_ref, fox_ref, mem_ref, w1_ref, w2_ref, w3_ref, g_ref, wr_ref, br_ref,
                     x2_ref, xn_ref, logit_ref):
    x2 = (x_ref[...] + _dot(gla_ref[...], w1_ref[...]) + _dot(fox_ref[...], w2_ref[...])
          + _dot(mem_ref[...], w3_ref[...]))
    x2_ref[...] = x2
    xn = _rms(x2, g_ref[...])
    _store_rowmajor(xn_ref, _pack_bf16_pairs(xn))
    xh = xn.astype(BF16)
    xl = (xn - xh.astype(F32)).astype(BF16)
    hi = _dot(xh, wr_ref[...])
    logit_ref[...] = hi[:, :LANES] + hi[:, LANES:] + _dot(xl, wr_ref[:, :LANES]) + br_ref[...]


def _out_proj(x2d, gla, fox, memo, w_out, g_ffn, wr, br, tm=256):
    n, d = x2d.shape
    w1, w2 = gla.shape[1], fox.shape[1]
    const = lambda i: (0, 0)
    return pl.pallas_call(
        _out_proj_kernel,
        grid=(n // tm,),
        in_specs=[
            pl.BlockSpec((tm, d), lambda i: (i, 0)),
            pl.BlockSpec((tm, w1), lambda i: (i, 0)),
            pl.BlockSpec((tm, w2), lambda i: (i, 0)),
            pl.BlockSpec((tm, w2), lambda i: (i, 0)),
            pl.BlockSpec((w1, d), lambda i: (0, 0)),
            pl.BlockSpec((w2, d), lambda i: (w1 // w2, 0)),
            pl.BlockSpec((w2, d), lambda i: (w1 // w2 + 1, 0)),
            pl.BlockSpec((1, d), const),
            pl.BlockSpec((d, 2 * LANES), const),
            pl.BlockSpec((1, LANES), const),
        ],
        out_specs=[
            pl.BlockSpec((tm, d), lambda i: (i, 0)),
            pl.BlockSpec((tm * (d // 2 // LANES), LANES), lambda i: (i, 0)),
            pl.BlockSpec((tm, LANES), lambda i: (i, 0)),
        ],
        out_shape=[
            jax.ShapeDtypeStruct((n, d), F32),
            jax.ShapeDtypeStruct((n * (d // 2 // LANES), LANES), jnp.uint32),
            jax.ShapeDtypeStruct((n, LANES), F32),
        ],
        compiler_params=_cparams(("parallel",)),
        name="out_proj",
    )(x2d, gla, fox, memo, w_out, w_out, w_out, g_ffn, wr, br)


def _route_kernel(l_ref, idx_ref, gate_ref, rank_ref, cnt_ref, carry_scr, *, tb):
    @pl.when(pl.program_id(0) == 0)
    def _():
        carry_scr[...] = jnp.zeros_like(carry_scr)

    lane = lax.broadcasted_iota(jnp.int32, (tb, LANES), 1)
    logit = jnp.where(lane < N_EXPERTS, l_ref[...], -jnp.inf)
    vals, hots = [], []
    idx_out = jnp.zeros((tb, LANES), jnp.int32)
    for k in range(TOP_K):
        m = jnp.max(logit, axis=-1, keepdims=True)
        ik = jnp.min(jnp.where(logit == m, lane, LANES), axis=-1, keepdims=True)
        hot = lane == ik
        logit = jnp.where(hot, -jnp.inf, logit)
        vals.append(m)
        hots.append(hot)
        idx_out = jnp.where(lane == k, ik, idx_out)
    idx_ref[...] = idx_out

    e = [jnp.exp(v - vals[0]) for v in vals]
    den = e[0] + e[1] + e[2] + e[3]
    gate_out = jnp.zeros((tb, LANES), F32)
    for k in range(TOP_K):
        gate_out = jnp.where(lane == k, e[k] / den, gate_out)
    gate_ref[...] = gate_out

    member = jnp.zeros((tb, LANES), F32)
    for hot in hots:
        member = member + hot.astype(F32)
    row = lax.broadcasted_iota(jnp.int32, (tb, tb), 0)
    col = lax.broadcasted_iota(jnp.int32, (tb, tb), 1)
    before = (col < row).astype(BF16)
    rank = _dot(before, member.astype(BF16)) + carry_scr[...]
    rank_out = jnp.zeros((tb, LANES), F32)
    for k in range(TOP_K):
        rk = jnp.sum(jnp.where(hots[k], rank, 0.0), axis=-1, keepdims=True)
        rank_out = jnp.where(lane == k, rk, rank_out)
    rank_ref[...] = rank_out.astype(jnp.int32)
    total = carry_scr[...] + jnp.sum(member, axis=0, keepdims=True)
    carry_scr[...] = total
    cnt_ref[...] = total.astype(jnp.int32)


def _route(logits, tb=512):
    n = logits.shape[0]
    blk = pl.BlockSpec((tb, LANES), lambda i: (i, 0))
    return pl.pallas_call(
        functools.partial(_route_kernel, tb=tb),
        grid=(n // tb,),
        in_specs=[blk],
        out_specs=[blk, blk, blk, pl.BlockSpec((1, LANES), lambda i: (0, 0))],
        out_shape=[
            jax.ShapeDtypeStruct((n, LANES), jnp.int32),
            jax.ShapeDtypeStruct((n, LANES), F32),
            jax.ShapeDtypeStruct((n, LANES), jnp.int32),
            jax.ShapeDtypeStruct((1, LANES), jnp.int32),
        ],
        scratch_shapes=[pltpu.VMEM((1, LANES), F32)],
        compiler_params=_cparams(("arbitrary",)),
        name="route",
    )(logits)


def _dispatch_kernel(zblk_ref, zok_ref, used_ref, pos_ref, xn_ref, xr_hbm, zero_scr, sem,
                     *, tb, pitch, n_blocks_max):
    blk_rows = MOE_TM * pitch

    def zero_copy(b):
        dst = xr_hbm.at[pl.ds(pl.multiple_of(b * blk_rows, blk_rows), blk_rows), :]
        return pltpu.make_async_copy(zero_scr, dst, sem.at[1])

    @pl.when(pl.program_id(0) == 0)
    def _():
        zero_scr[...] = jnp.zeros_like(zero_scr)

        def per_expert(action):
            def body(e, carry):
                @pl.when(zok_ref[e] == 1)
                def _():
                    action(zero_copy(zblk_ref[e]))
                return carry
            lax.fori_loop(0, N_EXPERTS, body, 0)

        def per_tail(action):
            def body(b, carry):
                action(zero_copy(b))
                return carry
            lax.fori_loop(used_ref[0], n_blocks_max, body, 0)

        per_expert(lambda c: c.start())
        per_tail(lambda c: c.start())
        per_expert(lambda c: c.wait())
        per_tail(lambda c: c.wait())

    def issue(t, carry):
        src = xn_ref.at[pl.ds(pl.multiple_of(t * pitch, pitch), pitch), :]
        for k in range(TOP_K):
            p = pos_ref[0, t * TOP_K + k]
            dst = xr_hbm.at[pl.ds(pl.multiple_of(p * pitch, pitch), pitch), :]
            pltpu.make_async_copy(src, dst, sem.at[0]).start(priority=k % 2)
        return carry

    lax.fori_loop(0, tb, issue, 0, unroll=4)
    for k in range(TOP_K):
        pltpu.make_async_copy(xn_ref, xr_hbm.at[pl.ds(0, tb * pitch), :], sem.at[0]).wait()


def _dispatch(xn_rm, pos, zblk, zok, n_used, n_blocks_max, pitch, tb=256):
    n = xn_rm.shape[0] // pitch
    nb = n // tb
    grid_spec = pltpu.PrefetchScalarGridSpec(
        num_scalar_prefetch=3,
        grid=(nb,),
        in_specs=[
            pl.BlockSpec((None, 1, TOP_K * tb), lambda i, *_: (i, 0, 0), memory_space=pltpu.SMEM),
            pl.BlockSpec((tb * pitch, LANES), lambda i, *_: (i, 0)),
        ],
        out_specs=pl.BlockSpec(memory_space=pl.ANY),
        scratch_shapes=[pltpu.VMEM((MOE_TM * pitch, LANES), xn_rm.dtype), pltpu.SemaphoreType.DMA((2,))],
    )
    return pl.pallas_call(
        functools.partial(_dispatch_kernel, tb=tb, pitch=pitch, n_blocks_max=n_blocks_max),
        grid_spec=grid_spec,
        out_shape=jax.ShapeDtypeStruct((n_blocks_max * MOE_TM * pitch, LANES), xn_rm.dtype),
        compiler_params=_cparams(("arbitrary",)),
        name="dispatch",
    )(zblk, zok, n_used, pos.reshape(nb, 1, TOP_K * tb), xn_rm)


def _cast_rows(src, dst, rows_per=256):
    def body(i, carry):
        r = pl.ds(pl.multiple_of(i * rows_per, rows_per), rows_per)
        dst[r, :] = src[r, :].astype(dst.dtype)
        return carry

    lax.fori_loop(0, src.shape[0] // rows_per, body, 0)


def _stream_row_blocks(cnt, in_copy, out_copy, compute):
    for ahead in range(MOE_NBUF - 1):
        @pl.when(cnt > ahead)
        def _():
            in_copy(ahead, ahead).start()

    def body(b, carry):
        slot = b % MOE_NBUF

        @pl.when(b + MOE_NBUF - 1 < cnt)
        def _():
            in_copy(b + MOE_NBUF - 1, (b + MOE_NBUF - 1) % MOE_NBUF).start()

        in_copy(b, slot).wait()

        @pl.when(b >= MOE_NBUF)
        def _():
            out_copy(b - MOE_NBUF, slot).wait()

        compute(slot)
        out_copy(b, slot).start()
        return carry

    lax.fori_loop(0, cnt, body, 0)

    for back in range(MOE_NBUF, 0, -1):
        @pl.when(cnt >= back)
        def _():
            out_copy(cnt - back, (cnt - back) % MOE_NBUF).wait()


def _stream_expert_rows(first, cnt, in_copy, out_copy, compute):
    n_units = cnt // MOE_UNIT
    _stream_row_blocks(n_units, in_copy(first, MOE_UNIT), out_copy(first, MOE_UNIT), compute(MOE_UNIT))
    done = n_units * MOE_UNIT
    _stream_row_blocks(cnt - done, in_copy(first + done, 1), out_copy(first + done, 1), compute(1))


def _zero_fill_blocks(first, last, zero_src, dst_copy):
    def start(b, carry):
        dst_copy(b).start()
        return carry

    def wait(b, carry):
        dst_copy(b).wait()
        return carry

    zero_src[...] = jnp.zeros_like(zero_src)
    lax.fori_loop(first, last, start, 0)
    lax.fori_loop(first, last, wait, 0)


def _moe_up_kernel(bs_ref, bc_ref, used_ref, x_hbm, w_hbm, bg_ref, bu_ref, act_hbm,
                   wf_scr, wb_scr, xbuf, lhs_scr, obuf, sem_w, sem_x, sem_o,
                   *, nt, tn, d_ff, pitch, n_blocks_max):
    g = pl.program_id(0)
    ng = pl.num_programs(0)
    e = g // nt
    col = pl.multiple_of((g % nt) * tn, tn)
    blk_rows = MOE_TM * pitch

    def w_copies(step):
        ee = step // nt
        cc = pl.multiple_of((step % nt) * tn, tn)
        return (pltpu.make_async_copy(w_hbm.at[ee, :, pl.ds(cc, tn)], wf_scr.at[0], sem_w.at[0]),
                pltpu.make_async_copy(w_hbm.at[ee, :, pl.ds(d_ff + cc, tn)], wf_scr.at[1], sem_w.at[1]))

    @pl.when(g == 0)
    def _():
        for c in w_copies(0):
            c.start(priority=1)

    for half, c in enumerate(w_copies(g)):
        c.wait()
        _cast_rows(wf_scr.at[half], wb_scr.at[half])

    @pl.when(g + 1 < ng)
    def _():
        for c in w_copies(g + 1):
            c.start(priority=1)

    start = bs_ref[e]

    def x_copy(first, nblk):
        def make(u, slot):
            r0 = pl.multiple_of((first + u * nblk) * blk_rows, blk_rows)
            return pltpu.make_async_copy(x_hbm.at[pl.ds(r0, nblk * blk_rows), :],
                                         xbuf.at[slot, pl.ds(0, nblk * blk_rows), :], sem_x.at[slot])
        return make

    def o_copy(first, nblk):
        def make(u, slot):
            r0 = pl.multiple_of((first + u * nblk) * MOE_TM, MOE_TM)
            return pltpu.make_async_copy(obuf.at[slot, pl.ds(0, nblk * MOE_TM), :],
                                         act_hbm.at[pl.ds(r0, nblk * MOE_TM), pl.ds(col, tn)], sem_o.at[slot])
        return make

    def compute(nblk):
        rows = nblk * MOE_TM

        def run(slot):
            half = pitch * LANES
            for c in range(pitch):
                w = xbuf[slot, pl.ds(c, rows, stride=pitch), :]
                lhs_scr[0:rows, c * LANES:(c + 1) * LANES] = _unpack_lo(w).astype(BF16)
                lhs_scr[0:rows, half + c * LANES:half + (c + 1) * LANES] = _unpack_hi(w).astype(BF16)
            x = lhs_scr[0:rows, :]
            gate = jnp.minimum(_dot(x, wb_scr[0]) + bg_ref[...], SWIGLU_LIMIT)
            up = jnp.clip(_dot(x, wb_scr[1]) + bu_ref[...], -SWIGLU_LIMIT, SWIGLU_LIMIT)
            obuf[slot, 0:rows, :] = (gate * jax.nn.sigmoid(SWIGLU_ALPHA * gate) * (up + 1.0)).astype(obuf.dtype)
        return run

    _stream_expert_rows(start, bc_ref[e], x_copy, o_copy, compute)

    @pl.when(e == N_EXPERTS - 1)
    def _():
        zero_blk = obuf.at[0, pl.ds(0, MOE_TM), :]

        def tail_copy(b):
            r0 = pl.multiple_of(b * MOE_TM, MOE_TM)
            return pltpu.make_async_copy(zero_blk, act_hbm.at[pl.ds(r0, MOE_TM), pl.ds(col, tn)], sem_o.at[0])
        _zero_fill_blocks(used_ref[0], n_blocks_max, zero_blk, tail_copy)


def _moe_up(x_rows_rm, w_up, b_up, blk_start, blk_count, n_used, n_blocks_max, tn=1024):
    n_exp, d, two_ff = w_up.shape
    d_ff = two_ff // 2
    pitch = d // 2 // LANES
    nt = d_ff // tn
    up_off = d_ff // tn
    grid_spec = pltpu.PrefetchScalarGridSpec(
        num_scalar_prefetch=3,
        grid=(n_exp * nt,),
        in_specs=[
            pl.BlockSpec(memory_space=pl.ANY),
            pl.BlockSpec(memory_space=pl.ANY),
            pl.BlockSpec((None, 1, tn), lambda g, *_: (g // nt, 0, g % nt)),
            pl.BlockSpec((None, 1, tn), lambda g, *_: (g // nt, 0, g % nt + up_off)),
        ],
        out_specs=pl.BlockSpec(memory_space=pl.ANY),
        scratch_shapes=[
            pltpu.VMEM((2, d, tn), F32),
            pltpu.VMEM((2, d, tn), BF16),
            pltpu.VMEM((MOE_NBUF, MOE_UNIT * MOE_TM * pitch, LANES), jnp.uint32),
            pltpu.VMEM((MOE_UNIT * MOE_TM, d), BF16),
            pltpu.VMEM((MOE_NBUF, MOE_UNIT * MOE_TM, tn), BF16),
            pltpu.SemaphoreType.DMA((2,)),
            pltpu.SemaphoreType.DMA((MOE_NBUF,)),
            pltpu.SemaphoreType.DMA((MOE_NBUF,)),
        ],
    )
    return pl.pallas_call(
        functools.partial(_moe_up_kernel, nt=nt, tn=tn, d_ff=d_ff, pitch=pitch, n_blocks_max=n_blocks_max),
        grid_spec=grid_spec,
        out_shape=jax.ShapeDtypeStruct((n_blocks_max * MOE_TM, d_ff), BF16),
        compiler_params=_cparams(("arbitrary",)),
        name="moe_up",
    )(blk_start, blk_count, n_used, x_rows_rm, w_up, b_up, b_up)


def _moe_down_kernel(bs_ref, bc_ref, used_ref, a_hbm, w_hbm, b_ref, y_hbm,
                     wf_scr, wb_scr, abuf, obuf, sem_w, sem_a, sem_o, *, pitch, n_blocks_max):
    e = pl.program_id(0)
    blk_rows = MOE_TM * pitch

    def w_copy(ee):
        return pltpu.make_async_copy(w_hbm.at[ee], wf_scr, sem_w.at[0])

    @pl.when(e == 0)
    def _():
        w_copy(0).start(priority=1)

    w_copy(e).wait()
    _cast_rows(wf_scr, wb_scr)

    @pl.when(e + 1 < pl.num_programs(0))
    def _():
        w_copy(e + 1).start(priority=1)

    start = bs_ref[e]

    def a_copy(first, nblk):
        def make(u, slot):
            r0 = pl.multiple_of((first + u * nblk) * MOE_TM, MOE_TM)
            return pltpu.make_async_copy(a_hbm.at[pl.ds(r0, nblk * MOE_TM), :],
                                         abuf.at[slot, pl.ds(0, nblk * MOE_TM), :], sem_a.at[slot])
        return make

    def o_copy(first, nblk):
        def make(u, slot):
            r0 = pl.multiple_of((first + u * nblk) * blk_rows, blk_rows)
            return pltpu.make_async_copy(obuf.at[slot, pl.ds(0, nblk * blk_rows), :],
                                         y_hbm.at[pl.ds(r0, nblk * blk_rows), :], sem_o.at[slot])
        return make

    def compute(nblk):
        rows = nblk * MOE_TM

        def run(slot):
            y = _dot(abuf[slot, 0:rows, :], wb_scr[...]) + b_ref[...]
            _store_rowmajor(obuf.at[slot], _pack_bf16_pairs(y))
        return run

    _stream_expert_rows(start, bc_ref[e], a_copy, o_copy, compute)

    @pl.when(e == N_EXPERTS - 1)
    def _():
        zero_blk = obuf.at[0, pl.ds(0, blk_rows), :]

        def tail_copy(b):
            r0 = pl.multiple_of(b * blk_rows, blk_rows)
            return pltpu.make_async_copy(zero_blk, y_hbm.at[pl.ds(r0, blk_rows), :], sem_o.at[0])
        _zero_fill_blocks(used_ref[0], n_blocks_max, zero_blk, tail_copy)


def _moe_down(act, w_down, b_down, blk_start, blk_count, n_used, n_blocks_max):
    n_rows, d_ff = act.shape
    n_exp, _, d = w_down.shape
    pitch = d // 2 // LANES
    grid_spec = pltpu.PrefetchScalarGridSpec(
        num_scalar_prefetch=3,
        grid=(n_exp,),
        in_specs=[
            pl.BlockSpec(memory_space=pl.ANY),
            pl.BlockSpec(memory_space=pl.ANY),
            pl.BlockSpec((None, 1, d), lambda e, *_: (e, 0, 0)),
        ],
        out_specs=pl.BlockSpec(memory_space=pl.ANY),
        scratch_shapes=[
            pltpu.VMEM((d_ff, d), F32),
            pltpu.VMEM((d_ff, d), BF16),
            pltpu.VMEM((MOE_NBUF, MOE_UNIT * MOE_TM, d_ff), BF16),
            pltpu.VMEM((MOE_NBUF, MOE_UNIT * MOE_TM * pitch, LANES), jnp.uint32),
            pltpu.SemaphoreType.DMA((1,)),
            pltpu.SemaphoreType.DMA((MOE_NBUF,)),
            pltpu.SemaphoreType.DMA((MOE_NBUF,)),
        ],
    )
    return pl.pallas_call(
        functools.partial(_moe_down_kernel, pitch=pitch, n_blocks_max=n_blocks_max),
        grid_spec=grid_spec,
        out_shape=jax.ShapeDtypeStruct((n_rows * pitch, LANES), jnp.uint32),
        compiler_params=_cparams(("arbitrary",)),
        name="moe_down",
    )(blk_start, blk_count, n_used, act, w_down, b_down)


def _combine_kernel(pos0_ref, posn_ref, y_hbm, x2_ref, gate_ref, g_ref, o_ref, buf, sem, *, tb, pitch):
    i = pl.program_id(0)
    n_rows = TOP_K * tb

    def issue_block(p_ref, slot):
        def issue(j, carry):
            for half in range(2):
                r = 2 * j + half
                src = y_hbm.at[pl.ds(pl.multiple_of(p_ref[0, r] * pitch, pitch), pitch), :]
                d0 = pl.multiple_of((slot * n_rows + r) * pitch, pitch)
                pltpu.make_async_copy(src, buf.at[pl.ds(d0, pitch), :], sem.at[slot]).start(priority=half)
            return carry
        lax.fori_loop(0, n_rows // 2, issue, 0, unroll=4)

    @pl.when(i == 0)
    def _():
        issue_block(pos0_ref, 0)

    @pl.when(i + 1 < pl.num_programs(0))
    def _():
        issue_block(posn_ref, (i + 1) % 2)

    slot = i % 2
    base = pl.multiple_of(slot * (n_rows * pitch), pitch)
    pltpu.make_async_copy(y_hbm.at[pl.ds(0, n_rows * pitch), :], buf.at[pl.ds(base, n_rows * pitch), :],
                          sem.at[slot]).wait()
    gates = gate_ref[...]
    half = pitch * LANES
    ssq = jnp.zeros((tb, 1), F32)
    for c in range(pitch):
        lo = slice(c * LANES, (c + 1) * LANES)
        hi = slice(half + c * LANES, half + (c + 1) * LANES)
        z_lo = x2_ref[:, lo]
        z_hi = x2_ref[:, hi]
        for k in range(TOP_K):
            w = buf[pl.ds(base + k * tb * pitch + c, tb, stride=pitch), :]
            z_lo = z_lo + gates[:, k:k + 1] * _unpack_lo(w)
            z_hi = z_hi + gates[:, k:k + 1] * _unpack_hi(w)
        o_ref[:, lo] = z_lo
        o_ref[:, hi] = z_hi
        ssq = ssq + jnp.sum(z_lo * z_lo, axis=-1, keepdims=True) + jnp.sum(z_hi * z_hi, axis=-1, keepdims=True)
    o_ref[...] = o_ref[...] * lax.rsqrt(ssq * (1.0 / (2 * half)) + EPS) * g_ref[...]


def _combine(y_rows_rm, pos, gates, x2, g_final, tb=256):
    n, d = x2.shape
    nb = n // tb
    pitch = d // 2 // LANES
    pos_blk = pos.reshape(nb, tb, TOP_K).transpose(0, 2, 1).reshape(nb, 1, TOP_K * tb)
    pos_spec = lambda imap: pl.BlockSpec((None, 1, TOP_K * tb), imap, memory_space=pltpu.SMEM)
    return pl.pallas_call(
        functools.partial(_combine_kernel, tb=tb, pitch=pitch),
        grid=(nb,),
        in_specs=[
            pos_spec(lambda i: (0, 0, 0)),
            pos_spec(lambda i: (jnp.minimum(i + 1, nb - 1), 0, 0)),
            pl.BlockSpec(memory_space=pl.ANY),
            pl.BlockSpec((tb, d), lambda i: (i, 0)),
            pl.BlockSpec((tb, LANES), lambda i: (i, 0)),
            pl.BlockSpec((1, d), lambda i: (0, 0)),
        ],
        out_specs=pl.BlockSpec((tb, d), lambda i: (i, 0)),
        out_shape=jax.ShapeDtypeStruct((n, d), F32),
        scratch_shapes=[pltpu.VMEM((2 * TOP_K * tb * pitch, LANES), jnp.uint32),
                        pltpu.SemaphoreType.DMA((2,))],
        compiler_params=_cparams(("arbitrary",)),
        name="combine",
    )(pos_blk, pos_blk, y_rows_rm, x2, gates, g_final)


def _routing_tables(idx, rank, cnt):
    counts = cnt[0, :N_EXPERTS]
    blk_count = (counts + MOE_TM - 1) // MOE_TM
    blk_end = jnp.cumsum(blk_count)
    blk_start = blk_end - blk_count
    hot = idx[:, :TOP_K, None] == jnp.arange(N_EXPERTS, dtype=jnp.int32)
    pos = jnp.sum(jnp.where(hot, blk_start * MOE_TM, 0), axis=-1) + rank[:, :TOP_K]
    return pos.astype(jnp.int32), blk_start.astype(jnp.int32), blk_count.astype(jnp.int32)


def kernel(x, mem, g_attn_norm, g_mem_norm, w_in, w_gla_a2, b_gla_a, g_gla_out, b_fox_f, g_fox_out,
           w_mem_kv, g_mem_out, w_out, g_ffn_norm, w_router, b_router, w_moe_up, b_moe_up,
           w_moe_down, b_moe_down, g_final):
    batch, seq, d = x.shape
    n_mem = mem.shape[1]
    n = batch * seq
    depth = w_in.shape[0]
    assert depth == 1, "the combine kernel applies the final norm, so exactly one layer is supported"
    kw = GLA_HEADS * GLA_DK
    vw = GLA_HEADS * GLA_DV
    fw = FOX_HEADS * FOX_DH
    mw = MEM_HEADS * MEM_DH
    o_q, o_k, o_v, o_g = 0, kw, 2 * kw, 2 * kw + vw
    o_a = o_g + vw
    o_fq = o_a + GLA_LOWRANK
    o_fk, o_fv = o_fq + fw, o_fq + 2 * fw
    o_ff = o_fq + 3 * fw
    o_mq = o_ff + FOX_HEADS
    f_lane = GLA_LOWRANK

    xf = x.reshape(n, d)
    for l in range(depth):
        wi = w_in[l]
        fox_q_scale = FOX_DH ** -0.5 * LOG2E
        wa = jnp.concatenate([wi[:, o_q:o_a], wi[:, o_fq:o_fk] * fox_q_scale, wi[:, o_fk:o_ff],
                              wi[:, o_mq:o_mq + mw]], axis=1).astype(BF16)
        wb = jnp.zeros((d, LANES), F32).at[:, :GLA_LOWRANK].set(wi[:, o_a:o_fq])
        wb = wb.at[:, f_lane:f_lane + FOX_HEADS].set(wi[:, o_ff:o_mq]).astype(BF16)
        proj, small = _in_proj(xf, g_attn_norm[l].reshape(1, d), wa, wb)

        wa2p = jnp.zeros((LANES, kw), F32).at[:GLA_LOWRANK].set(w_gla_a2[l]).astype(BF16)
        bfv = jnp.zeros((1, LANES), F32).at[0, f_lane:f_lane + FOX_HEADS].set(b_fox_f[l])
        gla, f_cum = _gla(proj, small, wa2p, b_gla_a[l].reshape(1, kw), bfv,
                          g_gla_out[l].reshape(1, vw), batch, seq)

        f_t = f_cum.reshape(batch, seq, LANES)[:, :, f_lane:f_lane + FOX_HEADS].transpose(0, 2, 1)
        f_t = jnp.concatenate([f_t, jnp.zeros_like(f_t)], axis=1)
        fox = _fox(proj, f_t, g_fox_out[l].reshape(1, fw), batch, seq)

        kv = _mem_kv(mem.reshape(batch * n_mem, d), g_mem_norm[l].reshape(1, d), w_mem_kv[l].astype(BF16))
        memo = _mem_attn(proj, kv, g_mem_out[l].reshape(1, mw), batch, seq, n_mem)

        wr = jnp.zeros((d, LANES), F32).at[:, :N_EXPERTS].set(w_router[l])
        wr_hi = wr.astype(BF16)
        wr = jnp.concatenate([wr_hi, (wr - wr_hi.astype(F32)).astype(BF16)], axis=1)
        br =jnp.zeros((1, LANES), F32).at[0, :N_EXPERTS].set(b_router[l])
        x2, xn, logits = _out_proj(xf, gla, fox, memo, w_out[l].astype(BF16),
                                   g_ffn_norm[l].reshape(1, d), wr, br)

        idx, gates, rank, cnt = _route(logits)
        pos, blk_start, blk_count = _routing_tables(idx, rank, cnt)
        n_blocks_max = -(-(n * TOP_K + N_EXPERTS * (MOE_TM - 1)) // MOE_TM)
        blk_end = blk_start + blk_count
        n_used = blk_end[-1:]
        x_rows = _dispatch(xn, pos, blk_end - 1, (blk_count > 0).astype(jnp.int32), n_used,
                           n_blocks_max, d // 2 // LANES)
        d_ff = w_moe_up.shape[3] // 2
        act = _moe_up(x_rows, w_moe_up[l], b_moe_up[l].reshape(N_EXPERTS, 1, 2 * d_ff),
                      blk_start, blk_count, n_used, n_blocks_max)
        y_rows = _moe_down(act, w_moe_down[l], b_moe_down[l].reshape(N_EXPERTS, 1, d),
                           blk_start, blk_count, n_used, n_blocks_max)
        xf = _combine(y_rows, pos, gates, x2, g_final.reshape(1, d))
    return xf.reshape(batch, seq, d)
```

```python
import functools

import jax
import jax.numpy as jnp
from jax import lax
from jax.experimental import pallas as pl
from jax.experimental.pallas import tpu as pltpu

EPS = 1e-5
CHUNK = 64
GLA_HEADS = 4
GLA_DK = 128
GLA_DV = 256
GLA_LOWRANK = 16
GLA_TAU = 16.0
FOX_HEADS = 4
FOX_DH = 128
MEM_HEADS = 4
MEM_DH = 128
N_EXPERTS = 32
TOP_K = 4
SWIGLU_LIMIT = 7.0
SWIGLU_ALPHA = 1.702
LANES = 128
MXU_COLS = 256
MOE_TM = 256
MOE_NBUF = 2
MOE_UNIT = 2
VMEM_LIMIT = 56 * 1024 * 1024

F32 = jnp.float32
BF16 = jnp.bfloat16


def _cparams(sem, vmem=VMEM_LIMIT):
    return pltpu.CompilerParams(dimension_semantics=sem, vmem_limit_bytes=vmem)


def _log_sigmoid(x):
    return jnp.minimum(x, 0.0) - jnp.log1p(jnp.exp(-jnp.abs(x)))


def _rms(x, g):
    return x * lax.rsqrt(jnp.mean(x * x, axis=-1, keepdims=True) + EPS) * g


def _dot(a, b, **kw):
    return jnp.dot(a, b, preferred_element_type=F32, **kw)


def _dot_nt(a, b):
    return lax.dot_general(a, b, (((1,), (1,)), ((), ())), preferred_element_type=F32)


def _dot_tn(a, b):
    return lax.dot_general(a, b, (((0,), (0,)), ((), ())), preferred_element_type=F32)


def _store_rowmajor(ref, x, base=0):
    rows, w = x.shape
    pitch = w // LANES
    for c in range(pitch):
        ref[pl.ds(base + c, rows, stride=pitch), :] = x[:, c * LANES:(c + 1) * LANES]


def _pack_bf16_pairs(x):
    half = x.shape[1] // 2
    bits = lambda v: lax.bitcast_convert_type(v.astype(BF16).astype(F32), jnp.uint32)
    return bits(x[:, half:]) | (bits(x[:, :half]) >> 16)


def _unpack_lo(w):
    return lax.bitcast_convert_type(w << 16, F32)


def _unpack_hi(w):
    return lax.bitcast_convert_type(w & jnp.uint32(0xFFFF0000), F32)


def _in_proj_kernel(x_ref, g_ref, wa_ref, wb_ref, proj_ref, small_ref, h_scr):
    @pl.when(pl.program_id(1) == 0)
    def _():
        hb = _rms(x_ref[...], g_ref[...]).astype(BF16)
        h_scr[...] = hb
        small_ref[...] = _dot(hb, wb_ref[...])

    proj_ref[...] = _dot(h_scr[...], wa_ref[...]).astype(proj_ref.dtype)


def _in_proj(x2d, g, wa, wb, tm=1024, tn=1280):
    n, d = x2d.shape
    na = wa.shape[1]
    return pl.pallas_call(
        _in_proj_kernel,
        grid=(n // tm, na // tn),
        in_specs=[
            pl.BlockSpec((tm, d), lambda i, j: (i, 0)),
            pl.BlockSpec((1, d), lambda i, j: (0, 0)),
            pl.BlockSpec((d, tn), lambda i, j: (0, j)),
            pl.BlockSpec((d, LANES), lambda i, j: (0, 0)),
        ],
        out_specs=[
            pl.BlockSpec((tm, tn), lambda i, j: (i, j)),
            pl.BlockSpec((tm, LANES), lambda i, j: (i, 0)),
        ],
        out_shape=[
            jax.ShapeDtypeStruct((n, na), BF16),
            jax.ShapeDtypeStruct((n, LANES), F32),
        ],
        scratch_shapes=[pltpu.VMEM((tm, d), BF16)],
        compiler_params=_cparams(("parallel", "arbitrary")),
        name="in_proj",
    )(x2d, g, wa, wb)


def _gla_kernel(q_ref, k_ref, v_ref, gate_ref, small_ref, wa2_ref, ba_ref, bf_ref, gout_ref,
                o_ref, f_ref, state_scr, fcar_scr, la_scr, lf_scr, *, n_chunks):
    @pl.when(pl.program_id(1) == 0)
    def _():
        state_scr[...] = jnp.zeros_like(state_scr)
        fcar_scr[...] = jnp.zeros_like(fcar_scr)

    small = small_ref[...]
    la_scr[...] = _log_sigmoid(_dot(small.astype(BF16), wa2_ref[...]) + ba_ref[...]) * (1.0 / GLA_TAU)
    lf_scr[...] = _log_sigmoid(small + bf_ref[...])
    row = lax.broadcasted_iota(jnp.int32, (CHUNK, CHUNK), 0)
    col = lax.broadcasted_iota(jnp.int32, (CHUNK, CHUNK), 1)
    tri = (col <= row).astype(F32)
    scale = GLA_DK ** -0.5

    def chunk_body(c, carry):
        r = pl.ds(pl.multiple_of(c * CHUNK, CHUNK), CHUNK)
        b = _dot(tri, la_scr[r, :], precision=lax.Precision.HIGHEST)
        b_end = b[CHUNK - 1:CHUNK, :]
        k_dec = k_ref[r, :].astype(F32) * jnp.exp(b_end - b)
        decay = jnp.exp(b_end)
        f_cum = _dot(tri, lf_scr[r, :], precision=lax.Precision.HIGHEST) + fcar_scr[...]
        f_ref[r, :] = f_cum
        fcar_scr[...] = f_cum[CHUNK - 1:CHUNK, :]
        heads = range(GLA_HEADS)
        ks = [slice(h * GLA_DK, (h + 1) * GLA_DK) for h in heads]
        vs = [slice(h * GLA_DV, (h + 1) * GLA_DV) for h in heads]
        old = [state_scr[h] for h in heads]
        new = [old[h] * decay[:, ks[h]] + _dot_tn(v_ref[r, vs[h]], k_dec[:, ks[h]].astype(BF16))
               for h in heads]
        for h in heads:
            state_scr[h] = new[h]
        for h in heads:
            o = _dot_nt(q_ref[r, ks[h]], new[h].astype(BF16)) * scale
            gt = gate_ref[r, vs[h]].astype(F32)
            o_ref[r, vs[h]] = (_rms(o, gout_ref[:, vs[h]]) * (gt * jax.nn.sigmoid(gt))).astype(o_ref.dtype)
        return carry

    lax.fori_loop(0, n_chunks, chunk_body, 0, unroll=True)


def _gla(proj, small, wa2p, ba, bfv, gout, batch, seq, ts=512):
    n = proj.shape[0]
    nsb = seq // ts
    kw = GLA_HEADS * GLA_DK
    vw = GLA_HEADS * GLA_DV
    row = lambda b, s: b * nsb + s
    return pl.pallas_call(
        functools.partial(_gla_kernel, n_chunks=ts // CHUNK),
        grid=(batch, nsb),
        in_specs=[
            pl.BlockSpec((ts, kw), lambda b, s: (row(b, s), 0)),
            pl.BlockSpec((ts, kw), lambda b, s: (row(b, s), 1)),
            pl.BlockSpec((ts, vw), lambda b, s: (row(b, s), 1)),
            pl.BlockSpec((ts, vw), lambda b, s: (row(b, s), 2)),
            pl.BlockSpec((ts, LANES), lambda b, s: (row(b, s), 0)),
            pl.BlockSpec((LANES, kw), lambda b, s: (0, 0)),
            pl.BlockSpec((1, kw), lambda b, s: (0, 0)),
            pl.BlockSpec((1, LANES), lambda b, s: (0, 0)),
            pl.BlockSpec((1, vw), lambda b, s: (0, 0)),
        ],
        out_specs=[
            pl.BlockSpec((ts, vw), lambda b, s: (row(b, s), 0)),
            pl.BlockSpec((ts, LANES), lambda b, s: (row(b, s), 0)),
        ],
        out_shape=[
            jax.ShapeDtypeStruct((n, vw), BF16),
            jax.ShapeDtypeStruct((n, LANES), F32),
        ],
        scratch_shapes=[
            pltpu.VMEM((GLA_HEADS, GLA_DV, GLA_DK), F32),
            pltpu.VMEM((1, LANES), F32),
            pltpu.VMEM((ts, kw), F32),
            pltpu.VMEM((ts, LANES), F32),
        ],
        compiler_params=_cparams(("parallel", "arbitrary")),
        name="gla",
    )(proj, proj, proj, proj, small, wa2p, ba, bfv, gout)


LOG2E = 1.4426950408889634
def _fox_kernel(q_ref, k_ref, v_ref, fk_ref, g_ref, o_ref, m_scr, l_scr, acc_scr, s_even, s_odd,
                *, tq, tk):
    i = pl.program_id(1)
    t = pl.program_id(2)

    @pl.when(t == 0)
    def _():
        m_scr[...] = jnp.full_like(m_scr, -jnp.inf)
        l_scr[...] = jnp.zeros_like(l_scr)
        acc_scr[...] = jnp.zeros_like(acc_scr)

    heads = range(FOX_HEADS)
    hs = [slice(h * FOX_DH, (h + 1) * FOX_DH) for h in heads]

    def score(s_out):
        for h in heads:
            s_out[h] = _dot_nt(q_ref[:, hs[h]], k_ref[:, hs[h]])

    def update(s_in, on_diagonal):
        m_prev = [m_scr[h] for h in heads]
        l_prev = [l_scr[h] for h in heads]
        acc_prev = [acc_scr[h] for h in heads]
        s = [s_in[h] - fk_ref[h:h + 1, :] * LOG2E for h in heads]
        if on_diagonal:
            row = lax.broadcasted_iota(jnp.int32, (tq, tk), 0)
            col = lax.broadcasted_iota(jnp.int32, (tq, tk), 1)
            s = [jnp.where(col <= row, sh, -jnp.inf) for sh in s]
        m_new = [jnp.maximum(m_prev[h], jnp.max(s[h], axis=-1, keepdims=True)) for h in heads]
        alpha = [jnp.exp2(m_prev[h] - m_new[h]) for h in heads]
        p = [jnp.exp2(s[h] - m_new[h]) for h in heads]
        l_new = [alpha[h] * l_prev[h] + jnp.sum(p[h], axis=-1, keepdims=True) for h in heads]
        acc_new = [alpha[h] * acc_prev[h] + _dot(p[h].astype(BF16), v_ref[:, hs[h]]) for h in heads]
        return m_new, l_new, acc_new

    def step(s_write, s_read):
        @pl.when(t == 0)
        def _():
            score(s_write)

        @pl.when(jnp.logical_and(t >= 1, t <= i))
        def _():
            m_new, l_new, acc_new = update(s_read, False)
            score(s_write)
            for h in heads:
                m_scr[h] = m_new[h]
                l_scr[h] = l_new[h]
                acc_scr[h] = acc_new[h]

        @pl.when(t == i + 1)
        def _():
            _, l_new, acc_new = update(s_read, True)
            for h in heads:
                o_ref[:, hs[h]] = _rms(acc_new[h] / l_new[h], g_ref[:, hs[h]]).astype(o_ref.dtype)

    @pl.when(t % 2 == 0)
    def _():
        step(s_even, s_odd)

    @pl.when(t % 2 == 1)
    def _():
        step(s_odd, s_even)


def _fox(proj, f_t, g_fox, batch, seq, tq=512, tk=512):
    assert tq == tk, "the diagonal-block mask assumes square blocks"
    n = proj.shape[0]
    w = FOX_HEADS * FOX_DH
    nq, nk = seq // tq, seq // tk
    qcol, kcol, vcol = 3072 // w, 3584 // w, 4096 // w
    return pl.pallas_call(
        functools.partial(_fox_kernel, tq=tq, tk=tk),
        grid=(batch, nq, nk + 1),
        in_specs=[
            pl.BlockSpec((tq, w), lambda b, i, t: (b * nq + i, qcol)),
            pl.BlockSpec((tk, w), lambda b, i, t: (b * nk + jnp.minimum(t, i), kcol)),
            pl.BlockSpec((tk, w), lambda b, i, t: (b * nk + jnp.clip(t - 1, 0, i), vcol)),
            pl.BlockSpec((None, 8, tk), lambda b, i, t: (b, 0, jnp.clip(t - 1, 0, i))),
            pl.BlockSpec((1, w), lambda b, i, t: (0, 0)),
        ],
        out_specs=pl.BlockSpec((tq, w), lambda b, i, t: (b * nq + i, 0)),
        out_shape=jax.ShapeDtypeStruct((n, w), BF16),
        scratch_shapes=[
            pltpu.VMEM((FOX_HEADS, tq, 1), F32),
            pltpu.VMEM((FOX_HEADS, tq, 1), F32),
            pltpu.VMEM((FOX_HEADS, tq, FOX_DH), F32),
            pltpu.VMEM((FOX_HEADS, tq, tk), F32),
            pltpu.VMEM((FOX_HEADS, tq, tk), F32),
        ],
        compiler_params=_cparams(("parallel", "parallel", "arbitrary")),
        name="fox",
    )(proj, proj, proj, f_t, g_fox)


def _mem_kv_kernel(m_ref, g_ref, w_ref, kv_ref):
    kv_ref[...] = _dot(_rms(m_ref[...], g_ref[...]).astype(BF16), w_ref[...]).astype(kv_ref.dtype)


def _mem_kv(mem2d, g, w, tm=256):
    n, d = mem2d.shape
    nw = w.shape[1]
    return pl.pallas_call(
        _mem_kv_kernel,
        grid=(n // tm,),
        in_specs=[
            pl.BlockSpec((tm, d), lambda i: (i, 0)),
            pl.BlockSpec((1, d), lambda i: (0, 0)),
            pl.BlockSpec((d, nw), lambda i: (0, 0)),
        ],
        out_specs=pl.BlockSpec((tm, nw), lambda i: (i, 0)),
        out_shape=jax.ShapeDtypeStruct((n, nw), BF16),
        compiler_params=_cparams(("parallel",)),
        name="mem_kv",
    )(mem2d, g, w)


def _mem_attn_kernel(q_ref, k_ref, v_ref, g_ref, o_ref):
    scale = MEM_DH ** -0.5
    for h in range(MEM_HEADS):
        hs = slice(h * MEM_DH, (h + 1) * MEM_DH)
        s = _dot_nt(q_ref[:, hs], k_ref[:, hs]) * scale
        p = jnp.exp(s - jnp.max(s, axis=-1, keepdims=True))
        l = jnp.sum(p, axis=-1, keepdims=True)
        o = _dot((p / l).astype(BF16), v_ref[:, hs])
        o_ref[:, hs] = _rms(o, g_ref[:, hs]).astype(o_ref.dtype)


def _mem_attn(proj, kv, g_mem_out, batch, seq, n_mem, tq=1024):
    n = proj.shape[0]
    w = MEM_HEADS * MEM_DH
    nq = seq // tq
    qcol = 4608 // w
    return pl.pallas_call(
        _mem_attn_kernel,
        grid=(n // tq,),
        in_specs=[
            pl.BlockSpec((tq, w), lambda i: (i, qcol)),
            pl.BlockSpec((n_mem, w), lambda i: (i // nq, 0)),
            pl.BlockSpec((n_mem, w), lambda i: (i // nq, 1)),
            pl.BlockSpec((1, w), lambda i: (0, 0)),
        ],
        out_specs=pl.BlockSpec((tq, w), lambda i: (i, 0)),
        out_shape=jax.ShapeDtypeStruct((n, w), BF16),
        compiler_params=_cparams(("parallel",)),
        name="mem_attn",
    )(proj, kv, kv, g_mem_out)


def _out_proj_kernel(x_ref, gla_ref, fox_ref, mem_ref, w1_ref, w2_ref, w3_ref, g_ref, wr_ref, br_ref,
                     x2_ref, xn_ref, logit_ref):
    x2 = (x_ref[...] + _dot(gla_ref[...], w1_ref[...]) + _dot(fox_ref[...], w2_ref[...])
          + _dot(mem_ref[...], w3_ref[...]))
    x2_ref[...] = x2
    xn = _rms(x2, g_ref[...])
    _store_rowmajor(xn_ref, _pack_bf16_pairs(xn))
    xh = xn.astype(BF16)
    xl = (xn - xh.astype(F32)).astype(BF16)
    hi = _dot(xh, wr_ref[...])
    logit_ref[...] = hi[:, :LANES] + hi[:, LANES:] + _dot(xl, wr_ref[:, :LANES]) + br_ref[...]


def _out_proj(x2d, gla, fox, memo, w_out, g_ffn, wr, br, tm=256):
    n, d = x2d.shape
    w1, w2 = gla.shape[1], fox.shape[1]
    const = lambda i: (0, 0)
    return pl.pallas_call(
        _out_proj_kernel,
        grid=(n // tm,),
        in_specs=[
            pl.BlockSpec((tm, d), lambda i: (i, 0)),
            pl.BlockSpec((tm, w1), lambda i: (i, 0)),
            pl.BlockSpec((tm, w2), lambda i: (i, 0)),
            pl.BlockSpec((tm, w2), lambda i: (i, 0)),
            pl.BlockSpec((w1, d), lambda i: (0, 0)),
            pl.BlockSpec((w2, d), lambda i: (w1 // w2, 0)),
            pl.BlockSpec((w2, d), lambda i: (w1 // w2 + 1, 0)),
            pl.BlockSpec((1, d), const),
            pl.BlockSpec((d, 2 * LANES), const),
            pl.BlockSpec((1, LANES), const),
        ],
        out_specs=[
            pl.BlockSpec((tm, d), lambda i: (i, 0)),
            pl.BlockSpec((tm * (d // 2 // LANES), LANES), lambda i: (i, 0)),
            pl.BlockSpec((tm, LANES), lambda i: (i, 0)),
        ],
        out_shape=[
            jax.ShapeDtypeStruct((n, d), F32),
            jax.ShapeDtypeStruct((n * (d // 2 // LANES), LANES), jnp.uint32),
            jax.ShapeDtypeStruct((n, LANES), F32),
        ],
        compiler_params=_cparams(("parallel",)),
        name="out_proj",
    )(x2d, gla, fox, memo, w_out, w_out, w_out, g_ffn, wr, br)


def _route_kernel(l_ref, idx_ref, gate_ref, rank_ref, cnt_ref, carry_scr, *, tb):
    @pl.when(pl.program_id(0) == 0)
    def _():
        carry_scr[...] = jnp.zeros_like(carry_scr)

    lane = lax.broadcasted_iota(jnp.int32, (tb, LANES), 1)
    logit = jnp.where(lane < N_EXPERTS, l_ref[...], -jnp.inf)
    vals, hots = [], []
    idx_out = jnp.zeros((tb, LANES), jnp.int32)
    for k in range(TOP_K):
        m = jnp.max(logit, axis=-1, keepdims=True)
        ik = jnp.min(jnp.where(logit == m, lane, LANES), axis=-1, keepdims=True)
        hot = lane == ik
        logit = jnp.where(hot, -jnp.inf, logit)
        vals.append(m)
        hots.append(hot)
        idx_out = jnp.where(lane == k, ik, idx_out)
    idx_ref[...] = idx_out

    e = [jnp.exp(v - vals[0]) for v in vals]
    den = e[0] + e[1] + e[2] + e[3]
    gate_out = jnp.zeros((tb, LANES), F32)
    for k in range(TOP_K):
        gate_out = jnp.where(lane == k, e[k] / den, gate_out)
    gate_ref[...] = gate_out

    member = jnp.zeros((tb, LANES), F32)
    for hot in hots:
        member = member + hot.astype(F32)
    row = lax.broadcasted_iota(jnp.int32, (tb, tb), 0)
    col = lax.broadcasted_iota(jnp.int32, (tb, tb), 1)
    before = (col < row).astype(BF16)
    rank = _dot(before, member.astype(BF16)) + carry_scr[...]
    rank_out = jnp.zeros((tb, LANES), F32)
    for k in range(TOP_K):
        rk = jnp.sum(jnp.where(hots[k], rank, 0.0), axis=-1, keepdims=True)
        rank_out = jnp.where(lane == k, rk, rank_out)
    rank_ref[...] = rank_out.astype(jnp.int32)
    total = carry_scr[...] + jnp.sum(member, axis=0, keepdims=True)
    carry_scr[...] = total
    cnt_ref[...] = total.astype(jnp.int32)


def _route(logits, tb=512):
    n = logits.shape[0]
    blk = pl.BlockSpec((tb, LANES), lambda i: (i, 0))
    return pl.pallas_call(
        functools.partial(_route_kernel, tb=tb),
        grid=(n // tb,),
        in_specs=[blk],
        out_specs=[blk, blk, blk, pl.BlockSpec((1, LANES), lambda i: (0, 0))],
        out_shape=[
            jax.ShapeDtypeStruct((n, LANES), jnp.int32),
            jax.ShapeDtypeStruct((n, LANES), F32),
            jax.ShapeDtypeStruct((n, LANES), jnp.int32),
            jax.ShapeDtypeStruct((1, LANES), jnp.int32),
        ],
        scratch_shapes=[pltpu.VMEM((1, LANES), F32)],
        compiler_params=_cparams(("arbitrary",)),
        name="route",
    )(logits)


def _dispatch_kernel(zblk_ref, zok_ref, used_ref, pos_ref, xn_ref, xr_hbm, zero_scr, sem,
                     *, tb, pitch, n_blocks_max):
    blk_rows = MOE_TM * pitch

    def zero_copy(b):
        dst = xr_hbm.at[pl.ds(pl.multiple_of(b * blk_rows, blk_rows), blk_rows), :]
        return pltpu.make_async_copy(zero_scr, dst, sem.at[1])

    @pl.when(pl.program_id(0) == 0)
    def _():
        zero_scr[...] = jnp.zeros_like(zero_scr)

        def per_expert(action):
            def body(e, carry):
                @pl.when(zok_ref[e] == 1)
                def _():
                    action(zero_copy(zblk_ref[e]))
                return carry
            lax.fori_loop(0, N_EXPERTS, body, 0)

        def per_tail(action):
            def body(b, carry):
                action(zero_copy(b))
                return carry
            lax.fori_loop(used_ref[0], n_blocks_max, body, 0)

        per_expert(lambda c: c.start())
        per_tail(lambda c: c.start())
        per_expert(lambda c: c.wait())
        per_tail(lambda c: c.wait())

    def issue(t, carry):
        src = xn_ref.at[pl.ds(pl.multiple_of(t * pitch, pitch), pitch), :]
        for k in range(TOP_K):
            p = pos_ref[0, t * TOP_K + k]
            dst = xr_hbm.at[pl.ds(pl.multiple_of(p * pitch, pitch), pitch), :]
            pltpu.make_async_copy(src, dst, sem.at[0]).start(priority=k % 2)
        return carry

    lax.fori_loop(0, tb, issue, 0, unroll=4)
    for k in range(TOP_K):
        pltpu.make_async_copy(xn_ref, xr_hbm.at[pl.ds(0, tb * pitch), :], sem.at[0]).wait()


def _dispatch(xn_rm, pos, zblk, zok, n_used, n_blocks_max, pitch, tb=256):
    n = xn_rm.shape[0] // pitch
    nb = n // tb
    grid_spec = pltpu.PrefetchScalarGridSpec(
        num_scalar_prefetch=3,
        grid=(nb,),
        in_specs=[
            pl.BlockSpec((None, 1, TOP_K * tb), lambda i, *_: (i, 0, 0), memory_space=pltpu.SMEM),
            pl.BlockSpec((tb * pitch, LANES), lambda i, *_: (i, 0)),
        ],
        out_specs=pl.BlockSpec(memory_space=pl.ANY),
        scratch_shapes=[pltpu.VMEM((MOE_TM * pitch, LANES), xn_rm.dtype), pltpu.SemaphoreType.DMA((2,))],
    )
    return pl.pallas_call(
        functools.partial(_dispatch_kernel, tb=tb, pitch=pitch, n_blocks_max=n_blocks_max),
        grid_spec=grid_spec,
        out_shape=jax.ShapeDtypeStruct((n_blocks_max * MOE_TM * pitch, LANES), xn_rm.dtype),
        compiler_params=_cparams(("arbitrary",)),
        name="dispatch",
    )(zblk, zok, n_used, pos.reshape(nb, 1, TOP_K * tb), xn_rm)


def _cast_rows(src, dst, rows_per=256):
    def body(i, carry):
        r = pl.ds(pl.multiple_of(i * rows_per, rows_per), rows_per)
        dst[r, :] = src[r, :].astype(dst.dtype)
        return carry

    lax.fori_loop(0, src.shape[0] // rows_per, body, 0)


def _stream_row_blocks(cnt, in_copy, out_copy, compute):
    for ahead in range(MOE_NBUF - 1):
        @pl.when(cnt > ahead)
        def _():
            in_copy(ahead, ahead).start()

    def body(b, carry):
        slot = b % MOE_NBUF

        @pl.when(b + MOE_NBUF - 1 < cnt)
        def _():
            in_copy(b + MOE_NBUF - 1, (b + MOE_NBUF - 1) % MOE_NBUF).start()

        in_copy(b, slot).wait()

        @pl.when(b >= MOE_NBUF)
        def _():
            out_copy(b - MOE_NBUF, slot).wait()

        compute(slot)
        out_copy(b, slot).start()
        return carry

    lax.fori_loop(0, cnt, body, 0)

    for back in range(MOE_NBUF, 0, -1):
        @pl.when(cnt >= back)
        def _():
            out_copy(cnt - back, (cnt - back) % MOE_NBUF).wait()


def _stream_expert_rows(first, cnt, in_copy, out_copy, compute):
    n_units = cnt // MOE_UNIT
    _stream_row_blocks(n_units, in_copy(first, MOE_UNIT), out_copy(first, MOE_UNIT), compute(MOE_UNIT))
    done = n_units * MOE_UNIT
    _stream_row_blocks(cnt - done, in_copy(first + done, 1), out_copy(first + done, 1), compute(1))


def _zero_fill_blocks(first, last, zero_src, dst_copy):
    def start(b, carry):
        dst_copy(b).start()
        return carry

    def wait(b, carry):
        dst_copy(b).wait()
        return carry

    zero_src[...] = jnp.zeros_like(zero_src)
    lax.fori_loop(first, last, start, 0)
    lax.fori_loop(first, last, wait, 0)


def _moe_up_kernel(bs_ref, bc_ref, used_ref, x_hbm, w_hbm, bg_ref, bu_ref, act_hbm,
                   wf_scr, wb_scr, xbuf, lhs_scr, obuf, sem_w, sem_x, sem_o,
                   *, nt, tn, d_ff, pitch, n_blocks_max):
    g = pl.program_id(0)
    ng = pl.num_programs(0)
    e = g // nt
    col = pl.multiple_of((g % nt) * tn, tn)
    blk_rows = MOE_TM * pitch

    def w_copies(step):
        ee = step // nt
        cc = pl.multiple_of((step % nt) * tn, tn)
        return (pltpu.make_async_copy(w_hbm.at[ee, :, pl.ds(cc, tn)], wf_scr.at[0], sem_w.at[0]),
                pltpu.make_async_copy(w_hbm.at[ee, :, pl.ds(d_ff + cc, tn)], wf_scr.at[1], sem_w.at[1]))

    @pl.when(g == 0)
    def _():
        for c in w_copies(0):
            c.start(priority=1)

    for half, c in enumerate(w_copies(g)):
        c.wait()
        _cast_rows(wf_scr.at[half], wb_scr.at[half])

    @pl.when(g + 1 < ng)
    def _():
        for c in w_copies(g + 1):
            c.start(priority=1)

    start = bs_ref[e]

    def x_copy(first, nblk):
        def make(u, slot):
            r0 = pl.multiple_of((first + u * nblk) * blk_rows, blk_rows)
            return pltpu.make_async_copy(x_hbm.at[pl.ds(r0, nblk * blk_rows), :],
                                         xbuf.at[slot, pl.ds(0, nblk * blk_rows), :], sem_x.at[slot])
        return make

    def o_copy(first, nblk):
        def make(u, slot):
            r0 = pl.multiple_of((first + u * nblk) * MOE_TM, MOE_TM)
            return pltpu.make_async_copy(obuf.at[slot, pl.ds(0, nblk * MOE_TM), :],
                                         act_hbm.at[pl.ds(r0, nblk * MOE_TM), pl.ds(col, tn)], sem_o.at[slot])
        return make

    def compute(nblk):
        rows = nblk * MOE_TM

        def run(slot):
            half = pitch * LANES
            for c in range(pitch):
                w = xbuf[slot, pl.ds(c, rows, stride=pitch), :]
                lhs_scr[0:rows, c * LANES:(c + 1) * LANES] = _unpack_lo(w).astype(BF16)
                lhs_scr[0:rows, half + c * LANES:half + (c + 1) * LANES] = _unpack_hi(w).astype(BF16)
            x = lhs_scr[0:rows, :]
            gate = jnp.minimum(_dot(x, wb_scr[0]) + bg_ref[...], SWIGLU_LIMIT)
            up = jnp.clip(_dot(x, wb_scr[1]) + bu_ref[...], -SWIGLU_LIMIT, SWIGLU_LIMIT)
            obuf[slot, 0:rows, :] = (gate * jax.nn.sigmoid(SWIGLU_ALPHA * gate) * (up + 1.0)).astype(obuf.dtype)
        return run

    _stream_expert_rows(start, bc_ref[e], x_copy, o_copy, compute)

    @pl.when(e == N_EXPERTS - 1)
    def _():
        zero_blk = obuf.at[0, pl.ds(0, MOE_TM), :]

        def tail_copy(b):
            r0 = pl.multiple_of(b * MOE_TM, MOE_TM)
            return pltpu.make_async_copy(zero_blk, act_hbm.at[pl.ds(r0, MOE_TM), pl.ds(col, tn)], sem_o.at[0])
        _zero_fill_blocks(used_ref[0], n_blocks_max, zero_blk, tail_copy)


def _moe_up(x_rows_rm, w_up, b_up, blk_start, blk_count, n_used, n_blocks_max, tn=1024):
    n_exp, d, two_ff = w_up.shape
    d_ff = two_ff // 2
    pitch = d // 2 // LANES
    nt = d_ff // tn
    up_off = d_ff // tn
    grid_spec = pltpu.PrefetchScalarGridSpec(
        num_scalar_prefetch=3,
        grid=(n_exp * nt,),
        in_specs=[
            pl.BlockSpec(memory_space=pl.ANY),
            pl.BlockSpec(memory_space=pl.ANY),
            pl.BlockSpec((None, 1, tn), lambda g, *_: (g // nt, 0, g % nt)),
            pl.BlockSpec((None, 1, tn), lambda g, *_: (g // nt, 0, g % nt + up_off)),
        ],
        out_specs=pl.BlockSpec(memory_space=pl.ANY),
        scratch_shapes=[
            pltpu.VMEM((2, d, tn), F32),
            pltpu.VMEM((2, d, tn), BF16),
            pltpu.VMEM((MOE_NBUF, MOE_UNIT * MOE_TM * pitch, LANES), jnp.uint32),
            pltpu.VMEM((MOE_UNIT * MOE_TM, d), BF16),
            pltpu.VMEM((MOE_NBUF, MOE_UNIT * MOE_TM, tn), BF16),
            pltpu.SemaphoreType.DMA((2,)),
            pltpu.SemaphoreType.DMA((MOE_NBUF,)),
            pltpu.SemaphoreType.DMA((MOE_NBUF,)),
        ],
    )
    return pl.pallas_call(
        functools.partial(_moe_up_kernel, nt=nt, tn=tn, d_ff=d_ff, pitch=pitch, n_blocks_max=n_blocks_max),
        grid_spec=grid_spec,
        out_shape=jax.ShapeDtypeStruct((n_blocks_max * MOE_TM, d_ff), BF16),
        compiler_params=_cparams(("arbitrary",)),
        name="moe_up",
    )(blk_start, blk_count, n_used, x_rows_rm, w_up, b_up, b_up)


def _moe_down_kernel(bs_ref, bc_ref, used_ref, a_hbm, w_hbm, b_ref, y_hbm,
                     wf_scr, wb_scr, abuf, obuf, sem_w, sem_a, sem_o, *, pitch, n_blocks_max):
    e = pl.program_id(0)
    blk_rows = MOE_TM * pitch

    def w_copy(ee):
        return pltpu.make_async_copy(w_hbm.at[ee], wf_scr, sem_w.at[0])

    @pl.when(e == 0)
    def _():
        w_copy(0).start(priority=1)

    w_copy(e).wait()
    _cast_rows(wf_scr, wb_scr)

    @pl.when(e + 1 < pl.num_programs(0))
    def _():
        w_copy(e + 1).start(priority=1)

    start = bs_ref[e]

    def a_copy(first, nblk):
        def make(u, slot):
            r0 = pl.multiple_of((first + u * nblk) * MOE_TM, MOE_TM)
            return pltpu.make_async_copy(a_hbm.at[pl.ds(r0, nblk * MOE_TM), :],
                                         abuf.at[slot, pl.ds(0, nblk * MOE_TM), :], sem_a.at[slot])
        return make

    def o_copy(first, nblk):
        def make(u, slot):
            r0 = pl.multiple_of((first + u * nblk) * blk_rows, blk_rows)
            return pltpu.make_async_copy(obuf.at[slot, pl.ds(0, nblk * blk_rows), :],
                                         y_hbm.at[pl.ds(r0, nblk * blk_rows), :], sem_o.at[slot])
        return make

    def compute(nblk):
        rows = nblk * MOE_TM

        def run(slot):
            y = _dot(abuf[slot, 0:rows, :], wb_scr[...]) + b_ref[...]
            _store_rowmajor(obuf.at[slot], _pack_bf16_pairs(y))
        return run

    _stream_expert_rows(start, bc_ref[e], a_copy, o_copy, compute)

    @pl.when(e == N_EXPERTS - 1)
    def _():
        zero_blk = obuf.at[0, pl.ds(0, blk_rows), :]

        def tail_copy(b):
            r0 = pl.multiple_of(b * blk_rows, blk_rows)
            return pltpu.make_async_copy(zero_blk, y_hbm.at[pl.ds(r0, blk_rows), :], sem_o.at[0])
        _zero_fill_blocks(used_ref[0], n_blocks_max, zero_blk, tail_copy)


def _moe_down(act, w_down, b_down, blk_start, blk_count, n_used, n_blocks_max):
    n_rows, d_ff = act.shape
    n_exp, _, d = w_down.shape
    pitch = d // 2 // LANES
    grid_spec = pltpu.PrefetchScalarGridSpec(
        num_scalar_prefetch=3,
        grid=(n_exp,),
        in_specs=[
            pl.BlockSpec(memory_space=pl.ANY),
            pl.BlockSpec(memory_space=pl.ANY),
            pl.BlockSpec((None, 1, d), lambda e, *_: (e, 0, 0)),
        ],
        out_specs=pl.BlockSpec(memory_space=pl.ANY),
        scratch_shapes=[
            pltpu.VMEM((d_ff, d), F32),
            pltpu.VMEM((d_ff, d), BF16),
            pltpu.VMEM((MOE_NBUF, MOE_UNIT * MOE_TM, d_ff), BF16),
            pltpu.VMEM((MOE_NBUF, MOE_UNIT * MOE_TM * pitch, LANES), jnp.uint32),
            pltpu.SemaphoreType.DMA((1,)),
            pltpu.SemaphoreType.DMA((MOE_NBUF,)),
            pltpu.SemaphoreType.DMA((MOE_NBUF,)),
        ],
    )
    return pl.pallas_call(
        functools.partial(_moe_down_kernel, pitch=pitch, n_blocks_max=n_blocks_max),
        grid_spec=grid_spec,
        out_shape=jax.ShapeDtypeStruct((n_rows * pitch, LANES), jnp.uint32),
        compiler_params=_cparams(("arbitrary",)),
        name="moe_down",
    )(blk_start, blk_count, n_used, act, w_down, b_down)


def _combine_kernel(pos0_ref, posn_ref, y_hbm, x2_ref, gate_ref, g_ref, o_ref, buf_even, buf_odd, sem,
                    *, tb, pitch):
    i = pl.program_id(0)
    last = pl.num_programs(0) - 1
    n_rows = TOP_K * tb

    def row_copy(p_ref, r, buf, sem_slot):
        src = y_hbm.at[pl.ds(pl.multiple_of(p_ref[0, r] * pitch, pitch), pitch), :]
        return pltpu.make_async_copy(src, buf.at[pl.ds(r * pitch, pitch), :], sem.at[sem_slot])

    def wait_block(buf, sem_slot):
        pltpu.make_async_copy(y_hbm.at[pl.ds(0, n_rows * pitch), :], buf, sem.at[sem_slot]).wait()

    @pl.when(i == 0)
    def _():
        def issue(j, carry):
            for half in range(2):
                row_copy(pos0_ref, 2 * j + half, buf_even, 0).start(priority=half)
            return carry
        lax.fori_loop(0, n_rows // 2, issue, 0, unroll=4)

    def step(buf_cur, sem_cur, buf_nxt, sem_nxt):
        wait_block(buf_cur, sem_cur)
        for r in range(n_rows):
            row_copy(posn_ref, r, buf_nxt, sem_nxt).start(priority=r % 2)
        gates = gate_ref[...]
        half = pitch * LANES
        ssq = jnp.zeros((tb, 1), F32)
        for c in range(pitch):
            lo = slice(c * LANES, (c + 1) * LANES)
            hi = slice(half + c * LANES, half + (c + 1) * LANES)
            z_lo = x2_ref[:, lo]
            z_hi = x2_ref[:, hi]
            for k in range(TOP_K):
                w = buf_cur[pl.ds(k * tb * pitch + c, tb, stride=pitch), :]
                z_lo = z_lo + gates[:, k:k + 1] * _unpack_lo(w)
                z_hi = z_hi + gates[:, k:k + 1] * _unpack_hi(w)
            o_ref[:, lo] = z_lo
            o_ref[:, hi] = z_hi
            ssq = ssq + (jnp.sum(z_lo * z_lo, axis=-1, keepdims=True)
                         + jnp.sum(z_hi * z_hi, axis=-1, keepdims=True))
        o_ref[...] = o_ref[...] * lax.rsqrt(ssq * (1.0 / (2 * half)) + EPS) * g_ref[...]

        @pl.when(i == last)
        def _():
            wait_block(buf_nxt, sem_nxt)

    @pl.when(i % 2 == 0)
    def _():
        step(buf_even, 0, buf_odd, 1)

    @pl.when(i % 2 == 1)
    def _():
        step(buf_odd, 1, buf_even, 0)


def _combine(y_rows_rm, pos, gates, x2, g_final, tb=256):
    n, d = x2.shape
    nb = n // tb
    pitch = d // 2 // LANES
    pos_blk = pos.reshape(nb, tb, TOP_K).transpose(0, 2, 1).reshape(nb, 1, TOP_K * tb)
    pos_spec = lambda imap: pl.BlockSpec((None, 1, TOP_K * tb), imap, memory_space=pltpu.SMEM)
    return pl.pallas_call(
        functools.partial(_combine_kernel, tb=tb, pitch=pitch),
        grid=(nb,),
        in_specs=[
            pos_spec(lambda i: (0, 0, 0)),
            pos_spec(lambda i: (jnp.minimum(i + 1, nb - 1), 0, 0)),
            pl.BlockSpec(memory_space=pl.ANY),
            pl.BlockSpec((tb, d), lambda i: (i, 0)),
            pl.BlockSpec((tb, LANES), lambda i: (i, 0)),
            pl.BlockSpec((1, d), lambda i: (0, 0)),
        ],
        out_specs=pl.BlockSpec((tb, d), lambda i: (i, 0)),
        out_shape=jax.ShapeDtypeStruct((n, d), F32),
        scratch_shapes=[pltpu.VMEM((TOP_K * tb * pitch, LANES), jnp.uint32),
                        pltpu.VMEM((TOP_K * tb * pitch, LANES), jnp.uint32),
                        pltpu.SemaphoreType.DMA((2,))],
        compiler_params=_cparams(("arbitrary",)),
        name="combine",
    )(pos_blk, pos_blk, y_rows_rm, x2, gates, g_final)


def _routing_tables(idx, rank, cnt):
    counts = cnt[0, :N_EXPERTS]
    blk_count = (counts + MOE_TM - 1) // MOE_TM
    blk_end = jnp.cumsum(blk_count)
    blk_start = blk_end - blk_count
    hot = idx[:, :TOP_K, None] == jnp.arange(N_EXPERTS, dtype=jnp.int32)
    pos = jnp.sum(jnp.where(hot, blk_start * MOE_TM, 0), axis=-1) + rank[:, :TOP_K]
    return pos.astype(jnp.int32), blk_start.astype(jnp.int32), blk_count.astype(jnp.int32)


def kernel(x, mem, g_attn_norm, g_mem_norm, w_in, w_gla_a2, b_gla_a, g_gla_out, b_fox_f, g_fox_out,
           w_mem_kv, g_mem_out, w_out, g_ffn_norm, w_router, b_router, w_moe_up, b_moe_up,
           w_moe_down, b_moe_down, g_final):
    batch, seq, d = x.shape
    n_mem = mem.shape[1]
    n = batch * seq
    depth = w_in.shape[0]
    assert depth == 1, "the combine kernel applies the final norm, so exactly one layer is supported"
    kw = GLA_HEADS * GLA_DK
    vw = GLA_HEADS * GLA_DV
    fw = FOX_HEADS * FOX_DH
    mw = MEM_HEADS * MEM_DH
    o_q, o_k, o_v, o_g = 0, kw, 2 * kw, 2 * kw + vw
    o_a = o_g + vw
    o_fq = o_a + GLA_LOWRANK
    o_fk, o_fv = o_fq + fw, o_fq + 2 * fw
    o_ff = o_fq + 3 * fw
    o_mq = o_ff + FOX_HEADS
    f_lane = GLA_LOWRANK

    xf = x.reshape(n, d)
    for l in range(depth):
        wi = w_in[l]
        fox_q_scale = FOX_DH ** -0.5 * LOG2E
        wa = jnp.concatenate([wi[:, o_q:o_a], wi[:, o_fq:o_fk] * fox_q_scale, wi[:, o_fk:o_ff],
                              wi[:, o_mq:o_mq + mw]], axis=1).astype(BF16)
        wb = jnp.zeros((d, LANES), F32).at[:, :GLA_LOWRANK].set(wi[:, o_a:o_fq])
        wb = wb.at[:, f_lane:f_lane + FOX_HEADS].set(wi[:, o_ff:o_mq]).astype(BF16)
        proj, small = _in_proj(xf, g_attn_norm[l].reshape(1, d), wa, wb)

        wa2p = jnp.zeros((LANES, kw), F32).at[:GLA_LOWRANK].set(w_gla_a2[l]).astype(BF16)
        bfv = jnp.zeros((1, LANES), F32).at[0, f_lane:f_lane + FOX_HEADS].set(b_fox_f[l])
        gla, f_cum = _gla(proj, small, wa2p, b_gla_a[l].reshape(1, kw), bfv,
                          g_gla_out[l].reshape(1, vw), batch, seq)

        f_t = f_cum.reshape(batch, seq, LANES)[:, :, f_lane:f_lane + FOX_HEADS].transpose(0, 2, 1)
        f_t = jnp.concatenate([f_t, jnp.zeros_like(f_t)], axis=1)
        fox = _fox(proj, f_t, g_fox_out[l].reshape(1, fw), batch, seq)

        kv = _mem_kv(mem.reshape(batch * n_mem, d), g_mem_norm[l].reshape(1, d), w_mem_kv[l].astype(BF16))
        memo = _mem_attn(proj, kv, g_mem_out[l].reshape(1, mw), batch, seq, n_mem)

        wr = jnp.zeros((d, LANES), F32).at[:, :N_EXPERTS].set(w_router[l])
        wr_hi = wr.astype(BF16)
        wr = jnp.concatenate([wr_hi, (wr - wr_hi.astype(F32)).astype(BF16)], axis=1)
        br =jnp.zeros((1, LANES), F32).at[0, :N_EXPERTS].set(b_router[l])
        x2, xn, logits = _out_proj(xf, gla, fox, memo, w_out[l].astype(BF16),
                                   g_ffn_norm[l].reshape(1, d), wr, br)

        idx, gates, rank, cnt = _route(logits)
        pos, blk_start, blk_count = _routing_tables(idx, rank, cnt)
        n_blocks_max = -(-(n * TOP_K + N_EXPERTS * (MOE_TM - 1)) // MOE_TM)
        blk_end = blk_start + blk_count
        n_used = blk_end[-1:]
        x_rows = _dispatch(xn, pos, blk_end - 1, (blk_count > 0).astype(jnp.int32), n_used,
                           n_blocks_max, d // 2 // LANES)
        d_ff = w_moe_up.shape[3] // 2
        act = _moe_up(x_rows, w_moe_up[l], b_moe_up[l].reshape(N_EXPERTS, 1, 2 * d_ff),
                      blk_start, blk_count, n_used, n_blocks_max)
        y_rows = _moe_down(act, w_moe_down[l], b_moe_down[l].reshape(N_EXPERTS, 1, d),
                           blk_start, blk_count, n_used, n_blocks_max)
        xf = _combine(y_rows, pos, gates, x2, g_final.reshape(1, d))
    return xf.reshape(batch, seq, d)
```

```python
import functools

import jax
import jax.numpy as jnp
from jax import lax
from jax.experimental import pallas as pl
from jax.experimental.pallas import tpu as pltpu

EPS = 1e-5
CHUNK = 64
GLA_HEADS = 4
GLA_DK = 128
GLA_DV = 256
GLA_LOWRANK = 16
GLA_TAU = 16.0
FOX_HEADS = 4
FOX_DH = 128
MEM_HEADS = 4
MEM_DH = 128
N_EXPERTS = 32
TOP_K = 4
SWIGLU_LIMIT = 7.0
SWIGLU_ALPHA = 1.702
LANES = 128
MXU_COLS = 256
MOE_TM = 256
MOE_NBUF = 2
MOE_UNIT = 2
VMEM_LIMIT = 56 * 1024 * 1024

F32 = jnp.float32
BF16 = jnp.bfloat16


def _cparams(sem, vmem=VMEM_LIMIT):
    return pltpu.CompilerParams(dimension_semantics=sem, vmem_limit_bytes=vmem)


def _log_sigmoid(x):
    return jnp.minimum(x, 0.0) - jnp.log1p(jnp.exp(-jnp.abs(x)))


def _rms(x, g):
    return x * lax.rsqrt(jnp.mean(x * x, axis=-1, keepdims=True) + EPS) * g


def _dot(a, b, **kw):
    return jnp.dot(a, b, preferred_element_type=F32, **kw)


def _dot_nt(a, b):
    return lax.dot_general(a, b, (((1,), (1,)), ((), ())), preferred_element_type=F32)


def _dot_tn(a, b):
    return lax.dot_general(a, b, (((0,), (0,)), ((), ())), preferred_element_type=F32)


def _store_rowmajor(ref, x, base=0):
    rows, w = x.shape
    pitch = w // LANES
    for c in range(pitch):
        ref[pl.ds(base + c, rows, stride=pitch), :] = x[:, c * LANES:(c + 1) * LANES]


def _pack_bf16_pairs(x):
    half = x.shape[1] // 2
    bits = lambda v: lax.bitcast_convert_type(v.astype(BF16).astype(F32), jnp.uint32)
    return bits(x[:, half:]) | (bits(x[:, :half]) >> 16)


def _unpack_lo(w):
    return lax.bitcast_convert_type(w << 16, F32)


def _unpack_hi(w):
    return lax.bitcast_convert_type(w & jnp.uint32(0xFFFF0000), F32)


def _in_proj_kernel(x_ref, g_ref, wa_ref, wb_ref, proj_ref, small_ref, h_scr):
    @pl.when(pl.program_id(1) == 0)
    def _():
        hb = _rms(x_ref[...], g_ref[...]).astype(BF16)
        h_scr[...] = hb
        small_ref[...] = _dot(hb, wb_ref[...])

    proj_ref[...] = _dot(h_scr[...], wa_ref[...]).astype(proj_ref.dtype)


def _in_proj(x2d, g, wa, wb, tm=1024, tn=1280):
    n, d = x2d.shape
    na = wa.shape[1]
    return pl.pallas_call(
        _in_proj_kernel,
        grid=(n // tm, na // tn),
        in_specs=[
            pl.BlockSpec((tm, d), lambda i, j: (i, 0)),
            pl.BlockSpec((1, d), lambda i, j: (0, 0)),
            pl.BlockSpec((d, tn), lambda i, j: (0, j)),
            pl.BlockSpec((d, LANES), lambda i, j: (0, 0)),
        ],
        out_specs=[
            pl.BlockSpec((tm, tn), lambda i, j: (i, j)),
            pl.BlockSpec((tm, LANES), lambda i, j: (i, 0)),
        ],
        out_shape=[
            jax.ShapeDtypeStruct((n, na), BF16),
            jax.ShapeDtypeStruct((n, LANES), F32),
        ],
        scratch_shapes=[pltpu.VMEM((tm, d), BF16)],
        compiler_params=_cparams(("parallel", "arbitrary")),
        name="in_proj",
    )(x2d, g, wa, wb)


def _gla_kernel(q_ref, k_ref, v_ref, gate_ref, small_ref, wa2_ref, ba_ref, bf_ref, gout_ref,
                o_ref, f_ref, state_scr, fcar_scr, la_scr, lf_scr, *, n_chunks):
    @pl.when(pl.program_id(1) == 0)
    def _():
        state_scr[...] = jnp.zeros_like(state_scr)
        fcar_scr[...] = jnp.zeros_like(fcar_scr)

    small = small_ref[...]
    la_scr[...] = _log_sigmoid(_dot(small.astype(BF16), wa2_ref[...]) + ba_ref[...]) * (1.0 / GLA_TAU)
    lf_scr[...] = _log_sigmoid(small + bf_ref[...])
    row = lax.broadcasted_iota(jnp.int32, (CHUNK, CHUNK), 0)
    col = lax.broadcasted_iota(jnp.int32, (CHUNK, CHUNK), 1)
    tri = (col <= row).astype(F32)
    scale = GLA_DK ** -0.5

    def chunk_body(c, carry):
        r = pl.ds(pl.multiple_of(c * CHUNK, CHUNK), CHUNK)
        b = _dot(tri, la_scr[r, :], precision=lax.Precision.HIGHEST)
        b_end = b[CHUNK - 1:CHUNK, :]
        k_dec = k_ref[r, :].astype(F32) * jnp.exp(b_end - b)
        decay = jnp.exp(b_end)
        f_cum = _dot(tri, lf_scr[r, :], precision=lax.Precision.HIGHEST) + fcar_scr[...]
        f_ref[r, :] = f_cum
        fcar_scr[...] = f_cum[CHUNK - 1:CHUNK, :]
        heads = range(GLA_HEADS)
        ks = [slice(h * GLA_DK, (h + 1) * GLA_DK) for h in heads]
        vs = [slice(h * GLA_DV, (h + 1) * GLA_DV) for h in heads]
        old = [state_scr[h] for h in heads]
        new = [old[h] * decay[:, ks[h]] + _dot_tn(v_ref[r, vs[h]], k_dec[:, ks[h]].astype(BF16))
               for h in heads]
        for h in heads:
            state_scr[h] = new[h]
        for h in heads:
            o = _dot_nt(q_ref[r, ks[h]], new[h].astype(BF16)) * scale
            gt = gate_ref[r, vs[h]].astype(F32)
            o_ref[r, vs[h]] = (_rms(o, gout_ref[:, vs[h]]) * (gt * jax.nn.sigmoid(gt))).astype(o_ref.dtype)
        return carry

    lax.fori_loop(0, n_chunks, chunk_body, 0, unroll=True)


def _gla(proj, small, wa2p, ba, bfv, gout, batch, seq, ts=512):
    n = proj.shape[0]
    nsb = seq // ts
    kw = GLA_HEADS * GLA_DK
    vw = GLA_HEADS * GLA_DV
    row = lambda b, s: b * nsb + s
    return pl.pallas_call(
        functools.partial(_gla_kernel, n_chunks=ts // CHUNK),
        grid=(batch, nsb),
        in_specs=[
            pl.BlockSpec((ts, kw), lambda b, s: (row(b, s), 0)),
            pl.BlockSpec((ts, kw), lambda b, s: (row(b, s), 1)),
            pl.BlockSpec((ts, vw), lambda b, s: (row(b, s), 1)),
            pl.BlockSpec((ts, vw), lambda b, s: (row(b, s), 2)),
            pl.BlockSpec((ts, LANES), lambda b, s: (row(b, s), 0)),
            pl.BlockSpec((LANES, kw), lambda b, s: (0, 0)),
            pl.BlockSpec((1, kw), lambda b, s: (0, 0)),
            pl.BlockSpec((1, LANES), lambda b, s: (0, 0)),
            pl.BlockSpec((1, vw), lambda b, s: (0, 0)),
        ],
        out_specs=[
            pl.BlockSpec((ts, vw), lambda b, s: (row(b, s), 0)),
            pl.BlockSpec((ts, LANES), lambda b, s: (row(b, s), 0)),
        ],
        out_shape=[
            jax.ShapeDtypeStruct((n, vw), BF16),
            jax.ShapeDtypeStruct((n, LANES), F32),
        ],
        scratch_shapes=[
            pltpu.VMEM((GLA_HEADS, GLA_DV, GLA_DK), F32),
            pltpu.VMEM((1, LANES), F32),
            pltpu.VMEM((ts, kw), F32),
            pltpu.VMEM((ts, LANES), F32),
        ],
        compiler_params=_cparams(("parallel", "arbitrary")),
        name="gla",
    )(proj, proj, proj, proj, small, wa2p, ba, bfv, gout)


LOG2E = 1.4426950408889634
def _fox_kernel(q_ref, k_ref, v_ref, fk_ref, g_ref, o_ref, m_scr, l_scr, acc_scr, s_even, s_odd,
                *, tq, tk):
    i = pl.program_id(1)
    t = pl.program_id(2)

    @pl.when(t == 0)
    def _():
        m_scr[...] = jnp.full_like(m_scr, -jnp.inf)
        l_scr[...] = jnp.zeros_like(l_scr)
        acc_scr[...] = jnp.zeros_like(acc_scr)

    heads = range(FOX_HEADS)
    hs = [slice(h * FOX_DH, (h + 1) * FOX_DH) for h in heads]

    def score(s_out):
        for h in heads:
            s_out[h] = _dot_nt(q_ref[:, hs[h]], k_ref[:, hs[h]])

    def update(s_in, on_diagonal):
        m_prev = [m_scr[h] for h in heads]
        l_prev = [l_scr[h] for h in heads]
        acc_prev = [acc_scr[h] for h in heads]
        s = [s_in[h] - fk_ref[h:h + 1, :] * LOG2E for h in heads]
        if on_diagonal:
            row = lax.broadcasted_iota(jnp.int32, (tq, tk), 0)
            col = lax.broadcasted_iota(jnp.int32, (tq, tk), 1)
            s = [jnp.where(col <= row, sh, -jnp.inf) for sh in s]
        m_new = [jnp.maximum(m_prev[h], jnp.max(s[h], axis=-1, keepdims=True)) for h in heads]
        alpha = [jnp.exp2(m_prev[h] - m_new[h]) for h in heads]
        p = [jnp.exp2(s[h] - m_new[h]) for h in heads]
        l_new = [alpha[h] * l_prev[h] + jnp.sum(p[h], axis=-1, keepdims=True) for h in heads]
        acc_new = [alpha[h] * acc_prev[h] + _dot(p[h].astype(BF16), v_ref[:, hs[h]]) for h in heads]
        return m_new, l_new, acc_new

    def step(s_write, s_read):
        @pl.when(t == 0)
        def _():
            score(s_write)

        @pl.when(jnp.logical_and(t >= 1, t <= i))
        def _():
            m_new, l_new, acc_new = update(s_read, False)
            score(s_write)
            for h in heads:
                m_scr[h] = m_new[h]
                l_scr[h] = l_new[h]
                acc_scr[h] = acc_new[h]

        @pl.when(t == i + 1)
        def _():
            _, l_new, acc_new = update(s_read, True)
            for h in heads:
                o_ref[:, hs[h]] = _rms(acc_new[h] / l_new[h], g_ref[:, hs[h]]).astype(o_ref.dtype)

    @pl.when(t % 2 == 0)
    def _():
        step(s_even, s_odd)

    @pl.when(t % 2 == 1)
    def _():
        step(s_odd, s_even)


def _fox(proj, f_t, g_fox, batch, seq, tq=512, tk=512):
    assert tq == tk, "the diagonal-block mask assumes square blocks"
    n = proj.shape[0]
    w = FOX_HEADS * FOX_DH
    nq, nk = seq // tq, seq // tk
    qcol, kcol, vcol = 3072 // w, 3584 // w, 4096 // w
    return pl.pallas_call(
        functools.partial(_fox_kernel, tq=tq, tk=tk),
        grid=(batch, nq, nk + 1),
        in_specs=[
            pl.BlockSpec((tq, w), lambda b, i, t: (b * nq + i, qcol)),
            pl.BlockSpec((tk, w), lambda b, i, t: (b * nk + jnp.minimum(t, i), kcol)),
            pl.BlockSpec((tk, w), lambda b, i, t: (b * nk + jnp.clip(t - 1, 0, i), vcol)),
            pl.BlockSpec((None, 8, tk), lambda b, i, t: (b, 0, jnp.clip(t - 1, 0, i))),
            pl.BlockSpec((1, w), lambda b, i, t: (0, 0)),
        ],
        out_specs=pl.BlockSpec((tq, w), lambda b, i, t: (b * nq + i, 0)),
        out_shape=jax.ShapeDtypeStruct((n, w), BF16),
        scratch_shapes=[
            pltpu.VMEM((FOX_HEADS, tq, 1), F32),
            pltpu.VMEM((FOX_HEADS, tq, 1), F32),
            pltpu.VMEM((FOX_HEADS, tq, FOX_DH), F32),
            pltpu.VMEM((FOX_HEADS, tq, tk), F32),
            pltpu.VMEM((FOX_HEADS, tq, tk), F32),
        ],
        compiler_params=_cparams(("parallel", "parallel", "arbitrary")),
        name="fox",
    )(proj, proj, proj, f_t, g_fox)


def _mem_kv_kernel(m_ref, g_ref, w_ref, kv_ref):
    kv_ref[...] = _dot(_rms(m_ref[...], g_ref[...]).astype(BF16), w_ref[...]).astype(kv_ref.dtype)


def _mem_kv(mem2d, g, w, tm=256):
    n, d = mem2d.shape
    nw = w.shape[1]
    return pl.pallas_call(
        _mem_kv_kernel,
        grid=(n // tm,),
        in_specs=[
            pl.BlockSpec((tm, d), lambda i: (i, 0)),
            pl.BlockSpec((1, d), lambda i: (0, 0)),
            pl.BlockSpec((d, nw), lambda i: (0, 0)),
        ],
        out_specs=pl.BlockSpec((tm, nw), lambda i: (i, 0)),
        out_shape=jax.ShapeDtypeStruct((n, nw), BF16),
        compiler_params=_cparams(("parallel",)),
        name="mem_kv",
    )(mem2d, g, w)


def _mem_attn_kernel(q_ref, k_ref, v_ref, g_ref, o_ref):
    scale = MEM_DH ** -0.5
    for h in range(MEM_HEADS):
        hs = slice(h * MEM_DH, (h + 1) * MEM_DH)
        s = _dot_nt(q_ref[:, hs], k_ref[:, hs]) * scale
        p = jnp.exp(s - jnp.max(s, axis=-1, keepdims=True))
        l = jnp.sum(p, axis=-1, keepdims=True)
        o = _dot((p / l).astype(BF16), v_ref[:, hs])
        o_ref[:, hs] = _rms(o, g_ref[:, hs]).astype(o_ref.dtype)


def _mem_attn(proj, kv, g_mem_out, batch, seq, n_mem, tq=1024):
    n = proj.shape[0]
    w = MEM_HEADS * MEM_DH
    nq = seq // tq
    qcol = 4608 // w
    return pl.pallas_call(
        _mem_attn_kernel,
        grid=(n // tq,),
        in_specs=[
            pl.BlockSpec((tq, w), lambda i: (i, qcol)),
            pl.BlockSpec((n_mem, w), lambda i: (i // nq, 0)),
            pl.BlockSpec((n_mem, w), lambda i: (i // nq, 1)),
            pl.BlockSpec((1, w), lambda i: (0, 0)),
        ],
        out_specs=pl.BlockSpec((tq, w), lambda i: (i, 0)),
        out_shape=jax.ShapeDtypeStruct((n, w), BF16),
        compiler_params=_cparams(("parallel",)),
        name="mem_attn",
    )(proj, kv, kv, g_mem_out)


def _out_proj_kernel(x_ref, gla_ref, fox_ref, mem_ref, w1_ref, w2_ref, w3_ref, g_ref, wr_ref, br_ref,
                     x2_ref, xn_ref, logit_ref):
    x2 = (x_ref[...] + _dot(gla_ref[...], w1_ref[...]) + _dot(fox_ref[...], w2_ref[...])
          + _dot(mem_ref[...], w3_ref[...]))
    x2_ref[...] = x2
    xn = _rms(x2, g_ref[...])
    _store_rowmajor(xn_ref, _pack_bf16_pairs(xn))
    xh = xn.astype(BF16)
    xl = (xn - xh.astype(F32)).astype(BF16)
    hi = _dot(xh, wr_ref[...])
    logit_ref[...] = hi[:, :LANES] + hi[:, LANES:] + _dot(xl, wr_ref[:, :LANES]) + br_ref[...]


def _out_proj(x2d, gla, fox, memo, w_out, g_ffn, wr, br, tm=256):
    n, d = x2d.shape
    w1, w2 = gla.shape[1], fox.shape[1]
    const = lambda i: (0, 0)
    return pl.pallas_call(
        _out_proj_kernel,
        grid=(n // tm,),
        in_specs=[
            pl.BlockSpec((tm, d), lambda i: (i, 0)),
            pl.BlockSpec((tm, w1), lambda i: (i, 0)),
            pl.BlockSpec((tm, w2), lambda i: (i, 0)),
            pl.BlockSpec((tm, w2), lambda i: (i, 0)),
            pl.BlockSpec((w1, d), lambda i: (0, 0)),
            pl.BlockSpec((w2, d), lambda i: (w1 // w2, 0)),
            pl.BlockSpec((w2, d), lambda i: (w1 // w2 + 1, 0)),
            pl.BlockSpec((1, d), const),
            pl.BlockSpec((d, 2 * LANES), const),
            pl.BlockSpec((1, LANES), const),
        ],
        out_specs=[
            pl.BlockSpec((tm, d), lambda i: (i, 0)),
            pl.BlockSpec((tm * (d // 2 // LANES), LANES), lambda i: (i, 0)),
            pl.BlockSpec((tm, LANES), lambda i: (i, 0)),
        ],
        out_shape=[
            jax.ShapeDtypeStruct((n, d), F32),
            jax.ShapeDtypeStruct((n * (d // 2 // LANES), LANES), jnp.uint32),
            jax.ShapeDtypeStruct((n, LANES), F32),
        ],
        compiler_params=_cparams(("parallel",)),
        name="out_proj",
    )(x2d, gla, fox, memo, w_out, w_out, w_out, g_ffn, wr, br)


def _route_kernel(l_ref, idx_ref, gate_ref, rank_ref, cnt_ref, carry_scr, *, tb):
    @pl.when(pl.program_id(0) == 0)
    def _():
        carry_scr[...] = jnp.zeros_like(carry_scr)

    lane = lax.broadcasted_iota(jnp.int32, (tb, LANES), 1)
    logit = jnp.where(lane < N_EXPERTS, l_ref[...], -jnp.inf)
    vals, hots = [], []
    idx_out = jnp.zeros((tb, LANES), jnp.int32)
    for k in range(TOP_K):
        m = jnp.max(logit, axis=-1, keepdims=True)
        ik = jnp.min(jnp.where(logit == m, lane, LANES), axis=-1, keepdims=True)
        hot = lane == ik
        logit = jnp.where(hot, -jnp.inf, logit)
        vals.append(m)
        hots.append(hot)
        idx_out = jnp.where(lane == k, ik, idx_out)
    idx_ref[...] = idx_out

    e = [jnp.exp(v - vals[0]) for v in vals]
    den = e[0] + e[1] + e[2] + e[3]
    gate_out = jnp.zeros((tb, LANES), F32)
    for k in range(TOP_K):
        gate_out = jnp.where(lane == k, e[k] / den, gate_out)
    gate_ref[...] = gate_out

    member = jnp.zeros((tb, LANES), F32)
    for hot in hots:
        member = member + hot.astype(F32)
    row = lax.broadcasted_iota(jnp.int32, (tb, tb), 0)
    col = lax.broadcasted_iota(jnp.int32, (tb, tb), 1)
    before = (col < row).astype(BF16)
    rank = _dot(before, member.astype(BF16)) + carry_scr[...]
    rank_out = jnp.zeros((tb, LANES), F32)
    for k in range(TOP_K):
        rk = jnp.sum(jnp.where(hots[k], rank, 0.0), axis=-1, keepdims=True)
        rank_out = jnp.where(lane == k, rk, rank_out)
    rank_ref[...] = rank_out.astype(jnp.int32)
    total = carry_scr[...] + jnp.sum(member, axis=0, keepdims=True)
    carry_scr[...] = total
    cnt_ref[...] = total.astype(jnp.int32)


def _route(logits, tb=512):
    n = logits.shape[0]
    blk = pl.BlockSpec((tb, LANES), lambda i: (i, 0))
    return pl.pallas_call(
        functools.partial(_route_kernel, tb=tb),
        grid=(n // tb,),
        in_specs=[blk],
        out_specs=[blk, blk, blk, pl.BlockSpec((1, LANES), lambda i: (0, 0))],
        out_shape=[
            jax.ShapeDtypeStruct((n, LANES), jnp.int32),
            jax.ShapeDtypeStruct((n, LANES), F32),
            jax.ShapeDtypeStruct((n, LANES), jnp.int32),
            jax.ShapeDtypeStruct((1, LANES), jnp.int32),
        ],
        scratch_shapes=[pltpu.VMEM((1, LANES), F32)],
        compiler_params=_cparams(("arbitrary",)),
        name="route",
    )(logits)


def _dispatch_kernel(zblk_ref, zok_ref, used_ref, pos_ref, xn_ref, xr_hbm, zero_scr, sem,
                     *, tb, pitch, n_blocks_max):
    blk_rows = MOE_TM * pitch

    def zero_copy(b):
        dst = xr_hbm.at[pl.ds(pl.multiple_of(b * blk_rows, blk_rows), blk_rows), :]
        return pltpu.make_async_copy(zero_scr, dst, sem.at[1])

    @pl.when(pl.program_id(0) == 0)
    def _():
        zero_scr[...] = jnp.zeros_like(zero_scr)

        def per_expert(action):
            def body(e, carry):
                @pl.when(zok_ref[e] == 1)
                def _():
                    action(zero_copy(zblk_ref[e]))
                return carry
            lax.fori_loop(0, N_EXPERTS, body, 0)

        def per_tail(action):
            def body(b, carry):
                action(zero_copy(b))
                return carry
            lax.fori_loop(used_ref[0], n_blocks_max, body, 0)

        per_expert(lambda c: c.start())
        per_tail(lambda c: c.start())
        per_expert(lambda c: c.wait())
        per_tail(lambda c: c.wait())

    def issue(t, carry):
        src = xn_ref.at[pl.ds(pl.multiple_of(t * pitch, pitch), pitch), :]
        for k in range(TOP_K):
            p = pos_ref[0, t * TOP_K + k]
            dst = xr_hbm.at[pl.ds(pl.multiple_of(p * pitch, pitch), pitch), :]
            pltpu.make_async_copy(src, dst, sem.at[0]).start(priority=k % 2)
        return carry

    lax.fori_loop(0, tb, issue, 0, unroll=4)
    for k in range(TOP_K):
        pltpu.make_async_copy(xn_ref, xr_hbm.at[pl.ds(0, tb * pitch), :], sem.at[0]).wait()


def _dispatch(xn_rm, pos, zblk, zok, n_used, n_blocks_max, pitch, tb=256):
    n = xn_rm.shape[0] // pitch
    nb = n // tb
    grid_spec = pltpu.PrefetchScalarGridSpec(
        num_scalar_prefetch=3,
        grid=(nb,),
        in_specs=[
            pl.BlockSpec((None, 1, TOP_K * tb), lambda i, *_: (i, 0, 0), memory_space=pltpu.SMEM),
            pl.BlockSpec((tb * pitch, LANES), lambda i, *_: (i, 0)),
        ],
        out_specs=pl.BlockSpec(memory_space=pl.ANY),
        scratch_shapes=[pltpu.VMEM((MOE_TM * pitch, LANES), xn_rm.dtype), pltpu.SemaphoreType.DMA((2,))],
    )
    return pl.pallas_call(
        functools.partial(_dispatch_kernel, tb=tb, pitch=pitch, n_blocks_max=n_blocks_max),
        grid_spec=grid_spec,
        out_shape=jax.ShapeDtypeStruct((n_blocks_max * MOE_TM * pitch, LANES), xn_rm.dtype),
        compiler_params=_cparams(("arbitrary",)),
        name="dispatch",
    )(zblk, zok, n_used, pos.reshape(nb, 1, TOP_K * tb), xn_rm)


def _cast_rows(src, dst, rows_per=256):
    def body(i, carry):
        r = pl.ds(pl.multiple_of(i * rows_per, rows_per), rows_per)
        dst[r, :] = src[r, :].astype(dst.dtype)
        return carry

    lax.fori_loop(0, src.shape[0] // rows_per, body, 0)


def _stream_row_blocks(cnt, in_copy, out_copy, compute):
    for ahead in range(MOE_NBUF - 1):
        @pl.when(cnt > ahead)
        def _():
            in_copy(ahead, ahead).start()

    def body(b, carry):
        slot = b % MOE_NBUF

        @pl.when(b + MOE_NBUF - 1 < cnt)
        def _():
            in_copy(b + MOE_NBUF - 1, (b + MOE_NBUF - 1) % MOE_NBUF).start()

        in_copy(b, slot).wait()

        @pl.when(b >= MOE_NBUF)
        def _():
            out_copy(b - MOE_NBUF, slot).wait()

        compute(slot)
        out_copy(b, slot).start()
        return carry

    lax.fori_loop(0, cnt, body, 0)

    for back in range(MOE_NBUF, 0, -1):
        @pl.when(cnt >= back)
        def _():
            out_copy(cnt - back, (cnt - back) % MOE_NBUF).wait()


def _stream_expert_rows(first, cnt, in_copy, out_copy, compute):
    n_units = cnt // MOE_UNIT
    _stream_row_blocks(n_units, in_copy(first, MOE_UNIT), out_copy(first, MOE_UNIT), compute(MOE_UNIT))
    done = n_units * MOE_UNIT
    _stream_row_blocks(cnt - done, in_copy(first + done, 1), out_copy(first + done, 1), compute(1))


def _zero_fill_blocks(first, last, zero_src, dst_copy):
    def start(b, carry):
        dst_copy(b).start()
        return carry

    def wait(b, carry):
        dst_copy(b).wait()
        return carry

    zero_src[...] = jnp.zeros_like(zero_src)
    lax.fori_loop(first, last, start, 0)
    lax.fori_loop(first, last, wait, 0)


def _moe_up_kernel(bs_ref, bc_ref, used_ref, x_hbm, w_hbm, bg_ref, bu_ref, act_hbm,
                   wf_scr, wb_scr, xbuf, lhs_scr, obuf, sem_w, sem_x, sem_o,
                   *, nt, tn, d_ff, pitch, n_blocks_max):
    g = pl.program_id(0)
    ng = pl.num_programs(0)
    e = g // nt
    col = pl.multiple_of((g % nt) * tn, tn)
    blk_rows = MOE_TM * pitch

    def w_copies(step):
        ee = step // nt
        cc = pl.multiple_of((step % nt) * tn, tn)
        return (pltpu.make_async_copy(w_hbm.at[ee, :, pl.ds(cc, tn)], wf_scr.at[0], sem_w.at[0]),
                pltpu.make_async_copy(w_hbm.at[ee, :, pl.ds(d_ff + cc, tn)], wf_scr.at[1], sem_w.at[1]))

    @pl.when(g == 0)
    def _():
        for c in w_copies(0):
            c.start(priority=1)

    for half, c in enumerate(w_copies(g)):
        c.wait()
        _cast_rows(wf_scr.at[half], wb_scr.at[half])

    @pl.when(g + 1 < ng)
    def _():
        for c in w_copies(g + 1):
            c.start(priority=1)

    start = bs_ref[e]

    def x_copy(first, nblk):
        def make(u, slot):
            r0 = pl.multiple_of((first + u * nblk) * blk_rows, blk_rows)
            return pltpu.make_async_copy(x_hbm.at[pl.ds(r0, nblk * blk_rows), :],
                                         xbuf.at[slot, pl.ds(0, nblk * blk_rows), :], sem_x.at[slot])
        return make

    def o_copy(first, nblk):
        def make(u, slot):
            r0 = pl.multiple_of((first + u * nblk) * MOE_TM, MOE_TM)
            return pltpu.make_async_copy(obuf.at[slot, pl.ds(0, nblk * MOE_TM), :],
                                         act_hbm.at[pl.ds(r0, nblk * MOE_TM), pl.ds(col, tn)], sem_o.at[slot])
        return make

    def compute(nblk):
        rows = nblk * MOE_TM

        def run(slot):
            half = pitch * LANES
            for c in range(pitch):
                w = xbuf[slot, pl.ds(c, rows, stride=pitch), :]
                lhs_scr[0:rows, c * LANES:(c + 1) * LANES] = _unpack_lo(w).astype(BF16)
                lhs_scr[0:rows, half + c * LANES:half + (c + 1) * LANES] = _unpack_hi(w).astype(BF16)
            x = lhs_scr[0:rows, :]
            gate = jnp.minimum(_dot(x, wb_scr[0]) + bg_ref[...], SWIGLU_LIMIT)
            up = jnp.clip(_dot(x, wb_scr[1]) + bu_ref[...], -SWIGLU_LIMIT, SWIGLU_LIMIT)
            obuf[slot, 0:rows, :] = (gate * jax.nn.sigmoid(SWIGLU_ALPHA * gate) * (up + 1.0)).astype(obuf.dtype)
        return run

    _stream_expert_rows(start, bc_ref[e], x_copy, o_copy, compute)

    @pl.when(e == N_EXPERTS - 1)
    def _():
        zero_blk = obuf.at[0, pl.ds(0, MOE_TM), :]

        def tail_copy(b):
            r0 = pl.multiple_of(b * MOE_TM, MOE_TM)
            return pltpu.make_async_copy(zero_blk, act_hbm.at[pl.ds(r0, MOE_TM), pl.ds(col, tn)], sem_o.at[0])
        _zero_fill_blocks(used_ref[0], n_blocks_max, zero_blk, tail_copy)


def _moe_up(x_rows_rm, w_up, b_up, blk_start, blk_count, n_used, n_blocks_max, tn=1024):
    n_exp, d, two_ff = w_up.shape
    d_ff = two_ff // 2
    pitch = d // 2 // LANES
    nt = d_ff // tn
    up_off = d_ff // tn
    grid_spec = pltpu.PrefetchScalarGridSpec(
        num_scalar_prefetch=3,
        grid=(n_exp * nt,),
        in_specs=[
            pl.BlockSpec(memory_space=pl.ANY),
            pl.BlockSpec(memory_space=pl.ANY),
            pl.BlockSpec((None, 1, tn), lambda g, *_: (g // nt, 0, g % nt)),
            pl.BlockSpec((None, 1, tn), lambda g, *_: (g // nt, 0, g % nt + up_off)),
        ],
        out_specs=pl.BlockSpec(memory_space=pl.ANY),
        scratch_shapes=[
            pltpu.VMEM((2, d, tn), F32),
            pltpu.VMEM((2, d, tn), BF16),
            pltpu.VMEM((MOE_NBUF, MOE_UNIT * MOE_TM * pitch, LANES), jnp.uint32),
            pltpu.VMEM((MOE_UNIT * MOE_TM, d), BF16),
            pltpu.VMEM((MOE_NBUF, MOE_UNIT * MOE_TM, tn), BF16),
            pltpu.SemaphoreType.DMA((2,)),
            pltpu.SemaphoreType.DMA((MOE_NBUF,)),
            pltpu.SemaphoreType.DMA((MOE_NBUF,)),
        ],
    )
    return pl.pallas_call(
        functools.partial(_moe_up_kernel, nt=nt, tn=tn, d_ff=d_ff, pitch=pitch, n_blocks_max=n_blocks_max),
        grid_spec=grid_spec,
        out_shape=jax.ShapeDtypeStruct((n_blocks_max * MOE_TM, d_ff), BF16),
        compiler_params=_cparams(("arbitrary",)),
        name="moe_up",
    )(blk_start, blk_count, n_used, x_rows_rm, w_up, b_up, b_up)


def _moe_down_kernel(bs_ref, bc_ref, used_ref, a_hbm, w_hbm, b_ref, y_hbm,
                     wf_scr, wb_scr, abuf, obuf, sem_w, sem_a, sem_o, *, pitch, n_blocks_max):
    e = pl.program_id(0)
    blk_rows = MOE_TM * pitch

    def w_copy(ee):
        return pltpu.make_async_copy(w_hbm.at[ee], wf_scr, sem_w.at[0])

    @pl.when(e == 0)
    def _():
        w_copy(0).start(priority=1)

    w_copy(e).wait()
    _cast_rows(wf_scr, wb_scr)

    @pl.when(e + 1 < pl.num_programs(0))
    def _():
        w_copy(e + 1).start(priority=1)

    start = bs_ref[e]

    def a_copy(first, nblk):
        def make(u, slot):
            r0 = pl.multiple_of((first + u * nblk) * MOE_TM, MOE_TM)
            return pltpu.make_async_copy(a_hbm.at[pl.ds(r0, nblk * MOE_TM), :],
                                         abuf.at[slot, pl.ds(0, nblk * MOE_TM), :], sem_a.at[slot])
        return make

    def o_copy(first, nblk):
        def make(u, slot):
            r0 = pl.multiple_of((first + u * nblk) * blk_rows, blk_rows)
            return pltpu.make_async_copy(obuf.at[slot, pl.ds(0, nblk * blk_rows), :],
                                         y_hbm.at[pl.ds(r0, nblk * blk_rows), :], sem_o.at[slot])
        return make

    def compute(nblk):
        rows = nblk * MOE_TM

        def run(slot):
            y = _dot(abuf[slot, 0:rows, :], wb_scr[...]) + b_ref[...]
            _store_rowmajor(obuf.at[slot], _pack_bf16_pairs(y))
        return run

    _stream_expert_rows(start, bc_ref[e], a_copy, o_copy, compute)

    @pl.when(e == N_EXPERTS - 1)
    def _():
        zero_blk = obuf.at[0, pl.ds(0, blk_rows), :]

        def tail_copy(b):
            r0 = pl.multiple_of(b * blk_rows, blk_rows)
            return pltpu.make_async_copy(zero_blk, y_hbm.at[pl.ds(r0, blk_rows), :], sem_o.at[0])
        _zero_fill_blocks(used_ref[0], n_blocks_max, zero_blk, tail_copy)


def _moe_down(act, w_down, b_down, blk_start, blk_count, n_used, n_blocks_max):
    n_rows, d_ff = act.shape
    n_exp, _, d = w_down.shape
    pitch = d // 2 // LANES
    grid_spec = pltpu.PrefetchScalarGridSpec(
        num_scalar_prefetch=3,
        grid=(n_exp,),
        in_specs=[
            pl.BlockSpec(memory_space=pl.ANY),
            pl.BlockSpec(memory_space=pl.ANY),
            pl.BlockSpec((None, 1, d), lambda e, *_: (e, 0, 0)),
        ],
        out_specs=pl.BlockSpec(memory_space=pl.ANY),
        scratch_shapes=[
            pltpu.VMEM((d_ff, d), F32),
            pltpu.VMEM((d_ff, d), BF16),
            pltpu.VMEM((MOE_NBUF, MOE_UNIT * MOE_TM, d_ff), BF16),
            pltpu.VMEM((MOE_NBUF, MOE_UNIT * MOE_TM * pitch, LANES), jnp.uint32),
            pltpu.SemaphoreType.DMA((1,)),
            pltpu.SemaphoreType.DMA((MOE_NBUF,)),
            pltpu.SemaphoreType.DMA((MOE_NBUF,)),
        ],
    )
    return pl.pallas_call(
        functools.partial(_moe_down_kernel, pitch=pitch, n_blocks_max=n_blocks_max),
        grid_spec=grid_spec,
        out_shape=jax.ShapeDtypeStruct((n_rows * pitch, LANES), jnp.uint32),
        compiler_params=_cparams(("arbitrary",)),
        name="moe_down",
    )(blk_start, blk_count, n_used, act, w_down, b_down)


def _combine_kernel(pos0_ref, posn_ref, y_hbm, x2_ref, gate_ref, g_ref, o_ref, buf_even, buf_odd, sem,
                    *, tb, pitch):
    i = pl.program_id(0)
    last = pl.num_programs(0) - 1
    n_rows = TOP_K * tb

    def row_copy(p_ref, r, buf, sem_slot):
        src = y_hbm.at[pl.ds(pl.multiple_of(p_ref[0, r] * pitch, pitch), pitch), :]
        return pltpu.make_async_copy(src, buf.at[pl.ds(r * pitch, pitch), :], sem.at[sem_slot])

    def wait_block(buf, sem_slot):
        pltpu.make_async_copy(y_hbm.at[pl.ds(0, n_rows * pitch), :], buf, sem.at[sem_slot]).wait()

    @pl.when(i == 0)
    def _():
        def issue(j, carry):
            for half in range(2):
                row_copy(pos0_ref, 2 * j + half, buf_even, 0).start(priority=half)
            return carry
        lax.fori_loop(0, n_rows // 2, issue, 0, unroll=4)

    def step(buf_cur, sem_cur, buf_nxt, sem_nxt):
        wait_block(buf_cur, sem_cur)
        for r in range(n_rows):
            row_copy(posn_ref, r, buf_nxt, sem_nxt).start(priority=r % 2)
        gates = gate_ref[...]
        half = pitch * LANES
        ssq = jnp.zeros((tb, 1), F32)
        for c in range(pitch):
            lo = slice(c * LANES, (c + 1) * LANES)
            hi = slice(half + c * LANES, half + (c + 1) * LANES)
            z_lo = x2_ref[:, lo]
            z_hi = x2_ref[:, hi]
            for k in range(TOP_K):
                w = buf_cur[pl.ds(k * tb * pitch + c, tb, stride=pitch), :]
                z_lo = z_lo + gates[:, k:k + 1] * _unpack_lo(w)
                z_hi = z_hi + gates[:, k:k + 1] * _unpack_hi(w)
            o_ref[:, lo] = z_lo
            o_ref[:, hi] = z_hi
            ssq = ssq + (jnp.sum(z_lo * z_lo, axis=-1, keepdims=True)
                         + jnp.sum(z_hi * z_hi, axis=-1, keepdims=True))
        o_ref[...] = o_ref[...] * lax.rsqrt(ssq * (1.0 / (2 * half)) + EPS) * g_ref[...]

        @pl.when(i == last)
        def _():
            wait_block(buf_nxt, sem_nxt)

    @pl.when(i % 2 == 0)
    def _():
        step(buf_even, 0, buf_odd, 1)

    @pl.when(i % 2 == 1)
    def _():
        step(buf_odd, 1, buf_even, 0)


def _combine(y_rows_rm, pos, gates, x2, g_final, tb=256):
    n, d = x2.shape
    nb = n // tb
    pitch = d // 2 // LANES
    pos_blk = pos.reshape(nb, tb, TOP_K).transpose(0, 2, 1).reshape(nb, 1, TOP_K * tb)
    pos_spec = lambda imap: pl.BlockSpec((None, 1, TOP_K * tb), imap, memory_space=pltpu.SMEM)
    return pl.pallas_call(
        functools.partial(_combine_kernel, tb=tb, pitch=pitch),
        grid=(nb,),
        in_specs=[
            pos_spec(lambda i: (0, 0, 0)),
            pos_spec(lambda i: (jnp.minimum(i + 1, nb - 1), 0, 0)),
            pl.BlockSpec(memory_space=pl.ANY),
            pl.BlockSpec((tb, d), lambda i: (i, 0)),
            pl.BlockSpec((tb, LANES), lambda i: (i, 0)),
            pl.BlockSpec((1, d), lambda i: (0, 0)),
        ],
        out_specs=pl.BlockSpec((tb, d), lambda i: (i, 0)),
        out_shape=jax.ShapeDtypeStruct((n, d), F32),
        scratch_shapes=[pltpu.VMEM((TOP_K * tb * pitch, LANES), jnp.uint32),
                        pltpu.VMEM((TOP_K * tb * pitch, LANES), jnp.uint32),
                        pltpu.SemaphoreType.DMA((2,))],
        compiler_params=_cparams(("arbitrary",)),
        name="combine",
    )(pos_blk, pos_blk, y_rows_rm, x2, gates, g_final)


def _routing_tables(idx, rank, cnt):
    counts = cnt[0, :N_EXPERTS]
    blk_count = (counts + MOE_TM - 1) // MOE_TM
    blk_end = jnp.cumsum(blk_count)
    blk_start = blk_end - blk_count
    hot = idx[:, :TOP_K, None] == jnp.arange(N_EXPERTS, dtype=jnp.int32)
    pos = jnp.sum(jnp.where(hot, blk_start * MOE_TM, 0), axis=-1) + rank[:, :TOP_K]
    return pos.astype(jnp.int32), blk_start.astype(jnp.int32), blk_count.astype(jnp.int32)


def kernel(x, mem, g_attn_norm, g_mem_norm, w_in, w_gla_a2, b_gla_a, g_gla_out, b_fox_f, g_fox_out,
           w_mem_kv, g_mem_out, w_out, g_ffn_norm, w_router, b_router, w_moe_up, b_moe_up,
           w_moe_down, b_moe_down, g_final):
    batch, seq, d = x.shape
    n_mem = mem.shape[1]
    n = batch * seq
    depth = w_in.shape[0]
    assert depth == 1, "the combine kernel applies the final norm, so exactly one layer is supported"
    kw = GLA_HEADS * GLA_DK
    vw = GLA_HEADS * GLA_DV
    fw = FOX_HEADS * FOX_DH
    mw = MEM_HEADS * MEM_DH
    o_q, o_k, o_v, o_g = 0, kw, 2 * kw, 2 * kw + vw
    o_a = o_g + vw
    o_fq = o_a + GLA_LOWRANK
    o_fk, o_fv = o_fq + fw, o_fq + 2 * fw
    o_ff = o_fq + 3 * fw
    o_mq = o_ff + FOX_HEADS
    f_lane = GLA_LOWRANK

    xf = x.reshape(n, d)
    for l in range(depth):
        fox_q_scale = FOX_DH ** -0.5 * LOG2E
        col_scale = jnp.ones((w_in.shape[2],), F32).at[o_fq:o_fk].set(fox_q_scale)
        wi = lax.optimization_barrier((w_in[l] * col_scale).astype(BF16))
        wa = jnp.concatenate([wi[:, o_q:o_a], wi[:, o_fq:o_ff], wi[:, o_mq:o_mq + mw]], axis=1)
        wb = jnp.zeros((d, LANES), BF16).at[:, :GLA_LOWRANK].set(wi[:, o_a:o_fq])
        wb = wb.at[:, f_lane:f_lane + FOX_HEADS].set(wi[:, o_ff:o_mq])
        proj, small = _in_proj(xf, g_attn_norm[l].reshape(1, d), wa, wb)

        wa2p = jnp.zeros((LANES, kw), F32).at[:GLA_LOWRANK].set(w_gla_a2[l]).astype(BF16)
        bfv = jnp.zeros((1, LANES), F32).at[0, f_lane:f_lane + FOX_HEADS].set(b_fox_f[l])
        gla, f_cum = _gla(proj, small, wa2p, b_gla_a[l].reshape(1, kw), bfv,
                          g_gla_out[l].reshape(1, vw), batch, seq)

        f_t = f_cum.reshape(batch, seq, LANES)[:, :, f_lane:f_lane + FOX_HEADS].transpose(0, 2, 1)
        f_t = jnp.concatenate([f_t, jnp.zeros_like(f_t)], axis=1)
        fox = _fox(proj, f_t, g_fox_out[l].reshape(1, fw), batch, seq)

        kv = _mem_kv(mem.reshape(batch * n_mem, d), g_mem_norm[l].reshape(1, d), w_mem_kv[l].astype(BF16))
        memo = _mem_attn(proj, kv, g_mem_out[l].reshape(1, mw), batch, seq, n_mem)

        wr = jnp.zeros((d, LANES), F32).at[:, :N_EXPERTS].set(w_router[l])
        wr_hi = wr.astype(BF16)
        wr = jnp.concatenate([wr_hi, (wr - wr_hi.astype(F32)).astype(BF16)], axis=1)
        br =jnp.zeros((1, LANES), F32).at[0, :N_EXPERTS].set(b_router[l])
        x2, xn, logits = _out_proj(xf, gla, fox, memo, w_out[l].astype(BF16),
                                   g_ffn_norm[l].reshape(1, d), wr, br)

        idx, gates, rank, cnt = _route(logits)
        pos, blk_start, blk_count = _routing_tables(idx, rank, cnt)
        n_blocks_max = -(-(n * TOP_K + N_EXPERTS * (MOE_TM - 1)) // MOE_TM)
        blk_end = blk_start + blk_count
        n_used = blk_end[-1:]
        x_rows = _dispatch(xn, pos, blk_end - 1, (blk_count > 0).astype(jnp.int32), n_used,
                           n_blocks_max, d // 2 // LANES)
        d_ff = w_moe_up.shape[3] // 2
        act = _moe_up(x_rows, w_moe_up[l], b_moe_up[l].reshape(N_EXPERTS, 1, 2 * d_ff),
                      blk_start, blk_count, n_used, n_blocks_max)
        y_rows = _moe_down(act, w_moe_down[l], b_moe_down[l].reshape(N_EXPERTS, 1, d),
                           blk_start, blk_count, n_used, n_blocks_max)
        xf = _combine(y_rows, pos, gates, x2, g_final.reshape(1, d))
    return xf.reshape(batch, seq, d)
```

```python
import functools

import jax
import jax.numpy as jnp
from jax import lax
from jax.experimental import pallas as pl
from jax.experimental.pallas import tpu as pltpu

EPS = 1e-5
CHUNK = 64
GLA_HEADS = 4
GLA_DK = 128
GLA_DV = 256
GLA_LOWRANK = 16
GLA_TAU = 16.0
FOX_HEADS = 4
FOX_DH = 128
MEM_HEADS = 4
MEM_DH = 128
N_EXPERTS = 32
TOP_K = 4
SWIGLU_LIMIT = 7.0
SWIGLU_ALPHA = 1.702
LANES = 128
MXU_COLS = 256
MOE_TM = 256
MOE_NBUF = 2
MOE_UNIT = 2
VMEM_LIMIT = 56 * 1024 * 1024

F32 = jnp.float32
BF16 = jnp.bfloat16


def _cparams(sem, vmem=VMEM_LIMIT):
    return pltpu.CompilerParams(dimension_semantics=sem, vmem_limit_bytes=vmem)


def _log_sigmoid(x):
    return jnp.minimum(x, 0.0) - jnp.log1p(jnp.exp(-jnp.abs(x)))


def _rms(x, g):
    return x * lax.rsqrt(jnp.mean(x * x, axis=-1, keepdims=True) + EPS) * g


def _dot(a, b, **kw):
    return jnp.dot(a, b, preferred_element_type=F32, **kw)


def _dot_nt(a, b):
    return lax.dot_general(a, b, (((1,), (1,)), ((), ())), preferred_element_type=F32)


def _dot_tn(a, b):
    return lax.dot_general(a, b, (((0,), (0,)), ((), ())), preferred_element_type=F32)


def _store_rowmajor(ref, x, base=0):
    rows, w = x.shape
    pitch = w // LANES
    for c in range(pitch):
        ref[pl.ds(base + c, rows, stride=pitch), :] = x[:, c * LANES:(c + 1) * LANES]


def _pack_bf16_pairs(x):
    half = x.shape[1] // 2
    bits = lambda v: lax.bitcast_convert_type(v.astype(BF16).astype(F32), jnp.uint32)
    return bits(x[:, half:]) | (bits(x[:, :half]) >> 16)


def _unpack_lo(w):
    return lax.bitcast_convert_type(w << 16, F32)


def _unpack_hi(w):
    return lax.bitcast_convert_type(w & jnp.uint32(0xFFFF0000), F32)


def _in_proj_kernel(x_ref, g_ref, wa_ref, wb_ref, proj_ref, small_ref, h_scr):
    @pl.when(pl.program_id(1) == 0)
    def _():
        hb = _rms(x_ref[...], g_ref[...]).astype(BF16)
        h_scr[...] = hb
        small_ref[...] = _dot(hb, wb_ref[...])

    proj_ref[...] = _dot(h_scr[...], wa_ref[...]).astype(proj_ref.dtype)


def _in_proj(x2d, g, wa, wb, tm=1024, tn=1280):
    n, d = x2d.shape
    na = wa.shape[1]
    return pl.pallas_call(
        _in_proj_kernel,
        grid=(n // tm, na // tn),
        in_specs=[
            pl.BlockSpec((tm, d), lambda i, j: (i, 0)),
            pl.BlockSpec((1, d), lambda i, j: (0, 0)),
            pl.BlockSpec((d, tn), lambda i, j: (0, j)),
            pl.BlockSpec((d, LANES), lambda i, j: (0, 0)),
        ],
        out_specs=[
            pl.BlockSpec((tm, tn), lambda i, j: (i, j)),
            pl.BlockSpec((tm, LANES), lambda i, j: (i, 0)),
        ],
        out_shape=[
            jax.ShapeDtypeStruct((n, na), BF16),
            jax.ShapeDtypeStruct((n, LANES), F32),
        ],
        scratch_shapes=[pltpu.VMEM((tm, d), BF16)],
        compiler_params=_cparams(("parallel", "arbitrary")),
        name="in_proj",
    )(x2d, g, wa, wb)


def _gla_kernel(q_ref, k_ref, v_ref, gate_ref, small_ref, wa2_ref, ba_ref, bf_ref, gout_ref,
                o_ref, f_ref, state_scr, fcar_scr, la_scr, lf_scr, *, n_chunks):
    @pl.when(pl.program_id(1) == 0)
    def _():
        state_scr[...] = jnp.zeros_like(state_scr)
        fcar_scr[...] = jnp.zeros_like(fcar_scr)

    small = small_ref[...]
    la_scr[...] = _log_sigmoid(_dot(small.astype(BF16), wa2_ref[...]) + ba_ref[...]) * (1.0 / GLA_TAU)
    lf_scr[...] = _log_sigmoid(small + bf_ref[...])
    row = lax.broadcasted_iota(jnp.int32, (CHUNK, CHUNK), 0)
    col = lax.broadcasted_iota(jnp.int32, (CHUNK, CHUNK), 1)
    tri = (col <= row).astype(F32)
    scale = GLA_DK ** -0.5

    def chunk_body(c, carry):
        r = pl.ds(pl.multiple_of(c * CHUNK, CHUNK), CHUNK)
        b = _dot(tri, la_scr[r, :], precision=lax.Precision.HIGHEST)
        b_end = b[CHUNK - 1:CHUNK, :]
        k_dec = k_ref[r, :].astype(F32) * jnp.exp(b_end - b)
        decay = jnp.exp(b_end)
        f_cum = _dot(tri, lf_scr[r, :], precision=lax.Precision.HIGHEST) + fcar_scr[...]
        f_ref[r, :] = f_cum
        fcar_scr[...] = f_cum[CHUNK - 1:CHUNK, :]
        heads = range(GLA_HEADS)
        ks = [slice(h * GLA_DK, (h + 1) * GLA_DK) for h in heads]
        vs = [slice(h * GLA_DV, (h + 1) * GLA_DV) for h in heads]
        old = [state_scr[h] for h in heads]
        new = [old[h] * decay[:, ks[h]] + _dot_tn(v_ref[r, vs[h]], k_dec[:, ks[h]].astype(BF16))
               for h in heads]
        for h in heads:
            state_scr[h] = new[h]
        for h in heads:
            o = _dot_nt(q_ref[r, ks[h]], new[h].astype(BF16)) * scale
            gt = gate_ref[r, vs[h]].astype(F32)
            o_ref[r, vs[h]] = (_rms(o, gout_ref[:, vs[h]]) * (gt * jax.nn.sigmoid(gt))).astype(o_ref.dtype)
        return carry

    lax.fori_loop(0, n_chunks, chunk_body, 0, unroll=True)


def _gla(proj, small, wa2p, ba, bfv, gout, batch, seq, ts=512):
    n = proj.shape[0]
    nsb = seq // ts
    kw = GLA_HEADS * GLA_DK
    vw = GLA_HEADS * GLA_DV
    row = lambda b, s: b * nsb + s
    return pl.pallas_call(
        functools.partial(_gla_kernel, n_chunks=ts // CHUNK),
        grid=(batch, nsb),
        in_specs=[
            pl.BlockSpec((ts, kw), lambda b, s: (row(b, s), 0)),
            pl.BlockSpec((ts, kw), lambda b, s: (row(b, s), 1)),
            pl.BlockSpec((ts, vw), lambda b, s: (row(b, s), 1)),
            pl.BlockSpec((ts, vw), lambda b, s: (row(b, s), 2)),
            pl.BlockSpec((ts, LANES), lambda b, s: (row(b, s), 0)),
            pl.BlockSpec((LANES, kw), lambda b, s: (0, 0)),
            pl.BlockSpec((1, kw), lambda b, s: (0, 0)),
            pl.BlockSpec((1, LANES), lambda b, s: (0, 0)),
            pl.BlockSpec((1, vw), lambda b, s: (0, 0)),
        ],
        out_specs=[
            pl.BlockSpec((ts, vw), lambda b, s: (row(b, s), 0)),
            pl.BlockSpec((ts, LANES), lambda b, s: (row(b, s), 0)),
        ],
        out_shape=[
            jax.ShapeDtypeStruct((n, vw), BF16),
            jax.ShapeDtypeStruct((n, LANES), F32),
        ],
        scratch_shapes=[
            pltpu.VMEM((GLA_HEADS, GLA_DV, GLA_DK), F32),
            pltpu.VMEM((1, LANES), F32),
            pltpu.VMEM((ts, kw), F32),
            pltpu.VMEM((ts, LANES), F32),
        ],
        compiler_params=_cparams(("parallel", "arbitrary")),
        name="gla",
    )(proj, proj, proj, proj, small, wa2p, ba, bfv, gout)


LOG2E = 1.4426950408889634
def _fox_kernel(q_ref, k_ref, v_ref, fk_ref, g_ref, o_ref, m_scr, l_scr, acc_scr, s_even, s_odd,
                *, tq, tk):
    i = pl.program_id(1)
    t = pl.program_id(2)

    @pl.when(t == 0)
    def _():
        m_scr[...] = jnp.full_like(m_scr, -jnp.inf)
        l_scr[...] = jnp.zeros_like(l_scr)
        acc_scr[...] = jnp.zeros_like(acc_scr)

    heads = range(FOX_HEADS)
    hs = [slice(h * FOX_DH, (h + 1) * FOX_DH) for h in heads]

    def score(s_out):
        for h in heads:
            s_out[h] = _dot_nt(q_ref[:, hs[h]], k_ref[:, hs[h]])

    def update(s_in, on_diagonal):
        m_prev = [m_scr[h] for h in heads]
        l_prev = [l_scr[h] for h in heads]
        acc_prev = [acc_scr[h] for h in heads]
        s = [s_in[h] - fk_ref[h:h + 1, :] * LOG2E for h in heads]
        if on_diagonal:
            row = lax.broadcasted_iota(jnp.int32, (tq, tk), 0)
            col = lax.broadcasted_iota(jnp.int32, (tq, tk), 1)
            s = [jnp.where(col <= row, sh, -jnp.inf) for sh in s]
        m_new = [jnp.maximum(m_prev[h], jnp.max(s[h], axis=-1, keepdims=True)) for h in heads]
        alpha = [jnp.exp2(m_prev[h] - m_new[h]) for h in heads]
        p = [jnp.exp2(s[h] - m_new[h]) for h in heads]
        l_new = [alpha[h] * l_prev[h] + jnp.sum(p[h], axis=-1, keepdims=True) for h in heads]
        acc_new = [alpha[h] * acc_prev[h] + _dot(p[h].astype(BF16), v_ref[:, hs[h]]) for h in heads]
        return m_new, l_new, acc_new

    def step(s_write, s_read):
        @pl.when(t == 0)
        def _():
            score(s_write)

        @pl.when(jnp.logical_and(t >= 1, t <= i))
        def _():
            m_new, l_new, acc_new = update(s_read, False)
            score(s_write)
            for h in heads:
                m_scr[h] = m_new[h]
                l_scr[h] = l_new[h]
                acc_scr[h] = acc_new[h]

        @pl.when(t == i + 1)
        def _():
            _, l_new, acc_new = update(s_read, True)
            for h in heads:
                o_ref[:, hs[h]] = _rms(acc_new[h] / l_new[h], g_ref[:, hs[h]]).astype(o_ref.dtype)

    @pl.when(t % 2 == 0)
    def _():
        step(s_even, s_odd)

    @pl.when(t % 2 == 1)
    def _():
        step(s_odd, s_even)


def _fox(proj, f_t, g_fox, batch, seq, tq=512, tk=512):
    assert tq == tk, "the diagonal-block mask assumes square blocks"
    n = proj.shape[0]
    w = FOX_HEADS * FOX_DH
    nq, nk = seq // tq, seq // tk
    qcol, kcol, vcol = 3072 // w, 3584 // w, 4096 // w
    return pl.pallas_call(
        functools.partial(_fox_kernel, tq=tq, tk=tk),
        grid=(batch, nq, nk + 1),
        in_specs=[
            pl.BlockSpec((tq, w), lambda b, i, t: (b * nq + i, qcol)),
            pl.BlockSpec((tk, w), lambda b, i, t: (b * nk + jnp.minimum(t, i), kcol)),
            pl.BlockSpec((tk, w), lambda b, i, t: (b * nk + jnp.clip(t - 1, 0, i), vcol)),
            pl.BlockSpec((None, 8, tk), lambda b, i, t: (b, 0, jnp.clip(t - 1, 0, i))),
            pl.BlockSpec((1, w), lambda b, i, t: (0, 0)),
        ],
        out_specs=pl.BlockSpec((tq, w), lambda b, i, t: (b * nq + i, 0)),
        out_shape=jax.ShapeDtypeStruct((n, w), BF16),
        scratch_shapes=[
            pltpu.VMEM((FOX_HEADS, tq, 1), F32),
            pltpu.VMEM((FOX_HEADS, tq, 1), F32),
            pltpu.VMEM((FOX_HEADS, tq, FOX_DH), F32),
            pltpu.VMEM((FOX_HEADS, tq, tk), F32),
            pltpu.VMEM((FOX_HEADS, tq, tk), F32),
        ],
        compiler_params=_cparams(("parallel", "parallel", "arbitrary")),
        name="fox",
    )(proj, proj, proj, f_t, g_fox)


def _mem_kv_kernel(m_ref, g_ref, w_ref, kv_ref):
    kv_ref[...] = _dot(_rms(m_ref[...], g_ref[...]).astype(BF16), w_ref[...]).astype(kv_ref.dtype)


def _mem_kv(mem2d, g, w, tm=256):
    n, d = mem2d.shape
    nw = w.shape[1]
    return pl.pallas_call(
        _mem_kv_kernel,
        grid=(n // tm,),
        in_specs=[
            pl.BlockSpec((tm, d), lambda i: (i, 0)),
            pl.BlockSpec((1, d), lambda i: (0, 0)),
            pl.BlockSpec((d, nw), lambda i: (0, 0)),
        ],
        out_specs=pl.BlockSpec((tm, nw), lambda i: (i, 0)),
        out_shape=jax.ShapeDtypeStruct((n, nw), BF16),
        compiler_params=_cparams(("parallel",)),
        name="mem_kv",
    )(mem2d, g, w)


def _mem_attn_kernel(q_ref, k_ref, v_ref, g_ref, o_ref):
    scale = MEM_DH ** -0.5
    for h in range(MEM_HEADS):
        hs = slice(h * MEM_DH, (h + 1) * MEM_DH)
        s = _dot_nt(q_ref[:, hs], k_ref[:, hs]) * scale
        p = jnp.exp(s - jnp.max(s, axis=-1, keepdims=True))
        l = jnp.sum(p, axis=-1, keepdims=True)
        o = _dot((p / l).astype(BF16), v_ref[:, hs])
        o_ref[:, hs] = _rms(o, g_ref[:, hs]).astype(o_ref.dtype)


def _mem_attn(proj, kv, g_mem_out, batch, seq, n_mem, tq=1024):
    n = proj.shape[0]
    w = MEM_HEADS * MEM_DH
    nq = seq // tq
    qcol = 4608 // w
    return pl.pallas_call(
        _mem_attn_kernel,
        grid=(n // tq,),
        in_specs=[
            pl.BlockSpec((tq, w), lambda i: (i, qcol)),
            pl.BlockSpec((n_mem, w), lambda i: (i // nq, 0)),
            pl.BlockSpec((n_mem, w), lambda i: (i // nq, 1)),
            pl.BlockSpec((1, w), lambda i: (0, 0)),
        ],
        out_specs=pl.BlockSpec((tq, w), lambda i: (i, 0)),
        out_shape=jax.ShapeDtypeStruct((n, w), BF16),
        compiler_params=_cparams(("parallel",)),
        name="mem_attn",
    )(proj, kv, kv, g_mem_out)


def _out_proj_kernel(x_ref, gla_ref, fox_ref, mem_ref, w1_ref, w2_ref, w3_ref, g_ref, wr_ref, br_ref,
                     x2_ref, xn_ref, logit_ref):
    x2 = (x_ref[...] + _dot(gla_ref[...], w1_ref[...]) + _dot(fox_ref[...], w2_ref[...])
          + _dot(mem_ref[...], w3_ref[...]))
    x2_ref[...] = x2
    xn = _rms(x2, g_ref[...])
    _store_rowmajor(xn_ref, _pack_bf16_pairs(xn))
    xh = xn.astype(BF16)
    xl = (xn - xh.astype(F32)).astype(BF16)
    hi = _dot(xh, wr_ref[...])
    logit_ref[...] = hi[:, :LANES] + hi[:, LANES:] + _dot(xl, wr_ref[:, :LANES]) + br_ref[...]


def _out_proj(x2d, gla, fox, memo, w_out, g_ffn, wr, br, tm=512):
    n, d = x2d.shape
    w1, w2 = gla.shape[1], fox.shape[1]
    const = lambda i: (0, 0)
    return pl.pallas_call(
        _out_proj_kernel,
        grid=(n // tm,),
        in_specs=[
            pl.BlockSpec((tm, d), lambda i: (i, 0)),
            pl.BlockSpec((tm, w1), lambda i: (i, 0)),
            pl.BlockSpec((tm, w2), lambda i: (i, 0)),
            pl.BlockSpec((tm, w2), lambda i: (i, 0)),
            pl.BlockSpec((w1, d), lambda i: (0, 0)),
            pl.BlockSpec((w2, d), lambda i: (w1 // w2, 0)),
            pl.BlockSpec((w2, d), lambda i: (w1 // w2 + 1, 0)),
            pl.BlockSpec((1, d), const),
            pl.BlockSpec((d, 2 * LANES), const),
            pl.BlockSpec((1, LANES), const),
        ],
        out_specs=[
            pl.BlockSpec((tm, d), lambda i: (i, 0)),
            pl.BlockSpec((tm * (d // 2 // LANES), LANES), lambda i: (i, 0)),
            pl.BlockSpec((tm, LANES), lambda i: (i, 0)),
        ],
        out_shape=[
            jax.ShapeDtypeStruct((n, d), F32),
            jax.ShapeDtypeStruct((n * (d // 2 // LANES), LANES), jnp.uint32),
            jax.ShapeDtypeStruct((n, LANES), F32),
        ],
        compiler_params=_cparams(("parallel",)),
        name="out_proj",
    )(x2d, gla, fox, memo, w_out, w_out, w_out, g_ffn, wr, br)


def _route_kernel(l_ref, idx_ref, gate_ref, rank_ref, cnt_ref, carry_scr, *, tb):
    @pl.when(pl.program_id(0) == 0)
    def _():
        carry_scr[...] = jnp.zeros_like(carry_scr)

    lane = lax.broadcasted_iota(jnp.int32, (tb, LANES), 1)
    logit = jnp.where(lane < N_EXPERTS, l_ref[...], -jnp.inf)
    vals, hots = [], []
    idx_out = jnp.zeros((tb, LANES), jnp.int32)
    for k in range(TOP_K):
        m = jnp.max(logit, axis=-1, keepdims=True)
        ik = jnp.min(jnp.where(logit == m, lane, LANES), axis=-1, keepdims=True)
        hot = lane == ik
        logit = jnp.where(hot, -jnp.inf, logit)
        vals.append(m)
        hots.append(hot)
        idx_out = jnp.where(lane == k, ik, idx_out)
    idx_ref[...] = idx_out

    e = [jnp.exp(v - vals[0]) for v in vals]
    den = e[0] + e[1] + e[2] + e[3]
    gate_out = jnp.zeros((tb, LANES), F32)
    for k in range(TOP_K):
        gate_out = jnp.where(lane == k, e[k] / den, gate_out)
    gate_ref[...] = gate_out

    member = jnp.zeros((tb, LANES), F32)
    for hot in hots:
        member = member + hot.astype(F32)
    row = lax.broadcasted_iota(jnp.int32, (tb, tb), 0)
    col = lax.broadcasted_iota(jnp.int32, (tb, tb), 1)
    before = (col < row).astype(BF16)
    rank = _dot(before, member.astype(BF16)) + carry_scr[...]
    rank_out = jnp.zeros((tb, LANES), F32)
    for k in range(TOP_K):
        rk = jnp.sum(jnp.where(hots[k], rank, 0.0), axis=-1, keepdims=True)
        rank_out = jnp.where(lane == k, rk, rank_out)
    rank_ref[...] = rank_out.astype(jnp.int32)
    total = carry_scr[...] + jnp.sum(member, axis=0, keepdims=True)
    carry_scr[...] = total
    cnt_ref[...] = total.astype(jnp.int32)


def _route(logits, tb=512):
    n = logits.shape[0]
    blk = pl.BlockSpec((tb, LANES), lambda i: (i, 0))
    return pl.pallas_call(
        functools.partial(_route_kernel, tb=tb),
        grid=(n // tb,),
        in_specs=[blk],
        out_specs=[blk, blk, blk, pl.BlockSpec((1, LANES), lambda i: (0, 0))],
        out_shape=[
            jax.ShapeDtypeStruct((n, LANES), jnp.int32),
            jax.ShapeDtypeStruct((n, LANES), F32),
            jax.ShapeDtypeStruct((n, LANES), jnp.int32),
            jax.ShapeDtypeStruct((1, LANES), jnp.int32),
        ],
        scratch_shapes=[pltpu.VMEM((1, LANES), F32)],
        compiler_params=_cparams(("arbitrary",)),
        name="route",
    )(logits)


def _dispatch_kernel(zblk_ref, zok_ref, used_ref, pos_ref, xn_ref, xr_hbm, zero_scr, sem,
                     *, tb, pitch, n_blocks_max):
    blk_rows = MOE_TM * pitch

    def zero_copy(b):
        dst = xr_hbm.at[pl.ds(pl.multiple_of(b * blk_rows, blk_rows), blk_rows), :]
        return pltpu.make_async_copy(zero_scr, dst, sem.at[1])

    @pl.when(pl.program_id(0) == 0)
    def _():
        zero_scr[...] = jnp.zeros_like(zero_scr)

        def per_expert(action):
            def body(e, carry):
                @pl.when(zok_ref[e] == 1)
                def _():
                    action(zero_copy(zblk_ref[e]))
                return carry
            lax.fori_loop(0, N_EXPERTS, body, 0)

        def per_tail(action):
            def body(b, carry):
                action(zero_copy(b))
                return carry
            lax.fori_loop(used_ref[0], n_blocks_max, body, 0)

        per_expert(lambda c: c.start())
        per_tail(lambda c: c.start())
        per_expert(lambda c: c.wait())
        per_tail(lambda c: c.wait())

    def issue(t, carry):
        src = xn_ref.at[pl.ds(pl.multiple_of(t * pitch, pitch), pitch), :]
        for k in range(TOP_K):
            p = pos_ref[0, t * TOP_K + k]
            dst = xr_hbm.at[pl.ds(pl.multiple_of(p * pitch, pitch), pitch), :]
            pltpu.make_async_copy(src, dst, sem.at[0]).start(priority=k % 2)
        return carry

    lax.fori_loop(0, tb, issue, 0, unroll=4)
    for k in range(TOP_K):
        pltpu.make_async_copy(xn_ref, xr_hbm.at[pl.ds(0, tb * pitch), :], sem.at[0]).wait()


def _dispatch(xn_rm, pos, zblk, zok, n_used, n_blocks_max, pitch, tb=512):
    n = xn_rm.shape[0] // pitch
    nb = n // tb
    grid_spec = pltpu.PrefetchScalarGridSpec(
        num_scalar_prefetch=3,
        grid=(nb,),
        in_specs=[
            pl.BlockSpec((None, 1, TOP_K * tb), lambda i, *_: (i, 0, 0), memory_space=pltpu.SMEM),
            pl.BlockSpec((tb * pitch, LANES), lambda i, *_: (i, 0)),
        ],
        out_specs=pl.BlockSpec(memory_space=pl.ANY),
        scratch_shapes=[pltpu.VMEM((MOE_TM * pitch, LANES), xn_rm.dtype), pltpu.SemaphoreType.DMA((2,))],
    )
    return pl.pallas_call(
        functools.partial(_dispatch_kernel, tb=tb, pitch=pitch, n_blocks_max=n_blocks_max),
        grid_spec=grid_spec,
        out_shape=jax.ShapeDtypeStruct((n_blocks_max * MOE_TM * pitch, LANES), xn_rm.dtype),
        compiler_params=_cparams(("arbitrary",)),
        name="dispatch",
    )(zblk, zok, n_used, pos.reshape(nb, 1, TOP_K * tb), xn_rm)


def _cast_rows(src, dst, rows_per=256):
    def body(i, carry):
        r = pl.ds(pl.multiple_of(i * rows_per, rows_per), rows_per)
        dst[r, :] = src[r, :].astype(dst.dtype)
        return carry

    lax.fori_loop(0, src.shape[0] // rows_per, body, 0)


def _stream_row_blocks(cnt, in_copy, out_copy, compute):
    for ahead in range(MOE_NBUF - 1):
        @pl.when(cnt > ahead)
        def _():
            in_copy(ahead, ahead).start()

    def body(b, carry):
        slot = b % MOE_NBUF

        @pl.when(b + MOE_NBUF - 1 < cnt)
        def _():
            in_copy(b + MOE_NBUF - 1, (b + MOE_NBUF - 1) % MOE_NBUF).start()

        in_copy(b, slot).wait()

        @pl.when(b >= MOE_NBUF)
        def _():
            out_copy(b - MOE_NBUF, slot).wait()

        compute(slot)
        out_copy(b, slot).start(priority=1)
        return carry

    lax.fori_loop(0, cnt, body, 0)

    for back in range(MOE_NBUF, 0, -1):
        @pl.when(cnt >= back)
        def _():
            out_copy(cnt - back, (cnt - back) % MOE_NBUF).wait()


def _stream_expert_rows(first, cnt, in_copy, out_copy, compute):
    n_units = cnt // MOE_UNIT
    _stream_row_blocks(n_units, in_copy(first, MOE_UNIT), out_copy(first, MOE_UNIT), compute(MOE_UNIT))
    done = n_units * MOE_UNIT
    _stream_row_blocks(cnt - done, in_copy(first + done, 1), out_copy(first + done, 1), compute(1))


def _zero_fill_blocks(first, last, zero_src, dst_copy):
    def start(b, carry):
        dst_copy(b).start()
        return carry

    def wait(b, carry):
        dst_copy(b).wait()
        return carry

    zero_src[...] = jnp.zeros_like(zero_src)
    lax.fori_loop(first, last, start, 0)
    lax.fori_loop(first, last, wait, 0)


def _moe_up_kernel(bs_ref, bc_ref, used_ref, x_hbm, w_hbm, bg_ref, bu_ref, act_hbm,
                   wf_scr, wb_scr, xbuf, lhs_scr, obuf, sem_w, sem_x, sem_o,
                   *, nt, tn, d_ff, pitch, n_blocks_max):
    g = pl.program_id(0)
    ng = pl.num_programs(0)
    e = g // nt
    col = pl.multiple_of((g % nt) * tn, tn)
    blk_rows = MOE_TM * pitch

    def w_copies(step):
        ee = step // nt
        cc = pl.multiple_of((step % nt) * tn, tn)
        return (pltpu.make_async_copy(w_hbm.at[ee, :, pl.ds(cc, tn)], wf_scr.at[0], sem_w.at[0]),
                pltpu.make_async_copy(w_hbm.at[ee, :, pl.ds(d_ff + cc, tn)], wf_scr.at[1], sem_w.at[1]))

    @pl.when(g == 0)
    def _():
        for c in w_copies(0):
            c.start(priority=1)

    for half, c in enumerate(w_copies(g)):
        c.wait()
        _cast_rows(wf_scr.at[half], wb_scr.at[half])

    @pl.when(g + 1 < ng)
    def _():
        for c in w_copies(g + 1):
            c.start(priority=1)

    start = bs_ref[e]

    def x_copy(first, nblk):
        def make(u, slot):
            r0 = pl.multiple_of((first + u * nblk) * blk_rows, blk_rows)
            return pltpu.make_async_copy(x_hbm.at[pl.ds(r0, nblk * blk_rows), :],
                                         xbuf.at[slot, pl.ds(0, nblk * blk_rows), :], sem_x.at[slot])
        return make

    def o_copy(first, nblk):
        def make(u, slot):
            r0 = pl.multiple_of((first + u * nblk) * MOE_TM, MOE_TM)
            return pltpu.make_async_copy(obuf.at[slot, pl.ds(0, nblk * MOE_TM), :],
                                         act_hbm.at[pl.ds(r0, nblk * MOE_TM), pl.ds(col, tn)], sem_o.at[slot])
        return make

    def compute(nblk):
        rows = nblk * MOE_TM

        def run(slot):
            half = pitch * LANES
            for c in range(pitch):
                w = xbuf[slot, pl.ds(c, rows, stride=pitch), :]
                lhs_scr[0:rows, c * LANES:(c + 1) * LANES] = _unpack_lo(w).astype(BF16)
                lhs_scr[0:rows, half + c * LANES:half + (c + 1) * LANES] = _unpack_hi(w).astype(BF16)
            x = lhs_scr[0:rows, :]
            gate = jnp.minimum(_dot(x, wb_scr[0]) + bg_ref[...], SWIGLU_LIMIT)
            up = jnp.clip(_dot(x, wb_scr[1]) + bu_ref[...], -SWIGLU_LIMIT, SWIGLU_LIMIT)
            obuf[slot, 0:rows, :] = (gate * jax.nn.sigmoid(SWIGLU_ALPHA * gate) * (up + 1.0)).astype(obuf.dtype)
        return run

    _stream_expert_rows(start, bc_ref[e], x_copy, o_copy, compute)

    @pl.when(e == N_EXPERTS - 1)
    def _():
        zero_blk = obuf.at[0, pl.ds(0, MOE_TM), :]

        def tail_copy(b):
            r0 = pl.multiple_of(b * MOE_TM, MOE_TM)
            return pltpu.make_async_copy(zero_blk, act_hbm.at[pl.ds(r0, MOE_TM), pl.ds(col, tn)], sem_o.at[0])
        _zero_fill_blocks(used_ref[0], n_blocks_max, zero_blk, tail_copy)


def _moe_up(x_rows_rm, w_up, b_up, blk_start, blk_count, n_used, n_blocks_max, tn=1024):
    n_exp, d, two_ff = w_up.shape
    d_ff = two_ff // 2
    pitch = d // 2 // LANES
    nt = d_ff // tn
    up_off = d_ff // tn
    grid_spec = pltpu.PrefetchScalarGridSpec(
        num_scalar_prefetch=3,
        grid=(n_exp * nt,),
        in_specs=[
            pl.BlockSpec(memory_space=pl.ANY),
            pl.BlockSpec(memory_space=pl.ANY),
            pl.BlockSpec((None, 1, tn), lambda g, *_: (g // nt, 0, g % nt)),
            pl.BlockSpec((None, 1, tn), lambda g, *_: (g // nt, 0, g % nt + up_off)),
        ],
        out_specs=pl.BlockSpec(memory_space=pl.ANY),
        scratch_shapes=[
            pltpu.VMEM((2, d, tn), F32),
            pltpu.VMEM((2, d, tn), BF16),
            pltpu.VMEM((MOE_NBUF, MOE_UNIT * MOE_TM * pitch, LANES), jnp.uint32),
            pltpu.VMEM((MOE_UNIT * MOE_TM, d), BF16),
            pltpu.VMEM((MOE_NBUF, MOE_UNIT * MOE_TM, tn), BF16),
            pltpu.SemaphoreType.DMA((2,)),
            pltpu.SemaphoreType.DMA((MOE_NBUF,)),
            pltpu.SemaphoreType.DMA((MOE_NBUF,)),
        ],
    )
    return pl.pallas_call(
        functools.partial(_moe_up_kernel, nt=nt, tn=tn, d_ff=d_ff, pitch=pitch, n_blocks_max=n_blocks_max),
        grid_spec=grid_spec,
        out_shape=jax.ShapeDtypeStruct((n_blocks_max * MOE_TM, d_ff), BF16),
        compiler_params=_cparams(("arbitrary",)),
        name="moe_up",
    )(blk_start, blk_count, n_used, x_rows_rm, w_up, b_up, b_up)


def _moe_down_kernel(bs_ref, bc_ref, used_ref, a_hbm, w_hbm, b_ref, y_hbm,
                     wf_scr, wb_scr, abuf, obuf, sem_w, sem_a, sem_o, *, pitch, n_blocks_max):
    e = pl.program_id(0)
    blk_rows = MOE_TM * pitch

    def w_copy(ee):
        return pltpu.make_async_copy(w_hbm.at[ee], wf_scr, sem_w.at[0])

    @pl.when(e == 0)
    def _():
        w_copy(0).start(priority=1)

    w_copy(e).wait()
    _cast_rows(wf_scr, wb_scr)

    @pl.when(e + 1 < pl.num_programs(0))
    def _():
        w_copy(e + 1).start(priority=1)

    start = bs_ref[e]

    def a_copy(first, nblk):
        def make(u, slot):
            r0 = pl.multiple_of((first + u * nblk) * MOE_TM, MOE_TM)
            return pltpu.make_async_copy(a_hbm.at[pl.ds(r0, nblk * MOE_TM), :],
                                         abuf.at[slot, pl.ds(0, nblk * MOE_TM), :], sem_a.at[slot])
        return make

    def o_copy(first, nblk):
        def make(u, slot):
            r0 = pl.multiple_of((first + u * nblk) * blk_rows, blk_rows)
            return pltpu.make_async_copy(obuf.at[slot, pl.ds(0, nblk * blk_rows), :],
                                         y_hbm.at[pl.ds(r0, nblk * blk_rows), :], sem_o.at[slot])
        return make

    def compute(nblk):
        rows = nblk * MOE_TM

        def run(slot):
            y = _dot(abuf[slot, 0:rows, :], wb_scr[...]) + b_ref[...]
            _store_rowmajor(obuf.at[slot], _pack_bf16_pairs(y))
        return run

    _stream_expert_rows(start, bc_ref[e], a_copy, o_copy, compute)

    @pl.when(e == N_EXPERTS - 1)
    def _():
        zero_blk = obuf.at[0, pl.ds(0, blk_rows), :]

        def tail_copy(b):
            r0 = pl.multiple_of(b * blk_rows, blk_rows)
            return pltpu.make_async_copy(zero_blk, y_hbm.at[pl.ds(r0, blk_rows), :], sem_o.at[0])
        _zero_fill_blocks(used_ref[0], n_blocks_max, zero_blk, tail_copy)


def _moe_down(act, w_down, b_down, blk_start, blk_count, n_used, n_blocks_max):
    n_rows, d_ff = act.shape
    n_exp, _, d = w_down.shape
    pitch = d // 2 // LANES
    grid_spec = pltpu.PrefetchScalarGridSpec(
        num_scalar_prefetch=3,
        grid=(n_exp,),
        in_specs=[
            pl.BlockSpec(memory_space=pl.ANY),
            pl.BlockSpec(memory_space=pl.ANY),
            pl.BlockSpec((None, 1, d), lambda e, *_: (e, 0, 0)),
        ],
        out_specs=pl.BlockSpec(memory_space=pl.ANY),
        scratch_shapes=[
            pltpu.VMEM((d_ff, d), F32),
            pltpu.VMEM((d_ff, d), BF16),
            pltpu.VMEM((MOE_NBUF, MOE_UNIT * MOE_TM, d_ff), BF16),
            pltpu.VMEM((MOE_NBUF, MOE_UNIT * MOE_TM * pitch, LANES), jnp.uint32),
            pltpu.SemaphoreType.DMA((1,)),
            pltpu.SemaphoreType.DMA((MOE_NBUF,)),
            pltpu.SemaphoreType.DMA((MOE_NBUF,)),
        ],
    )
    return pl.pallas_call(
        functools.partial(_moe_down_kernel, pitch=pitch, n_blocks_max=n_blocks_max),
        grid_spec=grid_spec,
        out_shape=jax.ShapeDtypeStruct((n_rows * pitch, LANES), jnp.uint32),
        compiler_params=_cparams(("arbitrary",)),
        name="moe_down",
    )(blk_start, blk_count, n_used, act, w_down, b_down)


def _combine_kernel(pos0_ref, posn_ref, y_hbm, x2_ref, gate_ref, g_ref, o_ref, buf_even, buf_odd, sem,
                    *, tb, pitch):
    i = pl.program_id(0)
    last = pl.num_programs(0) - 1
    n_rows = TOP_K * tb

    def row_copy(p_ref, r, buf, sem_slot):
        src = y_hbm.at[pl.ds(pl.multiple_of(p_ref[0, r] * pitch, pitch), pitch), :]
        return pltpu.make_async_copy(src, buf.at[pl.ds(r * pitch, pitch), :], sem.at[sem_slot])

    def wait_block(buf, sem_slot):
        pltpu.make_async_copy(y_hbm.at[pl.ds(0, n_rows * pitch), :], buf, sem.at[sem_slot]).wait()

    @pl.when(i == 0)
    def _():
        def issue(j, carry):
            for half in range(2):
                row_copy(pos0_ref, 2 * j + half, buf_even, 0).start(priority=half)
            return carry
        lax.fori_loop(0, n_rows // 2, issue, 0, unroll=4)

    def step(buf_cur, sem_cur, buf_nxt, sem_nxt):
        wait_block(buf_cur, sem_cur)
        for r in range(n_rows):
            row_copy(posn_ref, r, buf_nxt, sem_nxt).start(priority=r % 2)
        gates = gate_ref[...]
        half = pitch * LANES
        ssq = jnp.zeros((tb, 1), F32)
        for c in range(pitch):
            lo = slice(c * LANES, (c + 1) * LANES)
            hi = slice(half + c * LANES, half + (c + 1) * LANES)
            z_lo = x2_ref[:, lo]
            z_hi = x2_ref[:, hi]
            for k in range(TOP_K):
                w = buf_cur[pl.ds(k * tb * pitch + c, tb, stride=pitch), :]
                z_lo = z_lo + gates[:, k:k + 1] * _unpack_lo(w)
                z_hi = z_hi + gates[:, k:k + 1] * _unpack_hi(w)
            o_ref[:, lo] = z_lo
            o_ref[:, hi] = z_hi
            ssq = ssq + (jnp.sum(z_lo * z_lo, axis=-1, keepdims=True)
                         + jnp.sum(z_hi * z_hi, axis=-1, keepdims=True))
        o_ref[...] = o_ref[...] * lax.rsqrt(ssq * (1.0 / (2 * half)) + EPS) * g_ref[...]

        @pl.when(i == last)
        def _():
            wait_block(buf_nxt, sem_nxt)

    @pl.when(i % 2 == 0)
    def _():
        step(buf_even, 0, buf_odd, 1)

    @pl.when(i % 2 == 1)
    def _():
        step(buf_odd, 1, buf_even, 0)


def _combine(y_rows_rm, pos, gates, x2, g_final, tb=256):
    n, d = x2.shape
    nb = n // tb
    pitch = d // 2 // LANES
    pos_blk = pos.reshape(nb, tb, TOP_K).transpose(0, 2, 1).reshape(nb, 1, TOP_K * tb)
    pos_spec = lambda imap: pl.BlockSpec((None, 1, TOP_K * tb), imap, memory_space=pltpu.SMEM)
    return pl.pallas_call(
        functools.partial(_combine_kernel, tb=tb, pitch=pitch),
        grid=(nb,),
        in_specs=[
            pos_spec(lambda i: (0, 0, 0)),
            pos_spec(lambda i: (jnp.minimum(i + 1, nb - 1), 0, 0)),
            pl.BlockSpec(memory_space=pl.ANY),
            pl.BlockSpec((tb, d), lambda i: (i, 0)),
            pl.BlockSpec((tb, LANES), lambda i: (i, 0)),
            pl.BlockSpec((1, d), lambda i: (0, 0)),
        ],
        out_specs=pl.BlockSpec((tb, d), lambda i: (i, 0)),
        out_shape=jax.ShapeDtypeStruct((n, d), F32),
        scratch_shapes=[pltpu.VMEM((TOP_K * tb * pitch, LANES), jnp.uint32),
                        pltpu.VMEM((TOP_K * tb * pitch, LANES), jnp.uint32),
                        pltpu.SemaphoreType.DMA((2,))],
        compiler_params=_cparams(("arbitrary",)),
        name="combine",
    )(pos_blk, pos_blk, y_rows_rm, x2, gates, g_final)


def _routing_tables(idx, rank, cnt):
    counts = cnt[0, :N_EXPERTS]
    blk_count = (counts + MOE_TM - 1) // MOE_TM
    blk_end = jnp.cumsum(blk_count)
    blk_start = blk_end - blk_count
    hot = idx[:, :TOP_K, None] == jnp.arange(N_EXPERTS, dtype=jnp.int32)
    pos = jnp.sum(jnp.where(hot, blk_start * MOE_TM, 0), axis=-1) + rank[:, :TOP_K]
    return pos.astype(jnp.int32), blk_start.astype(jnp.int32), blk_count.astype(jnp.int32)


def kernel(x, mem, g_attn_norm, g_mem_norm, w_in, w_gla_a2, b_gla_a, g_gla_out, b_fox_f, g_fox_out,
           w_mem_kv, g_mem_out, w_out, g_ffn_norm, w_router, b_router, w_moe_up, b_moe_up,
           w_moe_down, b_moe_down, g_final):
    batch, seq, d = x.shape
    n_mem = mem.shape[1]
    n = batch * seq
    depth = w_in.shape[0]
    assert depth == 1, "the combine kernel applies the final norm, so exactly one layer is supported"
    kw = GLA_HEADS * GLA_DK
    vw = GLA_HEADS * GLA_DV
    fw = FOX_HEADS * FOX_DH
    mw = MEM_HEADS * MEM_DH
    o_q, o_k, o_v, o_g = 0, kw, 2 * kw, 2 * kw + vw
    o_a = o_g + vw
    o_fq = o_a + GLA_LOWRANK
    o_fk, o_fv = o_fq + fw, o_fq + 2 * fw
    o_ff = o_fq + 3 * fw
    o_mq = o_ff + FOX_HEADS
    f_lane = GLA_LOWRANK

    xf = x.reshape(n, d)
    for l in range(depth):
        fox_q_scale = FOX_DH ** -0.5 * LOG2E
        col_scale = jnp.ones((w_in.shape[2],), F32).at[o_fq:o_fk].set(fox_q_scale)
        wi = lax.optimization_barrier((w_in[l] * col_scale).astype(BF16))
        wa = jnp.concatenate([wi[:, o_q:o_a], wi[:, o_fq:o_ff], wi[:, o_mq:o_mq + mw]], axis=1)
        wb = jnp.zeros((d, LANES), BF16).at[:, :GLA_LOWRANK].set(wi[:, o_a:o_fq])
        wb = wb.at[:, f_lane:f_lane + FOX_HEADS].set(wi[:, o_ff:o_mq])
        proj, small = _in_proj(xf, g_attn_norm[l].reshape(1, d), wa, wb)

        wa2p = jnp.zeros((LANES, kw), F32).at[:GLA_LOWRANK].set(w_gla_a2[l]).astype(BF16)
        bfv = jnp.zeros((1, LANES), F32).at[0, f_lane:f_lane + FOX_HEADS].set(b_fox_f[l])
        gla, f_cum = _gla(proj, small, wa2p, b_gla_a[l].reshape(1, kw), bfv,
                          g_gla_out[l].reshape(1, vw), batch, seq)

        f_t = f_cum.reshape(batch, seq, LANES)[:, :, f_lane:f_lane + FOX_HEADS].transpose(0, 2, 1)
        f_t = jnp.concatenate([f_t, jnp.zeros_like(f_t)], axis=1)
        fox = _fox(proj, f_t, g_fox_out[l].reshape(1, fw), batch, seq)

        kv = _mem_kv(mem.reshape(batch * n_mem, d), g_mem_norm[l].reshape(1, d), w_mem_kv[l].astype(BF16))
        memo = _mem_attn(proj, kv, g_mem_out[l].reshape(1, mw), batch, seq, n_mem)

        wr = jnp.zeros((d, LANES), F32).at[:, :N_EXPERTS].set(w_router[l])
        wr_hi = wr.astype(BF16)
        wr = jnp.concatenate([wr_hi, (wr - wr_hi.astype(F32)).astype(BF16)], axis=1)
        br =jnp.zeros((1, LANES), F32).at[0, :N_EXPERTS].set(b_router[l])
        x2, xn, logits = _out_proj(xf, gla, fox, memo, w_out[l].astype(BF16),
                                   g_ffn_norm[l].reshape(1, d), wr, br)

        idx, gates, rank, cnt = _route(logits)
        pos, blk_start, blk_count = _routing_tables(idx, rank, cnt)
        n_blocks_max = -(-(n * TOP_K + N_EXPERTS * (MOE_TM - 1)) // MOE_TM)
        blk_end = blk_start + blk_count
        n_used = blk_end[-1:]
        x_rows = _dispatch(xn, pos, blk_end - 1, (blk_count > 0).astype(jnp.int32), n_used,
                           n_blocks_max, d // 2 // LANES)
        d_ff = w_moe_up.shape[3] // 2
        act = _moe_up(x_rows, w_moe_up[l], b_moe_up[l].reshape(N_EXPERTS, 1, 2 * d_ff),
                      blk_start, blk_count, n_used, n_blocks_max)
        y_rows = _moe_down(act, w_moe_down[l], b_moe_down[l].reshape(N_EXPERTS, 1, d),
                           blk_start, blk_count, n_used, n_blocks_max)
        xf = _combine(y_rows, pos, gates, x2, g_final.reshape(1, d))
    return xf.reshape(batch, seq, d)
```

```python
import functools

import jax
import jax.numpy as jnp
from jax import lax
from jax.experimental import pallas as pl
from jax.experimental.pallas import tpu as pltpu

EPS = 1e-5
CHUNK = 64
GLA_HEADS = 4
GLA_DK = 128
GLA_DV = 256
GLA_LOWRANK = 16
GLA_TAU = 16.0
FOX_HEADS = 4
FOX_DH = 128
MEM_HEADS = 4
MEM_DH = 128
N_EXPERTS = 32
TOP_K = 4
SWIGLU_LIMIT = 7.0
SWIGLU_ALPHA = 1.702
LANES = 128
MXU_COLS = 256
MOE_TM = 256
MOE_NBUF = 2
MOE_UNIT = 2
VMEM_LIMIT = 56 * 1024 * 1024

F32 = jnp.float32
BF16 = jnp.bfloat16


def _cparams(sem, vmem=VMEM_LIMIT):
    return pltpu.CompilerParams(dimension_semantics=sem, vmem_limit_bytes=vmem)


def _log_sigmoid(x):
    return jnp.minimum(x, 0.0) - jnp.log1p(jnp.exp(-jnp.abs(x)))


def _rms(x, g):
    return x * lax.rsqrt(jnp.mean(x * x, axis=-1, keepdims=True) + EPS) * g


def _dot(a, b, **kw):
    return jnp.dot(a, b, preferred_element_type=F32, **kw)


def _dot_nt(a, b):
    return lax.dot_general(a, b, (((1,), (1,)), ((), ())), preferred_element_type=F32)


def _dot_tn(a, b):
    return lax.dot_general(a, b, (((0,), (0,)), ((), ())), preferred_element_type=F32)


def _store_rowmajor(ref, x, base=0):
    rows, w = x.shape
    pitch = w // LANES
    for c in range(pitch):
        ref[pl.ds(base + c, rows, stride=pitch), :] = x[:, c * LANES:(c + 1) * LANES]


def _pack_bf16_pairs(x):
    half = x.shape[1] // 2
    bits = lambda v: lax.bitcast_convert_type(v.astype(BF16).astype(F32), jnp.uint32)
    return bits(x[:, half:]) | (bits(x[:, :half]) >> 16)


def _unpack_lo(w):
    return lax.bitcast_convert_type(w << 16, F32)


def _unpack_hi(w):
    return lax.bitcast_convert_type(w & jnp.uint32(0xFFFF0000), F32)


def _in_proj_kernel(x_ref, g_ref, wa_ref, wb_ref, proj_ref, small_ref, h_scr):
    @pl.when(pl.program_id(1) == 0)
    def _():
        hb = _rms(x_ref[...], g_ref[...]).astype(BF16)
        h_scr[...] = hb
        small_ref[...] = _dot_nt(hb, wb_ref[...])

    proj_ref[...] = _dot_nt(h_scr[...], wa_ref[...]).astype(proj_ref.dtype)


def _in_proj(x2d, g, wa_t, wb_t, tm=1024, tn=1280):
    n, d = x2d.shape
    na = wa_t.shape[0]
    return pl.pallas_call(
        _in_proj_kernel,
        grid=(n // tm, na // tn),
        in_specs=[
            pl.BlockSpec((tm, d), lambda i, j: (i, 0)),
            pl.BlockSpec((1, d), lambda i, j: (0, 0)),
            pl.BlockSpec((tn, d), lambda i, j: (j, 0)),
            pl.BlockSpec((LANES, d), lambda i, j: (0, 0)),
        ],
        out_specs=[
            pl.BlockSpec((tm, tn), lambda i, j: (i, j)),
            pl.BlockSpec((tm, LANES), lambda i, j: (i, 0)),
        ],
        out_shape=[
            jax.ShapeDtypeStruct((n, na), BF16),
            jax.ShapeDtypeStruct((n, LANES), F32),
        ],
        scratch_shapes=[pltpu.VMEM((tm, d), BF16)],
        compiler_params=_cparams(("parallel", "arbitrary")),
        name="in_proj",
    )(x2d, g, wa_t, wb_t)


def _gla_kernel(q_ref, k_ref, v_ref, gate_ref, small_ref, wa2_ref, ba_ref, bf_ref, gout_ref,
                o_ref, f_ref, state_scr, fcar_scr, la_scr, lf_scr, *, n_chunks):
    @pl.when(pl.program_id(1) == 0)
    def _():
        state_scr[...] = jnp.zeros_like(state_scr)
        fcar_scr[...] = jnp.zeros_like(fcar_scr)

    small = small_ref[...]
    la_scr[...] = _log_sigmoid(_dot(small.astype(BF16), wa2_ref[...]) + ba_ref[...]) * (1.0 / GLA_TAU)
    lf_scr[...] = _log_sigmoid(small + bf_ref[...])
    row = lax.broadcasted_iota(jnp.int32, (CHUNK, CHUNK), 0)
    col = lax.broadcasted_iota(jnp.int32, (CHUNK, CHUNK), 1)
    tri = (col <= row).astype(F32)
    scale = GLA_DK ** -0.5

    def chunk_body(c, carry):
        r = pl.ds(pl.multiple_of(c * CHUNK, CHUNK), CHUNK)
        b = _dot(tri, la_scr[r, :], precision=lax.Precision.HIGHEST)
        b_end = b[CHUNK - 1:CHUNK, :]
        k_dec = k_ref[r, :].astype(F32) * jnp.exp(b_end - b)
        decay = jnp.exp(b_end)
        f_cum = _dot(tri, lf_scr[r, :], precision=lax.Precision.HIGHEST) + fcar_scr[...]
        f_ref[r, :] = f_cum
        fcar_scr[...] = f_cum[CHUNK - 1:CHUNK, :]
        heads = range(GLA_HEADS)
        ks = [slice(h * GLA_DK, (h + 1) * GLA_DK) for h in heads]
        vs = [slice(h * GLA_DV, (h + 1) * GLA_DV) for h in heads]
        old = [state_scr[h] for h in heads]
        new = [old[h] * decay[:, ks[h]] + _dot_tn(v_ref[r, vs[h]], k_dec[:, ks[h]].astype(BF16))
               for h in heads]
        for h in heads:
            state_scr[h] = new[h]
        for h in heads:
            o = _dot_nt(q_ref[r, ks[h]], new[h].astype(BF16)) * scale
            gt = gate_ref[r, vs[h]].astype(F32)
            o_ref[r, vs[h]] = (_rms(o, gout_ref[:, vs[h]]) * (gt * jax.nn.sigmoid(gt))).astype(o_ref.dtype)
        return carry

    lax.fori_loop(0, n_chunks, chunk_body, 0, unroll=True)


def _gla(proj, small, wa2p, ba, bfv, gout, batch, seq, ts=512):
    n = proj.shape[0]
    nsb = seq // ts
    kw = GLA_HEADS * GLA_DK
    vw = GLA_HEADS * GLA_DV
    row = lambda b, s: b * nsb + s
    return pl.pallas_call(
        functools.partial(_gla_kernel, n_chunks=ts // CHUNK),
        grid=(batch, nsb),
        in_specs=[
            pl.BlockSpec((ts, kw), lambda b, s: (row(b, s), 0)),
            pl.BlockSpec((ts, kw), lambda b, s: (row(b, s), 1)),
            pl.BlockSpec((ts, vw), lambda b, s: (row(b, s), 1)),
            pl.BlockSpec((ts, vw), lambda b, s: (row(b, s), 2)),
            pl.BlockSpec((ts, LANES), lambda b, s: (row(b, s), 0)),
            pl.BlockSpec((LANES, kw), lambda b, s: (0, 0)),
            pl.BlockSpec((1, kw), lambda b, s: (0, 0)),
            pl.BlockSpec((1, LANES), lambda b, s: (0, 0)),
            pl.BlockSpec((1, vw), lambda b, s: (0, 0)),
        ],
        out_specs=[
            pl.BlockSpec((ts, vw), lambda b, s: (row(b, s), 0)),
            pl.BlockSpec((ts, LANES), lambda b, s: (row(b, s), 0)),
        ],
        out_shape=[
            jax.ShapeDtypeStruct((n, vw), BF16),
            jax.ShapeDtypeStruct((n, LANES), F32),
        ],
        scratch_shapes=[
            pltpu.VMEM((GLA_HEADS, GLA_DV, GLA_DK), F32),
            pltpu.VMEM((1, LANES), F32),
            pltpu.VMEM((ts, kw), F32),
            pltpu.VMEM((ts, LANES), F32),
        ],
        compiler_params=_cparams(("parallel", "arbitrary")),
        name="gla",
    )(proj, proj, proj, proj, small, wa2p, ba, bfv, gout)


LOG2E = 1.4426950408889634
def _fox_kernel(q_ref, k_ref, v_ref, fk_ref, g_ref, o_ref, m_scr, l_scr, acc_scr, s_even, s_odd,
                *, tq, tk):
    i = pl.program_id(1)
    t = pl.program_id(2)

    @pl.when(t == 0)
    def _():
        m_scr[...] = jnp.full_like(m_scr, -jnp.inf)
        l_scr[...] = jnp.zeros_like(l_scr)
        acc_scr[...] = jnp.zeros_like(acc_scr)

    heads = range(FOX_HEADS)
    hs = [slice(h * FOX_DH, (h + 1) * FOX_DH) for h in heads]

    def score(s_out):
        for h in heads:
            s_out[h] = _dot_nt(q_ref[:, hs[h]], k_ref[:, hs[h]])

    def update(s_in, on_diagonal):
        m_prev = [m_scr[h] for h in heads]
        l_prev = [l_scr[h] for h in heads]
        acc_prev = [acc_scr[h] for h in heads]
        s = [s_in[h] - fk_ref[h:h + 1, :] * LOG2E for h in heads]
        if on_diagonal:
            row = lax.broadcasted_iota(jnp.int32, (tq, tk), 0)
            col = lax.broadcasted_iota(jnp.int32, (tq, tk), 1)
            s = [jnp.where(col <= row, sh, -jnp.inf) for sh in s]
        m_new = [jnp.maximum(m_prev[h], jnp.max(s[h], axis=-1, keepdims=True)) for h in heads]
        alpha = [jnp.exp2(m_prev[h] - m_new[h]) for h in heads]
        p = [jnp.exp2(s[h] - m_new[h]) for h in heads]
        l_new = [alpha[h] * l_prev[h] + jnp.sum(p[h], axis=-1, keepdims=True) for h in heads]
        acc_new = [alpha[h] * acc_prev[h] + _dot(p[h].astype(BF16), v_ref[:, hs[h]]) for h in heads]
        return m_new, l_new, acc_new

    def step(s_write, s_read):
        @pl.when(t == 0)
        def _():
            score(s_write)

        @pl.when(jnp.logical_and(t >= 1, t <= i))
        def _():
            m_new, l_new, acc_new = update(s_read, False)
            score(s_write)
            for h in heads:
                m_scr[h] = m_new[h]
                l_scr[h] = l_new[h]
                acc_scr[h] = acc_new[h]

        @pl.when(t == i + 1)
        def _():
            _, l_new, acc_new = update(s_read, True)
            for h in heads:
                o_ref[:, hs[h]] = _rms(acc_new[h] / l_new[h], g_ref[:, hs[h]]).astype(o_ref.dtype)

    @pl.when(t % 2 == 0)
    def _():
        step(s_even, s_odd)

    @pl.when(t % 2 == 1)
    def _():
        step(s_odd, s_even)


def _fox(proj, f_t, g_fox, batch, seq, tq=512, tk=512):
    assert tq == tk, "the diagonal-block mask assumes square blocks"
    n = proj.shape[0]
    w = FOX_HEADS * FOX_DH
    nq, nk = seq // tq, seq // tk
    qcol, kcol, vcol = 3072 // w, 3584 // w, 4096 // w
    return pl.pallas_call(
        functools.partial(_fox_kernel, tq=tq, tk=tk),
        grid=(batch, nq, nk + 1),
        in_specs=[
            pl.BlockSpec((tq, w), lambda b, i, t: (b * nq + i, qcol)),
            pl.BlockSpec((tk, w), lambda b, i, t: (b * nk + jnp.minimum(t, i), kcol)),
            pl.BlockSpec((tk, w), lambda b, i, t: (b * nk + jnp.clip(t - 1, 0, i), vcol)),
            pl.BlockSpec((None, 8, tk), lambda b, i, t: (b, 0, jnp.clip(t - 1, 0, i))),
            pl.BlockSpec((1, w), lambda b, i, t: (0, 0)),
        ],
        out_specs=pl.BlockSpec((tq, w), lambda b, i, t: (b * nq + i, 0)),
        out_shape=jax.ShapeDtypeStruct((n, w), BF16),
        scratch_shapes=[
            pltpu.VMEM((FOX_HEADS, tq, 1), F32),
            pltpu.VMEM((FOX_HEADS, tq, 1), F32),
            pltpu.VMEM((FOX_HEADS, tq, FOX_DH), F32),
            pltpu.VMEM((FOX_HEADS, tq, tk), F32),
            pltpu.VMEM((FOX_HEADS, tq, tk), F32),
        ],
        compiler_params=_cparams(("parallel", "parallel", "arbitrary")),
        name="fox",
    )(proj, proj, proj, f_t, g_fox)


def _mem_kv_kernel(m_ref, g_ref, w_ref, kv_ref):
    kv_ref[...] = _dot(_rms(m_ref[...], g_ref[...]).astype(BF16), w_ref[...]).astype(kv_ref.dtype)


def _mem_kv(mem2d, g, w, tm=256):
    n, d = mem2d.shape
    nw = w.shape[1]
    return pl.pallas_call(
        _mem_kv_kernel,
        grid=(n // tm,),
        in_specs=[
            pl.BlockSpec((tm, d), lambda i: (i, 0)),
            pl.BlockSpec((1, d), lambda i: (0, 0)),
            pl.BlockSpec((d, nw), lambda i: (0, 0)),
        ],
        out_specs=pl.BlockSpec((tm, nw), lambda i: (i, 0)),
        out_shape=jax.ShapeDtypeStruct((n, nw), BF16),
        compiler_params=_cparams(("parallel",)),
        name="mem_kv",
    )(mem2d, g, w)


def _mem_attn_kernel(q_ref, k_ref, v_ref, g_ref, o_ref):
    scale = MEM_DH ** -0.5
    for h in range(MEM_HEADS):
        hs = slice(h * MEM_DH, (h + 1) * MEM_DH)
        s = _dot_nt(q_ref[:, hs], k_ref[:, hs]) * scale
        p = jnp.exp(s - jnp.max(s, axis=-1, keepdims=True))
        l = jnp.sum(p, axis=-1, keepdims=True)
        o = _dot((p / l).astype(BF16), v_ref[:, hs])
        o_ref[:, hs] = _rms(o, g_ref[:, hs]).astype(o_ref.dtype)


def _mem_attn(proj, kv, g_mem_out, batch, seq, n_mem, tq=1024):
    n = proj.shape[0]
    w = MEM_HEADS * MEM_DH
    nq = seq // tq
    qcol = 4608 // w
    return pl.pallas_call(
        _mem_attn_kernel,
        grid=(n // tq,),
        in_specs=[
            pl.BlockSpec((tq, w), lambda i: (i, qcol)),
            pl.BlockSpec((n_mem, w), lambda i: (i // nq, 0)),
            pl.BlockSpec((n_mem, w), lambda i: (i // nq, 1)),
            pl.BlockSpec((1, w), lambda i: (0, 0)),
        ],
        out_specs=pl.BlockSpec((tq, w), lambda i: (i, 0)),
        out_shape=jax.ShapeDtypeStruct((n, w), BF16),
        compiler_params=_cparams(("parallel",)),
        name="mem_attn",
    )(proj, kv, kv, g_mem_out)


def _out_proj_kernel(x_ref, gla_ref, fox_ref, mem_ref, w1_ref, w2_ref, w3_ref, g_ref, wr_ref, br_ref,
                     x2_ref, xn_ref, logit_ref):
    x2 = (x_ref[...] + _dot(gla_ref[...], w1_ref[...]) + _dot(fox_ref[...], w2_ref[...])
          + _dot(mem_ref[...], w3_ref[...]))
    x2_ref[...] = x2
    xn = _rms(x2, g_ref[...])
    _store_rowmajor(xn_ref, _pack_bf16_pairs(xn))
    xh = xn.astype(BF16)
    xl = (xn - xh.astype(F32)).astype(BF16)
    hi = _dot(xh, wr_ref[...])
    logit_ref[...] = hi[:, :LANES] + hi[:, LANES:] + _dot(xl, wr_ref[:, :LANES]) + br_ref[...]


def _out_proj(x2d, gla, fox, memo, w_out, g_ffn, wr, br, tm=512):
    n, d = x2d.shape
    w1, w2 = gla.shape[1], fox.shape[1]
    const = lambda i: (0, 0)
    return pl.pallas_call(
        _out_proj_kernel,
        grid=(n // tm,),
        in_specs=[
            pl.BlockSpec((tm, d), lambda i: (i, 0)),
            pl.BlockSpec((tm, w1), lambda i: (i, 0)),
            pl.BlockSpec((tm, w2), lambda i: (i, 0)),
            pl.BlockSpec((tm, w2), lambda i: (i, 0)),
            pl.BlockSpec((w1, d), lambda i: (0, 0)),
            pl.BlockSpec((w2, d), lambda i: (w1 // w2, 0)),
            pl.BlockSpec((w2, d), lambda i: (w1 // w2 + 1, 0)),
            pl.BlockSpec((1, d), const),
            pl.BlockSpec((d, 2 * LANES), const),
            pl.BlockSpec((1, LANES), const),
        ],
        out_specs=[
            pl.BlockSpec((tm, d), lambda i: (i, 0)),
            pl.BlockSpec((tm * (d // 2 // LANES), LANES), lambda i: (i, 0)),
            pl.BlockSpec((tm, LANES), lambda i: (i, 0)),
        ],
        out_shape=[
            jax.ShapeDtypeStruct((n, d), F32),
            jax.ShapeDtypeStruct((n * (d // 2 // LANES), LANES), jnp.uint32),
            jax.ShapeDtypeStruct((n, LANES), F32),
        ],
        compiler_params=_cparams(("parallel",)),
        name="out_proj",
    )(x2d, gla, fox, memo, w_out, w_out, w_out, g_ffn, wr, br)


def _route_kernel(l_ref, idx_ref, gate_ref, rank_ref, cnt_ref, carry_scr, *, tb):
    @pl.when(pl.program_id(0) == 0)
    def _():
        carry_scr[...] = jnp.zeros_like(carry_scr)

    lane = lax.broadcasted_iota(jnp.int32, (tb, LANES), 1)
    logit = jnp.where(lane < N_EXPERTS, l_ref[...], -jnp.inf)
    vals, hots = [], []
    idx_out = jnp.zeros((tb, LANES), jnp.int32)
    for k in range(TOP_K):
        m = jnp.max(logit, axis=-1, keepdims=True)
        ik = jnp.min(jnp.where(logit == m, lane, LANES), axis=-1, keepdims=True)
        hot = lane == ik
        logit = jnp.where(hot, -jnp.inf, logit)
        vals.append(m)
        hots.append(hot)
        idx_out = jnp.where(lane == k, ik, idx_out)
    idx_ref[...] = idx_out

    e = [jnp.exp(v - vals[0]) for v in vals]
    den = e[0] + e[1] + e[2] + e[3]
    gate_out = jnp.zeros((tb, LANES), F32)
    for k in range(TOP_K):
        gate_out = jnp.where(lane == k, e[k] / den, gate_out)
    gate_ref[...] = gate_out

    member = jnp.zeros((tb, LANES), F32)
    for hot in hots:
        member = member + hot.astype(F32)
    row = lax.broadcasted_iota(jnp.int32, (tb, tb), 0)
    col = lax.broadcasted_iota(jnp.int32, (tb, tb), 1)
    before = (col < row).astype(BF16)
    rank = _dot(before, member.astype(BF16)) + carry_scr[...]
    rank_out = jnp.zeros((tb, LANES), F32)
    for k in range(TOP_K):
        rk = jnp.sum(jnp.where(hots[k], rank, 0.0), axis=-1, keepdims=True)
        rank_out = jnp.where(lane == k, rk, rank_out)
    rank_ref[...] = rank_out.astype(jnp.int32)
    total = carry_scr[...] + jnp.sum(member, axis=0, keepdims=True)
    carry_scr[...] = total
    cnt_ref[...] = total.astype(jnp.int32)


def _route(logits, tb=512):
    n = logits.shape[0]
    blk = pl.BlockSpec((tb, LANES), lambda i: (i, 0))
    return pl.pallas_call(
        functools.partial(_route_kernel, tb=tb),
        grid=(n // tb,),
        in_specs=[blk],
        out_specs=[blk, blk, blk, pl.BlockSpec((1, LANES), lambda i: (0, 0))],
        out_shape=[
            jax.ShapeDtypeStruct((n, LANES), jnp.int32),
            jax.ShapeDtypeStruct((n, LANES), F32),
            jax.ShapeDtypeStruct((n, LANES), jnp.int32),
            jax.ShapeDtypeStruct((1, LANES), jnp.int32),
        ],
        scratch_shapes=[pltpu.VMEM((1, LANES), F32)],
        compiler_params=_cparams(("arbitrary",)),
        name="route",
    )(logits)


def _dispatch_kernel(zblk_ref, zok_ref, used_ref, pos_ref, xn_ref, xr_hbm, zero_scr, sem,
                     *, tb, pitch, n_blocks_max):
    blk_rows = MOE_TM * pitch

    def zero_copy(b):
        dst = xr_hbm.at[pl.ds(pl.multiple_of(b * blk_rows, blk_rows), blk_rows), :]
        return pltpu.make_async_copy(zero_scr, dst, sem.at[1])

    @pl.when(pl.program_id(0) == 0)
    def _():
        zero_scr[...] = jnp.zeros_like(zero_scr)

        def per_expert(action):
            def body(e, carry):
                @pl.when(zok_ref[e] == 1)
                def _():
                    action(zero_copy(zblk_ref[e]))
                return carry
            lax.fori_loop(0, N_EXPERTS, body, 0)

        def per_tail(action):
            def body(b, carry):
                action(zero_copy(b))
                return carry
            lax.fori_loop(used_ref[0], n_blocks_max, body, 0)

        per_expert(lambda c: c.start())
        per_tail(lambda c: c.start())
        per_expert(lambda c: c.wait())
        per_tail(lambda c: c.wait())

    def issue(t, carry):
        src = xn_ref.at[pl.ds(pl.multiple_of(t * pitch, pitch), pitch), :]
        for k in range(TOP_K):
            p = pos_ref[0, t * TOP_K + k]
            dst = xr_hbm.at[pl.ds(pl.multiple_of(p * pitch, pitch), pitch), :]
            pltpu.make_async_copy(src, dst, sem.at[0]).start(priority=k % 2)
        return carry

    lax.fori_loop(0, tb, issue, 0, unroll=4)
    for k in range(TOP_K):
        pltpu.make_async_copy(xn_ref, xr_hbm.at[pl.ds(0, tb * pitch), :], sem.at[0]).wait()


def _dispatch(xn_rm, pos, zblk, zok, n_used, n_blocks_max, pitch, tb=512):
    n = xn_rm.shape[0] // pitch
    nb = n // tb
    grid_spec = pltpu.PrefetchScalarGridSpec(
        num_scalar_prefetch=3,
        grid=(nb,),
        in_specs=[
            pl.BlockSpec((None, 1, TOP_K * tb), lambda i, *_: (i, 0, 0), memory_space=pltpu.SMEM),
            pl.BlockSpec((tb * pitch, LANES), lambda i, *_: (i, 0)),
        ],
        out_specs=pl.BlockSpec(memory_space=pl.ANY),
        scratch_shapes=[pltpu.VMEM((MOE_TM * pitch, LANES), xn_rm.dtype), pltpu.SemaphoreType.DMA((2,))],
    )
    return pl.pallas_call(
        functools.partial(_dispatch_kernel, tb=tb, pitch=pitch, n_blocks_max=n_blocks_max),
        grid_spec=grid_spec,
        out_shape=jax.ShapeDtypeStruct((n_blocks_max * MOE_TM * pitch, LANES), xn_rm.dtype),
        compiler_params=_cparams(("arbitrary",)),
        name="dispatch",
    )(zblk, zok, n_used, pos.reshape(nb, 1, TOP_K * tb), xn_rm)


def _cast_rows(src, dst, rows_per=256):
    def body(i, carry):
        r = pl.ds(pl.multiple_of(i * rows_per, rows_per), rows_per)
        dst[r, :] = src[r, :].astype(dst.dtype)
        return carry

    lax.fori_loop(0, src.shape[0] // rows_per, body, 0)


def _stream_row_blocks(cnt, in_copy, out_copy, compute):
    for ahead in range(MOE_NBUF - 1):
        @pl.when(cnt > ahead)
        def _():
            in_copy(ahead, ahead).start()

    def body(b, carry):
        slot = b % MOE_NBUF

        @pl.when(b + MOE_NBUF - 1 < cnt)
        def _():
            in_copy(b + MOE_NBUF - 1, (b + MOE_NBUF - 1) % MOE_NBUF).start()

        in_copy(b, slot).wait()

        @pl.when(b >= MOE_NBUF)
        def _():
            out_copy(b - MOE_NBUF, slot).wait()

        compute(slot)
        out_copy(b, slot).start(priority=1)
        return carry

    lax.fori_loop(0, cnt, body, 0)

    for back in range(MOE_NBUF, 0, -1):
        @pl.when(cnt >= back)
        def _():
            out_copy(cnt - back, (cnt - back) % MOE_NBUF).wait()


def _stream_expert_rows(first, cnt, in_copy, out_copy, compute):
    n_units = cnt // MOE_UNIT
    _stream_row_blocks(n_units, in_copy(first, MOE_UNIT), out_copy(first, MOE_UNIT), compute(MOE_UNIT))
    done = n_units * MOE_UNIT
    _stream_row_blocks(cnt - done, in_copy(first + done, 1), out_copy(first + done, 1), compute(1))


def _zero_fill_blocks(first, last, zero_src, dst_copy):
    def start(b, carry):
        dst_copy(b).start()
        return carry

    def wait(b, carry):
        dst_copy(b).wait()
        return carry

    zero_src[...] = jnp.zeros_like(zero_src)
    lax.fori_loop(first, last, start, 0)
    lax.fori_loop(first, last, wait, 0)


def _moe_up_kernel(bs_ref, bc_ref, used_ref, x_hbm, w_hbm, bg_ref, bu_ref, act_hbm,
                   wf_scr, wb_scr, xbuf, lhs_scr, obuf, sem_w, sem_x, sem_o,
                   *, nt, tn, d_ff, pitch, n_blocks_max):
    g = pl.program_id(0)
    ng = pl.num_programs(0)
    e = g // nt
    col = pl.multiple_of((g % nt) * tn, tn)
    blk_rows = MOE_TM * pitch

    def w_copies(step):
        ee = step // nt
        cc = pl.multiple_of((step % nt) * tn, tn)
        return (pltpu.make_async_copy(w_hbm.at[ee, :, pl.ds(cc, tn)], wf_scr.at[0], sem_w.at[0]),
                pltpu.make_async_copy(w_hbm.at[ee, :, pl.ds(d_ff + cc, tn)], wf_scr.at[1], sem_w.at[1]))

    @pl.when(g == 0)
    def _():
        for c in w_copies(0):
            c.start(priority=1)

    for half, c in enumerate(w_copies(g)):
        c.wait()
        _cast_rows(wf_scr.at[half], wb_scr.at[half])

    @pl.when(g + 1 < ng)
    def _():
        for c in w_copies(g + 1):
            c.start(priority=1)

    start = bs_ref[e]

    def x_copy(first, nblk):
        def make(u, slot):
            r0 = pl.multiple_of((first + u * nblk) * blk_rows, blk_rows)
            return pltpu.make_async_copy(x_hbm.at[pl.ds(r0, nblk * blk_rows), :],
                                         xbuf.at[slot, pl.ds(0, nblk * blk_rows), :], sem_x.at[slot])
        return make

    def o_copy(first, nblk):
        def make(u, slot):
            r0 = pl.multiple_of((first + u * nblk) * MOE_TM, MOE_TM)
            return pltpu.make_async_copy(obuf.at[slot, pl.ds(0, nblk * MOE_TM), :],
                                         act_hbm.at[pl.ds(r0, nblk * MOE_TM), pl.ds(col, tn)], sem_o.at[slot])
        return make

    def compute(nblk):
        rows = nblk * MOE_TM

        def run(slot):
            half = pitch * LANES
            for c in range(pitch):
                w = xbuf[slot, pl.ds(c, rows, stride=pitch), :]
                lhs_scr[0:rows, c * LANES:(c + 1) * LANES] = _unpack_lo(w).astype(BF16)
                lhs_scr[0:rows, half + c * LANES:half + (c + 1) * LANES] = _unpack_hi(w).astype(BF16)
            x = lhs_scr[0:rows, :]
            gate = jnp.minimum(_dot(x, wb_scr[0]) + bg_ref[...], SWIGLU_LIMIT)
            up = jnp.clip(_dot(x, wb_scr[1]) + bu_ref[...], -SWIGLU_LIMIT, SWIGLU_LIMIT)
            obuf[slot, 0:rows, :] = (gate * jax.nn.sigmoid(SWIGLU_ALPHA * gate) * (up + 1.0)).astype(obuf.dtype)
        return run

    _stream_expert_rows(start, bc_ref[e], x_copy, o_copy, compute)

    @pl.when(e == N_EXPERTS - 1)
    def _():
        zero_blk = obuf.at[0, pl.ds(0, MOE_TM), :]

        def tail_copy(b):
            r0 = pl.multiple_of(b * MOE_TM, MOE_TM)
            return pltpu.make_async_copy(zero_blk, act_hbm.at[pl.ds(r0, MOE_TM), pl.ds(col, tn)], sem_o.at[0])
        _zero_fill_blocks(used_ref[0], n_blocks_max, zero_blk, tail_copy)


def _moe_up(x_rows_rm, w_up, b_up, blk_start, blk_count, n_used, n_blocks_max, tn=1024):
    n_exp, d, two_ff = w_up.shape
    d_ff = two_ff // 2
    pitch = d // 2 // LANES
    nt = d_ff // tn
    up_off = d_ff // tn
    grid_spec = pltpu.PrefetchScalarGridSpec(
        num_scalar_prefetch=3,
        grid=(n_exp * nt,),
        in_specs=[
            pl.BlockSpec(memory_space=pl.ANY),
            pl.BlockSpec(memory_space=pl.ANY),
            pl.BlockSpec((None, 1, tn), lambda g, *_: (g // nt, 0, g % nt)),
            pl.BlockSpec((None, 1, tn), lambda g, *_: (g // nt, 0, g % nt + up_off)),
        ],
        out_specs=pl.BlockSpec(memory_space=pl.ANY),
        scratch_shapes=[
            pltpu.VMEM((2, d, tn), F32),
            pltpu.VMEM((2, d, tn), BF16),
            pltpu.VMEM((MOE_NBUF, MOE_UNIT * MOE_TM * pitch, LANES), jnp.uint32),
            pltpu.VMEM((MOE_UNIT * MOE_TM, d), BF16),
            pltpu.VMEM((MOE_NBUF, MOE_UNIT * MOE_TM, tn), BF16),
            pltpu.SemaphoreType.DMA((2,)),
            pltpu.SemaphoreType.DMA((MOE_NBUF,)),
            pltpu.SemaphoreType.DMA((MOE_NBUF,)),
        ],
    )
    return pl.pallas_call(
        functools.partial(_moe_up_kernel, nt=nt, tn=tn, d_ff=d_ff, pitch=pitch, n_blocks_max=n_blocks_max),
        grid_spec=grid_spec,
        out_shape=jax.ShapeDtypeStruct((n_blocks_max * MOE_TM, d_ff), BF16),
        compiler_params=_cparams(("arbitrary",)),
        name="moe_up",
    )(blk_start, blk_count, n_used, x_rows_rm, w_up, b_up, b_up)


def _moe_down_kernel(bs_ref, bc_ref, used_ref, a_hbm, w_hbm, b_ref, y_hbm,
                     wf_scr, wb_scr, abuf, obuf, sem_w, sem_a, sem_o, *, pitch, n_blocks_max):
    e = pl.program_id(0)
    blk_rows = MOE_TM * pitch

    def w_copy(ee):
        return pltpu.make_async_copy(w_hbm.at[ee], wf_scr, sem_w.at[0])

    @pl.when(e == 0)
    def _():
        w_copy(0).start(priority=1)

    w_copy(e).wait()
    _cast_rows(wf_scr, wb_scr)

    @pl.when(e + 1 < pl.num_programs(0))
    def _():
        w_copy(e + 1).start(priority=1)

    start = bs_ref[e]

    def a_copy(first, nblk):
        def make(u, slot):
            r0 = pl.multiple_of((first + u * nblk) * MOE_TM, MOE_TM)
            return pltpu.make_async_copy(a_hbm.at[pl.ds(r0, nblk * MOE_TM), :],
                                         abuf.at[slot, pl.ds(0, nblk * MOE_TM), :], sem_a.at[slot])
        return make

    def o_copy(first, nblk):
        def make(u, slot):
            r0 = pl.multiple_of((first + u * nblk) * blk_rows, blk_rows)
            return pltpu.make_async_copy(obuf.at[slot, pl.ds(0, nblk * blk_rows), :],
                                         y_hbm.at[pl.ds(r0, nblk * blk_rows), :], sem_o.at[slot])
        return make

    def compute(nblk):
        rows = nblk * MOE_TM

        def run(slot):
            y = _dot(abuf[slot, 0:rows, :], wb_scr[...]) + b_ref[...]
            _store_rowmajor(obuf.at[slot], _pack_bf16_pairs(y))
        return run

    _stream_expert_rows(start, bc_ref[e], a_copy, o_copy, compute)

    @pl.when(e == N_EXPERTS - 1)
    def _():
        zero_blk = obuf.at[0, pl.ds(0, blk_rows), :]

        def tail_copy(b):
            r0 = pl.multiple_of(b * blk_rows, blk_rows)
            return pltpu.make_async_copy(zero_blk, y_hbm.at[pl.ds(r0, blk_rows), :], sem_o.at[0])
        _zero_fill_blocks(used_ref[0], n_blocks_max, zero_blk, tail_copy)


def _moe_down(act, w_down, b_down, blk_start, blk_count, n_used, n_blocks_max):
    n_rows, d_ff = act.shape
    n_exp, _, d = w_down.shape
    pitch = d // 2 // LANES
    grid_spec = pltpu.PrefetchScalarGridSpec(
        num_scalar_prefetch=3,
        grid=(n_exp,),
        in_specs=[
            pl.BlockSpec(memory_space=pl.ANY),
            pl.BlockSpec(memory_space=pl.ANY),
            pl.BlockSpec((None, 1, d), lambda e, *_: (e, 0, 0)),
        ],
        out_specs=pl.BlockSpec(memory_space=pl.ANY),
        scratch_shapes=[
            pltpu.VMEM((d_ff, d), F32),
            pltpu.VMEM((d_ff, d), BF16),
            pltpu.VMEM((MOE_NBUF, MOE_UNIT * MOE_TM, d_ff), BF16),
            pltpu.VMEM((MOE_NBUF, MOE_UNIT * MOE_TM * pitch, LANES), jnp.uint32),
            pltpu.SemaphoreType.DMA((1,)),
            pltpu.SemaphoreType.DMA((MOE_NBUF,)),
            pltpu.SemaphoreType.DMA((MOE_NBUF,)),
        ],
    )
    return pl.pallas_call(
        functools.partial(_moe_down_kernel, pitch=pitch, n_blocks_max=n_blocks_max),
        grid_spec=grid_spec,
        out_shape=jax.ShapeDtypeStruct((n_rows * pitch, LANES), jnp.uint32),
        compiler_params=_cparams(("arbitrary",)),
        name="moe_down",
    )(blk_start, blk_count, n_used, act, w_down, b_down)


def _combine_kernel(pos0_ref, posn_ref, y_hbm, x2_ref, gate_ref, g_ref, o_ref, buf_even, buf_odd, sem,
                    *, tb, pitch):
    i = pl.program_id(0)
    last = pl.num_programs(0) - 1
    n_rows = TOP_K * tb

    def row_copy(p_ref, r, buf, sem_slot):
        src = y_hbm.at[pl.ds(pl.multiple_of(p_ref[0, r] * pitch, pitch), pitch), :]
        return pltpu.make_async_copy(src, buf.at[pl.ds(r * pitch, pitch), :], sem.at[sem_slot])

    def wait_block(buf, sem_slot):
        pltpu.make_async_copy(y_hbm.at[pl.ds(0, n_rows * pitch), :], buf, sem.at[sem_slot]).wait()

    @pl.when(i == 0)
    def _():
        def issue(j, carry):
            for half in range(2):
                row_copy(pos0_ref, 2 * j + half, buf_even, 0).start(priority=half)
            return carry
        lax.fori_loop(0, n_rows // 2, issue, 0, unroll=4)

    def step(buf_cur, sem_cur, buf_nxt, sem_nxt):
        wait_block(buf_cur, sem_cur)
        for r in range(n_rows):
            row_copy(posn_ref, r, buf_nxt, sem_nxt).start(priority=r % 2)
        gates = gate_ref[...]
        half = pitch * LANES
        ssq = jnp.zeros((tb, 1), F32)
        for c in range(pitch):
            lo = slice(c * LANES, (c + 1) * LANES)
            hi = slice(half + c * LANES, half + (c + 1) * LANES)
            z_lo = x2_ref[:, lo]
            z_hi = x2_ref[:, hi]
            for k in range(TOP_K):
                w = buf_cur[pl.ds(k * tb * pitch + c, tb, stride=pitch), :]
                z_lo = z_lo + gates[:, k:k + 1] * _unpack_lo(w)
                z_hi = z_hi + gates[:, k:k + 1] * _unpack_hi(w)
            o_ref[:, lo] = z_lo
            o_ref[:, hi] = z_hi
            ssq = ssq + (jnp.sum(z_lo * z_lo, axis=-1, keepdims=True)
                         + jnp.sum(z_hi * z_hi, axis=-1, keepdims=True))
        o_ref[...] = o_ref[...] * lax.rsqrt(ssq * (1.0 / (2 * half)) + EPS) * g_ref[...]

        @pl.when(i == last)
        def _():
            wait_block(buf_nxt, sem_nxt)

    @pl.when(i % 2 == 0)
    def _():
        step(buf_even, 0, buf_odd, 1)

    @pl.when(i % 2 == 1)
    def _():
        step(buf_odd, 1, buf_even, 0)


def _combine(y_rows_rm, pos, gates, x2, g_final, tb=256):
    n, d = x2.shape
    nb = n // tb
    pitch = d // 2 // LANES
    pos_blk = pos.reshape(nb, tb, TOP_K).transpose(0, 2, 1).reshape(nb, 1, TOP_K * tb)
    pos_spec = lambda imap: pl.BlockSpec((None, 1, TOP_K * tb), imap, memory_space=pltpu.SMEM)
    return pl.pallas_call(
        functools.partial(_combine_kernel, tb=tb, pitch=pitch),
        grid=(nb,),
        in_specs=[
            pos_spec(lambda i: (0, 0, 0)),
            pos_spec(lambda i: (jnp.minimum(i + 1, nb - 1), 0, 0)),
            pl.BlockSpec(memory_space=pl.ANY),
            pl.BlockSpec((tb, d), lambda i: (i, 0)),
            pl.BlockSpec((tb, LANES), lambda i: (i, 0)),
            pl.BlockSpec((1, d), lambda i: (0, 0)),
        ],
        out_specs=pl.BlockSpec((tb, d), lambda i: (i, 0)),
        out_shape=jax.ShapeDtypeStruct((n, d), F32),
        scratch_shapes=[pltpu.VMEM((TOP_K * tb * pitch, LANES), jnp.uint32),
                        pltpu.VMEM((TOP_K * tb * pitch, LANES), jnp.uint32),
                        pltpu.SemaphoreType.DMA((2,))],
        compiler_params=_cparams(("arbitrary",)),
        name="combine",
    )(pos_blk, pos_blk, y_rows_rm, x2, gates, g_final)


def _routing_tables(idx, rank, cnt):
    counts = cnt[0, :N_EXPERTS]
    blk_count = (counts + MOE_TM - 1) // MOE_TM
    blk_end = jnp.cumsum(blk_count)
    blk_start = blk_end - blk_count
    hot = idx[:, :TOP_K, None] == jnp.arange(N_EXPERTS, dtype=jnp.int32)
    pos = jnp.sum(jnp.where(hot, blk_start * MOE_TM, 0), axis=-1) + rank[:, :TOP_K]
    return pos.astype(jnp.int32), blk_start.astype(jnp.int32), blk_count.astype(jnp.int32)


def kernel(x, mem, g_attn_norm, g_mem_norm, w_in, w_gla_a2, b_gla_a, g_gla_out, b_fox_f, g_fox_out,
           w_mem_kv, g_mem_out, w_out, g_ffn_norm, w_router, b_router, w_moe_up, b_moe_up,
           w_moe_down, b_moe_down, g_final):
    batch, seq, d = x.shape
    n_mem = mem.shape[1]
    n = batch * seq
    depth = w_in.shape[0]
    assert depth == 1, "the combine kernel applies the final norm, so exactly one layer is supported"
    kw = GLA_HEADS * GLA_DK
    vw = GLA_HEADS * GLA_DV
    fw = FOX_HEADS * FOX_DH
    mw = MEM_HEADS * MEM_DH
    o_q, o_k, o_v, o_g = 0, kw, 2 * kw, 2 * kw + vw
    o_a = o_g + vw
    o_fq = o_a + GLA_LOWRANK
    o_fk, o_fv = o_fq + fw, o_fq + 2 * fw
    o_ff = o_fq + 3 * fw
    o_mq = o_ff + FOX_HEADS
    f_lane = GLA_LOWRANK

    xf = x.reshape(n, d)
    for l in range(depth):
        fox_q_scale = FOX_DH ** -0.5 * LOG2E
        col_scale = jnp.ones((w_in.shape[2],), F32).at[o_fq:o_fk].set(fox_q_scale)
        wt = (jnp.transpose(w_in[l]) * col_scale[:, None]).astype(BF16)
        wa_t = jnp.concatenate([wt[o_q:o_a], wt[o_fq:o_ff], wt[o_mq:o_mq + mw]], axis=0)
        wb_t = jnp.zeros((LANES, d), BF16).at[:GLA_LOWRANK].set(wt[o_a:o_fq])
        wb_t = wb_t.at[f_lane:f_lane + FOX_HEADS].set(wt[o_ff:o_mq])
        proj, small = _in_proj(xf, g_attn_norm[l].reshape(1, d), wa_t, wb_t)

        wa2p = jnp.zeros((LANES, kw), F32).at[:GLA_LOWRANK].set(w_gla_a2[l]).astype(BF16)
        bfv = jnp.zeros((1, LANES), F32).at[0, f_lane:f_lane + FOX_HEADS].set(b_fox_f[l])
        gla, f_cum = _gla(proj, small, wa2p, b_gla_a[l].reshape(1, kw), bfv,
                          g_gla_out[l].reshape(1, vw), batch, seq)

        f_t = f_cum.reshape(batch, seq, LANES)[:, :, f_lane:f_lane + FOX_HEADS].transpose(0, 2, 1)
        f_t = jnp.concatenate([f_t, jnp.zeros_like(f_t)], axis=1)
        fox = _fox(proj, f_t, g_fox_out[l].reshape(1, fw), batch, seq)

        kv = _mem_kv(mem.reshape(batch * n_mem, d), g_mem_norm[l].reshape(1, d), w_mem_kv[l].astype(BF16))
        memo = _mem_attn(proj, kv, g_mem_out[l].reshape(1, mw), batch, seq, n_mem)

        wr = jnp.zeros((d, LANES), F32).at[:, :N_EXPERTS].set(w_router[l])
        wr_hi = wr.astype(BF16)
        wr = jnp.concatenate([wr_hi, (wr - wr_hi.astype(F32)).astype(BF16)], axis=1)
        br =jnp.zeros((1, LANES), F32).at[0, :N_EXPERTS].set(b_router[l])
        x2, xn, logits = _out_proj(xf, gla, fox, memo, w_out[l].astype(BF16),
                                   g_ffn_norm[l].reshape(1, d), wr, br)

        idx, gates, rank, cnt = _route(logits)
        pos, blk_start, blk_count = _routing_tables(idx, rank, cnt)
        n_blocks_max = -(-(n * TOP_K + N_EXPERTS * (MOE_TM - 1)) // MOE_TM)
        blk_end = blk_start + blk_count
        n_used = blk_end[-1:]
        x_rows = _dispatch(xn, pos, blk_end - 1, (blk_count > 0).astype(jnp.int32), n_used,
                           n_blocks_max, d // 2 // LANES)
        d_ff = w_moe_up.shape[3] // 2
        act = _moe_up(x_rows, w_moe_up[l], b_moe_up[l].reshape(N_EXPERTS, 1, 2 * d_ff),
                      blk_start, blk_count, n_used, n_blocks_max)
        y_rows = _moe_down(act, w_moe_down[l], b_moe_down[l].reshape(N_EXPERTS, 1, d),
                           blk_start, blk_count, n_used, n_blocks_max)
        xf = _combine(y_rows, pos, gates, x2, g_final.reshape(1, d))
    return xf.reshape(batch, seq, d)
```

```python
import functools

import jax
import jax.numpy as jnp
from jax import lax
from jax.experimental import pallas as pl
from jax.experimental.pallas import tpu as pltpu

EPS = 1e-5
CHUNK = 64
GLA_HEADS = 4
GLA_DK = 128
GLA_DV = 256
GLA_LOWRANK = 16
GLA_TAU = 16.0
FOX_HEADS = 4
FOX_DH = 128
MEM_HEADS = 4
MEM_DH = 128
N_EXPERTS = 32
TOP_K = 4
SWIGLU_LIMIT = 7.0
SWIGLU_ALPHA = 1.702
LANES = 128
MXU_COLS = 256
MOE_TM = 256
MOE_NBUF = 2
MOE_UNIT = 2
VMEM_LIMIT = 56 * 1024 * 1024

F32 = jnp.float32
BF16 = jnp.bfloat16


def _cparams(sem, vmem=VMEM_LIMIT):
    return pltpu.CompilerParams(dimension_semantics=sem, vmem_limit_bytes=vmem)


def _log_sigmoid(x):
    return jnp.minimum(x, 0.0) - jnp.log1p(jnp.exp(-jnp.abs(x)))


def _rms(x, g):
    return x * lax.rsqrt(jnp.mean(x * x, axis=-1, keepdims=True) + EPS) * g


def _dot(a, b, **kw):
    return jnp.dot(a, b, preferred_element_type=F32, **kw)


def _dot_nt(a, b):
    return lax.dot_general(a, b, (((1,), (1,)), ((), ())), preferred_element_type=F32)


def _dot_tn(a, b):
    return lax.dot_general(a, b, (((0,), (0,)), ((), ())), preferred_element_type=F32)


def _store_rowmajor(ref, x, base=0):
    rows, w = x.shape
    pitch = w // LANES
    for c in range(pitch):
        ref[pl.ds(base + c, rows, stride=pitch), :] = x[:, c * LANES:(c + 1) * LANES]


def _pack_bf16_pairs(x):
    half = x.shape[1] // 2
    bits = lambda v: lax.bitcast_convert_type(v.astype(BF16).astype(F32), jnp.uint32)
    return bits(x[:, half:]) | (bits(x[:, :half]) >> 16)


def _unpack_lo(w):
    return lax.bitcast_convert_type(w << 16, F32)


def _unpack_hi(w):
    return lax.bitcast_convert_type(w & jnp.uint32(0xFFFF0000), F32)


def _in_proj_kernel(x_ref, g_ref, wa_ref, wb_ref, proj_ref, small_ref, h_scr):
    @pl.when(pl.program_id(1) == 0)
    def _():
        hb = _rms(x_ref[...], g_ref[...]).astype(BF16)
        h_scr[...] = hb
        small_ref[...] = _dot_nt(hb, wb_ref[...])

    proj_ref[...] = _dot_nt(h_scr[...], wa_ref[...]).astype(proj_ref.dtype)


def _in_proj(x2d, g, wa_t, wb_t, tm=1024, tn=1280):
    n, d = x2d.shape
    na = wa_t.shape[0]
    return pl.pallas_call(
        _in_proj_kernel,
        grid=(n // tm, na // tn),
        in_specs=[
            pl.BlockSpec((tm, d), lambda i, j: (i, 0)),
            pl.BlockSpec((1, d), lambda i, j: (0, 0)),
            pl.BlockSpec((tn, d), lambda i, j: (j, 0)),
            pl.BlockSpec((LANES, d), lambda i, j: (0, 0)),
        ],
        out_specs=[
            pl.BlockSpec((tm, tn), lambda i, j: (i, j)),
            pl.BlockSpec((tm, LANES), lambda i, j: (i, 0)),
        ],
        out_shape=[
            jax.ShapeDtypeStruct((n, na), BF16),
            jax.ShapeDtypeStruct((n, LANES), F32),
        ],
        scratch_shapes=[pltpu.VMEM((tm, d), BF16)],
        compiler_params=_cparams(("parallel", "arbitrary")),
        name="in_proj",
    )(x2d, g, wa_t, wb_t)


def _gla_kernel(q_ref, k_ref, v_ref, gate_ref, small_ref, wa2_ref, ba_ref, bf_ref, gout_ref,
                o_ref, f_ref, state_scr, fcar_scr, la_scr, lf_scr, *, n_chunks):
    @pl.when(pl.program_id(1) == 0)
    def _():
        state_scr[...] = jnp.zeros_like(state_scr)
        fcar_scr[...] = jnp.zeros_like(fcar_scr)

    small = small_ref[...]
    la_scr[...] = _log_sigmoid(_dot(small.astype(BF16), wa2_ref[...]) + ba_ref[...]) * (1.0 / GLA_TAU)
    lf_scr[...] = _log_sigmoid(small + bf_ref[...])
    row = lax.broadcasted_iota(jnp.int32, (CHUNK, CHUNK), 0)
    col = lax.broadcasted_iota(jnp.int32, (CHUNK, CHUNK), 1)
    tri = (col <= row).astype(F32)
    scale = GLA_DK ** -0.5

    def chunk_body(c, carry):
        r = pl.ds(pl.multiple_of(c * CHUNK, CHUNK), CHUNK)
        b = _dot(tri, la_scr[r, :], precision=lax.Precision.HIGHEST)
        b_end = b[CHUNK - 1:CHUNK, :]
        k_dec = k_ref[r, :].astype(F32) * jnp.exp(b_end - b)
        decay = jnp.exp(b_end)
        f_cum = _dot(tri, lf_scr[r, :], precision=lax.Precision.HIGHEST) + fcar_scr[...]
        f_ref[r, :] = f_cum
        fcar_scr[...] = f_cum[CHUNK - 1:CHUNK, :]
        heads = range(GLA_HEADS)
        ks = [slice(h * GLA_DK, (h + 1) * GLA_DK) for h in heads]
        vs = [slice(h * GLA_DV, (h + 1) * GLA_DV) for h in heads]
        old = [state_scr[h] for h in heads]
        new = [old[h] * decay[:, ks[h]] + _dot_tn(v_ref[r, vs[h]], k_dec[:, ks[h]].astype(BF16))
               for h in heads]
        for h in heads:
            state_scr[h] = new[h]
        for h in heads:
            o = _dot_nt(q_ref[r, ks[h]], new[h].astype(BF16)) * scale
            gt = gate_ref[r, vs[h]].astype(F32)
            o_ref[r, vs[h]] = (_rms(o, gout_ref[:, vs[h]]) * (gt * jax.nn.sigmoid(gt))).astype(o_ref.dtype)
        return carry

    lax.fori_loop(0, n_chunks, chunk_body, 0, unroll=True)


def _gla(proj, small, wa2p, ba, bfv, gout, batch, seq, ts=512):
    n = proj.shape[0]
    nsb = seq // ts
    kw = GLA_HEADS * GLA_DK
    vw = GLA_HEADS * GLA_DV
    row = lambda b, s: b * nsb + s
    return pl.pallas_call(
        functools.partial(_gla_kernel, n_chunks=ts // CHUNK),
        grid=(batch, nsb),
        in_specs=[
            pl.BlockSpec((ts, kw), lambda b, s: (row(b, s), 0)),
            pl.BlockSpec((ts, kw), lambda b, s: (row(b, s), 1)),
            pl.BlockSpec((ts, vw), lambda b, s: (row(b, s), 1)),
            pl.BlockSpec((ts, vw), lambda b, s: (row(b, s), 2)),
            pl.BlockSpec((ts, LANES), lambda b, s: (row(b, s), 0)),
            pl.BlockSpec((LANES, kw), lambda b, s: (0, 0)),
            pl.BlockSpec((1, kw), lambda b, s: (0, 0)),
            pl.BlockSpec((1, LANES), lambda b, s: (0, 0)),
            pl.BlockSpec((1, vw), lambda b, s: (0, 0)),
        ],
        out_specs=[
            pl.BlockSpec((ts, vw), lambda b, s: (row(b, s), 0)),
            pl.BlockSpec((ts, LANES), lambda b, s: (row(b, s), 0)),
        ],
        out_shape=[
            jax.ShapeDtypeStruct((n, vw), BF16),
            jax.ShapeDtypeStruct((n, LANES), F32),
        ],
        scratch_shapes=[
            pltpu.VMEM((GLA_HEADS, GLA_DV, GLA_DK), F32),
            pltpu.VMEM((1, LANES), F32),
            pltpu.VMEM((ts, kw), F32),
            pltpu.VMEM((ts, LANES), F32),
        ],
        compiler_params=_cparams(("parallel", "arbitrary")),
        name="gla",
    )(proj, proj, proj, proj, small, wa2p, ba, bfv, gout)


LOG2E = 1.4426950408889634
def _fox_kernel(q_ref, k_ref, v_ref, fk_ref, g_ref, o_ref, m_scr, l_scr, acc_scr, s_even, s_odd,
                *, tq, tk):
    i = pl.program_id(1)
    t = pl.program_id(2)

    @pl.when(t == 0)
    def _():
        m_scr[...] = jnp.full_like(m_scr, -jnp.inf)
        l_scr[...] = jnp.zeros_like(l_scr)
        acc_scr[...] = jnp.zeros_like(acc_scr)

    heads = range(FOX_HEADS)
    hs = [slice(h * FOX_DH, (h + 1) * FOX_DH) for h in heads]

    def score(s_out):
        for h in heads:
            s_out[h] = _dot_nt(q_ref[:, hs[h]], k_ref[:, hs[h]])

    def update(s_in, on_diagonal):
        m_prev = [m_scr[h] for h in heads]
        l_prev = [l_scr[h] for h in heads]
        acc_prev = [acc_scr[h] for h in heads]
        s = [s_in[h] - fk_ref[h:h + 1, :] * LOG2E for h in heads]
        if on_diagonal:
            row = lax.broadcasted_iota(jnp.int32, (tq, tk), 0)
            col = lax.broadcasted_iota(jnp.int32, (tq, tk), 1)
            s = [jnp.where(col <= row, sh, -jnp.inf) for sh in s]
        m_new = [jnp.maximum(m_prev[h], jnp.max(s[h], axis=-1, keepdims=True)) for h in heads]
        alpha = [jnp.exp2(m_prev[h] - m_new[h]) for h in heads]
        p = [jnp.exp2(s[h] - m_new[h]) for h in heads]
        l_new = [alpha[h] * l_prev[h] + jnp.sum(p[h], axis=-1, keepdims=True) for h in heads]
        acc_new = [alpha[h] * acc_prev[h] + _dot(p[h].astype(BF16), v_ref[:, hs[h]]) for h in heads]
        return m_new, l_new, acc_new

    def step(s_write, s_read):
        @pl.when(t == 0)
        def _():
            score(s_write)

        @pl.when(jnp.logical_and(t >= 1, t <= i))
        def _():
            m_new, l_new, acc_new = update(s_read, False)
            score(s_write)
            for h in heads:
                m_scr[h] = m_new[h]
                l_scr[h] = l_new[h]
                acc_scr[h] = acc_new[h]

        @pl.when(t == i + 1)
        def _():
            _, l_new, acc_new = update(s_read, True)
            for h in heads:
                o_ref[:, hs[h]] = _rms(acc_new[h] / l_new[h], g_ref[:, hs[h]]).astype(o_ref.dtype)

    @pl.when(t % 2 == 0)
    def _():
        step(s_even, s_odd)

    @pl.when(t % 2 == 1)
    def _():
        step(s_odd, s_even)


def _fox(proj, f_t, g_fox, batch, seq, tq=512, tk=512):
    assert tq == tk, "the diagonal-block mask assumes square blocks"
    n = proj.shape[0]
    w = FOX_HEADS * FOX_DH
    nq, nk = seq // tq, seq // tk
    qcol, kcol, vcol = 3072 // w, 3584 // w, 4096 // w
    return pl.pallas_call(
        functools.partial(_fox_kernel, tq=tq, tk=tk),
        grid=(batch, nq, nk + 1),
        in_specs=[
            pl.BlockSpec((tq, w), lambda b, i, t: (b * nq + i, qcol)),
            pl.BlockSpec((tk, w), lambda b, i, t: (b * nk + jnp.minimum(t, i), kcol)),
            pl.BlockSpec((tk, w), lambda b, i, t: (b * nk + jnp.clip(t - 1, 0, i), vcol)),
            pl.BlockSpec((None, 8, tk), lambda b, i, t: (b, 0, jnp.clip(t - 1, 0, i))),
            pl.BlockSpec((1, w), lambda b, i, t: (0, 0)),
        ],
        out_specs=pl.BlockSpec((tq, w), lambda b, i, t: (b * nq + i, 0)),
        out_shape=jax.ShapeDtypeStruct((n, w), BF16),
        scratch_shapes=[
            pltpu.VMEM((FOX_HEADS, tq, 1), F32),
            pltpu.VMEM((FOX_HEADS, tq, 1), F32),
            pltpu.VMEM((FOX_HEADS, tq, FOX_DH), F32),
            pltpu.VMEM((FOX_HEADS, tq, tk), F32),
            pltpu.VMEM((FOX_HEADS, tq, tk), F32),
        ],
        compiler_params=_cparams(("parallel", "parallel", "arbitrary")),
        name="fox",
    )(proj, proj, proj, f_t, g_fox)


def _mem_kv_kernel(m_ref, g_ref, w_ref, kv_ref):
    kv_ref[...] = _dot(_rms(m_ref[...], g_ref[...]).astype(BF16), w_ref[...]).astype(kv_ref.dtype)


def _mem_kv(mem2d, g, w, tm=256):
    n, d = mem2d.shape
    nw = w.shape[1]
    return pl.pallas_call(
        _mem_kv_kernel,
        grid=(n // tm,),
        in_specs=[
            pl.BlockSpec((tm, d), lambda i: (i, 0)),
            pl.BlockSpec((1, d), lambda i: (0, 0)),
            pl.BlockSpec((d, nw), lambda i: (0, 0)),
        ],
        out_specs=pl.BlockSpec((tm, nw), lambda i: (i, 0)),
        out_shape=jax.ShapeDtypeStruct((n, nw), BF16),
        compiler_params=_cparams(("parallel",)),
        name="mem_kv",
    )(mem2d, g, w)


def _mem_attn_kernel(q_ref, k_ref, v_ref, g_ref, o_ref):
    scale = MEM_DH ** -0.5
    for h in range(MEM_HEADS):
        hs = slice(h * MEM_DH, (h + 1) * MEM_DH)
        s = _dot_nt(q_ref[:, hs], k_ref[:, hs]) * scale
        p = jnp.exp(s - jnp.max(s, axis=-1, keepdims=True))
        l = jnp.sum(p, axis=-1, keepdims=True)
        o = _dot((p / l).astype(BF16), v_ref[:, hs])
        o_ref[:, hs] = _rms(o, g_ref[:, hs]).astype(o_ref.dtype)


def _mem_attn(proj, kv, g_mem_out, batch, seq, n_mem, tq=1024):
    n = proj.shape[0]
    w = MEM_HEADS * MEM_DH
    nq = seq // tq
    qcol = 4608 // w
    return pl.pallas_call(
        _mem_attn_kernel,
        grid=(n // tq,),
        in_specs=[
            pl.BlockSpec((tq, w), lambda i: (i, qcol)),
            pl.BlockSpec((n_mem, w), lambda i: (i // nq, 0)),
            pl.BlockSpec((n_mem, w), lambda i: (i // nq, 1)),
            pl.BlockSpec((1, w), lambda i: (0, 0)),
        ],
        out_specs=pl.BlockSpec((tq, w), lambda i: (i, 0)),
        out_shape=jax.ShapeDtypeStruct((n, w), BF16),
        compiler_params=_cparams(("parallel",)),
        name="mem_attn",
    )(proj, kv, kv, g_mem_out)


def _out_proj_kernel(x_ref, gla_ref, fox_ref, mem_ref, w1_ref, w2_ref, w3_ref, g_ref, wr_ref, br_ref,
                     x2_ref, xn_ref, logit_ref):
    x2 = (x_ref[...] + _dot(gla_ref[...], w1_ref[...]) + _dot(fox_ref[...], w2_ref[...])
          + _dot(mem_ref[...], w3_ref[...]))
    x2_ref[...] = x2
    xn = _rms(x2, g_ref[...])
    _store_rowmajor(xn_ref, _pack_bf16_pairs(xn))
    xh = xn.astype(BF16)
    xl = (xn - xh.astype(F32)).astype(BF16)
    hi = _dot(xh, wr_ref[...])
    logit_ref[...] = hi[:, :LANES] + hi[:, LANES:] + _dot(xl, wr_ref[:, :LANES]) + br_ref[...]


def _out_proj(x2d, gla, fox, memo, w_out, g_ffn, wr, br, tm=512):
    n, d = x2d.shape
    w1, w2 = gla.shape[1], fox.shape[1]
    const = lambda i: (0, 0)
    return pl.pallas_call(
        _out_proj_kernel,
        grid=(n // tm,),
        in_specs=[
            pl.BlockSpec((tm, d), lambda i: (i, 0)),
            pl.BlockSpec((tm, w1), lambda i: (i, 0)),
            pl.BlockSpec((tm, w2), lambda i: (i, 0)),
            pl.BlockSpec((tm, w2), lambda i: (i, 0)),
            pl.BlockSpec((w1, d), lambda i: (0, 0)),
            pl.BlockSpec((w2, d), lambda i: (w1 // w2, 0)),
            pl.BlockSpec((w2, d), lambda i: (w1 // w2 + 1, 0)),
            pl.BlockSpec((1, d), const),
            pl.BlockSpec((d, 2 * LANES), const),
            pl.BlockSpec((1, LANES), const),
        ],
        out_specs=[
            pl.BlockSpec((tm, d), lambda i: (i, 0)),
            pl.BlockSpec((tm * (d // 2 // LANES), LANES), lambda i: (i, 0)),
            pl.BlockSpec((tm, LANES), lambda i: (i, 0)),
        ],
        out_shape=[
            jax.ShapeDtypeStruct((n, d), F32),
            jax.ShapeDtypeStruct((n * (d // 2 // LANES), LANES), jnp.uint32),
            jax.ShapeDtypeStruct((n, LANES), F32),
        ],
        compiler_params=_cparams(("parallel",)),
        name="out_proj",
    )(x2d, gla, fox, memo, w_out, w_out, w_out, g_ffn, wr, br)


def _route_kernel(l_ref, idx_ref, gate_ref, rank_ref, cnt_ref, carry_scr, *, tb):
    @pl.when(pl.program_id(0) == 0)
    def _():
        carry_scr[...] = jnp.zeros_like(carry_scr)

    lane = lax.broadcasted_iota(jnp.int32, (tb, LANES), 1)
    logit = jnp.where(lane < N_EXPERTS, l_ref[...], -jnp.inf)
    vals, hots = [], []
    idx_out = jnp.zeros((tb, LANES), jnp.int32)
    for k in range(TOP_K):
        m = jnp.max(logit, axis=-1, keepdims=True)
        ik = jnp.min(jnp.where(logit == m, lane, LANES), axis=-1, keepdims=True)
        hot = lane == ik
        logit = jnp.where(hot, -jnp.inf, logit)
        vals.append(m)
        hots.append(hot)
        idx_out = jnp.where(lane == k, ik, idx_out)
    idx_ref[...] = idx_out

    e = [jnp.exp(v - vals[0]) for v in vals]
    den = e[0] + e[1] + e[2] + e[3]
    gate_out = jnp.zeros((tb, LANES), F32)
    for k in range(TOP_K):
        gate_out = jnp.where(lane == k, e[k] / den, gate_out)
    gate_ref[...] = gate_out

    member = jnp.zeros((tb, LANES), F32)
    for hot in hots:
        member = member + hot.astype(F32)
    row = lax.broadcasted_iota(jnp.int32, (tb, tb), 0)
    col = lax.broadcasted_iota(jnp.int32, (tb, tb), 1)
    before = (col < row).astype(BF16)
    rank = _dot(before, member.astype(BF16)) + carry_scr[...]
    rank_out = jnp.zeros((tb, LANES), F32)
    for k in range(TOP_K):
        rk = jnp.sum(jnp.where(hots[k], rank, 0.0), axis=-1, keepdims=True)
        rank_out = jnp.where(lane == k, rk, rank_out)
    rank_ref[...] = rank_out.astype(jnp.int32)
    total = carry_scr[...] + jnp.sum(member, axis=0, keepdims=True)
    carry_scr[...] = total
    cnt_ref[...] = total.astype(jnp.int32)


def _route(logits, tb=512):
    n = logits.shape[0]
    blk = pl.BlockSpec((tb, LANES), lambda i: (i, 0))
    return pl.pallas_call(
        functools.partial(_route_kernel, tb=tb),
        grid=(n // tb,),
        in_specs=[blk],
        out_specs=[blk, blk, blk, pl.BlockSpec((1, LANES), lambda i: (0, 0))],
        out_shape=[
            jax.ShapeDtypeStruct((n, LANES), jnp.int32),
            jax.ShapeDtypeStruct((n, LANES), F32),
            jax.ShapeDtypeStruct((n, LANES), jnp.int32),
            jax.ShapeDtypeStruct((1, LANES), jnp.int32),
        ],
        scratch_shapes=[pltpu.VMEM((1, LANES), F32)],
        compiler_params=_cparams(("arbitrary",)),
        name="route",
    )(logits)


def _dispatch_kernel(zblk_ref, zok_ref, used_ref, pos_ref, xn_ref, xr_hbm, zero_scr, sem,
                     *, tb, pitch, n_blocks_max):
    blk_rows = MOE_TM * pitch

    def zero_copy(b):
        dst = xr_hbm.at[pl.ds(pl.multiple_of(b * blk_rows, blk_rows), blk_rows), :]
        return pltpu.make_async_copy(zero_scr, dst, sem.at[1])

    @pl.when(pl.program_id(0) == 0)
    def _():
        zero_scr[...] = jnp.zeros_like(zero_scr)

        def per_expert(action):
            def body(e, carry):
                @pl.when(zok_ref[e] == 1)
                def _():
                    action(zero_copy(zblk_ref[e]))
                return carry
            lax.fori_loop(0, N_EXPERTS, body, 0)

        def per_tail(action):
            def body(b, carry):
                action(zero_copy(b))
                return carry
            lax.fori_loop(used_ref[0], n_blocks_max, body, 0)

        per_expert(lambda c: c.start())
        per_tail(lambda c: c.start())
        per_expert(lambda c: c.wait())
        per_tail(lambda c: c.wait())

    def issue(t, carry):
        src = xn_ref.at[pl.ds(pl.multiple_of(t * pitch, pitch), pitch), :]
        for k in range(TOP_K):
            p = pos_ref[0, t * TOP_K + k]
            dst = xr_hbm.at[pl.ds(pl.multiple_of(p * pitch, pitch), pitch), :]
            pltpu.make_async_copy(src, dst, sem.at[0]).start(priority=k % 2)
        return carry

    lax.fori_loop(0, tb, issue, 0, unroll=4)
    for k in range(TOP_K):
        pltpu.make_async_copy(xn_ref, xr_hbm.at[pl.ds(0, tb * pitch), :], sem.at[0]).wait()


def _dispatch(xn_rm, pos, zblk, zok, n_used, n_blocks_max, pitch, tb=512):
    n = xn_rm.shape[0] // pitch
    nb = n // tb
    grid_spec = pltpu.PrefetchScalarGridSpec(
        num_scalar_prefetch=3,
        grid=(nb,),
        in_specs=[
            pl.BlockSpec((None, 1, TOP_K * tb), lambda i, *_: (i, 0, 0), memory_space=pltpu.SMEM),
            pl.BlockSpec((tb * pitch, LANES), lambda i, *_: (i, 0)),
        ],
        out_specs=pl.BlockSpec(memory_space=pl.ANY),
        scratch_shapes=[pltpu.VMEM((MOE_TM * pitch, LANES), xn_rm.dtype), pltpu.SemaphoreType.DMA((2,))],
    )
    return pl.pallas_call(
        functools.partial(_dispatch_kernel, tb=tb, pitch=pitch, n_blocks_max=n_blocks_max),
        grid_spec=grid_spec,
        out_shape=jax.ShapeDtypeStruct((n_blocks_max * MOE_TM * pitch, LANES), xn_rm.dtype),
        compiler_params=_cparams(("arbitrary",)),
        name="dispatch",
    )(zblk, zok, n_used, pos.reshape(nb, 1, TOP_K * tb), xn_rm)


def _stream_row_blocks(cnt, in_copy, out_copy, compute):
    for ahead in range(MOE_NBUF - 1):
        @pl.when(cnt > ahead)
        def _():
            in_copy(ahead, ahead).start()

    def body(b, carry):
        slot = b % MOE_NBUF

        @pl.when(b + MOE_NBUF - 1 < cnt)
        def _():
            in_copy(b + MOE_NBUF - 1, (b + MOE_NBUF - 1) % MOE_NBUF).start()

        in_copy(b, slot).wait()

        @pl.when(b >= MOE_NBUF)
        def _():
            out_copy(b - MOE_NBUF, slot).wait()

        compute(slot)
        out_copy(b, slot).start(priority=1)
        return carry

    lax.fori_loop(0, cnt, body, 0)

    for back in range(MOE_NBUF, 0, -1):
        @pl.when(cnt >= back)
        def _():
            out_copy(cnt - back, (cnt - back) % MOE_NBUF).wait()


def _stream_expert_rows(first, cnt, in_copy, out_copy, compute):
    n_units = cnt // MOE_UNIT
    _stream_row_blocks(n_units, in_copy(first, MOE_UNIT), out_copy(first, MOE_UNIT), compute(MOE_UNIT))
    done = n_units * MOE_UNIT
    _stream_row_blocks(cnt - done, in_copy(first + done, 1), out_copy(first + done, 1), compute(1))


def _zero_fill_blocks(first, last, zero_src, dst_copy):
    def start(b, carry):
        dst_copy(b).start()
        return carry

    def wait(b, carry):
        dst_copy(b).wait()
        return carry

    zero_src[...] = jnp.zeros_like(zero_src)
    lax.fori_loop(first, last, start, 0)
    lax.fori_loop(first, last, wait, 0)


def _moe_up_kernel(bs_ref, bc_ref, used_ref, x_hbm, w_hbm, bg_ref, bu_ref, act_hbm,
                   w_scr, xbuf, lhs_scr, obuf, sem_w, sem_x, sem_o,
                   *, nt, tn, d_ff, pitch, n_blocks_max):
    g = pl.program_id(0)
    ng = pl.num_programs(0)
    e = g // nt
    col = pl.multiple_of((g % nt) * tn, tn)
    blk_rows = MOE_TM * pitch
    wslot = g % 2

    def w_copies(step, slot):
        ee = step // nt
        cc = pl.multiple_of((step % nt) * tn, tn)
        return (pltpu.make_async_copy(w_hbm.at[ee, :, pl.ds(cc, tn)], w_scr.at[slot, 0], sem_w.at[slot, 0]),
                pltpu.make_async_copy(w_hbm.at[ee, :, pl.ds(d_ff + cc, tn)], w_scr.at[slot, 1],
                                      sem_w.at[slot, 1]))

    @pl.when(g == 0)
    def _():
        for c in w_copies(0, 0):
            c.start(priority=1)

    @pl.when(g + 1 < ng)
    def _():
        for c in w_copies(g + 1, 1 - wslot):
            c.start(priority=1)

    for c in w_copies(g, wslot):
        c.wait()

    start = bs_ref[e]

    def x_copy(first, nblk):
        def make(u, slot):
            r0 = pl.multiple_of((first + u * nblk) * blk_rows, blk_rows)
            return pltpu.make_async_copy(x_hbm.at[pl.ds(r0, nblk * blk_rows), :],
                                         xbuf.at[slot, pl.ds(0, nblk * blk_rows), :], sem_x.at[slot])
        return make

    def o_copy(first, nblk):
        def make(u, slot):
            r0 = pl.multiple_of((first + u * nblk) * MOE_TM, MOE_TM)
            return pltpu.make_async_copy(obuf.at[slot, pl.ds(0, nblk * MOE_TM), :],
                                         act_hbm.at[pl.ds(r0, nblk * MOE_TM), pl.ds(col, tn)], sem_o.at[slot])
        return make

    def compute(nblk):
        rows = nblk * MOE_TM

        def run(slot):
            half = pitch * LANES
            for c in range(pitch):
                w = xbuf[slot, pl.ds(c, rows, stride=pitch), :]
                lhs_scr[0:rows, c * LANES:(c + 1) * LANES] = _unpack_lo(w)
                lhs_scr[0:rows, half + c * LANES:half + (c + 1) * LANES] = _unpack_hi(w)
            x = lhs_scr[0:rows, :]
            gate = jnp.minimum(_dot(x, w_scr[wslot, 0]) + bg_ref[...], SWIGLU_LIMIT)
            up = jnp.clip(_dot(x, w_scr[wslot, 1]) + bu_ref[...], -SWIGLU_LIMIT, SWIGLU_LIMIT)
            obuf[slot, 0:rows, :] = (gate * jax.nn.sigmoid(SWIGLU_ALPHA * gate) * (up + 1.0)).astype(obuf.dtype)
        return run

    _stream_expert_rows(start, bc_ref[e], x_copy, o_copy, compute)

    @pl.when(e == N_EXPERTS - 1)
    def _():
        zero_blk = obuf.at[0, pl.ds(0, MOE_TM), :]

        def tail_copy(b):
            r0 = pl.multiple_of(b * MOE_TM, MOE_TM)
            return pltpu.make_async_copy(zero_blk, act_hbm.at[pl.ds(r0, MOE_TM), pl.ds(col, tn)], sem_o.at[0])
        _zero_fill_blocks(used_ref[0], n_blocks_max, zero_blk, tail_copy)


def _moe_up(x_rows_rm, w_up, b_up, blk_start, blk_count, n_used, n_blocks_max, tn=1024):
    n_exp, d, two_ff = w_up.shape
    d_ff = two_ff // 2
    pitch = d // 2 // LANES
    nt = d_ff // tn
    up_off = d_ff // tn
    grid_spec = pltpu.PrefetchScalarGridSpec(
        num_scalar_prefetch=3,
        grid=(n_exp * nt,),
        in_specs=[
            pl.BlockSpec(memory_space=pl.ANY),
            pl.BlockSpec(memory_space=pl.ANY),
            pl.BlockSpec((None, 1, tn), lambda g, *_: (g // nt, 0, g % nt)),
            pl.BlockSpec((None, 1, tn), lambda g, *_: (g // nt, 0, g % nt + up_off)),
        ],
        out_specs=pl.BlockSpec(memory_space=pl.ANY),
        scratch_shapes=[
            pltpu.VMEM((2, 2, d, tn), F32),
            pltpu.VMEM((MOE_NBUF, MOE_UNIT * MOE_TM * pitch, LANES), jnp.uint32),
            pltpu.VMEM((MOE_UNIT * MOE_TM, d), F32),
            pltpu.VMEM((MOE_NBUF, MOE_UNIT * MOE_TM, tn), BF16),
            pltpu.SemaphoreType.DMA((2, 2)),
            pltpu.SemaphoreType.DMA((MOE_NBUF,)),
            pltpu.SemaphoreType.DMA((MOE_NBUF,)),
        ],
    )
    return pl.pallas_call(
        functools.partial(_moe_up_kernel, nt=nt, tn=tn, d_ff=d_ff, pitch=pitch, n_blocks_max=n_blocks_max),
        grid_spec=grid_spec,
        out_shape=jax.ShapeDtypeStruct((n_blocks_max * MOE_TM, d_ff), BF16),
        compiler_params=_cparams(("arbitrary",)),
        name="moe_up",
    )(blk_start, blk_count, n_used, x_rows_rm, w_up, b_up, b_up)


def _moe_down_kernel(bs_ref, bc_ref, used_ref, a_hbm, w_hbm, b_ref, y_hbm,
                     w_scr, abuf, obuf, sem_w, sem_a, sem_o, *, pitch, n_blocks_max):
    e = pl.program_id(0)
    blk_rows = MOE_TM * pitch
    wslot = e % 2

    def w_copy(ee, slot):
        return pltpu.make_async_copy(w_hbm.at[ee], w_scr.at[slot], sem_w.at[slot])

    @pl.when(e == 0)
    def _():
        w_copy(0, 0).start(priority=1)

    @pl.when(e + 1 < pl.num_programs(0))
    def _():
        w_copy(e + 1, 1 - wslot).start(priority=1)

    w_copy(e, wslot).wait()
    start = bs_ref[e]

    def a_copy(first, nblk):
        def make(u, slot):
            r0 = pl.multiple_of((first + u * nblk) * MOE_TM, MOE_TM)
            return pltpu.make_async_copy(a_hbm.at[pl.ds(r0, nblk * MOE_TM), :],
                                         abuf.at[slot, pl.ds(0, nblk * MOE_TM), :], sem_a.at[slot])
        return make

    def o_copy(first, nblk):
        def make(u, slot):
            r0 = pl.multiple_of((first + u * nblk) * blk_rows, blk_rows)
            return pltpu.make_async_copy(obuf.at[slot, pl.ds(0, nblk * blk_rows), :],
                                         y_hbm.at[pl.ds(r0, nblk * blk_rows), :], sem_o.at[slot])
        return make

    def compute(nblk):
        rows = nblk * MOE_TM

        def run(slot):
            y = _dot(abuf[slot, 0:rows, :].astype(F32), w_scr[wslot]) + b_ref[...]
            _store_rowmajor(obuf.at[slot], _pack_bf16_pairs(y))
        return run

    _stream_expert_rows(start, bc_ref[e], a_copy, o_copy, compute)

    @pl.when(e == N_EXPERTS - 1)
    def _():
        zero_blk = obuf.at[0, pl.ds(0, blk_rows), :]

        def tail_copy(b):
            r0 = pl.multiple_of(b * blk_rows, blk_rows)
            return pltpu.make_async_copy(zero_blk, y_hbm.at[pl.ds(r0, blk_rows), :], sem_o.at[0])
        _zero_fill_blocks(used_ref[0], n_blocks_max, zero_blk, tail_copy)


def _moe_down(act, w_down, b_down, blk_start, blk_count, n_used, n_blocks_max):
    n_rows, d_ff = act.shape
    n_exp, _, d = w_down.shape
    pitch = d // 2 // LANES
    grid_spec = pltpu.PrefetchScalarGridSpec(
        num_scalar_prefetch=3,
        grid=(n_exp,),
        in_specs=[
            pl.BlockSpec(memory_space=pl.ANY),
            pl.BlockSpec(memory_space=pl.ANY),
            pl.BlockSpec((None, 1, d), lambda e, *_: (e, 0, 0)),
        ],
        out_specs=pl.BlockSpec(memory_space=pl.ANY),
        scratch_shapes=[
            pltpu.VMEM((2, d_ff, d), F32),
            pltpu.VMEM((MOE_NBUF, MOE_UNIT * MOE_TM, d_ff), BF16),
            pltpu.VMEM((MOE_NBUF, MOE_UNIT * MOE_TM * pitch, LANES), jnp.uint32),
            pltpu.SemaphoreType.DMA((2,)),
            pltpu.SemaphoreType.DMA((MOE_NBUF,)),
            pltpu.SemaphoreType.DMA((MOE_NBUF,)),
        ],
    )
    return pl.pallas_call(
        functools.partial(_moe_down_kernel, pitch=pitch, n_blocks_max=n_blocks_max),
        grid_spec=grid_spec,
        out_shape=jax.ShapeDtypeStruct((n_rows * pitch, LANES), jnp.uint32),
        compiler_params=_cparams(("arbitrary",)),
        name="moe_down",
    )(blk_start, blk_count, n_used, act, w_down, b_down)


def _combine_kernel(pos0_ref, posn_ref, y_hbm, x2_ref, gate_ref, g_ref, o_ref, buf_even, buf_odd, sem,
                    *, tb, pitch):
    i = pl.program_id(0)
    last = pl.num_programs(0) - 1
    n_rows = TOP_K * tb

    def row_copy(p_ref, r, buf, sem_slot):
        src = y_hbm.at[pl.ds(pl.multiple_of(p_ref[0, r] * pitch, pitch), pitch), :]
        return pltpu.make_async_copy(src, buf.at[pl.ds(r * pitch, pitch), :], sem.at[sem_slot])

    def wait_block(buf, sem_slot):
        pltpu.make_async_copy(y_hbm.at[pl.ds(0, n_rows * pitch), :], buf, sem.at[sem_slot]).wait()

    @pl.when(i == 0)
    def _():
        def issue(j, carry):
            for half in range(2):
                row_copy(pos0_ref, 2 * j + half, buf_even, 0).start(priority=half)
            return carry
        lax.fori_loop(0, n_rows // 2, issue, 0, unroll=4)

    def step(buf_cur, sem_cur, buf_nxt, sem_nxt):
        wait_block(buf_cur, sem_cur)
        for r in range(n_rows):
            row_copy(posn_ref, r, buf_nxt, sem_nxt).start(priority=r % 2)
        gates = gate_ref[...]
        half = pitch * LANES
        ssq = jnp.zeros((tb, 1), F32)
        for c in range(pitch):
            lo = slice(c * LANES, (c + 1) * LANES)
            hi = slice(half + c * LANES, half + (c + 1) * LANES)
            z_lo = x2_ref[:, lo]
            z_hi = x2_ref[:, hi]
            for k in range(TOP_K):
                w = buf_cur[pl.ds(k * tb * pitch + c, tb, stride=pitch), :]
                z_lo = z_lo + gates[:, k:k + 1] * _unpack_lo(w)
                z_hi = z_hi + gates[:, k:k + 1] * _unpack_hi(w)
            o_ref[:, lo] = z_lo
            o_ref[:, hi] = z_hi
            ssq = ssq + (jnp.sum(z_lo * z_lo, axis=-1, keepdims=True)
                         + jnp.sum(z_hi * z_hi, axis=-1, keepdims=True))
        o_ref[...] = o_ref[...] * lax.rsqrt(ssq * (1.0 / (2 * half)) + EPS) * g_ref[...]

        @pl.when(i == last)
        def _():
            wait_block(buf_nxt, sem_nxt)

    @pl.when(i % 2 == 0)
    def _():
        step(buf_even, 0, buf_odd, 1)

    @pl.when(i % 2 == 1)
    def _():
        step(buf_odd, 1, buf_even, 0)


def _combine(y_rows_rm, pos, gates, x2, g_final, tb=256):
    n, d = x2.shape
    nb = n // tb
    pitch = d // 2 // LANES
    pos_blk = pos.reshape(nb, tb, TOP_K).transpose(0, 2, 1).reshape(nb, 1, TOP_K * tb)
    pos_spec = lambda imap: pl.BlockSpec((None, 1, TOP_K * tb), imap, memory_space=pltpu.SMEM)
    return pl.pallas_call(
        functools.partial(_combine_kernel, tb=tb, pitch=pitch),
        grid=(nb,),
        in_specs=[
            pos_spec(lambda i: (0, 0, 0)),
            pos_spec(lambda i: (jnp.minimum(i + 1, nb - 1), 0, 0)),
            pl.BlockSpec(memory_space=pl.ANY),
            pl.BlockSpec((tb, d), lambda i: (i, 0)),
            pl.BlockSpec((tb, LANES), lambda i: (i, 0)),
            pl.BlockSpec((1, d), lambda i: (0, 0)),
        ],
        out_specs=pl.BlockSpec((tb, d), lambda i: (i, 0)),
        out_shape=jax.ShapeDtypeStruct((n, d), F32),
        scratch_shapes=[pltpu.VMEM((TOP_K * tb * pitch, LANES), jnp.uint32),
                        pltpu.VMEM((TOP_K * tb * pitch, LANES), jnp.uint32),
                        pltpu.SemaphoreType.DMA((2,))],
        compiler_params=_cparams(("arbitrary",)),
        name="combine",
    )(pos_blk, pos_blk, y_rows_rm, x2, gates, g_final)


def _routing_tables(idx, rank, cnt):
    counts = cnt[0, :N_EXPERTS]
    blk_count = (counts + MOE_TM - 1) // MOE_TM
    blk_end = jnp.cumsum(blk_count)
    blk_start = blk_end - blk_count
    hot = idx[:, :TOP_K, None] == jnp.arange(N_EXPERTS, dtype=jnp.int32)
    pos = jnp.sum(jnp.where(hot, blk_start * MOE_TM, 0), axis=-1) + rank[:, :TOP_K]
    return pos.astype(jnp.int32), blk_start.astype(jnp.int32), blk_count.astype(jnp.int32)


def kernel(x, mem, g_attn_norm, g_mem_norm, w_in, w_gla_a2, b_gla_a, g_gla_out, b_fox_f, g_fox_out,
           w_mem_kv, g_mem_out, w_out, g_ffn_norm, w_router, b_router, w_moe_up, b_moe_up,
           w_moe_down, b_moe_down, g_final):
    batch, seq, d = x.shape
    n_mem = mem.shape[1]
    n = batch * seq
    depth = w_in.shape[0]
    assert depth == 1, "the combine kernel applies the final norm, so exactly one layer is supported"
    kw = GLA_HEADS * GLA_DK
    vw = GLA_HEADS * GLA_DV
    fw = FOX_HEADS * FOX_DH
    mw = MEM_HEADS * MEM_DH
    o_q, o_k, o_v, o_g = 0, kw, 2 * kw, 2 * kw + vw
    o_a = o_g + vw
    o_fq = o_a + GLA_LOWRANK
    o_fk, o_fv = o_fq + fw, o_fq + 2 * fw
    o_ff = o_fq + 3 * fw
    o_mq = o_ff + FOX_HEADS
    f_lane = GLA_LOWRANK

    xf = x.reshape(n, d)
    for l in range(depth):
        fox_q_scale = FOX_DH ** -0.5 * LOG2E
        col_scale = jnp.ones((w_in.shape[2],), F32).at[o_fq:o_fk].set(fox_q_scale)
        wt = (jnp.transpose(w_in[l]) * col_scale[:, None]).astype(BF16)
        wa_t = jnp.concatenate([wt[o_q:o_a], wt[o_fq:o_ff], wt[o_mq:o_mq + mw]], axis=0)
        wb_t = jnp.zeros((LANES, d), BF16).at[:GLA_LOWRANK].set(wt[o_a:o_fq])
        wb_t = wb_t.at[f_lane:f_lane + FOX_HEADS].set(wt[o_ff:o_mq])
        proj, small = _in_proj(xf, g_attn_norm[l].reshape(1, d), wa_t, wb_t)

        wa2p = jnp.zeros((LANES, kw), F32).at[:GLA_LOWRANK].set(w_gla_a2[l]).astype(BF16)
        bfv = jnp.zeros((1, LANES), F32).at[0, f_lane:f_lane + FOX_HEADS].set(b_fox_f[l])
        gla, f_cum = _gla(proj, small, wa2p, b_gla_a[l].reshape(1, kw), bfv,
                          g_gla_out[l].reshape(1, vw), batch, seq)

        f_t = f_cum.reshape(batch, seq, LANES)[:, :, f_lane:f_lane + FOX_HEADS].transpose(0, 2, 1)
        f_t = jnp.concatenate([f_t, jnp.zeros_like(f_t)], axis=1)
        fox = _fox(proj, f_t, g_fox_out[l].reshape(1, fw), batch, seq)

        kv = _mem_kv(mem.reshape(batch * n_mem, d), g_mem_norm[l].reshape(1, d), w_mem_kv[l].astype(BF16))
        memo = _mem_attn(proj, kv, g_mem_out[l].reshape(1, mw), batch, seq, n_mem)

        wr = jnp.zeros((d, LANES), F32).at[:, :N_EXPERTS].set(w_router[l])
        wr_hi = wr.astype(BF16)
        wr = jnp.concatenate([wr_hi, (wr - wr_hi.astype(F32)).astype(BF16)], axis=1)
        br =jnp.zeros((1, LANES), F32).at[0, :N_EXPERTS].set(b_router[l])
        x2, xn, logits = _out_proj(xf, gla, fox, memo, w_out[l].astype(BF16),
                                   g_ffn_norm[l].reshape(1, d), wr, br)

        idx, gates, rank, cnt = _route(logits)
        pos, blk_start, blk_count = _routing_tables(idx, rank, cnt)
        n_blocks_max = -(-(n * TOP_K + N_EXPERTS * (MOE_TM - 1)) // MOE_TM)
        blk_end = blk_start + blk_count
        n_used = blk_end[-1:]
        x_rows = _dispatch(xn, pos, blk_end - 1, (blk_count > 0).astype(jnp.int32), n_used,
                           n_blocks_max, d // 2 // LANES)
        d_ff = w_moe_up.shape[3] // 2
        act = _moe_up(x_rows, w_moe_up[l], b_moe_up[l].reshape(N_EXPERTS, 1, 2 * d_ff),
                      blk_start, blk_count, n_used, n_blocks_max)
        y_rows = _moe_down(act, w_moe_down[l], b_moe_down[l].reshape(N_EXPERTS, 1, d),
                           blk_start, blk_count, n_used, n_blocks_max)
        xf = _combine(y_rows, pos, gates, x2, g_final.reshape(1, d))
    return xf.reshape(batch, seq, d)
```

```python
import functools

import jax
import jax.numpy as jnp
from jax import lax
from jax.experimental import pallas as pl
from jax.experimental.pallas import tpu as pltpu

EPS = 1e-5
CHUNK = 64
GLA_HEADS = 4
GLA_DK = 128
GLA_DV = 256
GLA_LOWRANK = 16
GLA_TAU = 16.0
FOX_HEADS = 4
FOX_DH = 128
MEM_HEADS = 4
MEM_DH = 128
N_EXPERTS = 32
TOP_K = 4
SWIGLU_LIMIT = 7.0
SWIGLU_ALPHA = 1.702
LANES = 128
MXU_COLS = 256
MOE_TM = 256
MOE_NBUF = 2
MOE_UNIT = 2
VMEM_LIMIT = 56 * 1024 * 1024

GLA_KW = GLA_HEADS * GLA_DK
GLA_VW = GLA_HEADS * GLA_DV
FOX_W = FOX_HEADS * FOX_DH
MEM_W = MEM_HEADS * MEM_DH
PROJ_GLA_Q = 0
PROJ_GLA_K = PROJ_GLA_Q + GLA_KW
PROJ_GLA_V = PROJ_GLA_K + GLA_KW
PROJ_GLA_GATE = PROJ_GLA_V + GLA_VW
PROJ_FOX_Q = PROJ_GLA_GATE + GLA_VW
PROJ_FOX_K = PROJ_FOX_Q + FOX_W
PROJ_FOX_V = PROJ_FOX_K + FOX_W
PROJ_MEM_Q = PROJ_FOX_V + FOX_W

F32 = jnp.float32
BF16 = jnp.bfloat16


def _cparams(sem, vmem=VMEM_LIMIT):
    return pltpu.CompilerParams(dimension_semantics=sem, vmem_limit_bytes=vmem)


def _log_sigmoid(x):
    return jnp.minimum(x, 0.0) - jnp.log1p(jnp.exp(-jnp.abs(x)))


def _rms(x, g):
    return x * lax.rsqrt(jnp.mean(x * x, axis=-1, keepdims=True) + EPS) * g


def _dot(a, b, **kw):
    return jnp.dot(a, b, preferred_element_type=F32, **kw)


def _dot_nt(a, b):
    return lax.dot_general(a, b, (((1,), (1,)), ((), ())), preferred_element_type=F32)


def _dot_tn(a, b):
    return lax.dot_general(a, b, (((0,), (0,)), ((), ())), preferred_element_type=F32)


def _store_rowmajor(ref, x, base=0):
    rows, w = x.shape
    pitch = w // LANES
    for c in range(pitch):
        ref[pl.ds(base + c, rows, stride=pitch), :] = x[:, c * LANES:(c + 1) * LANES]


def _pack_bf16_pairs(x):
    half = x.shape[1] // 2
    bits = lambda v: lax.bitcast_convert_type(v.astype(BF16).astype(F32), jnp.uint32)
    return bits(x[:, half:]) | (bits(x[:, :half]) >> 16)


def _unpack_lo(w):
    return lax.bitcast_convert_type(w << 16, F32)


def _unpack_hi(w):
    return lax.bitcast_convert_type(w & jnp.uint32(0xFFFF0000), F32)


def _in_proj_kernel(x_ref, g_ref, wa_ref, wb_ref, proj_ref, small_ref, h_scr):
    @pl.when(pl.program_id(1) == 0)
    def _():
        hb = _rms(x_ref[...], g_ref[...]).astype(BF16)
        h_scr[...] = hb
        small_ref[...] = _dot_nt(hb, wb_ref[...])

    proj_ref[...] = _dot_nt(h_scr[...], wa_ref[...]).astype(proj_ref.dtype)


def _in_proj(x2d, g, wa_t, wb_t, tm=1024, tn=1280):
    n, d = x2d.shape
    na = wa_t.shape[0]
    return pl.pallas_call(
        _in_proj_kernel,
        grid=(n // tm, na // tn),
        in_specs=[
            pl.BlockSpec((tm, d), lambda i, j: (i, 0)),
            pl.BlockSpec((1, d), lambda i, j: (0, 0)),
            pl.BlockSpec((tn, d), lambda i, j: (j, 0)),
            pl.BlockSpec((LANES, d), lambda i, j: (0, 0)),
        ],
        out_specs=[
            pl.BlockSpec((tm, tn), lambda i, j: (i, j)),
            pl.BlockSpec((tm, LANES), lambda i, j: (i, 0)),
        ],
        out_shape=[
            jax.ShapeDtypeStruct((n, na), BF16),
            jax.ShapeDtypeStruct((n, LANES), F32),
        ],
        scratch_shapes=[pltpu.VMEM((tm, d), BF16)],
        compiler_params=_cparams(("parallel", "arbitrary")),
        name="in_proj",
    )(x2d, g, wa_t, wb_t)


def _gla_kernel(q_ref, k_ref, v_ref, gate_ref, small_ref, wa2_ref, ba_ref, bf_ref, gout_ref,
                o_ref, f_ref, state_scr, fcar_scr, la_scr, lf_scr, *, n_chunks):
    @pl.when(pl.program_id(1) == 0)
    def _():
        state_scr[...] = jnp.zeros_like(state_scr)
        fcar_scr[...] = jnp.zeros_like(fcar_scr)

    small = small_ref[...]
    la_scr[...] = _log_sigmoid(_dot(small.astype(BF16), wa2_ref[...]) + ba_ref[...]) * (1.0 / GLA_TAU)
    lf_scr[...] = _log_sigmoid(small + bf_ref[...])
    row = lax.broadcasted_iota(jnp.int32, (CHUNK, CHUNK), 0)
    col = lax.broadcasted_iota(jnp.int32, (CHUNK, CHUNK), 1)
    tri = (col <= row).astype(F32)
    scale = GLA_DK ** -0.5

    def chunk_body(c, carry):
        r = pl.ds(pl.multiple_of(c * CHUNK, CHUNK), CHUNK)
        b = _dot(tri, la_scr[r, :], precision=lax.Precision.HIGHEST)
        b_end = b[CHUNK - 1:CHUNK, :]
        k_dec = k_ref[r, :].astype(F32) * jnp.exp(b_end - b)
        decay = jnp.exp(b_end)
        f_cum = _dot(tri, lf_scr[r, :], precision=lax.Precision.HIGHEST) + fcar_scr[...]
        f_ref[r, :] = f_cum
        fcar_scr[...] = f_cum[CHUNK - 1:CHUNK, :]
        heads = range(GLA_HEADS)
        ks = [slice(h * GLA_DK, (h + 1) * GLA_DK) for h in heads]
        vs = [slice(h * GLA_DV, (h + 1) * GLA_DV) for h in heads]
        old = [state_scr[h] for h in heads]
        new = [old[h] * decay[:, ks[h]] + _dot_tn(v_ref[r, vs[h]], k_dec[:, ks[h]].astype(BF16))
               for h in heads]
        for h in heads:
            state_scr[h] = new[h]
        for h in heads:
            o = _dot_nt(q_ref[r, ks[h]], new[h].astype(BF16)) * scale
            gt = gate_ref[r, vs[h]].astype(F32)
            o_ref[r, vs[h]] = (_rms(o, gout_ref[:, vs[h]]) * (gt * jax.nn.sigmoid(gt))).astype(o_ref.dtype)
        return carry

    lax.fori_loop(0, n_chunks, chunk_body, 0, unroll=True)


def _gla(proj, small, wa2p, ba, bfv, gout, batch, seq, ts=512):
    n = proj.shape[0]
    nsb = seq // ts
    kw, vw = GLA_KW, GLA_VW
    row = lambda b, s: b * nsb + s
    return pl.pallas_call(
        functools.partial(_gla_kernel, n_chunks=ts // CHUNK),
        grid=(batch, nsb),
        in_specs=[
            pl.BlockSpec((ts, kw), lambda b, s: (row(b, s), PROJ_GLA_Q // kw)),
            pl.BlockSpec((ts, kw), lambda b, s: (row(b, s), PROJ_GLA_K // kw)),
            pl.BlockSpec((ts, vw), lambda b, s: (row(b, s), PROJ_GLA_V // vw)),
            pl.BlockSpec((ts, vw), lambda b, s: (row(b, s), PROJ_GLA_GATE // vw)),
            pl.BlockSpec((ts, LANES), lambda b, s: (row(b, s), 0)),
            pl.BlockSpec((LANES, kw), lambda b, s: (0, 0)),
            pl.BlockSpec((1, kw), lambda b, s: (0, 0)),
            pl.BlockSpec((1, LANES), lambda b, s: (0, 0)),
            pl.BlockSpec((1, vw), lambda b, s: (0, 0)),
        ],
        out_specs=[
            pl.BlockSpec((ts, vw), lambda b, s: (row(b, s), 0)),
            pl.BlockSpec((ts, LANES), lambda b, s: (row(b, s), 0)),
        ],
        out_shape=[
            jax.ShapeDtypeStruct((n, vw), BF16),
            jax.ShapeDtypeStruct((n, LANES), F32),
        ],
        scratch_shapes=[
            pltpu.VMEM((GLA_HEADS, GLA_DV, GLA_DK), F32),
            pltpu.VMEM((1, LANES), F32),
            pltpu.VMEM((ts, kw), F32),
            pltpu.VMEM((ts, LANES), F32),
        ],
        compiler_params=_cparams(("parallel", "arbitrary")),
        name="gla",
    )(proj, proj, proj, proj, small, wa2p, ba, bfv, gout)


LOG2E = 1.4426950408889634
def _fox_kernel(q_ref, k_ref, v_ref, fk_ref, g_ref, o_ref, m_scr, l_scr, acc_scr, s_even, s_odd,
                *, tq, tk):
    i = pl.program_id(1)
    t = pl.program_id(2)

    @pl.when(t == 0)
    def _():
        m_scr[...] = jnp.full_like(m_scr, -jnp.inf)
        l_scr[...] = jnp.zeros_like(l_scr)
        acc_scr[...] = jnp.zeros_like(acc_scr)

    heads = range(FOX_HEADS)
    hs = [slice(h * FOX_DH, (h + 1) * FOX_DH) for h in heads]

    def score(s_out):
        for h in heads:
            s_out[h] = _dot_nt(q_ref[:, hs[h]], k_ref[:, hs[h]])

    def update(s_in, on_diagonal):
        m_prev = [m_scr[h] for h in heads]
        l_prev = [l_scr[h] for h in heads]
        acc_prev = [acc_scr[h] for h in heads]
        s = [s_in[h] - fk_ref[h:h + 1, :] * LOG2E for h in heads]
        if on_diagonal:
            row = lax.broadcasted_iota(jnp.int32, (tq, tk), 0)
            col = lax.broadcasted_iota(jnp.int32, (tq, tk), 1)
            s = [jnp.where(col <= row, sh, -jnp.inf) for sh in s]
        m_new = [jnp.maximum(m_prev[h], jnp.max(s[h], axis=-1, keepdims=True)) for h in heads]
        alpha = [jnp.exp2(m_prev[h] - m_new[h]) for h in heads]
        p = [jnp.exp2(s[h] - m_new[h]) for h in heads]
        l_new = [alpha[h] * l_prev[h] + jnp.sum(p[h], axis=-1, keepdims=True) for h in heads]
        acc_new = [alpha[h] * acc_prev[h] + _dot(p[h].astype(BF16), v_ref[:, hs[h]]) for h in heads]
        return m_new, l_new, acc_new

    def step(s_write, s_read):
        @pl.when(t == 0)
        def _():
            score(s_write)

        @pl.when(jnp.logical_and(t >= 1, t <= i))
        def _():
            m_new, l_new, acc_new = update(s_read, False)
            score(s_write)
            for h in heads:
                m_scr[h] = m_new[h]
                l_scr[h] = l_new[h]
                acc_scr[h] = acc_new[h]

        @pl.when(t == i + 1)
        def _():
            _, l_new, acc_new = update(s_read, True)
            for h in heads:
                o_ref[:, hs[h]] = _rms(acc_new[h] / l_new[h], g_ref[:, hs[h]]).astype(o_ref.dtype)

    @pl.when(t % 2 == 0)
    def _():
        step(s_even, s_odd)

    @pl.when(t % 2 == 1)
    def _():
        step(s_odd, s_even)


def _fox(proj, f_t, g_fox, batch, seq, tq=512, tk=512):
    assert tq == tk, "the diagonal-block mask assumes square blocks"
    n = proj.shape[0]
    w = FOX_HEADS * FOX_DH
    nq, nk = seq // tq, seq // tk
    qcol, kcol, vcol = PROJ_FOX_Q // w, PROJ_FOX_K // w, PROJ_FOX_V // w
    return pl.pallas_call(
        functools.partial(_fox_kernel, tq=tq, tk=tk),
        grid=(batch, nq, nk + 1),
        in_specs=[
            pl.BlockSpec((tq, w), lambda b, i, t: (b * nq + i, qcol)),
            pl.BlockSpec((tk, w), lambda b, i, t: (b * nk + jnp.minimum(t, i), kcol)),
            pl.BlockSpec((tk, w), lambda b, i, t: (b * nk + jnp.clip(t - 1, 0, i), vcol)),
            pl.BlockSpec((None, 8, tk), lambda b, i, t: (b, 0, jnp.clip(t - 1, 0, i))),
            pl.BlockSpec((1, w), lambda b, i, t: (0, 0)),
        ],
        out_specs=pl.BlockSpec((tq, w), lambda b, i, t: (b * nq + i, 0)),
        out_shape=jax.ShapeDtypeStruct((n, w), BF16),
        scratch_shapes=[
            pltpu.VMEM((FOX_HEADS, tq, 1), F32),
            pltpu.VMEM((FOX_HEADS, tq, 1), F32),
            pltpu.VMEM((FOX_HEADS, tq, FOX_DH), F32),
            pltpu.VMEM((FOX_HEADS, tq, tk), F32),
            pltpu.VMEM((FOX_HEADS, tq, tk), F32),
        ],
        compiler_params=_cparams(("parallel", "parallel", "arbitrary")),
        name="fox",
    )(proj, proj, proj, f_t, g_fox)


def _mem_kv_kernel(m_ref, g_ref, w_ref, kv_ref):
    kv_ref[...] = _dot(_rms(m_ref[...], g_ref[...]).astype(BF16), w_ref[...]).astype(kv_ref.dtype)


def _mem_kv(mem2d, g, w, tm=256):
    n, d = mem2d.shape
    nw = w.shape[1]
    return pl.pallas_call(
        _mem_kv_kernel,
        grid=(n // tm,),
        in_specs=[
            pl.BlockSpec((tm, d), lambda i: (i, 0)),
            pl.BlockSpec((1, d), lambda i: (0, 0)),
            pl.BlockSpec((d, nw), lambda i: (0, 0)),
        ],
        out_specs=pl.BlockSpec((tm, nw), lambda i: (i, 0)),
        out_shape=jax.ShapeDtypeStruct((n, nw), BF16),
        compiler_params=_cparams(("parallel",)),
        name="mem_kv",
    )(mem2d, g, w)


def _mem_attn_kernel(q_ref, k_ref, v_ref, g_ref, o_ref):
    scale = MEM_DH ** -0.5
    for h in range(MEM_HEADS):
        hs = slice(h * MEM_DH, (h + 1) * MEM_DH)
        s = _dot_nt(q_ref[:, hs], k_ref[:, hs]) * scale
        p = jnp.exp(s - jnp.max(s, axis=-1, keepdims=True))
        l = jnp.sum(p, axis=-1, keepdims=True)
        o = _dot((p / l).astype(BF16), v_ref[:, hs])
        o_ref[:, hs] = _rms(o, g_ref[:, hs]).astype(o_ref.dtype)


def _mem_attn(proj, kv, g_mem_out, batch, seq, n_mem, tq=1024):
    n = proj.shape[0]
    w = MEM_HEADS * MEM_DH
    nq = seq // tq
    qcol = PROJ_MEM_Q // w
    return pl.pallas_call(
        _mem_attn_kernel,
        grid=(n // tq,),
        in_specs=[
            pl.BlockSpec((tq, w), lambda i: (i, qcol)),
            pl.BlockSpec((n_mem, w), lambda i: (i // nq, 0)),
            pl.BlockSpec((n_mem, w), lambda i: (i // nq, 1)),
            pl.BlockSpec((1, w), lambda i: (0, 0)),
        ],
        out_specs=pl.BlockSpec((tq, w), lambda i: (i, 0)),
        out_shape=jax.ShapeDtypeStruct((n, w), BF16),
        compiler_params=_cparams(("parallel",)),
        name="mem_attn",
    )(proj, kv, kv, g_mem_out)


def _out_proj_kernel(x_ref, gla_ref, fox_ref, mem_ref, w1_ref, w2_ref, w3_ref, g_ref, wr_ref, br_ref,
                     x2_ref, xn_ref, logit_ref):
    x2 = (x_ref[...] + _dot(gla_ref[...], w1_ref[...]) + _dot(fox_ref[...], w2_ref[...])
          + _dot(mem_ref[...], w3_ref[...]))
    x2_ref[...] = x2
    xn = _rms(x2, g_ref[...])
    _store_rowmajor(xn_ref, _pack_bf16_pairs(xn))
    xh = xn.astype(BF16)
    xl = (xn - xh.astype(F32)).astype(BF16)
    hi = _dot(xh, wr_ref[...])
    logit_ref[...] = hi[:, :LANES] + hi[:, LANES:] + _dot(xl, wr_ref[:, :LANES]) + br_ref[...]


def _out_proj(x2d, gla, fox, memo, w_out, g_ffn, wr, br, tm=512):
    n, d = x2d.shape
    w1, w2 = gla.shape[1], fox.shape[1]
    const = lambda i: (0, 0)
    return pl.pallas_call(
        _out_proj_kernel,
        grid=(n // tm,),
        in_specs=[
            pl.BlockSpec((tm, d), lambda i: (i, 0)),
            pl.BlockSpec((tm, w1), lambda i: (i, 0)),
            pl.BlockSpec((tm, w2), lambda i: (i, 0)),
            pl.BlockSpec((tm, w2), lambda i: (i, 0)),
            pl.BlockSpec((w1, d), lambda i: (0, 0)),
            pl.BlockSpec((w2, d), lambda i: (w1 // w2, 0)),
            pl.BlockSpec((w2, d), lambda i: (w1 // w2 + 1, 0)),
            pl.BlockSpec((1, d), const),
            pl.BlockSpec((d, 2 * LANES), const),
            pl.BlockSpec((1, LANES), const),
        ],
        out_specs=[
            pl.BlockSpec((tm, d), lambda i: (i, 0)),
            pl.BlockSpec((tm * (d // 2 // LANES), LANES), lambda i: (i, 0)),
            pl.BlockSpec((tm, LANES), lambda i: (i, 0)),
        ],
        out_shape=[
            jax.ShapeDtypeStruct((n, d), F32),
            jax.ShapeDtypeStruct((n * (d // 2 // LANES), LANES), jnp.uint32),
            jax.ShapeDtypeStruct((n, LANES), F32),
        ],
        compiler_params=_cparams(("parallel",)),
        name="out_proj",
    )(x2d, gla, fox, memo, w_out, w_out, w_out, g_ffn, wr, br)


def _route_kernel(l_ref, idx_ref, gate_ref, rank_ref, cnt_ref, carry_scr, *, tb):
    @pl.when(pl.program_id(0) == 0)
    def _():
        carry_scr[...] = jnp.zeros_like(carry_scr)

    lane = lax.broadcasted_iota(jnp.int32, (tb, LANES), 1)
    logit = jnp.where(lane < N_EXPERTS, l_ref[...], -jnp.inf)
    vals, hots = [], []
    idx_out = jnp.zeros((tb, LANES), jnp.int32)
    for k in range(TOP_K):
        m = jnp.max(logit, axis=-1, keepdims=True)
        ik = jnp.min(jnp.where(logit == m, lane, LANES), axis=-1, keepdims=True)
        hot = lane == ik
        logit = jnp.where(hot, -jnp.inf, logit)
        vals.append(m)
        hots.append(hot)
        idx_out = jnp.where(lane == k, ik, idx_out)
    idx_ref[...] = idx_out

    e = [jnp.exp(v - vals[0]) for v in vals]
    den = e[0] + e[1] + e[2] + e[3]
    gate_out = jnp.zeros((tb, LANES), F32)
    for k in range(TOP_K):
        gate_out = jnp.where(lane == k, e[k] / den, gate_out)
    gate_ref[...] = gate_out

    member = jnp.zeros((tb, LANES), F32)
    for hot in hots:
        member = member + hot.astype(F32)
    row = lax.broadcasted_iota(jnp.int32, (tb, tb), 0)
    col = lax.broadcasted_iota(jnp.int32, (tb, tb), 1)
    before = (col < row).astype(BF16)
    rank = _dot(before, member.astype(BF16)) + carry_scr[...]
    rank_out = jnp.zeros((tb, LANES), F32)
    for k in range(TOP_K):
        rk = jnp.sum(jnp.where(hots[k], rank, 0.0), axis=-1, keepdims=True)
        rank_out = jnp.where(lane == k, rk, rank_out)
    rank_ref[...] = rank_out.astype(jnp.int32)
    total = carry_scr[...] + jnp.sum(member, axis=0, keepdims=True)
    carry_scr[...] = total
    cnt_ref[...] = total.astype(jnp.int32)


def _route(logits, tb=512):
    n = logits.shape[0]
    blk = pl.BlockSpec((tb, LANES), lambda i: (i, 0))
    return pl.pallas_call(
        functools.partial(_route_kernel, tb=tb),
        grid=(n // tb,),
        in_specs=[blk],
        out_specs=[blk, blk, blk, pl.BlockSpec((1, LANES), lambda i: (0, 0))],
        out_shape=[
            jax.ShapeDtypeStruct((n, LANES), jnp.int32),
            jax.ShapeDtypeStruct((n, LANES), F32),
            jax.ShapeDtypeStruct((n, LANES), jnp.int32),
            jax.ShapeDtypeStruct((1, LANES), jnp.int32),
        ],
        scratch_shapes=[pltpu.VMEM((1, LANES), F32)],
        compiler_params=_cparams(("arbitrary",)),
        name="route",
    )(logits)


def _dispatch_kernel(zblk_ref, zok_ref, used_ref, pos_ref, xn_ref, xr_hbm, zero_scr, sem,
                     *, tb, pitch, n_blocks_max):
    blk_rows = MOE_TM * pitch

    def zero_copy(b):
        dst = xr_hbm.at[pl.ds(pl.multiple_of(b * blk_rows, blk_rows), blk_rows), :]
        return pltpu.make_async_copy(zero_scr, dst, sem.at[1])

    @pl.when(pl.program_id(0) == 0)
    def _():
        zero_scr[...] = jnp.zeros_like(zero_scr)

        def per_expert(action):
            def body(e, carry):
                @pl.when(zok_ref[e] == 1)
                def _():
                    action(zero_copy(zblk_ref[e]))
                return carry
            lax.fori_loop(0, N_EXPERTS, body, 0)

        def per_tail(action):
            def body(b, carry):
                action(zero_copy(b))
                return carry
            lax.fori_loop(used_ref[0], n_blocks_max, body, 0)

        per_expert(lambda c: c.start())
        per_tail(lambda c: c.start())
        per_expert(lambda c: c.wait())
        per_tail(lambda c: c.wait())

    def issue(t, carry):
        src = xn_ref.at[pl.ds(pl.multiple_of(t * pitch, pitch), pitch), :]
        for k in range(TOP_K):
            p = pos_ref[0, t * TOP_K + k]
            dst = xr_hbm.at[pl.ds(pl.multiple_of(p * pitch, pitch), pitch), :]
            pltpu.make_async_copy(src, dst, sem.at[0]).start(priority=k % 2)
        return carry

    lax.fori_loop(0, tb, issue, 0, unroll=4)
    for k in range(TOP_K):
        pltpu.make_async_copy(xn_ref, xr_hbm.at[pl.ds(0, tb * pitch), :], sem.at[0]).wait()


def _dispatch(xn_rm, pos, zblk, zok, n_used, n_blocks_max, pitch, tb=512):
    n = xn_rm.shape[0] // pitch
    nb = n // tb
    grid_spec = pltpu.PrefetchScalarGridSpec(
        num_scalar_prefetch=3,
        grid=(nb,),
        in_specs=[
            pl.BlockSpec((None, 1, TOP_K * tb), lambda i, *_: (i, 0, 0), memory_space=pltpu.SMEM),
            pl.BlockSpec((tb * pitch, LANES), lambda i, *_: (i, 0)),
        ],
        out_specs=pl.BlockSpec(memory_space=pl.ANY),
        scratch_shapes=[pltpu.VMEM((MOE_TM * pitch, LANES), xn_rm.dtype), pltpu.SemaphoreType.DMA((2,))],
    )
    return pl.pallas_call(
        functools.partial(_dispatch_kernel, tb=tb, pitch=pitch, n_blocks_max=n_blocks_max),
        grid_spec=grid_spec,
        out_shape=jax.ShapeDtypeStruct((n_blocks_max * MOE_TM * pitch, LANES), xn_rm.dtype),
        compiler_params=_cparams(("arbitrary",)),
        name="dispatch",
    )(zblk, zok, n_used, pos.reshape(nb, 1, TOP_K * tb), xn_rm)


def _stream_row_blocks(cnt, in_copy, out_copy, compute):
    for ahead in range(MOE_NBUF - 1):
        @pl.when(cnt > ahead)
        def _():
            in_copy(ahead, ahead).start()

    def body(b, carry):
        slot = b % MOE_NBUF

        @pl.when(b + MOE_NBUF - 1 < cnt)
        def _():
            in_copy(b + MOE_NBUF - 1, (b + MOE_NBUF - 1) % MOE_NBUF).start()

        in_copy(b, slot).wait()

        @pl.when(b >= MOE_NBUF)
        def _():
            out_copy(b - MOE_NBUF, slot).wait()

        compute(slot)
        out_copy(b, slot).start(priority=1)
        return carry

    lax.fori_loop(0, cnt, body, 0)

    for back in range(MOE_NBUF, 0, -1):
        @pl.when(cnt >= back)
        def _():
            out_copy(cnt - back, (cnt - back) % MOE_NBUF).wait()


def _stream_expert_rows(first, cnt, in_copy, out_copy, compute):
    n_units = cnt // MOE_UNIT
    _stream_row_blocks(n_units, in_copy(first, MOE_UNIT), out_copy(first, MOE_UNIT), compute(MOE_UNIT))
    done = n_units * MOE_UNIT
    _stream_row_blocks(cnt - done, in_copy(first + done, 1), out_copy(first + done, 1), compute(1))


def _zero_fill_blocks(first, last, zero_src, dst_copy):
    def start(b, carry):
        dst_copy(b).start()
        return carry

    def wait(b, carry):
        dst_copy(b).wait()
        return carry

    zero_src[...] = jnp.zeros_like(zero_src)
    lax.fori_loop(first, last, start, 0)
    lax.fori_loop(first, last, wait, 0)


def _moe_up_kernel(bs_ref, bc_ref, used_ref, x_hbm, w_hbm, bg_ref, bu_ref, act_hbm,
                   w_scr, xbuf, lhs_scr, obuf, sem_w, sem_x, sem_o,
                   *, nt, tn, d_ff, pitch, n_blocks_max):
    g = pl.program_id(0)
    ng = pl.num_programs(0)
    e = g // nt
    col = pl.multiple_of((g % nt) * tn, tn)
    blk_rows = MOE_TM * pitch
    wslot = g % 2

    def w_copies(step, slot):
        ee = step // nt
        cc = pl.multiple_of((step % nt) * tn, tn)
        return (pltpu.make_async_copy(w_hbm.at[ee, :, pl.ds(cc, tn)], w_scr.at[slot, 0], sem_w.at[slot, 0]),
                pltpu.make_async_copy(w_hbm.at[ee, :, pl.ds(d_ff + cc, tn)], w_scr.at[slot, 1],
                                      sem_w.at[slot, 1]))

    @pl.when(g == 0)
    def _():
        for c in w_copies(0, 0):
            c.start(priority=1)

    @pl.when(g + 1 < ng)
    def _():
        for c in w_copies(g + 1, 1 - wslot):
            c.start(priority=1)

    for c in w_copies(g, wslot):
        c.wait()

    start = bs_ref[e]

    def x_copy(first, nblk):
        def make(u, slot):
            r0 = pl.multiple_of((first + u * nblk) * blk_rows, blk_rows)
            return pltpu.make_async_copy(x_hbm.at[pl.ds(r0, nblk * blk_rows), :],
                                         xbuf.at[slot, pl.ds(0, nblk * blk_rows), :], sem_x.at[slot])
        return make

    def o_copy(first, nblk):
        def make(u, slot):
            r0 = pl.multiple_of((first + u * nblk) * MOE_TM, MOE_TM)
            return pltpu.make_async_copy(obuf.at[slot, pl.ds(0, nblk * MOE_TM), :],
                                         act_hbm.at[pl.ds(r0, nblk * MOE_TM), pl.ds(col, tn)], sem_o.at[slot])
        return make

    def compute(nblk):
        rows = nblk * MOE_TM

        def run(slot):
            half = pitch * LANES
            for c in range(pitch):
                w = xbuf[slot, pl.ds(c, rows, stride=pitch), :]
                lhs_scr[0:rows, c * LANES:(c + 1) * LANES] = _unpack_lo(w)
                lhs_scr[0:rows, half + c * LANES:half + (c + 1) * LANES] = _unpack_hi(w)
            x = lhs_scr[0:rows, :]
            gate = jnp.minimum(_dot(x, w_scr[wslot, 0]) + bg_ref[...], SWIGLU_LIMIT)
            up = jnp.clip(_dot(x, w_scr[wslot, 1]) + bu_ref[...], -SWIGLU_LIMIT, SWIGLU_LIMIT)
            obuf[slot, 0:rows, :] = (gate * jax.nn.sigmoid(SWIGLU_ALPHA * gate) * (up + 1.0)).astype(obuf.dtype)
        return run

    _stream_expert_rows(start, bc_ref[e], x_copy, o_copy, compute)

    @pl.when(e == N_EXPERTS - 1)
    def _():
        zero_blk = obuf.at[0, pl.ds(0, MOE_TM), :]

        def tail_copy(b):
            r0 = pl.multiple_of(b * MOE_TM, MOE_TM)
            return pltpu.make_async_copy(zero_blk, act_hbm.at[pl.ds(r0, MOE_TM), pl.ds(col, tn)], sem_o.at[0])
        _zero_fill_blocks(used_ref[0], n_blocks_max, zero_blk, tail_copy)


def _moe_up(x_rows_rm, w_up, b_up, blk_start, blk_count, n_used, n_blocks_max, tn=1024):
    n_exp, d, two_ff = w_up.shape
    d_ff = two_ff // 2
    pitch = d // 2 // LANES
    nt = d_ff // tn
    up_off = d_ff // tn
    grid_spec = pltpu.PrefetchScalarGridSpec(
        num_scalar_prefetch=3,
        grid=(n_exp * nt,),
        in_specs=[
            pl.BlockSpec(memory_space=pl.ANY),
            pl.BlockSpec(memory_space=pl.ANY),
            pl.BlockSpec((None, 1, tn), lambda g, *_: (g // nt, 0, g % nt)),
            pl.BlockSpec((None, 1, tn), lambda g, *_: (g // nt, 0, g % nt + up_off)),
        ],
        out_specs=pl.BlockSpec(memory_space=pl.ANY),
        scratch_shapes=[
            pltpu.VMEM((2, 2, d, tn), F32),
            pltpu.VMEM((MOE_NBUF, MOE_UNIT * MOE_TM * pitch, LANES), jnp.uint32),
            pltpu.VMEM((MOE_UNIT * MOE_TM, d), F32),
            pltpu.VMEM((MOE_NBUF, MOE_UNIT * MOE_TM, tn), BF16),
            pltpu.SemaphoreType.DMA((2, 2)),
            pltpu.SemaphoreType.DMA((MOE_NBUF,)),
            pltpu.SemaphoreType.DMA((MOE_NBUF,)),
        ],
    )
    return pl.pallas_call(
        functools.partial(_moe_up_kernel, nt=nt, tn=tn, d_ff=d_ff, pitch=pitch, n_blocks_max=n_blocks_max),
        grid_spec=grid_spec,
        out_shape=jax.ShapeDtypeStruct((n_blocks_max * MOE_TM, d_ff), BF16),
        compiler_params=_cparams(("arbitrary",)),
        name="moe_up",
    )(blk_start, blk_count, n_used, x_rows_rm, w_up, b_up, b_up)


def _moe_down_kernel(bs_ref, bc_ref, used_ref, a_hbm, w_hbm, b_ref, y_hbm,
                     w_scr, abuf, obuf, sem_w, sem_a, sem_o, *, pitch, n_blocks_max):
    e = pl.program_id(0)
    blk_rows = MOE_TM * pitch
    wslot = e % 2

    def w_copy(ee, slot):
        return pltpu.make_async_copy(w_hbm.at[ee], w_scr.at[slot], sem_w.at[slot])

    @pl.when(e == 0)
    def _():
        w_copy(0, 0).start(priority=1)

    @pl.when(e + 1 < pl.num_programs(0))
    def _():
        w_copy(e + 1, 1 - wslot).start(priority=1)

    w_copy(e, wslot).wait()
    start = bs_ref[e]

    def a_copy(first, nblk):
        def make(u, slot):
            r0 = pl.multiple_of((first + u * nblk) * MOE_TM, MOE_TM)
            return pltpu.make_async_copy(a_hbm.at[pl.ds(r0, nblk * MOE_TM), :],
                                         abuf.at[slot, pl.ds(0, nblk * MOE_TM), :], sem_a.at[slot])
        return make

    def o_copy(first, nblk):
        def make(u, slot):
            r0 = pl.multiple_of((first + u * nblk) * blk_rows, blk_rows)
            return pltpu.make_async_copy(obuf.at[slot, pl.ds(0, nblk * blk_rows), :],
                                         y_hbm.at[pl.ds(r0, nblk * blk_rows), :], sem_o.at[slot])
        return make

    def compute(nblk):
        rows = nblk * MOE_TM

        def run(slot):
            y = _dot(abuf[slot, 0:rows, :].astype(F32), w_scr[wslot]) + b_ref[...]
            _store_rowmajor(obuf.at[slot], _pack_bf16_pairs(y))
        return run

    _stream_expert_rows(start, bc_ref[e], a_copy, o_copy, compute)

    @pl.when(e == N_EXPERTS - 1)
    def _():
        zero_blk = obuf.at[0, pl.ds(0, blk_rows), :]

        def tail_copy(b):
            r0 = pl.multiple_of(b * blk_rows, blk_rows)
            return pltpu.make_async_copy(zero_blk, y_hbm.at[pl.ds(r0, blk_rows), :], sem_o.at[0])
        _zero_fill_blocks(used_ref[0], n_blocks_max, zero_blk, tail_copy)


def _moe_down(act, w_down, b_down, blk_start, blk_count, n_used, n_blocks_max):
    n_rows, d_ff = act.shape
    n_exp, _, d = w_down.shape
    pitch = d // 2 // LANES
    grid_spec = pltpu.PrefetchScalarGridSpec(
        num_scalar_prefetch=3,
        grid=(n_exp,),
        in_specs=[
            pl.BlockSpec(memory_space=pl.ANY),
            pl.BlockSpec(memory_space=pl.ANY),
            pl.BlockSpec((None, 1, d), lambda e, *_: (e, 0, 0)),
        ],
        out_specs=pl.BlockSpec(memory_space=pl.ANY),
        scratch_shapes=[
            pltpu.VMEM((2, d_ff, d), F32),
            pltpu.VMEM((MOE_NBUF, MOE_UNIT * MOE_TM, d_ff), BF16),
            pltpu.VMEM((MOE_NBUF, MOE_UNIT * MOE_TM * pitch, LANES), jnp.uint32),
            pltpu.SemaphoreType.DMA((2,)),
            pltpu.SemaphoreType.DMA((MOE_NBUF,)),
            pltpu.SemaphoreType.DMA((MOE_NBUF,)),
        ],
    )
    return pl.pallas_call(
        functools.partial(_moe_down_kernel, pitch=pitch, n_blocks_max=n_blocks_max),
        grid_spec=grid_spec,
        out_shape=jax.ShapeDtypeStruct((n_rows * pitch, LANES), jnp.uint32),
        compiler_params=_cparams(("arbitrary",)),
        name="moe_down",
    )(blk_start, blk_count, n_used, act, w_down, b_down)


def _combine_kernel(pos0_ref, posn_ref, y_hbm, x2_ref, gate_ref, g_ref, o_ref, buf_even, buf_odd, sem,
                    *, tb, pitch):
    i = pl.program_id(0)
    last = pl.num_programs(0) - 1
    n_rows = TOP_K * tb

    def row_copy(p_ref, r, buf, sem_slot):
        src = y_hbm.at[pl.ds(pl.multiple_of(p_ref[0, r] * pitch, pitch), pitch), :]
        return pltpu.make_async_copy(src, buf.at[pl.ds(r * pitch, pitch), :], sem.at[sem_slot])

    def wait_block(buf, sem_slot):
        pltpu.make_async_copy(y_hbm.at[pl.ds(0, n_rows * pitch), :], buf, sem.at[sem_slot]).wait()

    @pl.when(i == 0)
    def _():
        def issue(j, carry):
            for half in range(2):
                row_copy(pos0_ref, 2 * j + half, buf_even, 0).start(priority=half)
            return carry
        lax.fori_loop(0, n_rows // 2, issue, 0, unroll=4)

    def step(buf_cur, sem_cur, buf_nxt, sem_nxt):
        wait_block(buf_cur, sem_cur)
        for r in range(n_rows):
            row_copy(posn_ref, r, buf_nxt, sem_nxt).start(priority=r % 2)
        gates = gate_ref[...]
        half = pitch * LANES
        ssq = jnp.zeros((tb, 1), F32)
        for c in range(pitch):
            lo = slice(c * LANES, (c + 1) * LANES)
            hi = slice(half + c * LANES, half + (c + 1) * LANES)
            z_lo = x2_ref[:, lo]
            z_hi = x2_ref[:, hi]
            for k in range(TOP_K):
                w = buf_cur[pl.ds(k * tb * pitch + c, tb, stride=pitch), :]
                z_lo = z_lo + gates[:, k:k + 1] * _unpack_lo(w)
                z_hi = z_hi + gates[:, k:k + 1] * _unpack_hi(w)
            o_ref[:, lo] = z_lo
            o_ref[:, hi] = z_hi
            ssq = ssq + (jnp.sum(z_lo * z_lo, axis=-1, keepdims=True)
                         + jnp.sum(z_hi * z_hi, axis=-1, keepdims=True))
        o_ref[...] = o_ref[...] * lax.rsqrt(ssq * (1.0 / (2 * half)) + EPS) * g_ref[...]

        @pl.when(i == last)
        def _():
            wait_block(buf_nxt, sem_nxt)

    @pl.when(i % 2 == 0)
    def _():
        step(buf_even, 0, buf_odd, 1)

    @pl.when(i % 2 == 1)
    def _():
        step(buf_odd, 1, buf_even, 0)


def _combine(y_rows_rm, pos, gates, x2, g_final, tb=256):
    n, d = x2.shape
    nb = n // tb
    pitch = d // 2 // LANES
    pos_blk = pos.reshape(nb, tb, TOP_K).transpose(0, 2, 1).reshape(nb, 1, TOP_K * tb)
    pos_spec = lambda imap: pl.BlockSpec((None, 1, TOP_K * tb), imap, memory_space=pltpu.SMEM)
    return pl.pallas_call(
        functools.partial(_combine_kernel, tb=tb, pitch=pitch),
        grid=(nb,),
        in_specs=[
            pos_spec(lambda i: (0, 0, 0)),
            pos_spec(lambda i: (jnp.minimum(i + 1, nb - 1), 0, 0)),
            pl.BlockSpec(memory_space=pl.ANY),
            pl.BlockSpec((tb, d), lambda i: (i, 0)),
            pl.BlockSpec((tb, LANES), lambda i: (i, 0)),
            pl.BlockSpec((1, d), lambda i: (0, 0)),
        ],
        out_specs=pl.BlockSpec((tb, d), lambda i: (i, 0)),
        out_shape=jax.ShapeDtypeStruct((n, d), F32),
        scratch_shapes=[pltpu.VMEM((TOP_K * tb * pitch, LANES), jnp.uint32),
                        pltpu.VMEM((TOP_K * tb * pitch, LANES), jnp.uint32),
                        pltpu.SemaphoreType.DMA((2,))],
        compiler_params=_cparams(("arbitrary",)),
        name="combine",
    )(pos_blk, pos_blk, y_rows_rm, x2, gates, g_final)


def _routing_tables(idx, rank, cnt):
    counts = cnt[0, :N_EXPERTS]
    blk_count = (counts + MOE_TM - 1) // MOE_TM
    blk_end = jnp.cumsum(blk_count)
    blk_start = blk_end - blk_count
    hot = idx[:, :TOP_K, None] == jnp.arange(N_EXPERTS, dtype=jnp.int32)
    pos = jnp.sum(jnp.where(hot, blk_start * MOE_TM, 0), axis=-1) + rank[:, :TOP_K]
    return pos.astype(jnp.int32), blk_start.astype(jnp.int32), blk_count.astype(jnp.int32)


def kernel(x, mem, g_attn_norm, g_mem_norm, w_in, w_gla_a2, b_gla_a, g_gla_out, b_fox_f, g_fox_out,
           w_mem_kv, g_mem_out, w_out, g_ffn_norm, w_router, b_router, w_moe_up, b_moe_up,
           w_moe_down, b_moe_down, g_final):
    batch, seq, d = x.shape
    n_mem = mem.shape[1]
    n = batch * seq
    depth = w_in.shape[0]
    assert depth == 1, "the combine kernel applies the final norm, so exactly one layer is supported"
    kw = GLA_HEADS * GLA_DK
    vw = GLA_HEADS * GLA_DV
    fw = FOX_HEADS * FOX_DH
    mw = MEM_HEADS * MEM_DH
    o_q, o_k, o_v, o_g = 0, kw, 2 * kw, 2 * kw + vw
    o_a = o_g + vw
    o_fq = o_a + GLA_LOWRANK
    o_fk, o_fv = o_fq + fw, o_fq + 2 * fw
    o_ff = o_fq + 3 * fw
    o_mq = o_ff + FOX_HEADS
    f_lane = GLA_LOWRANK

    xf = x.reshape(n, d)
    for l in range(depth):
        fox_q_scale = FOX_DH ** -0.5 * LOG2E
        col_scale = jnp.ones((w_in.shape[2],), F32).at[o_fq:o_fk].set(fox_q_scale)
        wt = (jnp.transpose(w_in[l]) * col_scale[:, None]).astype(BF16)
        wa_t = jnp.concatenate([wt[o_q:o_a], wt[o_fq:o_ff], wt[o_mq:o_mq + mw]], axis=0)
        wb_t = jnp.zeros((LANES, d), BF16).at[:GLA_LOWRANK].set(wt[o_a:o_fq])
        wb_t = wb_t.at[f_lane:f_lane + FOX_HEADS].set(wt[o_ff:o_mq])
        proj, small = _in_proj(xf, g_attn_norm[l].reshape(1, d), wa_t, wb_t)

        wa2p = jnp.zeros((LANES, kw), F32).at[:GLA_LOWRANK].set(w_gla_a2[l]).astype(BF16)
        bfv = jnp.zeros((1, LANES), F32).at[0, f_lane:f_lane + FOX_HEADS].set(b_fox_f[l])
        gla, f_cum = _gla(proj, small, wa2p, b_gla_a[l].reshape(1, kw), bfv,
                          g_gla_out[l].reshape(1, vw), batch, seq)

        f_t = f_cum.reshape(batch, seq, LANES)[:, :, f_lane:f_lane + FOX_HEADS].transpose(0, 2, 1)
        f_t = jnp.concatenate([f_t, jnp.zeros_like(f_t)], axis=1)
        fox = _fox(proj, f_t, g_fox_out[l].reshape(1, fw), batch, seq)

        kv = _mem_kv(mem.reshape(batch * n_mem, d), g_mem_norm[l].reshape(1, d), w_mem_kv[l].astype(BF16))
        memo = _mem_attn(proj, kv, g_mem_out[l].reshape(1, mw), batch, seq, n_mem)

        wr = jnp.zeros((d, LANES), F32).at[:, :N_EXPERTS].set(w_router[l])
        wr_hi = wr.astype(BF16)
        wr = jnp.concatenate([wr_hi, (wr - wr_hi.astype(F32)).astype(BF16)], axis=1)
        br =jnp.zeros((1, LANES), F32).at[0, :N_EXPERTS].set(b_router[l])
        x2, xn, logits = _out_proj(xf, gla, fox, memo, w_out[l].astype(BF16),
                                   g_ffn_norm[l].reshape(1, d), wr, br)

        idx, gates, rank, cnt = _route(logits)
        pos, blk_start, blk_count = _routing_tables(idx, rank, cnt)
        n_blocks_max = -(-(n * TOP_K + N_EXPERTS * (MOE_TM - 1)) // MOE_TM)
        blk_end = blk_start + blk_count
        n_used = blk_end[-1:]
        x_rows = _dispatch(xn, pos, blk_end - 1, (blk_count > 0).astype(jnp.int32), n_used,
                           n_blocks_max, d // 2 // LANES)
        d_ff = w_moe_up.shape[3] // 2
        act = _moe_up(x_rows, w_moe_up[l], b_moe_up[l].reshape(N_EXPERTS, 1, 2 * d_ff),
                      blk_start, blk_count, n_used, n_blocks_max)
        y_rows = _moe_down(act, w_moe_down[l], b_moe_down[l].reshape(N_EXPERTS, 1, d),
                           blk_start, blk_count, n_used, n_blocks_max)
        xf = _combine(y_rows, pos, gates, x2, g_final.reshape(1, d))
    return xf.reshape(batch, seq, d)
```

```python
import functools

import jax
import jax.numpy as jnp
from jax import lax
from jax.experimental import pallas as pl
from jax.experimental.pallas import tpu as pltpu

EPS = 1e-5
CHUNK = 64
GLA_HEADS = 4
GLA_DK = 128
GLA_DV = 256
GLA_LOWRANK = 16
GLA_TAU = 16.0
FOX_HEADS = 4
FOX_DH = 128
MEM_HEADS = 4
MEM_DH = 128
N_EXPERTS = 32
TOP_K = 4
SWIGLU_LIMIT = 7.0
SWIGLU_ALPHA = 1.702
LANES = 128
MXU_COLS = 256
MOE_TM = 256
MOE_NBUF = 2
MOE_UNIT = 2
VMEM_LIMIT = 56 * 1024 * 1024

GLA_KW = GLA_HEADS * GLA_DK
GLA_VW = GLA_HEADS * GLA_DV
FOX_W = FOX_HEADS * FOX_DH
MEM_W = MEM_HEADS * MEM_DH
PROJ_GLA_Q = 0
PROJ_GLA_K = PROJ_GLA_Q + GLA_KW
PROJ_GLA_V = PROJ_GLA_K + GLA_KW
PROJ_GLA_GATE = PROJ_GLA_V + GLA_VW
PROJ_FOX_Q = PROJ_GLA_GATE + GLA_VW
PROJ_FOX_K = PROJ_FOX_Q + FOX_W
PROJ_FOX_V = PROJ_FOX_K + FOX_W
PROJ_MEM_Q = PROJ_FOX_V + FOX_W

F32 = jnp.float32
BF16 = jnp.bfloat16


def _cparams(sem, vmem=VMEM_LIMIT):
    return pltpu.CompilerParams(dimension_semantics=sem, vmem_limit_bytes=vmem)


def _log_sigmoid(x):
    return jnp.minimum(x, 0.0) - jnp.log1p(jnp.exp(-jnp.abs(x)))


def _rms(x, g):
    return x * lax.rsqrt(jnp.mean(x * x, axis=-1, keepdims=True) + EPS) * g


def _dot(a, b, **kw):
    return jnp.dot(a, b, preferred_element_type=F32, **kw)


def _dot_nt(a, b):
    return lax.dot_general(a, b, (((1,), (1,)), ((), ())), preferred_element_type=F32)


def _dot_tn(a, b):
    return lax.dot_general(a, b, (((0,), (0,)), ((), ())), preferred_element_type=F32)


def _store_rowmajor(ref, x, base=0):
    rows, w = x.shape
    pitch = w // LANES
    for c in range(pitch):
        ref[pl.ds(base + c, rows, stride=pitch), :] = x[:, c * LANES:(c + 1) * LANES]


def _pack_bf16_pairs(x):
    half = x.shape[1] // 2
    bits = lambda v: lax.bitcast_convert_type(v.astype(BF16).astype(F32), jnp.uint32)
    return bits(x[:, half:]) | (bits(x[:, :half]) >> 16)


def _unpack_lo(w):
    return lax.bitcast_convert_type(w << 16, F32)


def _unpack_hi(w):
    return lax.bitcast_convert_type(w & jnp.uint32(0xFFFF0000), F32)


def _in_proj_kernel(x_ref, g_ref, wa_ref, wb_ref, proj_ref, small_ref, h_scr):
    @pl.when(pl.program_id(1) == 0)
    def _():
        hb = _rms(x_ref[...], g_ref[...]).astype(BF16)
        h_scr[...] = hb
        small_ref[...] = _dot_nt(hb, wb_ref[...])

    proj_ref[...] = _dot_nt(h_scr[...], wa_ref[...]).astype(proj_ref.dtype)


def _in_proj(x2d, g, wa_t, wb_t, tm=1024, tn=1280):
    n, d = x2d.shape
    na = wa_t.shape[0]
    return pl.pallas_call(
        _in_proj_kernel,
        grid=(n // tm, na // tn),
        in_specs=[
            pl.BlockSpec((tm, d), lambda i, j: (i, 0)),
            pl.BlockSpec((1, d), lambda i, j: (0, 0)),
            pl.BlockSpec((tn, d), lambda i, j: (j, 0)),
            pl.BlockSpec((LANES, d), lambda i, j: (0, 0)),
        ],
        out_specs=[
            pl.BlockSpec((tm, tn), lambda i, j: (i, j)),
            pl.BlockSpec((tm, LANES), lambda i, j: (i, 0)),
        ],
        out_shape=[
            jax.ShapeDtypeStruct((n, na), BF16),
            jax.ShapeDtypeStruct((n, LANES), F32),
        ],
        scratch_shapes=[pltpu.VMEM((tm, d), BF16)],
        compiler_params=_cparams(("parallel", "arbitrary")),
        name="in_proj",
    )(x2d, g, wa_t, wb_t)


def _gla_kernel(q_ref, k_ref, v_ref, gate_ref, small_ref, wa2_ref, ba_ref, bf_ref, gout_ref,
                o_ref, f_ref, state_scr, fcar_scr, la_scr, lf_scr, *, n_chunks):
    @pl.when(pl.program_id(1) == 0)
    def _():
        state_scr[...] = jnp.zeros_like(state_scr)
        fcar_scr[...] = jnp.zeros_like(fcar_scr)

    small = small_ref[...]
    la_scr[...] = _log_sigmoid(_dot(small.astype(BF16), wa2_ref[...]) + ba_ref[...]) * (1.0 / GLA_TAU)
    lf_scr[...] = _log_sigmoid(small + bf_ref[...])
    row = lax.broadcasted_iota(jnp.int32, (CHUNK, CHUNK), 0)
    col = lax.broadcasted_iota(jnp.int32, (CHUNK, CHUNK), 1)
    tri = (col <= row).astype(F32)
    scale = GLA_DK ** -0.5

    def chunk_body(c, carry):
        r = pl.ds(pl.multiple_of(c * CHUNK, CHUNK), CHUNK)
        b = _dot(tri, la_scr[r, :], precision=lax.Precision.HIGHEST)
        b_end = b[CHUNK - 1:CHUNK, :]
        k_dec = k_ref[r, :].astype(F32) * jnp.exp(b_end - b)
        decay = jnp.exp(b_end)
        f_cum = _dot(tri, lf_scr[r, :], precision=lax.Precision.HIGHEST) + fcar_scr[...]
        f_ref[r, :] = f_cum
        fcar_scr[...] = f_cum[CHUNK - 1:CHUNK, :]
        heads = range(GLA_HEADS)
        ks = [slice(h * GLA_DK, (h + 1) * GLA_DK) for h in heads]
        vs = [slice(h * GLA_DV, (h + 1) * GLA_DV) for h in heads]
        old = [state_scr[h] for h in heads]
        new = [old[h] * decay[:, ks[h]] + _dot_tn(v_ref[r, vs[h]], k_dec[:, ks[h]].astype(BF16))
               for h in heads]
        for h in heads:
            state_scr[h] = new[h]
        for h in heads:
            o = _dot_nt(q_ref[r, ks[h]], new[h].astype(BF16)) * scale
            gt = gate_ref[r, vs[h]].astype(F32)
            o_ref[r, vs[h]] = (_rms(o, gout_ref[:, vs[h]]) * (gt * jax.nn.sigmoid(gt))).astype(o_ref.dtype)
        return carry

    lax.fori_loop(0, n_chunks, chunk_body, 0, unroll=True)


def _gla(proj, small, wa2p, ba, bfv, gout, batch, seq, ts=512):
    n = proj.shape[0]
    nsb = seq // ts
    kw, vw = GLA_KW, GLA_VW
    row = lambda b, s: b * nsb + s
    return pl.pallas_call(
        functools.partial(_gla_kernel, n_chunks=ts // CHUNK),
        grid=(batch, nsb),
        in_specs=[
            pl.BlockSpec((ts, kw), lambda b, s: (row(b, s), PROJ_GLA_Q // kw)),
            pl.BlockSpec((ts, kw), lambda b, s: (row(b, s), PROJ_GLA_K // kw)),
            pl.BlockSpec((ts, vw), lambda b, s: (row(b, s), PROJ_GLA_V // vw)),
            pl.BlockSpec((ts, vw), lambda b, s: (row(b, s), PROJ_GLA_GATE // vw)),
            pl.BlockSpec((ts, LANES), lambda b, s: (row(b, s), 0)),
            pl.BlockSpec((LANES, kw), lambda b, s: (0, 0)),
            pl.BlockSpec((1, kw), lambda b, s: (0, 0)),
            pl.BlockSpec((1, LANES), lambda b, s: (0, 0)),
            pl.BlockSpec((1, vw), lambda b, s: (0, 0)),
        ],
        out_specs=[
            pl.BlockSpec((ts, vw), lambda b, s: (row(b, s), 0)),
            pl.BlockSpec((ts, LANES), lambda b, s: (row(b, s), 0)),
        ],
        out_shape=[
            jax.ShapeDtypeStruct((n, vw), BF16),
            jax.ShapeDtypeStruct((n, LANES), F32),
        ],
        scratch_shapes=[
            pltpu.VMEM((GLA_HEADS, GLA_DV, GLA_DK), F32),
            pltpu.VMEM((1, LANES), F32),
            pltpu.VMEM((ts, kw), F32),
            pltpu.VMEM((ts, LANES), F32),
        ],
        compiler_params=_cparams(("parallel", "arbitrary")),
        name="gla",
    )(proj, proj, proj, proj, small, wa2p, ba, bfv, gout)


LOG2E = 1.4426950408889634
def _fox_kernel(q_ref, k_ref, v_ref, fk_ref, g_ref, o_ref, m_scr, l_scr, acc_scr, s_even, s_odd,
                *, tq, tk):
    i = pl.program_id(1)
    t = pl.program_id(2)

    @pl.when(t == 0)
    def _():
        m_scr[...] = jnp.full_like(m_scr, -jnp.inf)
        l_scr[...] = jnp.zeros_like(l_scr)
        acc_scr[...] = jnp.zeros_like(acc_scr)

    heads = range(FOX_HEADS)
    hs = [slice(h * FOX_DH, (h + 1) * FOX_DH) for h in heads]

    def score(s_out):
        for h in heads:
            s_out[h] = _dot_nt(q_ref[:, hs[h]], k_ref[:, hs[h]])

    def update(s_in, on_diagonal):
        m_prev = [m_scr[h] for h in heads]
        l_prev = [l_scr[h] for h in heads]
        acc_prev = [acc_scr[h] for h in heads]
        s = [s_in[h] - fk_ref[h:h + 1, :] * LOG2E for h in heads]
        if on_diagonal:
            row = lax.broadcasted_iota(jnp.int32, (tq, tk), 0)
            col = lax.broadcasted_iota(jnp.int32, (tq, tk), 1)
            s = [jnp.where(col <= row, sh, -jnp.inf) for sh in s]
        lane_chunks = [slice(c * LANES, (c + 1) * LANES) for c in range(tk // LANES)]
        m_new, l_new, acc_new = [], [], []
        for h in heads:
            chunks = [s[h][:, lc] for lc in lane_chunks]
            cmax = functools.reduce(jnp.maximum, chunks)
            m_h = jnp.maximum(m_prev[h], jnp.max(cmax, axis=-1, keepdims=True))
            alpha = jnp.exp2(m_prev[h] - m_h)
            p_chunks = [jnp.exp2(ch - m_h) for ch in chunks]
            row_sum = jnp.sum(functools.reduce(jnp.add, p_chunks), axis=-1, keepdims=True)
            p = jnp.concatenate(p_chunks, axis=1).astype(BF16)
            m_new.append(m_h)
            l_new.append(alpha * l_prev[h] + row_sum)
            acc_new.append(alpha * acc_prev[h] + _dot(p, v_ref[:, hs[h]]))
        return m_new, l_new, acc_new

    def step(s_write, s_read):
        @pl.when(t == 0)
        def _():
            score(s_write)

        @pl.when(jnp.logical_and(t >= 1, t <= i))
        def _():
            m_new, l_new, acc_new = update(s_read, False)
            score(s_write)
            for h in heads:
                m_scr[h] = m_new[h]
                l_scr[h] = l_new[h]
                acc_scr[h] = acc_new[h]

        @pl.when(t == i + 1)
        def _():
            _, l_new, acc_new = update(s_read, True)
            for h in heads:
                o_ref[:, hs[h]] = _rms(acc_new[h] / l_new[h], g_ref[:, hs[h]]).astype(o_ref.dtype)

    @pl.when(t % 2 == 0)
    def _():
        step(s_even, s_odd)

    @pl.when(t % 2 == 1)
    def _():
        step(s_odd, s_even)


def _fox(proj, f_t, g_fox, batch, seq, tq=512, tk=512):
    assert tq == tk, "the diagonal-block mask assumes square blocks"
    n = proj.shape[0]
    w = FOX_HEADS * FOX_DH
    nq, nk = seq // tq, seq // tk
    qcol, kcol, vcol = PROJ_FOX_Q // w, PROJ_FOX_K // w, PROJ_FOX_V // w
    return pl.pallas_call(
        functools.partial(_fox_kernel, tq=tq, tk=tk),
        grid=(batch, nq, nk + 1),
        in_specs=[
            pl.BlockSpec((tq, w), lambda b, i, t: (b * nq + i, qcol)),
            pl.BlockSpec((tk, w), lambda b, i, t: (b * nk + jnp.minimum(t, i), kcol)),
            pl.BlockSpec((tk, w), lambda b, i, t: (b * nk + jnp.clip(t - 1, 0, i), vcol)),
            pl.BlockSpec((None, 8, tk), lambda b, i, t: (b, 0, jnp.clip(t - 1, 0, i))),
            pl.BlockSpec((1, w), lambda b, i, t: (0, 0)),
        ],
        out_specs=pl.BlockSpec((tq, w), lambda b, i, t: (b * nq + i, 0)),
        out_shape=jax.ShapeDtypeStruct((n, w), BF16),
        scratch_shapes=[
            pltpu.VMEM((FOX_HEADS, tq, LANES), F32),
            pltpu.VMEM((FOX_HEADS, tq, LANES), F32),
            pltpu.VMEM((FOX_HEADS, tq, FOX_DH), F32),
            pltpu.VMEM((FOX_HEADS, tq, tk), F32),
            pltpu.VMEM((FOX_HEADS, tq, tk), F32),
        ],
        compiler_params=_cparams(("parallel", "parallel", "arbitrary")),
        name="fox",
    )(proj, proj, proj, f_t, g_fox)


def _mem_kv_kernel(m_ref, g_ref, w_ref, kv_ref):
    kv_ref[...] = _dot(_rms(m_ref[...], g_ref[...]).astype(BF16), w_ref[...]).astype(kv_ref.dtype)


def _mem_kv(mem2d, g, w, tm=256):
    n, d = mem2d.shape
    nw = w.shape[1]
    return pl.pallas_call(
        _mem_kv_kernel,
        grid=(n // tm,),
        in_specs=[
            pl.BlockSpec((tm, d), lambda i: (i, 0)),
            pl.BlockSpec((1, d), lambda i: (0, 0)),
            pl.BlockSpec((d, nw), lambda i: (0, 0)),
        ],
        out_specs=pl.BlockSpec((tm, nw), lambda i: (i, 0)),
        out_shape=jax.ShapeDtypeStruct((n, nw), BF16),
        compiler_params=_cparams(("parallel",)),
        name="mem_kv",
    )(mem2d, g, w)


def _mem_attn_kernel(q_ref, k_ref, v_ref, g_ref, o_ref):
    scale = MEM_DH ** -0.5
    for h in range(MEM_HEADS):
        hs = slice(h * MEM_DH, (h + 1) * MEM_DH)
        s = _dot_nt(q_ref[:, hs], k_ref[:, hs]) * scale
        p = jnp.exp(s - jnp.max(s, axis=-1, keepdims=True))
        l = jnp.sum(p, axis=-1, keepdims=True)
        o = _dot((p / l).astype(BF16), v_ref[:, hs])
        o_ref[:, hs] = _rms(o, g_ref[:, hs]).astype(o_ref.dtype)


def _mem_attn(proj, kv, g_mem_out, batch, seq, n_mem, tq=1024):
    n = proj.shape[0]
    w = MEM_HEADS * MEM_DH
    nq = seq // tq
    qcol = PROJ_MEM_Q // w
    return pl.pallas_call(
        _mem_attn_kernel,
        grid=(n // tq,),
        in_specs=[
            pl.BlockSpec((tq, w), lambda i: (i, qcol)),
            pl.BlockSpec((n_mem, w), lambda i: (i // nq, 0)),
            pl.BlockSpec((n_mem, w), lambda i: (i // nq, 1)),
            pl.BlockSpec((1, w), lambda i: (0, 0)),
        ],
        out_specs=pl.BlockSpec((tq, w), lambda i: (i, 0)),
        out_shape=jax.ShapeDtypeStruct((n, w), BF16),
        compiler_params=_cparams(("parallel",)),
        name="mem_attn",
    )(proj, kv, kv, g_mem_out)


def _out_proj_kernel(x_ref, gla_ref, fox_ref, mem_ref, w1_ref, w2_ref, w3_ref, g_ref, wr_ref, br_ref,
                     x2_ref, xn_ref, logit_ref):
    x2 = (x_ref[...] + _dot(gla_ref[...], w1_ref[...]) + _dot(fox_ref[...], w2_ref[...])
          + _dot(mem_ref[...], w3_ref[...]))
    x2_ref[...] = x2
    xn = _rms(x2, g_ref[...])
    _store_rowmajor(xn_ref, _pack_bf16_pairs(xn))
    xh = xn.astype(BF16)
    xl = (xn - xh.astype(F32)).astype(BF16)
    hi = _dot(xh, wr_ref[...])
    logit_ref[...] = hi[:, :LANES] + hi[:, LANES:] + _dot(xl, wr_ref[:, :LANES]) + br_ref[...]


def _out_proj(x2d, gla, fox, memo, w_out, g_ffn, wr, br, tm=512):
    n, d = x2d.shape
    w1, w2 = gla.shape[1], fox.shape[1]
    const = lambda i: (0, 0)
    return pl.pallas_call(
        _out_proj_kernel,
        grid=(n // tm,),
        in_specs=[
            pl.BlockSpec((tm, d), lambda i: (i, 0)),
            pl.BlockSpec((tm, w1), lambda i: (i, 0)),
            pl.BlockSpec((tm, w2), lambda i: (i, 0)),
            pl.BlockSpec((tm, w2), lambda i: (i, 0)),
            pl.BlockSpec((w1, d), lambda i: (0, 0)),
            pl.BlockSpec((w2, d), lambda i: (w1 // w2, 0)),
            pl.BlockSpec((w2, d), lambda i: (w1 // w2 + 1, 0)),
            pl.BlockSpec((1, d), const),
            pl.BlockSpec((d, 2 * LANES), const),
            pl.BlockSpec((1, LANES), const),
        ],
        out_specs=[
            pl.BlockSpec((tm, d), lambda i: (i, 0)),
            pl.BlockSpec((tm * (d // 2 // LANES), LANES), lambda i: (i, 0)),
            pl.BlockSpec((tm, LANES), lambda i: (i, 0)),
        ],
        out_shape=[
            jax.ShapeDtypeStruct((n, d), F32),
            jax.ShapeDtypeStruct((n * (d // 2 // LANES), LANES), jnp.uint32),
            jax.ShapeDtypeStruct((n, LANES), F32),
        ],
        compiler_params=_cparams(("parallel",)),
        name="out_proj",
    )(x2d, gla, fox, memo, w_out, w_out, w_out, g_ffn, wr, br)


def _route_kernel(l_ref, idx_ref, gate_ref, rank_ref, cnt_ref, carry_scr, *, tb):
    @pl.when(pl.program_id(0) == 0)
    def _():
        carry_scr[...] = jnp.zeros_like(carry_scr)

    lane = lax.broadcasted_iota(jnp.int32, (tb, LANES), 1)
    logit = jnp.where(lane < N_EXPERTS, l_ref[...], -jnp.inf)
    vals, hots = [], []
    idx_out = jnp.zeros((tb, LANES), jnp.int32)
    for k in range(TOP_K):
        m = jnp.max(logit, axis=-1, keepdims=True)
        ik = jnp.min(jnp.where(logit == m, lane, LANES), axis=-1, keepdims=True)
        hot = lane == ik
        logit = jnp.where(hot, -jnp.inf, logit)
        vals.append(m)
        hots.append(hot)
        idx_out = jnp.where(lane == k, ik, idx_out)
    idx_ref[...] = idx_out

    e = [jnp.exp(v - vals[0]) for v in vals]
    den = e[0] + e[1] + e[2] + e[3]
    gate_out = jnp.zeros((tb, LANES), F32)
    for k in range(TOP_K):
        gate_out = jnp.where(lane == k, e[k] / den, gate_out)
    gate_ref[...] = gate_out

    member = jnp.zeros((tb, LANES), F32)
    for hot in hots:
        member = member + hot.astype(F32)
    row = lax.broadcasted_iota(jnp.int32, (tb, tb), 0)
    col = lax.broadcasted_iota(jnp.int32, (tb, tb), 1)
    before = (col < row).astype(BF16)
    rank = _dot(before, member.astype(BF16)) + carry_scr[...]
    rank_out = jnp.zeros((tb, LANES), F32)
    for k in range(TOP_K):
        rk = jnp.sum(jnp.where(hots[k], rank, 0.0), axis=-1, keepdims=True)
        rank_out = jnp.where(lane == k, rk, rank_out)
    rank_ref[...] = rank_out.astype(jnp.int32)
    total = carry_scr[...] + jnp.sum(member, axis=0, keepdims=True)
    carry_scr[...] = total
    cnt_ref[...] = total.astype(jnp.int32)


def _route(logits, tb=512):
    n = logits.shape[0]
    blk = pl.BlockSpec((tb, LANES), lambda i: (i, 0))
    return pl.pallas_call(
        functools.partial(_route_kernel, tb=tb),
        grid=(n // tb,),
        in_specs=[blk],
        out_specs=[blk, blk, blk, pl.BlockSpec((1, LANES), lambda i: (0, 0))],
        out_shape=[
            jax.ShapeDtypeStruct((n, LANES), jnp.int32),
            jax.ShapeDtypeStruct((n, LANES), F32),
            jax.ShapeDtypeStruct((n, LANES), jnp.int32),
            jax.ShapeDtypeStruct((1, LANES), jnp.int32),
        ],
        scratch_shapes=[pltpu.VMEM((1, LANES), F32)],
        compiler_params=_cparams(("arbitrary",)),
        name="route",
    )(logits)


def _dispatch_kernel(zblk_ref, zok_ref, used_ref, pos_ref, xn_ref, xr_hbm, zero_scr, sem,
                     *, tb, pitch, n_blocks_max):
    blk_rows = MOE_TM * pitch

    def zero_copy(b):
        dst = xr_hbm.at[pl.ds(pl.multiple_of(b * blk_rows, blk_rows), blk_rows), :]
        return pltpu.make_async_copy(zero_scr, dst, sem.at[1])

    @pl.when(pl.program_id(0) == 0)
    def _():
        zero_scr[...] = jnp.zeros_like(zero_scr)

        def per_expert(action):
            def body(e, carry):
                @pl.when(zok_ref[e] == 1)
                def _():
                    action(zero_copy(zblk_ref[e]))
                return carry
            lax.fori_loop(0, N_EXPERTS, body, 0)

        def per_tail(action):
            def body(b, carry):
                action(zero_copy(b))
                return carry
            lax.fori_loop(used_ref[0], n_blocks_max, body, 0)

        per_expert(lambda c: c.start())
        per_tail(lambda c: c.start())
        per_expert(lambda c: c.wait())
        per_tail(lambda c: c.wait())

    def issue(t, carry):
        src = xn_ref.at[pl.ds(pl.multiple_of(t * pitch, pitch), pitch), :]
        for k in range(TOP_K):
            p = pos_ref[0, t * TOP_K + k]
            dst = xr_hbm.at[pl.ds(pl.multiple_of(p * pitch, pitch), pitch), :]
            pltpu.make_async_copy(src, dst, sem.at[0]).start(priority=k % 2)
        return carry

    lax.fori_loop(0, tb, issue, 0, unroll=4)
    for k in range(TOP_K):
        pltpu.make_async_copy(xn_ref, xr_hbm.at[pl.ds(0, tb * pitch), :], sem.at[0]).wait()


def _dispatch(xn_rm, pos, zblk, zok, n_used, n_blocks_max, pitch, tb=512):
    n = xn_rm.shape[0] // pitch
    nb = n // tb
    grid_spec = pltpu.PrefetchScalarGridSpec(
        num_scalar_prefetch=3,
        grid=(nb,),
        in_specs=[
            pl.BlockSpec((None, 1, TOP_K * tb), lambda i, *_: (i, 0, 0), memory_space=pltpu.SMEM),
            pl.BlockSpec((tb * pitch, LANES), lambda i, *_: (i, 0)),
        ],
        out_specs=pl.BlockSpec(memory_space=pl.ANY),
        scratch_shapes=[pltpu.VMEM((MOE_TM * pitch, LANES), xn_rm.dtype), pltpu.SemaphoreType.DMA((2,))],
    )
    return pl.pallas_call(
        functools.partial(_dispatch_kernel, tb=tb, pitch=pitch, n_blocks_max=n_blocks_max),
        grid_spec=grid_spec,
        out_shape=jax.ShapeDtypeStruct((n_blocks_max * MOE_TM * pitch, LANES), xn_rm.dtype),
        compiler_params=_cparams(("arbitrary",)),
        name="dispatch",
    )(zblk, zok, n_used, pos.reshape(nb, 1, TOP_K * tb), xn_rm)


def _stream_row_blocks(cnt, in_copy, out_copy, compute):
    for ahead in range(MOE_NBUF - 1):
        @pl.when(cnt > ahead)
        def _():
            in_copy(ahead, ahead).start()

    def body(b, carry):
        slot = b % MOE_NBUF

        @pl.when(b + MOE_NBUF - 1 < cnt)
        def _():
            in_copy(b + MOE_NBUF - 1, (b + MOE_NBUF - 1) % MOE_NBUF).start()

        in_copy(b, slot).wait()

        @pl.when(b >= MOE_NBUF)
        def _():
            out_copy(b - MOE_NBUF, slot).wait()

        compute(slot)
        out_copy(b, slot).start(priority=1)
        return carry

    lax.fori_loop(0, cnt, body, 0)

    for back in range(MOE_NBUF, 0, -1):
        @pl.when(cnt >= back)
        def _():
            out_copy(cnt - back, (cnt - back) % MOE_NBUF).wait()


def _stream_expert_rows(first, cnt, in_copy, out_copy, compute):
    n_units = cnt // MOE_UNIT
    _stream_row_blocks(n_units, in_copy(first, MOE_UNIT), out_copy(first, MOE_UNIT), compute(MOE_UNIT))
    done = n_units * MOE_UNIT
    _stream_row_blocks(cnt - done, in_copy(first + done, 1), out_copy(first + done, 1), compute(1))


def _zero_fill_blocks(first, last, zero_src, dst_copy):
    def start(b, carry):
        dst_copy(b).start()
        return carry

    def wait(b, carry):
        dst_copy(b).wait()
        return carry

    zero_src[...] = jnp.zeros_like(zero_src)
    lax.fori_loop(first, last, start, 0)
    lax.fori_loop(first, last, wait, 0)


def _moe_up_kernel(bs_ref, bc_ref, used_ref, x_hbm, w_hbm, bg_ref, bu_ref, act_hbm,
                   w_scr, xbuf, lhs_scr, obuf, sem_w, sem_x, sem_o,
                   *, nt, tn, d_ff, pitch, n_blocks_max):
    g = pl.program_id(0)
    ng = pl.num_programs(0)
    e = g // nt
    col = pl.multiple_of((g % nt) * tn, tn)
    blk_rows = MOE_TM * pitch
    wslot = g % 2

    def w_copies(step, slot):
        ee = step // nt
        cc = pl.multiple_of((step % nt) * tn, tn)
        return (pltpu.make_async_copy(w_hbm.at[ee, :, pl.ds(cc, tn)], w_scr.at[slot, 0], sem_w.at[slot, 0]),
                pltpu.make_async_copy(w_hbm.at[ee, :, pl.ds(d_ff + cc, tn)], w_scr.at[slot, 1],
                                      sem_w.at[slot, 1]))

    @pl.when(g == 0)
    def _():
        for c in w_copies(0, 0):
            c.start(priority=1)

    @pl.when(g + 1 < ng)
    def _():
        for c in w_copies(g + 1, 1 - wslot):
            c.start(priority=1)

    for c in w_copies(g, wslot):
        c.wait()

    start = bs_ref[e]

    def x_copy(first, nblk):
        def make(u, slot):
            r0 = pl.multiple_of((first + u * nblk) * blk_rows, blk_rows)
            return pltpu.make_async_copy(x_hbm.at[pl.ds(r0, nblk * blk_rows), :],
                                         xbuf.at[slot, pl.ds(0, nblk * blk_rows), :], sem_x.at[slot])
        return make

    def o_copy(first, nblk):
        def make(u, slot):
            r0 = pl.multiple_of((first + u * nblk) * MOE_TM, MOE_TM)
            return pltpu.make_async_copy(obuf.at[slot, pl.ds(0, nblk * MOE_TM), :],
                                         act_hbm.at[pl.ds(r0, nblk * MOE_TM), pl.ds(col, tn)], sem_o.at[slot])
        return make

    def compute(nblk):
        rows = nblk * MOE_TM

        def run(slot):
            half = pitch * LANES
            for c in range(pitch):
                w = xbuf[slot, pl.ds(c, rows, stride=pitch), :]
                lhs_scr[0:rows, c * LANES:(c + 1) * LANES] = _unpack_lo(w)
                lhs_scr[0:rows, half + c * LANES:half + (c + 1) * LANES] = _unpack_hi(w)
            x = lhs_scr[0:rows, :]
            gate = jnp.minimum(_dot(x, w_scr[wslot, 0]) + bg_ref[...], SWIGLU_LIMIT)
            up = jnp.clip(_dot(x, w_scr[wslot, 1]) + bu_ref[...], -SWIGLU_LIMIT, SWIGLU_LIMIT)
            obuf[slot, 0:rows, :] = (gate * jax.nn.sigmoid(SWIGLU_ALPHA * gate) * (up + 1.0)).astype(obuf.dtype)
        return run

    _stream_expert_rows(start, bc_ref[e], x_copy, o_copy, compute)

    @pl.when(e == N_EXPERTS - 1)
    def _():
        zero_blk = obuf.at[0, pl.ds(0, MOE_TM), :]

        def tail_copy(b):
            r0 = pl.multiple_of(b * MOE_TM, MOE_TM)
            return pltpu.make_async_copy(zero_blk, act_hbm.at[pl.ds(r0, MOE_TM), pl.ds(col, tn)], sem_o.at[0])
        _zero_fill_blocks(used_ref[0], n_blocks_max, zero_blk, tail_copy)


def _moe_up(x_rows_rm, w_up, b_up, blk_start, blk_count, n_used, n_blocks_max, tn=1024):
    n_exp, d, two_ff = w_up.shape
    d_ff = two_ff // 2
    pitch = d // 2 // LANES
    nt = d_ff // tn
    up_off = d_ff // tn
    grid_spec = pltpu.PrefetchScalarGridSpec(
        num_scalar_prefetch=3,
        grid=(n_exp * nt,),
        in_specs=[
            pl.BlockSpec(memory_space=pl.ANY),
            pl.BlockSpec(memory_space=pl.ANY),
            pl.BlockSpec((None, 1, tn), lambda g, *_: (g // nt, 0, g % nt)),
            pl.BlockSpec((None, 1, tn), lambda g, *_: (g // nt, 0, g % nt + up_off)),
        ],
        out_specs=pl.BlockSpec(memory_space=pl.ANY),
        scratch_shapes=[
            pltpu.VMEM((2, 2, d, tn), F32),
            pltpu.VMEM((MOE_NBUF, MOE_UNIT * MOE_TM * pitch, LANES), jnp.uint32),
            pltpu.VMEM((MOE_UNIT * MOE_TM, d), F32),
            pltpu.VMEM((MOE_NBUF, MOE_UNIT * MOE_TM, tn), BF16),
            pltpu.SemaphoreType.DMA((2, 2)),
            pltpu.SemaphoreType.DMA((MOE_NBUF,)),
            pltpu.SemaphoreType.DMA((MOE_NBUF,)),
        ],
    )
    return pl.pallas_call(
        functools.partial(_moe_up_kernel, nt=nt, tn=tn, d_ff=d_ff, pitch=pitch, n_blocks_max=n_blocks_max),
        grid_spec=grid_spec,
        out_shape=jax.ShapeDtypeStruct((n_blocks_max * MOE_TM, d_ff), BF16),
        compiler_params=_cparams(("arbitrary",)),
        name="moe_up",
    )(blk_start, blk_count, n_used, x_rows_rm, w_up, b_up, b_up)


def _moe_down_kernel(bs_ref, bc_ref, used_ref, a_hbm, w_hbm, b_ref, y_hbm,
                     w_scr, abuf, obuf, sem_w, sem_a, sem_o, *, pitch, n_blocks_max):
    e = pl.program_id(0)
    blk_rows = MOE_TM * pitch
    wslot = e % 2

    def w_copy(ee, slot):
        return pltpu.make_async_copy(w_hbm.at[ee], w_scr.at[slot], sem_w.at[slot])

    @pl.when(e == 0)
    def _():
        w_copy(0, 0).start(priority=1)

    @pl.when(e + 1 < pl.num_programs(0))
    def _():
        w_copy(e + 1, 1 - wslot).start(priority=1)

    w_copy(e, wslot).wait()
    start = bs_ref[e]

    def a_copy(first, nblk):
        def make(u, slot):
            r0 = pl.multiple_of((first + u * nblk) * MOE_TM, MOE_TM)
            return pltpu.make_async_copy(a_hbm.at[pl.ds(r0, nblk * MOE_TM), :],
                                         abuf.at[slot, pl.ds(0, nblk * MOE_TM), :], sem_a.at[slot])
        return make

    def o_copy(first, nblk):
        def make(u, slot):
            r0 = pl.multiple_of((first + u * nblk) * blk_rows, blk_rows)
            return pltpu.make_async_copy(obuf.at[slot, pl.ds(0, nblk * blk_rows), :],
                                         y_hbm.at[pl.ds(r0, nblk * blk_rows), :], sem_o.at[slot])
        return make

    def compute(nblk):
        rows = nblk * MOE_TM

        def run(slot):
            y = _dot(abuf[slot, 0:rows, :].astype(F32), w_scr[wslot]) + b_ref[...]
            _store_rowmajor(obuf.at[slot], _pack_bf16_pairs(y))
        return run

    _stream_expert_rows(start, bc_ref[e], a_copy, o_copy, compute)

    @pl.when(e == N_EXPERTS - 1)
    def _():
        zero_blk = obuf.at[0, pl.ds(0, blk_rows), :]

        def tail_copy(b):
            r0 = pl.multiple_of(b * blk_rows, blk_rows)
            return pltpu.make_async_copy(zero_blk, y_hbm.at[pl.ds(r0, blk_rows), :], sem_o.at[0])
        _zero_fill_blocks(used_ref[0], n_blocks_max, zero_blk, tail_copy)


def _moe_down(act, w_down, b_down, blk_start, blk_count, n_used, n_blocks_max):
    n_rows, d_ff = act.shape
    n_exp, _, d = w_down.shape
    pitch = d // 2 // LANES
    grid_spec = pltpu.PrefetchScalarGridSpec(
        num_scalar_prefetch=3,
        grid=(n_exp,),
        in_specs=[
            pl.BlockSpec(memory_space=pl.ANY),
            pl.BlockSpec(memory_space=pl.ANY),
            pl.BlockSpec((None, 1, d), lambda e, *_: (e, 0, 0)),
        ],
        out_specs=pl.BlockSpec(memory_space=pl.ANY),
        scratch_shapes=[
            pltpu.VMEM((2, d_ff, d), F32),
            pltpu.VMEM((MOE_NBUF, MOE_UNIT * MOE_TM, d_ff), BF16),
            pltpu.VMEM((MOE_NBUF, MOE_UNIT * MOE_TM * pitch, LANES), jnp.uint32),
            pltpu.SemaphoreType.DMA((2,)),
            pltpu.SemaphoreType.DMA((MOE_NBUF,)),
            pltpu.SemaphoreType.DMA((MOE_NBUF,)),
        ],
    )
    return pl.pallas_call(
        functools.partial(_moe_down_kernel, pitch=pitch, n_blocks_max=n_blocks_max),
        grid_spec=grid_spec,
        out_shape=jax.ShapeDtypeStruct((n_rows * pitch, LANES), jnp.uint32),
        compiler_params=_cparams(("arbitrary",)),
        name="moe_down",
    )(blk_start, blk_count, n_used, act, w_down, b_down)


def _combine_kernel(pos0_ref, posn_ref, y_hbm, x2_ref, gate_ref, g_ref, o_ref, buf_even, buf_odd, sem,
                    *, tb, pitch):
    i = pl.program_id(0)
    last = pl.num_programs(0) - 1
    n_rows = TOP_K * tb

    def row_copy(p_ref, r, buf, sem_slot):
        src = y_hbm.at[pl.ds(pl.multiple_of(p_ref[0, r] * pitch, pitch), pitch), :]
        return pltpu.make_async_copy(src, buf.at[pl.ds(r * pitch, pitch), :], sem.at[sem_slot])

    def wait_block(buf, sem_slot):
        pltpu.make_async_copy(y_hbm.at[pl.ds(0, n_rows * pitch), :], buf, sem.at[sem_slot]).wait()

    @pl.when(i == 0)
    def _():
        def issue(j, carry):
            for half in range(2):
                row_copy(pos0_ref, 2 * j + half, buf_even, 0).start(priority=half)
            return carry
        lax.fori_loop(0, n_rows // 2, issue, 0, unroll=4)

    def step(buf_cur, sem_cur, buf_nxt, sem_nxt):
        wait_block(buf_cur, sem_cur)
        for r in range(n_rows):
            row_copy(posn_ref, r, buf_nxt, sem_nxt).start(priority=r % 2)
        gates = gate_ref[...]
        half = pitch * LANES
        ssq = jnp.zeros((tb, 1), F32)
        for c in range(pitch):
            lo = slice(c * LANES, (c + 1) * LANES)
            hi = slice(half + c * LANES, half + (c + 1) * LANES)
            z_lo = x2_ref[:, lo]
            z_hi = x2_ref[:, hi]
            for k in range(TOP_K):
                w = buf_cur[pl.ds(k * tb * pitch + c, tb, stride=pitch), :]
                z_lo = z_lo + gates[:, k:k + 1] * _unpack_lo(w)
                z_hi = z_hi + gates[:, k:k + 1] * _unpack_hi(w)
            o_ref[:, lo] = z_lo
            o_ref[:, hi] = z_hi
            ssq = ssq + (jnp.sum(z_lo * z_lo, axis=-1, keepdims=True)
                         + jnp.sum(z_hi * z_hi, axis=-1, keepdims=True))
        o_ref[...] = o_ref[...] * lax.rsqrt(ssq * (1.0 / (2 * half)) + EPS) * g_ref[...]

        @pl.when(i == last)
        def _():
            wait_block(buf_nxt, sem_nxt)

    @pl.when(i % 2 == 0)
    def _():
        step(buf_even, 0, buf_odd, 1)

    @pl.when(i % 2 == 1)
    def _():
        step(buf_odd, 1, buf_even, 0)


def _combine(y_rows_rm, pos, gates, x2, g_final, tb=256):
    n, d = x2.shape
    nb = n // tb
    pitch = d // 2 // LANES
    pos_blk = pos.reshape(nb, tb, TOP_K).transpose(0, 2, 1).reshape(nb, 1, TOP_K * tb)
    pos_spec = lambda imap: pl.BlockSpec((None, 1, TOP_K * tb), imap, memory_space=pltpu.SMEM)
    return pl.pallas_call(
        functools.partial(_combine_kernel, tb=tb, pitch=pitch),
        grid=(nb,),
        in_specs=[
            pos_spec(lambda i: (0, 0, 0)),
            pos_spec(lambda i: (jnp.minimum(i + 1, nb - 1), 0, 0)),
            pl.BlockSpec(memory_space=pl.ANY),
            pl.BlockSpec((tb, d), lambda i: (i, 0)),
            pl.BlockSpec((tb, LANES), lambda i: (i, 0)),
            pl.BlockSpec((1, d), lambda i: (0, 0)),
        ],
        out_specs=pl.BlockSpec((tb, d), lambda i: (i, 0)),
        out_shape=jax.ShapeDtypeStruct((n, d), F32),
        scratch_shapes=[pltpu.VMEM((TOP_K * tb * pitch, LANES), jnp.uint32),
                        pltpu.VMEM((TOP_K * tb * pitch, LANES), jnp.uint32),
                        pltpu.SemaphoreType.DMA((2,))],
        compiler_params=_cparams(("arbitrary",)),
        name="combine",
    )(pos_blk, pos_blk, y_rows_rm, x2, gates, g_final)


def _routing_tables(idx, rank, cnt):
    counts = cnt[0, :N_EXPERTS]
    blk_count = (counts + MOE_TM - 1) // MOE_TM
    blk_end = jnp.cumsum(blk_count)
    blk_start = blk_end - blk_count
    hot = idx[:, :TOP_K, None] == jnp.arange(N_EXPERTS, dtype=jnp.int32)
    pos = jnp.sum(jnp.where(hot, blk_start * MOE_TM, 0), axis=-1) + rank[:, :TOP_K]
    return pos.astype(jnp.int32), blk_start.astype(jnp.int32), blk_count.astype(jnp.int32)


def kernel(x, mem, g_attn_norm, g_mem_norm, w_in, w_gla_a2, b_gla_a, g_gla_out, b_fox_f, g_fox_out,
           w_mem_kv, g_mem_out, w_out, g_ffn_norm, w_router, b_router, w_moe_up, b_moe_up,
           w_moe_down, b_moe_down, g_final):
    batch, seq, d = x.shape
    n_mem = mem.shape[1]
    n = batch * seq
    depth = w_in.shape[0]
    assert depth == 1, "the combine kernel applies the final norm, so exactly one layer is supported"
    kw = GLA_HEADS * GLA_DK
    vw = GLA_HEADS * GLA_DV
    fw = FOX_HEADS * FOX_DH
    mw = MEM_HEADS * MEM_DH
    o_q, o_k, o_v, o_g = 0, kw, 2 * kw, 2 * kw + vw
    o_a = o_g + vw
    o_fq = o_a + GLA_LOWRANK
    o_fk, o_fv = o_fq + fw, o_fq + 2 * fw
    o_ff = o_fq + 3 * fw
    o_mq = o_ff + FOX_HEADS
    f_lane = GLA_LOWRANK

    xf = x.reshape(n, d)
    for l in range(depth):
        fox_q_scale = FOX_DH ** -0.5 * LOG2E
        col_scale = jnp.ones((w_in.shape[2],), F32).at[o_fq:o_fk].set(fox_q_scale)
        wt = (jnp.transpose(w_in[l]) * col_scale[:, None]).astype(BF16)
        wa_t = jnp.concatenate([wt[o_q:o_a], wt[o_fq:o_ff], wt[o_mq:o_mq + mw]], axis=0)
        wb_t = jnp.zeros((LANES, d), BF16).at[:GLA_LOWRANK].set(wt[o_a:o_fq])
        wb_t = wb_t.at[f_lane:f_lane + FOX_HEADS].set(wt[o_ff:o_mq])
        proj, small = _in_proj(xf, g_attn_norm[l].reshape(1, d), wa_t, wb_t)

        wa2p = jnp.zeros((LANES, kw), F32).at[:GLA_LOWRANK].set(w_gla_a2[l]).astype(BF16)
        bfv = jnp.zeros((1, LANES), F32).at[0, f_lane:f_lane + FOX_HEADS].set(b_fox_f[l])
        gla, f_cum = _gla(proj, small, wa2p, b_gla_a[l].reshape(1, kw), bfv,
                          g_gla_out[l].reshape(1, vw), batch, seq)

        f_t = f_cum.reshape(batch, seq, LANES)[:, :, f_lane:f_lane + FOX_HEADS].transpose(0, 2, 1)
        f_t = jnp.concatenate([f_t, jnp.zeros_like(f_t)], axis=1)
        fox = _fox(proj, f_t, g_fox_out[l].reshape(1, fw), batch, seq)

        kv = _mem_kv(mem.reshape(batch * n_mem, d), g_mem_norm[l].reshape(1, d), w_mem_kv[l].astype(BF16))
        memo = _mem_attn(proj, kv, g_mem_out[l].reshape(1, mw), batch, seq, n_mem)

        wr = jnp.zeros((d, LANES), F32).at[:, :N_EXPERTS].set(w_router[l])
        wr_hi = wr.astype(BF16)
        wr = jnp.concatenate([wr_hi, (wr - wr_hi.astype(F32)).astype(BF16)], axis=1)
        br =jnp.zeros((1, LANES), F32).at[0, :N_EXPERTS].set(b_router[l])
        x2, xn, logits = _out_proj(xf, gla, fox, memo, w_out[l].astype(BF16),
                                   g_ffn_norm[l].reshape(1, d), wr, br)

        idx, gates, rank, cnt = _route(logits)
        pos, blk_start, blk_count = _routing_tables(idx, rank, cnt)
        n_blocks_max = -(-(n * TOP_K + N_EXPERTS * (MOE_TM - 1)) // MOE_TM)
        blk_end = blk_start + blk_count
        n_used = blk_end[-1:]
        x_rows = _dispatch(xn, pos, blk_end - 1, (blk_count > 0).astype(jnp.int32), n_used,
                           n_blocks_max, d // 2 // LANES)
        d_ff = w_moe_up.shape[3] // 2
        act = _moe_up(x_rows, w_moe_up[l], b_moe_up[l].reshape(N_EXPERTS, 1, 2 * d_ff),
                      blk_start, blk_count, n_used, n_blocks_max)
        y_rows = _moe_down(act, w_moe_down[l], b_moe_down[l].reshape(N_EXPERTS, 1, d),
                           blk_start, blk_count, n_used, n_blocks_max)
        xf = _combine(y_rows, pos, gates, x2, g_final.reshape(1, d))
    return xf.reshape(batch, seq, d)
```

```python
import functools

import jax
import jax.numpy as jnp
from jax import lax
from jax.experimental import pallas as pl
from jax.experimental.pallas import tpu as pltpu

EPS = 1e-5
CHUNK = 64
GLA_HEADS = 4
GLA_DK = 128
GLA_DV = 256
GLA_LOWRANK = 16
GLA_TAU = 16.0
FOX_HEADS = 4
FOX_DH = 128
MEM_HEADS = 4
MEM_DH = 128
N_EXPERTS = 32
TOP_K = 4
SWIGLU_LIMIT = 7.0
SWIGLU_ALPHA = 1.702
LANES = 128
BF16_SUBLANES = 16
MXU_COLS = 256
MOE_TM = 256
MOE_NBUF = 2
MOE_UNIT = 2
VMEM_LIMIT = 56 * 1024 * 1024

GLA_KW = GLA_HEADS * GLA_DK
GLA_VW = GLA_HEADS * GLA_DV
FOX_W = FOX_HEADS * FOX_DH
MEM_W = MEM_HEADS * MEM_DH
PROJ_GLA_Q = 0
PROJ_GLA_K = PROJ_GLA_Q + GLA_KW
PROJ_GLA_V = PROJ_GLA_K + GLA_KW
PROJ_GLA_GATE = PROJ_GLA_V + GLA_VW
PROJ_FOX_Q = PROJ_GLA_GATE + GLA_VW
PROJ_FOX_K = PROJ_FOX_Q + FOX_W
PROJ_FOX_V = PROJ_FOX_K + FOX_W

F32 = jnp.float32
BF16 = jnp.bfloat16


def _cparams(sem, vmem=VMEM_LIMIT):
    return pltpu.CompilerParams(dimension_semantics=sem, vmem_limit_bytes=vmem)


def _log_sigmoid(x):
    return jnp.minimum(x, 0.0) - jnp.log1p(jnp.exp(-jnp.abs(x)))


def _rms(x, g):
    return x * lax.rsqrt(jnp.mean(x * x, axis=-1, keepdims=True) + EPS) * g


def _dot(a, b, **kw):
    return jnp.dot(a, b, preferred_element_type=F32, **kw)


def _dot_nt(a, b):
    return lax.dot_general(a, b, (((1,), (1,)), ((), ())), preferred_element_type=F32)


def _dot_tn(a, b):
    return lax.dot_general(a, b, (((0,), (0,)), ((), ())), preferred_element_type=F32)


def _store_rowmajor(ref, x, base=0):
    rows, w = x.shape
    pitch = w // LANES
    for c in range(pitch):
        ref[pl.ds(base + c, rows, stride=pitch), :] = x[:, c * LANES:(c + 1) * LANES]


def _pack_bf16_pairs(x):
    half = x.shape[1] // 2
    bits = lambda v: lax.bitcast_convert_type(v.astype(BF16).astype(F32), jnp.uint32)
    return bits(x[:, half:]) | (bits(x[:, :half]) >> 16)


def _unpack_lo(w):
    return lax.bitcast_convert_type(w << 16, F32)


def _unpack_hi(w):
    return lax.bitcast_convert_type(w & jnp.uint32(0xFFFF0000), F32)


def _in_proj_kernel(offs_ref, x_ref, g_ref, wt_ref, wq_ref, wb_ref, proj_ref, projm_ref, small_ref, h_scr):
    del offs_ref
    @pl.when(pl.program_id(1) == 0)
    def _():
        hb = _rms(x_ref[...], g_ref[...]).astype(BF16)
        h_scr[...] = hb
        small_ref[...] = _dot_nt(hb, wb_ref[...])
        projm_ref[...] = _dot_nt(hb, wq_ref[...]).astype(projm_ref.dtype)

    proj_ref[...] = _dot_nt(h_scr[...], wt_ref[...]).astype(proj_ref.dtype)


def _in_proj(x2d, g, wt, wq_t, wb_t, row_offs, tn, tm=1024):
    n, d = x2d.shape
    nj = len(row_offs)
    mw = wq_t.shape[0]
    offs = jnp.asarray(row_offs, jnp.int32)
    grid_spec = pltpu.PrefetchScalarGridSpec(
        num_scalar_prefetch=1,
        grid=(n // tm, nj),
        in_specs=[
            pl.BlockSpec((tm, d), lambda i, j, offs: (i, 0)),
            pl.BlockSpec((1, d), lambda i, j, offs: (0, 0)),
            pl.BlockSpec((pl.Element(tn), pl.Element(d)),
                         lambda i, j, offs: (pl.multiple_of(offs[j], BF16_SUBLANES), 0)),
            pl.BlockSpec((mw, d), lambda i, j, offs: (0, 0)),
            pl.BlockSpec((LANES, d), lambda i, j, offs: (0, 0)),
        ],
        out_specs=[
            pl.BlockSpec((tm, tn), lambda i, j, offs: (i, j)),
            pl.BlockSpec((tm, mw), lambda i, j, offs: (i, 0)),
            pl.BlockSpec((tm, LANES), lambda i, j, offs: (i, 0)),
        ],
        scratch_shapes=[pltpu.VMEM((tm, d), BF16)],
    )
    return pl.pallas_call(
        _in_proj_kernel,
        grid_spec=grid_spec,
        out_shape=[
            jax.ShapeDtypeStruct((n, nj * tn), BF16),
            jax.ShapeDtypeStruct((n, mw), BF16),
            jax.ShapeDtypeStruct((n, LANES), F32),
        ],
        compiler_params=_cparams(("parallel", "arbitrary")),
        name="in_proj",
    )(offs, x2d, g, wt, wq_t, wb_t)


def _gla_kernel(q_ref, k_ref, v_ref, gate_ref, small_ref, wa2_ref, ba_ref, bf_ref, gout_ref,
                o_ref, f_ref, state_scr, fcar_scr, la_scr, lf_scr, *, n_chunks):
    @pl.when(pl.program_id(1) == 0)
    def _():
        state_scr[...] = jnp.zeros_like(state_scr)
        fcar_scr[...] = jnp.zeros_like(fcar_scr)

    small = small_ref[...]
    la_scr[...] = _log_sigmoid(_dot(small.astype(BF16), wa2_ref[...]) + ba_ref[...]) * (1.0 / GLA_TAU)
    lf_scr[...] = _log_sigmoid(small + bf_ref[...])
    row = lax.broadcasted_iota(jnp.int32, (CHUNK, CHUNK), 0)
    col = lax.broadcasted_iota(jnp.int32, (CHUNK, CHUNK), 1)
    tri = (col <= row).astype(F32)
    scale = GLA_DK ** -0.5

    def chunk_body(c, carry):
        r = pl.ds(pl.multiple_of(c * CHUNK, CHUNK), CHUNK)
        b = _dot(tri, la_scr[r, :], precision=lax.Precision.HIGHEST)
        b_end = b[CHUNK - 1:CHUNK, :]
        k_dec = k_ref[r, :].astype(F32) * jnp.exp(b_end - b)
        decay = jnp.exp(b_end)
        f_cum = _dot(tri, lf_scr[r, :], precision=lax.Precision.HIGHEST) + fcar_scr[...]
        f_ref[r, :] = f_cum
        fcar_scr[...] = f_cum[CHUNK - 1:CHUNK, :]
        heads = range(GLA_HEADS)
        ks = [slice(h * GLA_DK, (h + 1) * GLA_DK) for h in heads]
        vs = [slice(h * GLA_DV, (h + 1) * GLA_DV) for h in heads]
        old = [state_scr[h] for h in heads]
        new = [old[h] * decay[:, ks[h]] + _dot_tn(v_ref[r, vs[h]], k_dec[:, ks[h]].astype(BF16))
               for h in heads]
        for h in heads:
            state_scr[h] = new[h]
        for h in heads:
            o = _dot_nt(q_ref[r, ks[h]], new[h].astype(BF16)) * scale
            gt = gate_ref[r, vs[h]].astype(F32)
            o_ref[r, vs[h]] = (_rms(o, gout_ref[:, vs[h]]) * (gt * jax.nn.sigmoid(gt))).astype(o_ref.dtype)
        return carry

    lax.fori_loop(0, n_chunks, chunk_body, 0, unroll=True)


def _gla(proj, small, wa2p, ba, bfv, gout, batch, seq, ts=512):
    n = proj.shape[0]
    nsb = seq // ts
    kw, vw = GLA_KW, GLA_VW
    row = lambda b, s: b * nsb + s
    return pl.pallas_call(
        functools.partial(_gla_kernel, n_chunks=ts // CHUNK),
        grid=(batch, nsb),
        in_specs=[
            pl.BlockSpec((ts, kw), lambda b, s: (row(b, s), PROJ_GLA_Q // kw)),
            pl.BlockSpec((ts, kw), lambda b, s: (row(b, s), PROJ_GLA_K // kw)),
            pl.BlockSpec((ts, vw), lambda b, s: (row(b, s), PROJ_GLA_V // vw)),
            pl.BlockSpec((ts, vw), lambda b, s: (row(b, s), PROJ_GLA_GATE // vw)),
            pl.BlockSpec((ts, LANES), lambda b, s: (row(b, s), 0)),
            pl.BlockSpec((LANES, kw), lambda b, s: (0, 0)),
            pl.BlockSpec((1, kw), lambda b, s: (0, 0)),
            pl.BlockSpec((1, LANES), lambda b, s: (0, 0)),
            pl.BlockSpec((1, vw), lambda b, s: (0, 0)),
        ],
        out_specs=[
            pl.BlockSpec((ts, vw), lambda b, s: (row(b, s), 0)),
            pl.BlockSpec((ts, LANES), lambda b, s: (row(b, s), 0)),
        ],
        out_shape=[
            jax.ShapeDtypeStruct((n, vw), BF16),
            jax.ShapeDtypeStruct((n, LANES), F32),
        ],
        scratch_shapes=[
            pltpu.VMEM((GLA_HEADS, GLA_DV, GLA_DK), F32),
            pltpu.VMEM((1, LANES), F32),
            pltpu.VMEM((ts, kw), F32),
            pltpu.VMEM((ts, LANES), F32),
        ],
        compiler_params=_cparams(("parallel", "arbitrary")),
        name="gla",
    )(proj, proj, proj, proj, small, wa2p, ba, bfv, gout)


LOG2E = 1.4426950408889634
def _fox_kernel(q_ref, k_ref, v_ref, fk_ref, g_ref, o_ref, m_scr, l_scr, acc_scr, s_even, s_odd,
                *, tq, tk):
    i = pl.program_id(1)
    t = pl.program_id(2)

    @pl.when(t == 0)
    def _():
        m_scr[...] = jnp.full_like(m_scr, -jnp.inf)
        l_scr[...] = jnp.zeros_like(l_scr)
        acc_scr[...] = jnp.zeros_like(acc_scr)

    heads = range(FOX_HEADS)
    hs = [slice(h * FOX_DH, (h + 1) * FOX_DH) for h in heads]

    def score(s_out):
        for h in heads:
            s_out[h] = _dot_nt(q_ref[:, hs[h]], k_ref[:, hs[h]])

    def update(s_in, on_diagonal):
        m_prev = [m_scr[h] for h in heads]
        l_prev = [l_scr[h] for h in heads]
        acc_prev = [acc_scr[h] for h in heads]
        s = [s_in[h] - fk_ref[h:h + 1, :] * LOG2E for h in heads]
        if on_diagonal:
            row = lax.broadcasted_iota(jnp.int32, (tq, tk), 0)
            col = lax.broadcasted_iota(jnp.int32, (tq, tk), 1)
            s = [jnp.where(col <= row, sh, -jnp.inf) for sh in s]
        lane_chunks = [slice(c * LANES, (c + 1) * LANES) for c in range(tk // LANES)]
        m_new, l_new, acc_new = [], [], []
        for h in heads:
            chunks = [s[h][:, lc] for lc in lane_chunks]
            cmax = functools.reduce(jnp.maximum, chunks)
            m_h = jnp.maximum(m_prev[h], jnp.max(cmax, axis=-1, keepdims=True))
            alpha = jnp.exp2(m_prev[h] - m_h)
            p_chunks = [jnp.exp2(ch - m_h) for ch in chunks]
            row_sum = jnp.sum(functools.reduce(jnp.add, p_chunks), axis=-1, keepdims=True)
            p = jnp.concatenate(p_chunks, axis=1).astype(BF16)
            m_new.append(m_h)
            l_new.append(alpha * l_prev[h] + row_sum)
            acc_new.append(alpha * acc_prev[h] + _dot(p, v_ref[:, hs[h]]))
        return m_new, l_new, acc_new

    def step(s_write, s_read):
        @pl.when(t == 0)
        def _():
            score(s_write)

        @pl.when(jnp.logical_and(t >= 1, t <= i))
        def _():
            m_new, l_new, acc_new = update(s_read, False)
            score(s_write)
            for h in heads:
                m_scr[h] = m_new[h]
                l_scr[h] = l_new[h]
                acc_scr[h] = acc_new[h]

        @pl.when(t == i + 1)
        def _():
            _, l_new, acc_new = update(s_read, True)
            for h in heads:
                o_ref[:, hs[h]] = _rms(acc_new[h] / l_new[h], g_ref[:, hs[h]]).astype(o_ref.dtype)

    @pl.when(t % 2 == 0)
    def _():
        step(s_even, s_odd)

    @pl.when(t % 2 == 1)
    def _():
        step(s_odd, s_even)


def _fox(proj, f_t, g_fox, batch, seq, tq=512, tk=512):
    assert tq == tk, "the diagonal-block mask assumes square blocks"
    n = proj.shape[0]
    w = FOX_HEADS * FOX_DH
    nq, nk = seq // tq, seq // tk
    qcol, kcol, vcol = PROJ_FOX_Q // w, PROJ_FOX_K // w, PROJ_FOX_V // w
    return pl.pallas_call(
        functools.partial(_fox_kernel, tq=tq, tk=tk),
        grid=(batch, nq, nk + 1),
        in_specs=[
            pl.BlockSpec((tq, w), lambda b, i, t: (b * nq + i, qcol)),
            pl.BlockSpec((tk, w), lambda b, i, t: (b * nk + jnp.minimum(t, i), kcol)),
            pl.BlockSpec((tk, w), lambda b, i, t: (b * nk + jnp.clip(t - 1, 0, i), vcol)),
            pl.BlockSpec((None, 8, tk), lambda b, i, t: (b, 0, jnp.clip(t - 1, 0, i))),
            pl.BlockSpec((1, w), lambda b, i, t: (0, 0)),
        ],
        out_specs=pl.BlockSpec((tq, w), lambda b, i, t: (b * nq + i, 0)),
        out_shape=jax.ShapeDtypeStruct((n, w), BF16),
        scratch_shapes=[
            pltpu.VMEM((FOX_HEADS, tq, LANES), F32),
            pltpu.VMEM((FOX_HEADS, tq, LANES), F32),
            pltpu.VMEM((FOX_HEADS, tq, FOX_DH), F32),
            pltpu.VMEM((FOX_HEADS, tq, tk), F32),
            pltpu.VMEM((FOX_HEADS, tq, tk), F32),
        ],
        compiler_params=_cparams(("parallel", "parallel", "arbitrary")),
        name="fox",
    )(proj, proj, proj, f_t, g_fox)


def _mem_kv_kernel(m_ref, g_ref, w_ref, kv_ref):
    kv_ref[...] = _dot(_rms(m_ref[...], g_ref[...]).astype(BF16), w_ref[...]).astype(kv_ref.dtype)


def _mem_kv(mem2d, g, w, tm=256):
    n, d = mem2d.shape
    nw = w.shape[1]
    return pl.pallas_call(
        _mem_kv_kernel,
        grid=(n // tm,),
        in_specs=[
            pl.BlockSpec((tm, d), lambda i: (i, 0)),
            pl.BlockSpec((1, d), lambda i: (0, 0)),
            pl.BlockSpec((d, nw), lambda i: (0, 0)),
        ],
        out_specs=pl.BlockSpec((tm, nw), lambda i: (i, 0)),
        out_shape=jax.ShapeDtypeStruct((n, nw), BF16),
        compiler_params=_cparams(("parallel",)),
        name="mem_kv",
    )(mem2d, g, w)


def _mem_attn_kernel(q_ref, k_ref, v_ref, g_ref, o_ref):
    scale = MEM_DH ** -0.5
    for h in range(MEM_HEADS):
        hs = slice(h * MEM_DH, (h + 1) * MEM_DH)
        s = _dot_nt(q_ref[:, hs], k_ref[:, hs]) * scale
        p = jnp.exp(s - jnp.max(s, axis=-1, keepdims=True))
        l = jnp.sum(p, axis=-1, keepdims=True)
        o = _dot((p / l).astype(BF16), v_ref[:, hs])
        o_ref[:, hs] = _rms(o, g_ref[:, hs]).astype(o_ref.dtype)


def _mem_attn(proj, kv, g_mem_out, batch, seq, n_mem, tq=1024):
    n = proj.shape[0]
    w = MEM_HEADS * MEM_DH
    nq = seq // tq
    qcol = 0
    return pl.pallas_call(
        _mem_attn_kernel,
        grid=(n // tq,),
        in_specs=[
            pl.BlockSpec((tq, w), lambda i: (i, qcol)),
            pl.BlockSpec((n_mem, w), lambda i: (i // nq, 0)),
            pl.BlockSpec((n_mem, w), lambda i: (i // nq, 1)),
            pl.BlockSpec((1, w), lambda i: (0, 0)),
        ],
        out_specs=pl.BlockSpec((tq, w), lambda i: (i, 0)),
        out_shape=jax.ShapeDtypeStruct((n, w), BF16),
        compiler_params=_cparams(("parallel",)),
        name="mem_attn",
    )(proj, kv, kv, g_mem_out)


def _out_proj_kernel(x_ref, gla_ref, fox_ref, mem_ref, w1_ref, w2_ref, w3_ref, g_ref, wr_ref, br_ref,
                     x2_ref, xn_ref, logit_ref):
    x2 = (x_ref[...] + _dot(gla_ref[...], w1_ref[...]) + _dot(fox_ref[...], w2_ref[...])
          + _dot(mem_ref[...], w3_ref[...]))
    x2_ref[...] = x2
    xn = _rms(x2, g_ref[...])
    _store_rowmajor(xn_ref, _pack_bf16_pairs(xn))
    xh = xn.astype(BF16)
    xl = (xn - xh.astype(F32)).astype(BF16)
    hi = _dot(xh, wr_ref[...])
    logit_ref[...] = hi[:, :LANES] + hi[:, LANES:] + _dot(xl, wr_ref[:, :LANES]) + br_ref[...]


def _out_proj(x2d, gla, fox, memo, w_out, g_ffn, wr, br, tm=512):
    n, d = x2d.shape
    w1, w2 = gla.shape[1], fox.shape[1]
    const = lambda i: (0, 0)
    return pl.pallas_call(
        _out_proj_kernel,
        grid=(n // tm,),
        in_specs=[
            pl.BlockSpec((tm, d), lambda i: (i, 0)),
            pl.BlockSpec((tm, w1), lambda i: (i, 0)),
            pl.BlockSpec((tm, w2), lambda i: (i, 0)),
            pl.BlockSpec((tm, w2), lambda i: (i, 0)),
            pl.BlockSpec((w1, d), lambda i: (0, 0)),
            pl.BlockSpec((w2, d), lambda i: (w1 // w2, 0)),
            pl.BlockSpec((w2, d), lambda i: (w1 // w2 + 1, 0)),
            pl.BlockSpec((1, d), const),
            pl.BlockSpec((d, 2 * LANES), const),
            pl.BlockSpec((1, LANES), const),
        ],
        out_specs=[
            pl.BlockSpec((tm, d), lambda i: (i, 0)),
            pl.BlockSpec((tm * (d // 2 // LANES), LANES), lambda i: (i, 0)),
            pl.BlockSpec((tm, LANES), lambda i: (i, 0)),
        ],
        out_shape=[
            jax.ShapeDtypeStruct((n, d), F32),
            jax.ShapeDtypeStruct((n * (d // 2 // LANES), LANES), jnp.uint32),
            jax.ShapeDtypeStruct((n, LANES), F32),
        ],
        compiler_params=_cparams(("parallel",)),
        name="out_proj",
    )(x2d, gla, fox, memo, w_out, w_out, w_out, g_ffn, wr, br)


def _route_kernel(l_ref, idx_ref, gate_ref, rank_ref, cnt_ref, carry_scr, *, tb):
    @pl.when(pl.program_id(0) == 0)
    def _():
        carry_scr[...] = jnp.zeros_like(carry_scr)

    lane = lax.broadcasted_iota(jnp.int32, (tb, LANES), 1)
    logit = jnp.where(lane < N_EXPERTS, l_ref[...], -jnp.inf)
    vals, hots = [], []
    idx_out = jnp.zeros((tb, LANES), jnp.int32)
    for k in range(TOP_K):
        m = jnp.max(logit, axis=-1, keepdims=True)
        ik = jnp.min(jnp.where(logit == m, lane, LANES), axis=-1, keepdims=True)
        hot = lane == ik
        logit = jnp.where(hot, -jnp.inf, logit)
        vals.append(m)
        hots.append(hot)
        idx_out = jnp.where(lane == k, ik, idx_out)
    idx_ref[...] = idx_out

    e = [jnp.exp(v - vals[0]) for v in vals]
    den = e[0] + e[1] + e[2] + e[3]
    gate_out = jnp.zeros((tb, LANES), F32)
    for k in range(TOP_K):
        gate_out = jnp.where(lane == k, e[k] / den, gate_out)
    gate_ref[...] = gate_out

    member = jnp.zeros((tb, LANES), F32)
    for hot in hots:
        member = member + hot.astype(F32)
    row = lax.broadcasted_iota(jnp.int32, (tb, tb), 0)
    col = lax.broadcasted_iota(jnp.int32, (tb, tb), 1)
    before = (col < row).astype(BF16)
    rank = _dot(before, member.astype(BF16)) + carry_scr[...]
    rank_out = jnp.zeros((tb, LANES), F32)
    for k in range(TOP_K):
        rk = jnp.sum(jnp.where(hots[k], rank, 0.0), axis=-1, keepdims=True)
        rank_out = jnp.where(lane == k, rk, rank_out)
    rank_ref[...] = rank_out.astype(jnp.int32)
    total = carry_scr[...] + jnp.sum(member, axis=0, keepdims=True)
    carry_scr[...] = total
    cnt_ref[...] = total.astype(jnp.int32)


def _route(logits, tb=512):
    n = logits.shape[0]
    blk = pl.BlockSpec((tb, LANES), lambda i: (i, 0))
    return pl.pallas_call(
        functools.partial(_route_kernel, tb=tb),
        grid=(n // tb,),
        in_specs=[blk],
        out_specs=[blk, blk, blk, pl.BlockSpec((1, LANES), lambda i: (0, 0))],
        out_shape=[
            jax.ShapeDtypeStruct((n, LANES), jnp.int32),
            jax.ShapeDtypeStruct((n, LANES), F32),
            jax.ShapeDtypeStruct((n, LANES), jnp.int32),
            jax.ShapeDtypeStruct((1, LANES), jnp.int32),
        ],
        scratch_shapes=[pltpu.VMEM((1, LANES), F32)],
        compiler_params=_cparams(("arbitrary",)),
        name="route",
    )(logits)


def _dispatch_kernel(zblk_ref, zok_ref, used_ref, pos_ref, xn_ref, xr_hbm, zero_scr, sem,
                     *, tb, pitch, n_blocks_max):
    blk_rows = MOE_TM * pitch

    def zero_copy(b):
        dst = xr_hbm.at[pl.ds(pl.multiple_of(b * blk_rows, blk_rows), blk_rows), :]
        return pltpu.make_async_copy(zero_scr, dst, sem.at[1])

    @pl.when(pl.program_id(0) == 0)
    def _():
        zero_scr[...] = jnp.zeros_like(zero_scr)

        def per_expert(action):
            def body(e, carry):
                @pl.when(zok_ref[e] == 1)
                def _():
                    action(zero_copy(zblk_ref[e]))
                return carry
            lax.fori_loop(0, N_EXPERTS, body, 0)

        def per_tail(action):
            def body(b, carry):
                action(zero_copy(b))
                return carry
            lax.fori_loop(used_ref[0], n_blocks_max, body, 0)

        per_expert(lambda c: c.start())
        per_tail(lambda c: c.start())
        per_expert(lambda c: c.wait())
        per_tail(lambda c: c.wait())

    def issue(t, carry):
        src = xn_ref.at[pl.ds(pl.multiple_of(t * pitch, pitch), pitch), :]
        for k in range(TOP_K):
            p = pos_ref[0, t * TOP_K + k]
            dst = xr_hbm.at[pl.ds(pl.multiple_of(p * pitch, pitch), pitch), :]
            pltpu.make_async_copy(src, dst, sem.at[0]).start(priority=k % 2)
        return carry

    lax.fori_loop(0, tb, issue, 0, unroll=4)
    for k in range(TOP_K):
        pltpu.make_async_copy(xn_ref, xr_hbm.at[pl.ds(0, tb * pitch), :], sem.at[0]).wait()


def _dispatch(xn_rm, pos, zblk, zok, n_used, n_blocks_max, pitch, tb=512):
    n = xn_rm.shape[0] // pitch
    nb = n // tb
    grid_spec = pltpu.PrefetchScalarGridSpec(
        num_scalar_prefetch=3,
        grid=(nb,),
        in_specs=[
            pl.BlockSpec((None, 1, TOP_K * tb), lambda i, *_: (i, 0, 0), memory_space=pltpu.SMEM),
            pl.BlockSpec((tb * pitch, LANES), lambda i, *_: (i, 0)),
        ],
        out_specs=pl.BlockSpec(memory_space=pl.ANY),
        scratch_shapes=[pltpu.VMEM((MOE_TM * pitch, LANES), xn_rm.dtype), pltpu.SemaphoreType.DMA((2,))],
    )
    return pl.pallas_call(
        functools.partial(_dispatch_kernel, tb=tb, pitch=pitch, n_blocks_max=n_blocks_max),
        grid_spec=grid_spec,
        out_shape=jax.ShapeDtypeStruct((n_blocks_max * MOE_TM * pitch, LANES), xn_rm.dtype),
        compiler_params=_cparams(("arbitrary",)),
        name="dispatch",
    )(zblk, zok, n_used, pos.reshape(nb, 1, TOP_K * tb), xn_rm)


def _stream_row_blocks(cnt, in_copy, out_copy, compute):
    for ahead in range(MOE_NBUF - 1):
        @pl.when(cnt > ahead)
        def _():
            in_copy(ahead, ahead).start()

    def body(b, carry):
        slot = b % MOE_NBUF

        @pl.when(b + MOE_NBUF - 1 < cnt)
        def _():
            in_copy(b + MOE_NBUF - 1, (b + MOE_NBUF - 1) % MOE_NBUF).start()

        in_copy(b, slot).wait()

        @pl.when(b >= MOE_NBUF)
        def _():
            out_copy(b - MOE_NBUF, slot).wait()

        compute(slot)
        out_copy(b, slot).start(priority=1)
        return carry

    lax.fori_loop(0, cnt, body, 0)

    for back in range(MOE_NBUF, 0, -1):
        @pl.when(cnt >= back)
        def _():
            out_copy(cnt - back, (cnt - back) % MOE_NBUF).wait()


def _stream_expert_rows(first, cnt, in_copy, out_copy, compute):
    n_units = cnt // MOE_UNIT
    _stream_row_blocks(n_units, in_copy(first, MOE_UNIT), out_copy(first, MOE_UNIT), compute(MOE_UNIT))
    done = n_units * MOE_UNIT
    _stream_row_blocks(cnt - done, in_copy(first + done, 1), out_copy(first + done, 1), compute(1))


def _zero_fill_blocks(first, last, zero_src, dst_copy):
    def start(b, carry):
        dst_copy(b).start()
        return carry

    def wait(b, carry):
        dst_copy(b).wait()
        return carry

    zero_src[...] = jnp.zeros_like(zero_src)
    lax.fori_loop(first, last, start, 0)
    lax.fori_loop(first, last, wait, 0)


def _moe_up_kernel(bs_ref, bc_ref, used_ref, x_hbm, w_hbm, bg_ref, bu_ref, act_hbm,
                   w_scr, xbuf, lhs_scr, obuf, sem_w, sem_x, sem_o,
                   *, nt, tn, d_ff, pitch, n_blocks_max):
    g = pl.program_id(0)
    ng = pl.num_programs(0)
    e = g // nt
    col = pl.multiple_of((g % nt) * tn, tn)
    blk_rows = MOE_TM * pitch
    wslot = g % 2

    def w_copies(step, slot):
        ee = step // nt
        cc = pl.multiple_of((step % nt) * tn, tn)
        return (pltpu.make_async_copy(w_hbm.at[ee, :, pl.ds(cc, tn)], w_scr.at[slot, 0], sem_w.at[slot, 0]),
                pltpu.make_async_copy(w_hbm.at[ee, :, pl.ds(d_ff + cc, tn)], w_scr.at[slot, 1],
                                      sem_w.at[slot, 1]))

    @pl.when(g == 0)
    def _():
        for c in w_copies(0, 0):
            c.start(priority=1)

    @pl.when(g + 1 < ng)
    def _():
        for c in w_copies(g + 1, 1 - wslot):
            c.start(priority=1)

    for c in w_copies(g, wslot):
        c.wait()

    start = bs_ref[e]

    def x_copy(first, nblk):
        def make(u, slot):
            r0 = pl.multiple_of((first + u * nblk) * blk_rows, blk_rows)
            return pltpu.make_async_copy(x_hbm.at[pl.ds(r0, nblk * blk_rows), :],
                                         xbuf.at[slot, pl.ds(0, nblk * blk_rows), :], sem_x.at[slot])
        return make

    def o_copy(first, nblk):
        def make(u, slot):
            r0 = pl.multiple_of((first + u * nblk) * MOE_TM, MOE_TM)
            return pltpu.make_async_copy(obuf.at[slot, pl.ds(0, nblk * MOE_TM), :],
                                         act_hbm.at[pl.ds(r0, nblk * MOE_TM), pl.ds(col, tn)], sem_o.at[slot])
        return make

    def compute(nblk):
        rows = nblk * MOE_TM

        def run(slot):
            half = pitch * LANES
            for c in range(pitch):
                w = xbuf[slot, pl.ds(c, rows, stride=pitch), :]
                lhs_scr[0:rows, c * LANES:(c + 1) * LANES] = _unpack_lo(w)
                lhs_scr[0:rows, half + c * LANES:half + (c + 1) * LANES] = _unpack_hi(w)
            x = lhs_scr[0:rows, :]
            gate = jnp.minimum(_dot(x, w_scr[wslot, 0]) + bg_ref[...], SWIGLU_LIMIT)
            up = jnp.clip(_dot(x, w_scr[wslot, 1]) + bu_ref[...], -SWIGLU_LIMIT, SWIGLU_LIMIT)
            obuf[slot, 0:rows, :] = (gate * jax.nn.sigmoid(SWIGLU_ALPHA * gate) * (up + 1.0)).astype(obuf.dtype)
        return run

    _stream_expert_rows(start, bc_ref[e], x_copy, o_copy, compute)

    @pl.when(e == N_EXPERTS - 1)
    def _():
        zero_blk = obuf.at[0, pl.ds(0, MOE_TM), :]

        def tail_copy(b):
            r0 = pl.multiple_of(b * MOE_TM, MOE_TM)
            return pltpu.make_async_copy(zero_blk, act_hbm.at[pl.ds(r0, MOE_TM), pl.ds(col, tn)], sem_o.at[0])
        _zero_fill_blocks(used_ref[0], n_blocks_max, zero_blk, tail_copy)


def _moe_up(x_rows_rm, w_up, b_up, blk_start, blk_count, n_used, n_blocks_max, tn=1024):
    n_exp, d, two_ff = w_up.shape
    d_ff = two_ff // 2
    pitch = d // 2 // LANES
    nt = d_ff // tn
    up_off = d_ff // tn
    grid_spec = pltpu.PrefetchScalarGridSpec(
        num_scalar_prefetch=3,
        grid=(n_exp * nt,),
        in_specs=[
            pl.BlockSpec(memory_space=pl.ANY),
            pl.BlockSpec(memory_space=pl.ANY),
            pl.BlockSpec((None, 1, tn), lambda g, *_: (g // nt, 0, g % nt)),
            pl.BlockSpec((None, 1, tn), lambda g, *_: (g // nt, 0, g % nt + up_off)),
        ],
        out_specs=pl.BlockSpec(memory_space=pl.ANY),
        scratch_shapes=[
            pltpu.VMEM((2, 2, d, tn), F32),
            pltpu.VMEM((MOE_NBUF, MOE_UNIT * MOE_TM * pitch, LANES), jnp.uint32),
            pltpu.VMEM((MOE_UNIT * MOE_TM, d), F32),
            pltpu.VMEM((MOE_NBUF, MOE_UNIT * MOE_TM, tn), BF16),
            pltpu.SemaphoreType.DMA((2, 2)),
            pltpu.SemaphoreType.DMA((MOE_NBUF,)),
            pltpu.SemaphoreType.DMA((MOE_NBUF,)),
        ],
    )
    return pl.pallas_call(
        functools.partial(_moe_up_kernel, nt=nt, tn=tn, d_ff=d_ff, pitch=pitch, n_blocks_max=n_blocks_max),
        grid_spec=grid_spec,
        out_shape=jax.ShapeDtypeStruct((n_blocks_max * MOE_TM, d_ff), BF16),
        compiler_params=_cparams(("arbitrary",)),
        name="moe_up",
    )(blk_start, blk_count, n_used, x_rows_rm, w_up, b_up, b_up)


def _moe_down_kernel(bs_ref, bc_ref, used_ref, a_hbm, w_hbm, b_ref, y_hbm,
                     w_scr, abuf, obuf, sem_w, sem_a, sem_o, *, pitch, n_blocks_max):
    e = pl.program_id(0)
    blk_rows = MOE_TM * pitch
    wslot = e % 2

    def w_copy(ee, slot):
        return pltpu.make_async_copy(w_hbm.at[ee], w_scr.at[slot], sem_w.at[slot])

    @pl.when(e == 0)
    def _():
        w_copy(0, 0).start(priority=1)

    @pl.when(e + 1 < pl.num_programs(0))
    def _():
        w_copy(e + 1, 1 - wslot).start(priority=1)

    w_copy(e, wslot).wait()
    start = bs_ref[e]

    def a_copy(first, nblk):
        def make(u, slot):
            r0 = pl.multiple_of((first + u * nblk) * MOE_TM, MOE_TM)
            return pltpu.make_async_copy(a_hbm.at[pl.ds(r0, nblk * MOE_TM), :],
                                         abuf.at[slot, pl.ds(0, nblk * MOE_TM), :], sem_a.at[slot])
        return make

    def o_copy(first, nblk):
        def make(u, slot):
            r0 = pl.multiple_of((first + u * nblk) * blk_rows, blk_rows)
            return pltpu.make_async_copy(obuf.at[slot, pl.ds(0, nblk * blk_rows), :],
                                         y_hbm.at[pl.ds(r0, nblk * blk_rows), :], sem_o.at[slot])
        return make

    def compute(nblk):
        rows = nblk * MOE_TM

        def run(slot):
            y = _dot(abuf[slot, 0:rows, :].astype(F32), w_scr[wslot]) + b_ref[...]
            _store_rowmajor(obuf.at[slot], _pack_bf16_pairs(y))
        return run

    _stream_expert_rows(start, bc_ref[e], a_copy, o_copy, compute)

    @pl.when(e == N_EXPERTS - 1)
    def _():
        zero_blk = obuf.at[0, pl.ds(0, blk_rows), :]

        def tail_copy(b):
            r0 = pl.multiple_of(b * blk_rows, blk_rows)
            return pltpu.make_async_copy(zero_blk, y_hbm.at[pl.ds(r0, blk_rows), :], sem_o.at[0])
        _zero_fill_blocks(used_ref[0], n_blocks_max, zero_blk, tail_copy)


def _moe_down(act, w_down, b_down, blk_start, blk_count, n_used, n_blocks_max):
    n_rows, d_ff = act.shape
    n_exp, _, d = w_down.shape
    pitch = d // 2 // LANES
    grid_spec = pltpu.PrefetchScalarGridSpec(
        num_scalar_prefetch=3,
        grid=(n_exp,),
        in_specs=[
            pl.BlockSpec(memory_space=pl.ANY),
            pl.BlockSpec(memory_space=pl.ANY),
            pl.BlockSpec((None, 1, d), lambda e, *_: (e, 0, 0)),
        ],
        out_specs=pl.BlockSpec(memory_space=pl.ANY),
        scratch_shapes=[
            pltpu.VMEM((2, d_ff, d), F32),
            pltpu.VMEM((MOE_NBUF, MOE_UNIT * MOE_TM, d_ff), BF16),
            pltpu.VMEM((MOE_NBUF, MOE_UNIT * MOE_TM * pitch, LANES), jnp.uint32),
            pltpu.SemaphoreType.DMA((2,)),
            pltpu.SemaphoreType.DMA((MOE_NBUF,)),
            pltpu.SemaphoreType.DMA((MOE_NBUF,)),
        ],
    )
    return pl.pallas_call(
        functools.partial(_moe_down_kernel, pitch=pitch, n_blocks_max=n_blocks_max),
        grid_spec=grid_spec,
        out_shape=jax.ShapeDtypeStruct((n_rows * pitch, LANES), jnp.uint32),
        compiler_params=_cparams(("arbitrary",)),
        name="moe_down",
    )(blk_start, blk_count, n_used, act, w_down, b_down)


def _combine_kernel(pos0_ref, posn_ref, y_hbm, x2_ref, gate_ref, g_ref, o_ref, buf_even, buf_odd, sem,
                    *, tb, pitch):
    i = pl.program_id(0)
    last = pl.num_programs(0) - 1
    n_rows = TOP_K * tb

    def row_copy(p_ref, r, buf, sem_slot):
        src = y_hbm.at[pl.ds(pl.multiple_of(p_ref[0, r] * pitch, pitch), pitch), :]
        return pltpu.make_async_copy(src, buf.at[pl.ds(r * pitch, pitch), :], sem.at[sem_slot])

    def wait_block(buf, sem_slot):
        pltpu.make_async_copy(y_hbm.at[pl.ds(0, n_rows * pitch), :], buf, sem.at[sem_slot]).wait()

    @pl.when(i == 0)
    def _():
        def issue(j, carry):
            for half in range(2):
                row_copy(pos0_ref, 2 * j + half, buf_even, 0).start(priority=half)
            return carry
        lax.fori_loop(0, n_rows // 2, issue, 0, unroll=4)

    def step(buf_cur, sem_cur, buf_nxt, sem_nxt):
        wait_block(buf_cur, sem_cur)
        for r in range(n_rows):
            row_copy(posn_ref, r, buf_nxt, sem_nxt).start(priority=r % 2)
        gates = gate_ref[...]
        half = pitch * LANES
        ssq = jnp.zeros((tb, 1), F32)
        for c in range(pitch):
            lo = slice(c * LANES, (c + 1) * LANES)
            hi = slice(half + c * LANES, half + (c + 1) * LANES)
            z_lo = x2_ref[:, lo]
            z_hi = x2_ref[:, hi]
            for k in range(TOP_K):
                w = buf_cur[pl.ds(k * tb * pitch + c, tb, stride=pitch), :]
                z_lo = z_lo + gates[:, k:k + 1] * _unpack_lo(w)
                z_hi = z_hi + gates[:, k:k + 1] * _unpack_hi(w)
            o_ref[:, lo] = z_lo
            o_ref[:, hi] = z_hi
            ssq = ssq + (jnp.sum(z_lo * z_lo, axis=-1, keepdims=True)
                         + jnp.sum(z_hi * z_hi, axis=-1, keepdims=True))
        o_ref[...] = o_ref[...] * lax.rsqrt(ssq * (1.0 / (2 * half)) + EPS) * g_ref[...]

        @pl.when(i == last)
        def _():
            wait_block(buf_nxt, sem_nxt)

    @pl.when(i % 2 == 0)
    def _():
        step(buf_even, 0, buf_odd, 1)

    @pl.when(i % 2 == 1)
    def _():
        step(buf_odd, 1, buf_even, 0)


def _combine(y_rows_rm, pos, gates, x2, g_final, tb=256):
    n, d = x2.shape
    nb = n // tb
    pitch = d // 2 // LANES
    pos_blk = pos.reshape(nb, tb, TOP_K).transpose(0, 2, 1).reshape(nb, 1, TOP_K * tb)
    pos_spec = lambda imap: pl.BlockSpec((None, 1, TOP_K * tb), imap, memory_space=pltpu.SMEM)
    return pl.pallas_call(
        functools.partial(_combine_kernel, tb=tb, pitch=pitch),
        grid=(nb,),
        in_specs=[
            pos_spec(lambda i: (0, 0, 0)),
            pos_spec(lambda i: (jnp.minimum(i + 1, nb - 1), 0, 0)),
            pl.BlockSpec(memory_space=pl.ANY),
            pl.BlockSpec((tb, d), lambda i: (i, 0)),
            pl.BlockSpec((tb, LANES), lambda i: (i, 0)),
            pl.BlockSpec((1, d), lambda i: (0, 0)),
        ],
        out_specs=pl.BlockSpec((tb, d), lambda i: (i, 0)),
        out_shape=jax.ShapeDtypeStruct((n, d), F32),
        scratch_shapes=[pltpu.VMEM((TOP_K * tb * pitch, LANES), jnp.uint32),
                        pltpu.VMEM((TOP_K * tb * pitch, LANES), jnp.uint32),
                        pltpu.SemaphoreType.DMA((2,))],
        compiler_params=_cparams(("arbitrary",)),
        name="combine",
    )(pos_blk, pos_blk, y_rows_rm, x2, gates, g_final)


def _routing_tables(idx, rank, cnt):
    counts = cnt[0, :N_EXPERTS]
    blk_count = (counts + MOE_TM - 1) // MOE_TM
    blk_end = jnp.cumsum(blk_count)
    blk_start = blk_end - blk_count
    hot = idx[:, :TOP_K, None] == jnp.arange(N_EXPERTS, dtype=jnp.int32)
    pos = jnp.sum(jnp.where(hot, blk_start * MOE_TM, 0), axis=-1) + rank[:, :TOP_K]
    return pos.astype(jnp.int32), blk_start.astype(jnp.int32), blk_count.astype(jnp.int32)


def kernel(x, mem, g_attn_norm, g_mem_norm, w_in, w_gla_a2, b_gla_a, g_gla_out, b_fox_f, g_fox_out,
           w_mem_kv, g_mem_out, w_out, g_ffn_norm, w_router, b_router, w_moe_up, b_moe_up,
           w_moe_down, b_moe_down, g_final):
    batch, seq, d = x.shape
    n_mem = mem.shape[1]
    n = batch * seq
    depth = w_in.shape[0]
    assert depth == 1, "the combine kernel applies the final norm, so exactly one layer is supported"
    kw = GLA_HEADS * GLA_DK
    vw = GLA_HEADS * GLA_DV
    fw = FOX_HEADS * FOX_DH
    mw = MEM_HEADS * MEM_DH
    o_q, o_k, o_v, o_g = 0, kw, 2 * kw, 2 * kw + vw
    o_a = o_g + vw
    o_fq = o_a + GLA_LOWRANK
    o_fk, o_fv = o_fq + fw, o_fq + 2 * fw
    o_ff = o_fq + 3 * fw
    o_mq = o_ff + FOX_HEADS
    f_lane = GLA_LOWRANK

    xf = x.reshape(n, d)
    for l in range(depth):
        fox_q_scale = FOX_DH ** -0.5 * LOG2E
        col_scale = jnp.ones((w_in.shape[2],), F32).at[o_fq:o_fk].set(fox_q_scale)
        wt = (jnp.transpose(w_in[l]) * col_scale[:, None]).astype(BF16)
        wb_t = jnp.zeros((LANES, d), BF16).at[:GLA_LOWRANK].set(wt[o_a:o_fq])
        wb_t = wb_t.at[f_lane:f_lane + FOX_HEADS].set(wt[o_ff:o_mq])
        in_tn = o_ff - o_fq
        assert (o_a - o_q) % in_tn == 0 and PROJ_FOX_Q == o_a - o_q
        row_offs = tuple(range(o_q, o_a, in_tn)) + (o_fq,)
        proj, projm, small = _in_proj(xf, g_attn_norm[l].reshape(1, d), wt, wt[o_mq:o_mq + mw], wb_t,
                                      row_offs, in_tn)

        wa2p = jnp.zeros((LANES, kw), F32).at[:GLA_LOWRANK].set(w_gla_a2[l]).astype(BF16)
        bfv = jnp.zeros((1, LANES), F32).at[0, f_lane:f_lane + FOX_HEADS].set(b_fox_f[l])
        gla, f_cum = _gla(proj, small, wa2p, b_gla_a[l].reshape(1, kw), bfv,
                          g_gla_out[l].reshape(1, vw), batch, seq)

        f_t = f_cum.reshape(batch, seq, LANES)[:, :, f_lane:f_lane + FOX_HEADS].transpose(0, 2, 1)
        f_t = jnp.concatenate([f_t, jnp.zeros_like(f_t)], axis=1)
        fox = _fox(proj, f_t, g_fox_out[l].reshape(1, fw), batch, seq)

        kv = _mem_kv(mem.reshape(batch * n_mem, d), g_mem_norm[l].reshape(1, d), w_mem_kv[l].astype(BF16))
        memo = _mem_attn(projm, kv, g_mem_out[l].reshape(1, mw), batch, seq, n_mem)

        wr = jnp.zeros((d, LANES), F32).at[:, :N_EXPERTS].set(w_router[l])
        wr_hi = wr.astype(BF16)
        wr = jnp.concatenate([wr_hi, (wr - wr_hi.astype(F32)).astype(BF16)], axis=1)
        br =jnp.zeros((1, LANES), F32).at[0, :N_EXPERTS].set(b_router[l])
        x2, xn, logits = _out_proj(xf, gla, fox, memo, w_out[l].astype(BF16),
                                   g_ffn_norm[l].reshape(1, d), wr, br)

        idx, gates, rank, cnt = _route(logits)
        pos, blk_start, blk_count = _routing_tables(idx, rank, cnt)
        n_blocks_max = -(-(n * TOP_K + N_EXPERTS * (MOE_TM - 1)) // MOE_TM)
        blk_end = blk_start + blk_count
        n_used = blk_end[-1:]
        x_rows = _dispatch(xn, pos, blk_end - 1, (blk_count > 0).astype(jnp.int32), n_used,
                           n_blocks_max, d // 2 // LANES)
        d_ff = w_moe_up.shape[3] // 2
        act = _moe_up(x_rows, w_moe_up[l], b_moe_up[l].reshape(N_EXPERTS, 1, 2 * d_ff),
                      blk_start, blk_count, n_used, n_blocks_max)
        y_rows = _moe_down(act, w_moe_down[l], b_moe_down[l].reshape(N_EXPERTS, 1, d),
                           blk_start, blk_count, n_used, n_blocks_max)
        xf = _combine(y_rows, pos, gates, x2, g_final.reshape(1, d))
    return xf.reshape(batch, seq, d)
```

```python
import functools

import jax
import jax.numpy as jnp
from jax import lax
from jax.experimental import pallas as pl
from jax.experimental.pallas import tpu as pltpu

EPS = 1e-5
CHUNK = 64
GLA_HEADS = 4
GLA_DK = 128
GLA_DV = 256
GLA_LOWRANK = 16
GLA_TAU = 16.0
FOX_HEADS = 4
FOX_DH = 128
MEM_HEADS = 4
MEM_DH = 128
N_EXPERTS = 32
TOP_K = 4
SWIGLU_LIMIT = 7.0
SWIGLU_ALPHA = 1.702
LANES = 128
BF16_SUBLANES = 16
MXU_COLS = 256
MOE_TM = 256
MOE_NBUF = 2
MOE_UNIT = 2
VMEM_LIMIT = 56 * 1024 * 1024

GLA_KW = GLA_HEADS * GLA_DK
GLA_VW = GLA_HEADS * GLA_DV
FOX_W = FOX_HEADS * FOX_DH
MEM_W = MEM_HEADS * MEM_DH
PROJ_GLA_Q = 0
PROJ_GLA_K = PROJ_GLA_Q + GLA_KW
PROJ_GLA_V = PROJ_GLA_K + GLA_KW
PROJ_GLA_GATE = PROJ_GLA_V + GLA_VW
PROJ_FOX_Q = PROJ_GLA_GATE + GLA_VW
PROJ_FOX_K = PROJ_FOX_Q + FOX_W
PROJ_FOX_V = PROJ_FOX_K + FOX_W

F32 = jnp.float32
BF16 = jnp.bfloat16


def _cparams(sem, vmem=VMEM_LIMIT):
    return pltpu.CompilerParams(dimension_semantics=sem, vmem_limit_bytes=vmem)


def _log_sigmoid(x):
    return jnp.minimum(x, 0.0) - jnp.log1p(jnp.exp(-jnp.abs(x)))


def _rms(x, g):
    return x * lax.rsqrt(jnp.mean(x * x, axis=-1, keepdims=True) + EPS) * g


def _dot(a, b, **kw):
    return jnp.dot(a, b, preferred_element_type=F32, **kw)


def _dot_nt(a, b):
    return lax.dot_general(a, b, (((1,), (1,)), ((), ())), preferred_element_type=F32)


def _dot_tn(a, b):
    return lax.dot_general(a, b, (((0,), (0,)), ((), ())), preferred_element_type=F32)


def _store_rowmajor(ref, x, base=0):
    rows, w = x.shape
    pitch = w // LANES
    for c in range(pitch):
        ref[pl.ds(base + c, rows, stride=pitch), :] = x[:, c * LANES:(c + 1) * LANES]


def _pack_bf16_pairs(x):
    half = x.shape[1] // 2
    bits = lambda v: lax.bitcast_convert_type(v.astype(BF16).astype(F32), jnp.uint32)
    return bits(x[:, half:]) | (bits(x[:, :half]) >> 16)


def _unpack_lo(w):
    return lax.bitcast_convert_type(w << 16, F32)


def _unpack_hi(w):
    return lax.bitcast_convert_type(w & jnp.uint32(0xFFFF0000), F32)


def _in_proj_kernel(offs_ref, x_ref, g_ref, wt_ref, wq_ref, wb_ref, proj_ref, projm_ref, small_ref, h_scr):
    del offs_ref
    @pl.when(pl.program_id(1) == 0)
    def _():
        hb = _rms(x_ref[...], g_ref[...]).astype(BF16)
        h_scr[...] = hb
        small_ref[...] = _dot_nt(hb, wb_ref[...])
        projm_ref[...] = _dot_nt(hb, wq_ref[...]).astype(projm_ref.dtype)

    proj_ref[...] = _dot_nt(h_scr[...], wt_ref[...]).astype(proj_ref.dtype)


def _in_proj(x2d, g, wt, wq_t, wb_t, row_offs, tn, tm=1024):
    n, d = x2d.shape
    nj = len(row_offs)
    mw = wq_t.shape[0]
    offs = jnp.asarray(row_offs, jnp.int32)
    grid_spec = pltpu.PrefetchScalarGridSpec(
        num_scalar_prefetch=1,
        grid=(n // tm, nj),
        in_specs=[
            pl.BlockSpec((tm, d), lambda i, j, offs: (i, 0)),
            pl.BlockSpec((1, d), lambda i, j, offs: (0, 0)),
            pl.BlockSpec((pl.Element(tn), pl.Element(d)),
                         lambda i, j, offs: (pl.multiple_of(offs[j], BF16_SUBLANES), 0)),
            pl.BlockSpec((mw, d), lambda i, j, offs: (0, 0)),
            pl.BlockSpec((LANES, d), lambda i, j, offs: (0, 0)),
        ],
        out_specs=[
            pl.BlockSpec((tm, tn), lambda i, j, offs: (i, j)),
            pl.BlockSpec((tm, mw), lambda i, j, offs: (i, 0)),
            pl.BlockSpec((tm, LANES), lambda i, j, offs: (i, 0)),
        ],
        scratch_shapes=[pltpu.VMEM((tm, d), BF16)],
    )
    return pl.pallas_call(
        _in_proj_kernel,
        grid_spec=grid_spec,
        out_shape=[
            jax.ShapeDtypeStruct((n, nj * tn), BF16),
            jax.ShapeDtypeStruct((n, mw), BF16),
            jax.ShapeDtypeStruct((n, LANES), F32),
        ],
        compiler_params=_cparams(("parallel", "arbitrary")),
        name="in_proj",
    )(offs, x2d, g, wt, wq_t, wb_t)


def _gla_kernel(q_ref, k_ref, v_ref, gate_ref, small_ref, wa2_ref, ba_ref, bf_ref, gout_ref,
                o_ref, f_ref, state_scr, fcar_scr, la_scr, lf_scr, *, n_chunks):
    @pl.when(pl.program_id(1) == 0)
    def _():
        state_scr[...] = jnp.zeros_like(state_scr)
        fcar_scr[...] = jnp.zeros_like(fcar_scr)

    small = small_ref[...]
    la_scr[...] = _log_sigmoid(_dot(small.astype(BF16), wa2_ref[...]) + ba_ref[...]) * (1.0 / GLA_TAU)
    lf_scr[...] = _log_sigmoid(small + bf_ref[...])
    row = lax.broadcasted_iota(jnp.int32, (CHUNK, CHUNK), 0)
    col = lax.broadcasted_iota(jnp.int32, (CHUNK, CHUNK), 1)
    tri = (col <= row).astype(F32)
    scale = GLA_DK ** -0.5

    def chunk_body(c, carry):
        r = pl.ds(pl.multiple_of(c * CHUNK, CHUNK), CHUNK)
        b = _dot(tri, la_scr[r, :], precision=lax.Precision.HIGHEST)
        b_end = b[CHUNK - 1:CHUNK, :]
        k_dec = k_ref[r, :].astype(F32) * jnp.exp(b_end - b)
        decay = jnp.exp(b_end)
        f_cum = _dot(tri, lf_scr[r, :], precision=lax.Precision.HIGHEST) + fcar_scr[...]
        f_ref[r, :] = f_cum
        fcar_scr[...] = f_cum[CHUNK - 1:CHUNK, :]
        heads = range(GLA_HEADS)
        ks = [slice(h * GLA_DK, (h + 1) * GLA_DK) for h in heads]
        vs = [slice(h * GLA_DV, (h + 1) * GLA_DV) for h in heads]
        old = [state_scr[h] for h in heads]
        new = [old[h] * decay[:, ks[h]] + _dot_tn(v_ref[r, vs[h]], k_dec[:, ks[h]].astype(BF16))
               for h in heads]
        for h in heads:
            state_scr[h] = new[h]
        for h in heads:
            o = _dot_nt(q_ref[r, ks[h]], new[h].astype(BF16)) * scale
            gt = gate_ref[r, vs[h]].astype(F32)
            o_ref[r, vs[h]] = (_rms(o, gout_ref[:, vs[h]]) * (gt * jax.nn.sigmoid(gt))).astype(o_ref.dtype)
        return carry

    lax.fori_loop(0, n_chunks, chunk_body, 0, unroll=True)


def _gla(proj, small, wa2p, ba, bfv, gout, batch, seq, ts=512):
    n = proj.shape[0]
    nsb = seq // ts
    kw, vw = GLA_KW, GLA_VW
    row = lambda b, s: b * nsb + s
    return pl.pallas_call(
        functools.partial(_gla_kernel, n_chunks=ts // CHUNK),
        grid=(batch, nsb),
        in_specs=[
            pl.BlockSpec((ts, kw), lambda b, s: (row(b, s), PROJ_GLA_Q // kw)),
            pl.BlockSpec((ts, kw), lambda b, s: (row(b, s), PROJ_GLA_K // kw)),
            pl.BlockSpec((ts, vw), lambda b, s: (row(b, s), PROJ_GLA_V // vw)),
            pl.BlockSpec((ts, vw), lambda b, s: (row(b, s), PROJ_GLA_GATE // vw)),
            pl.BlockSpec((ts, LANES), lambda b, s: (row(b, s), 0)),
            pl.BlockSpec((LANES, kw), lambda b, s: (0, 0)),
            pl.BlockSpec((1, kw), lambda b, s: (0, 0)),
            pl.BlockSpec((1, LANES), lambda b, s: (0, 0)),
            pl.BlockSpec((1, vw), lambda b, s: (0, 0)),
        ],
        out_specs=[
            pl.BlockSpec((ts, vw), lambda b, s: (row(b, s), 0)),
            pl.BlockSpec((ts, LANES), lambda b, s: (row(b, s), 0)),
        ],
        out_shape=[
            jax.ShapeDtypeStruct((n, vw), BF16),
            jax.ShapeDtypeStruct((n, LANES), F32),
        ],
        scratch_shapes=[
            pltpu.VMEM((GLA_HEADS, GLA_DV, GLA_DK), F32),
            pltpu.VMEM((1, LANES), F32),
            pltpu.VMEM((ts, kw), F32),
            pltpu.VMEM((ts, LANES), F32),
        ],
        compiler_params=_cparams(("parallel", "arbitrary")),
        name="gla",
    )(proj, proj, proj, proj, small, wa2p, ba, bfv, gout)


LOG2E = 1.4426950408889634
def _fox_kernel(q_ref, k_ref, v_ref, fk_ref, g_ref, o_ref, m_scr, l_scr, acc_scr, s_even, s_odd,
                *, tq, tk):
    i = pl.program_id(1)
    t = pl.program_id(2)

    @pl.when(t == 0)
    def _():
        m_scr[...] = jnp.full_like(m_scr, -jnp.inf)
        l_scr[...] = jnp.zeros_like(l_scr)
        acc_scr[...] = jnp.zeros_like(acc_scr)

    heads = range(FOX_HEADS)
    hs = [slice(h * FOX_DH, (h + 1) * FOX_DH) for h in heads]

    def score(s_out):
        for h in heads:
            s_out[h] = _dot_nt(q_ref[:, hs[h]], k_ref[:, hs[h]])

    def update(s_in, on_diagonal):
        m_prev = [m_scr[h] for h in heads]
        l_prev = [l_scr[h] for h in heads]
        acc_prev = [acc_scr[h] for h in heads]
        s = [s_in[h] - fk_ref[h:h + 1, :] * LOG2E for h in heads]
        if on_diagonal:
            row = lax.broadcasted_iota(jnp.int32, (tq, tk), 0)
            col = lax.broadcasted_iota(jnp.int32, (tq, tk), 1)
            s = [jnp.where(col <= row, sh, -jnp.inf) for sh in s]
        lane_chunks = [slice(c * LANES, (c + 1) * LANES) for c in range(tk // LANES)]
        m_new, l_new, acc_new = [], [], []
        for h in heads:
            chunks = [s[h][:, lc] for lc in lane_chunks]
            cmax = functools.reduce(jnp.maximum, chunks)
            m_h = jnp.maximum(m_prev[h], jnp.max(cmax, axis=-1, keepdims=True))
            alpha = jnp.exp2(m_prev[h] - m_h)
            p_chunks = [jnp.exp2(ch - m_h) for ch in chunks]
            row_sum = jnp.sum(functools.reduce(jnp.add, p_chunks), axis=-1, keepdims=True)
            p = jnp.concatenate(p_chunks, axis=1).astype(BF16)
            m_new.append(m_h)
            l_new.append(alpha * l_prev[h] + row_sum)
            acc_new.append(alpha * acc_prev[h] + _dot(p, v_ref[:, hs[h]]))
        return m_new, l_new, acc_new

    def step(s_write, s_read):
        @pl.when(t == 0)
        def _():
            score(s_write)

        @pl.when(jnp.logical_and(t >= 1, t <= i))
        def _():
            m_new, l_new, acc_new = update(s_read, False)
            score(s_write)
            for h in heads:
                m_scr[h] = m_new[h]
                l_scr[h] = l_new[h]
                acc_scr[h] = acc_new[h]

        @pl.when(t == i + 1)
        def _():
            _, l_new, acc_new = update(s_read, True)
            for h in heads:
                o_ref[:, hs[h]] = _rms(acc_new[h] / l_new[h], g_ref[:, hs[h]]).astype(o_ref.dtype)

    @pl.when(t % 2 == 0)
    def _():
        step(s_even, s_odd)

    @pl.when(t % 2 == 1)
    def _():
        step(s_odd, s_even)


def _fox(proj, f_t, g_fox, batch, seq, tq=512, tk=512):
    assert tq == tk, "the diagonal-block mask assumes square blocks"
    n = proj.shape[0]
    w = FOX_HEADS * FOX_DH
    nq, nk = seq // tq, seq // tk
    qcol, kcol, vcol = PROJ_FOX_Q // w, PROJ_FOX_K // w, PROJ_FOX_V // w
    return pl.pallas_call(
        functools.partial(_fox_kernel, tq=tq, tk=tk),
        grid=(batch, nq, nk + 1),
        in_specs=[
            pl.BlockSpec((tq, w), lambda b, i, t: (b * nq + i, qcol)),
            pl.BlockSpec((tk, w), lambda b, i, t: (b * nk + jnp.minimum(t, i), kcol)),
            pl.BlockSpec((tk, w), lambda b, i, t: (b * nk + jnp.clip(t - 1, 0, i), vcol)),
            pl.BlockSpec((None, 8, tk), lambda b, i, t: (b, 0, jnp.clip(t - 1, 0, i))),
            pl.BlockSpec((1, w), lambda b, i, t: (0, 0)),
        ],
        out_specs=pl.BlockSpec((tq, w), lambda b, i, t: (b * nq + i, 0)),
        out_shape=jax.ShapeDtypeStruct((n, w), BF16),
        scratch_shapes=[
            pltpu.VMEM((FOX_HEADS, tq, LANES), F32),
            pltpu.VMEM((FOX_HEADS, tq, LANES), F32),
            pltpu.VMEM((FOX_HEADS, tq, FOX_DH), F32),
            pltpu.VMEM((FOX_HEADS, tq, tk), F32),
            pltpu.VMEM((FOX_HEADS, tq, tk), F32),
        ],
        compiler_params=_cparams(("parallel", "parallel", "arbitrary")),
        name="fox",
    )(proj, proj, proj, f_t, g_fox)


def _mem_kv_kernel(m_ref, g_ref, w_ref, kv_ref):
    kv_ref[...] = _dot(_rms(m_ref[...], g_ref[...]).astype(BF16), w_ref[...]).astype(kv_ref.dtype)


def _mem_kv(mem2d, g, w, tm=256):
    n, d = mem2d.shape
    nw = w.shape[1]
    return pl.pallas_call(
        _mem_kv_kernel,
        grid=(n // tm,),
        in_specs=[
            pl.BlockSpec((tm, d), lambda i: (i, 0)),
            pl.BlockSpec((1, d), lambda i: (0, 0)),
            pl.BlockSpec((d, nw), lambda i: (0, 0)),
        ],
        out_specs=pl.BlockSpec((tm, nw), lambda i: (i, 0)),
        out_shape=jax.ShapeDtypeStruct((n, nw), BF16),
        compiler_params=_cparams(("parallel",)),
        name="mem_kv",
    )(mem2d, g, w)


def _mem_attn_kernel(q_ref, k_ref, v_ref, g_ref, o_ref):
    scale = MEM_DH ** -0.5
    for h in range(MEM_HEADS):
        hs = slice(h * MEM_DH, (h + 1) * MEM_DH)
        s = _dot_nt(q_ref[:, hs], k_ref[:, hs]) * scale
        p = jnp.exp(s - jnp.max(s, axis=-1, keepdims=True))
        l = jnp.sum(p, axis=-1, keepdims=True)
        o = _dot((p / l).astype(BF16), v_ref[:, hs])
        o_ref[:, hs] = _rms(o, g_ref[:, hs]).astype(o_ref.dtype)


def _mem_attn(proj, kv, g_mem_out, batch, seq, n_mem, tq=1024):
    n = proj.shape[0]
    w = MEM_HEADS * MEM_DH
    nq = seq // tq
    qcol = 0
    return pl.pallas_call(
        _mem_attn_kernel,
        grid=(n // tq,),
        in_specs=[
            pl.BlockSpec((tq, w), lambda i: (i, qcol)),
            pl.BlockSpec((n_mem, w), lambda i: (i // nq, 0)),
            pl.BlockSpec((n_mem, w), lambda i: (i // nq, 1)),
            pl.BlockSpec((1, w), lambda i: (0, 0)),
        ],
        out_specs=pl.BlockSpec((tq, w), lambda i: (i, 0)),
        out_shape=jax.ShapeDtypeStruct((n, w), BF16),
        compiler_params=_cparams(("parallel",)),
        name="mem_attn",
    )(proj, kv, kv, g_mem_out)


def _out_proj_kernel(x_ref, gla_ref, fox_ref, mem_ref, w1_ref, w2_ref, w3_ref, g_ref, wr_ref, br_ref,
                     x2_ref, xn_ref, logit_ref):
    x2 = (x_ref[...] + _dot(gla_ref[...], w1_ref[...]) + _dot(fox_ref[...], w2_ref[...])
          + _dot(mem_ref[...], w3_ref[...]))
    x2_ref[...] = x2
    xn = _rms(x2, g_ref[...])
    _store_rowmajor(xn_ref, _pack_bf16_pairs(xn))
    xh = xn.astype(BF16)
    xl = (xn - xh.astype(F32)).astype(BF16)
    hi = _dot(xh, wr_ref[...])
    logit_ref[...] = hi[:, :LANES] + hi[:, LANES:] + _dot(xl, wr_ref[:, :LANES]) + br_ref[...]


def _out_proj(x2d, gla, fox, memo, w_out, g_ffn, wr, br, tm=512):
    n, d = x2d.shape
    w1, w2 = gla.shape[1], fox.shape[1]
    const = lambda i: (0, 0)
    return pl.pallas_call(
        _out_proj_kernel,
        grid=(n // tm,),
        in_specs=[
            pl.BlockSpec((tm, d), lambda i: (i, 0)),
            pl.BlockSpec((tm, w1), lambda i: (i, 0)),
            pl.BlockSpec((tm, w2), lambda i: (i, 0)),
            pl.BlockSpec((tm, w2), lambda i: (i, 0)),
            pl.BlockSpec((w1, d), lambda i: (0, 0)),
            pl.BlockSpec((w2, d), lambda i: (w1 // w2, 0)),
            pl.BlockSpec((w2, d), lambda i: (w1 // w2 + 1, 0)),
            pl.BlockSpec((1, d), const),
            pl.BlockSpec((d, 2 * LANES), const),
            pl.BlockSpec((1, LANES), const),
        ],
        out_specs=[
            pl.BlockSpec((tm, d), lambda i: (i, 0)),
            pl.BlockSpec((tm * (d // 2 // LANES), LANES), lambda i: (i, 0)),
            pl.BlockSpec((tm, LANES), lambda i: (i, 0)),
        ],
        out_shape=[
            jax.ShapeDtypeStruct((n, d), F32),
            jax.ShapeDtypeStruct((n * (d // 2 // LANES), LANES), jnp.uint32),
            jax.ShapeDtypeStruct((n, LANES), F32),
        ],
        compiler_params=_cparams(("parallel",)),
        name="out_proj",
    )(x2d, gla, fox, memo, w_out, w_out, w_out, g_ffn, wr, br)


def _route_kernel(l_ref, idx_ref, gate_ref, rank_ref, cnt_ref, carry_scr, *, tb):
    @pl.when(pl.program_id(0) == 0)
    def _():
        carry_scr[...] = jnp.zeros_like(carry_scr)

    lane = lax.broadcasted_iota(jnp.int32, (tb, LANES), 1)
    logit = jnp.where(lane < N_EXPERTS, l_ref[...], -jnp.inf)
    vals, hots = [], []
    idx_out = jnp.zeros((tb, LANES), jnp.int32)
    for k in range(TOP_K):
        m = jnp.max(logit, axis=-1, keepdims=True)
        ik = jnp.min(jnp.where(logit == m, lane, LANES), axis=-1, keepdims=True)
        hot = lane == ik
        logit = jnp.where(hot, -jnp.inf, logit)
        vals.append(m)
        hots.append(hot)
        idx_out = jnp.where(lane == k, ik, idx_out)
    idx_ref[...] = idx_out

    e = [jnp.exp(v - vals[0]) for v in vals]
    den = e[0] + e[1] + e[2] + e[3]
    gate_out = jnp.zeros((tb, LANES), F32)
    for k in range(TOP_K):
        gate_out = jnp.where(lane == k, e[k] / den, gate_out)
    gate_ref[...] = gate_out

    member = jnp.zeros((tb, LANES), F32)
    for hot in hots:
        member = member + hot.astype(F32)
    row = lax.broadcasted_iota(jnp.int32, (tb, tb), 0)
    col = lax.broadcasted_iota(jnp.int32, (tb, tb), 1)
    before = (col < row).astype(BF16)
    rank = _dot(before, member.astype(BF16)) + carry_scr[...]
    rank_out = jnp.zeros((tb, LANES), F32)
    for k in range(TOP_K):
        rk = jnp.sum(jnp.where(hots[k], rank, 0.0), axis=-1, keepdims=True)
        rank_out = jnp.where(lane == k, rk, rank_out)
    rank_ref[...] = rank_out.astype(jnp.int32)
    total = carry_scr[...] + jnp.sum(member, axis=0, keepdims=True)
    carry_scr[...] = total
    cnt_ref[...] = total.astype(jnp.int32)


def _route(logits, tb=512):
    n = logits.shape[0]
    blk = pl.BlockSpec((tb, LANES), lambda i: (i, 0))
    return pl.pallas_call(
        functools.partial(_route_kernel, tb=tb),
        grid=(n // tb,),
        in_specs=[blk],
        out_specs=[blk, blk, blk, pl.BlockSpec((1, LANES), lambda i: (0, 0))],
        out_shape=[
            jax.ShapeDtypeStruct((n, LANES), jnp.int32),
            jax.ShapeDtypeStruct((n, LANES), F32),
            jax.ShapeDtypeStruct((n, LANES), jnp.int32),
            jax.ShapeDtypeStruct((1, LANES), jnp.int32),
        ],
        scratch_shapes=[pltpu.VMEM((1, LANES), F32)],
        compiler_params=_cparams(("arbitrary",)),
        name="route",
    )(logits)


def _dispatch_kernel(zblk_ref, zok_ref, used_ref, pos_ref, xn_ref, xr_hbm, zero_scr, sem,
                     *, tb, pitch, n_blocks_max):
    blk_rows = MOE_TM * pitch

    def zero_copy(b):
        dst = xr_hbm.at[pl.ds(pl.multiple_of(b * blk_rows, blk_rows), blk_rows), :]
        return pltpu.make_async_copy(zero_scr, dst, sem.at[1])

    @pl.when(pl.program_id(0) == 0)
    def _():
        zero_scr[...] = jnp.zeros_like(zero_scr)

        def per_expert(action):
            def body(e, carry):
                @pl.when(zok_ref[e] == 1)
                def _():
                    action(zero_copy(zblk_ref[e]))
                return carry
            lax.fori_loop(0, N_EXPERTS, body, 0)

        def per_tail(action):
            def body(b, carry):
                action(zero_copy(b))
                return carry
            lax.fori_loop(used_ref[0], n_blocks_max, body, 0)

        per_expert(lambda c: c.start())
        per_tail(lambda c: c.start())
        per_expert(lambda c: c.wait())
        per_tail(lambda c: c.wait())

    def issue(t, carry):
        src = xn_ref.at[pl.ds(pl.multiple_of(t * pitch, pitch), pitch), :]
        for k in range(TOP_K):
            p = pos_ref[0, t * TOP_K + k]
            dst = xr_hbm.at[pl.ds(pl.multiple_of(p * pitch, pitch), pitch), :]
            pltpu.make_async_copy(src, dst, sem.at[0]).start(priority=k % 2)
        return carry

    lax.fori_loop(0, tb, issue, 0, unroll=4)
    for k in range(TOP_K):
        pltpu.make_async_copy(xn_ref, xr_hbm.at[pl.ds(0, tb * pitch), :], sem.at[0]).wait()


def _dispatch(xn_rm, pos, zblk, zok, n_used, n_blocks_max, pitch, tb=512):
    n = xn_rm.shape[0] // pitch
    nb = n // tb
    grid_spec = pltpu.PrefetchScalarGridSpec(
        num_scalar_prefetch=3,
        grid=(nb,),
        in_specs=[
            pl.BlockSpec((None, 1, TOP_K * tb), lambda i, *_: (i, 0, 0), memory_space=pltpu.SMEM),
            pl.BlockSpec((tb * pitch, LANES), lambda i, *_: (i, 0)),
        ],
        out_specs=pl.BlockSpec(memory_space=pl.ANY),
        scratch_shapes=[pltpu.VMEM((MOE_TM * pitch, LANES), xn_rm.dtype), pltpu.SemaphoreType.DMA((2,))],
    )
    return pl.pallas_call(
        functools.partial(_dispatch_kernel, tb=tb, pitch=pitch, n_blocks_max=n_blocks_max),
        grid_spec=grid_spec,
        out_shape=jax.ShapeDtypeStruct((n_blocks_max * MOE_TM * pitch, LANES), xn_rm.dtype),
        compiler_params=_cparams(("arbitrary",)),
        name="dispatch",
    )(zblk, zok, n_used, pos.reshape(nb, 1, TOP_K * tb), xn_rm)


def _stream_row_blocks(cnt, in_copy, out_copy, compute):
    for ahead in range(MOE_NBUF - 1):
        @pl.when(cnt > ahead)
        def _():
            in_copy(ahead, ahead).start()

    def body(b, carry):
        slot = b % MOE_NBUF

        @pl.when(b + MOE_NBUF - 1 < cnt)
        def _():
            in_copy(b + MOE_NBUF - 1, (b + MOE_NBUF - 1) % MOE_NBUF).start()

        in_copy(b, slot).wait()

        @pl.when(b >= MOE_NBUF)
        def _():
            out_copy(b - MOE_NBUF, slot).wait()

        compute(slot)
        out_copy(b, slot).start(priority=1)
        return carry

    lax.fori_loop(0, cnt, body, 0)

    for back in range(MOE_NBUF, 0, -1):
        @pl.when(cnt >= back)
        def _():
            out_copy(cnt - back, (cnt - back) % MOE_NBUF).wait()


def _stream_expert_rows(first, cnt, in_copy, out_copy, compute):
    n_units = cnt // MOE_UNIT
    _stream_row_blocks(n_units, in_copy(first, MOE_UNIT), out_copy(first, MOE_UNIT), compute(MOE_UNIT))
    done = n_units * MOE_UNIT
    _stream_row_blocks(cnt - done, in_copy(first + done, 1), out_copy(first + done, 1), compute(1))


def _zero_fill_blocks(first, last, zero_src, dst_copy):
    def start(b, carry):
        dst_copy(b).start()
        return carry

    def wait(b, carry):
        dst_copy(b).wait()
        return carry

    zero_src[...] = jnp.zeros_like(zero_src)
    lax.fori_loop(first, last, start, 0)
    lax.fori_loop(first, last, wait, 0)


def _moe_up_kernel(bs_ref, bc_ref, used_ref, x_hbm, w_hbm, bg_ref, bu_ref, act_hbm,
                   w_scr, xbuf, lhs_scr, obuf, sem_w, sem_x, sem_o,
                   *, nt, tn, d_ff, pitch, n_blocks_max):
    g = pl.program_id(0)
    ng = pl.num_programs(0)
    e = g // nt
    col = pl.multiple_of((g % nt) * tn, tn)
    blk_rows = MOE_TM * pitch
    wslot = g % 2

    def w_copies(step, slot):
        ee = step // nt
        cc = pl.multiple_of((step % nt) * tn, tn)
        return (pltpu.make_async_copy(w_hbm.at[ee, :, pl.ds(cc, tn)], w_scr.at[slot, 0], sem_w.at[slot, 0]),
                pltpu.make_async_copy(w_hbm.at[ee, :, pl.ds(d_ff + cc, tn)], w_scr.at[slot, 1],
                                      sem_w.at[slot, 1]))

    @pl.when(g == 0)
    def _():
        for c in w_copies(0, 0):
            c.start(priority=1)

    @pl.when(g + 1 < ng)
    def _():
        for c in w_copies(g + 1, 1 - wslot):
            c.start(priority=1)

    for c in w_copies(g, wslot):
        c.wait()

    start = bs_ref[e]

    def x_copy(first, nblk):
        def make(u, slot):
            r0 = pl.multiple_of((first + u * nblk) * blk_rows, blk_rows)
            return pltpu.make_async_copy(x_hbm.at[pl.ds(r0, nblk * blk_rows), :],
                                         xbuf.at[slot, pl.ds(0, nblk * blk_rows), :], sem_x.at[slot])
        return make

    def o_copy(first, nblk):
        def make(u, slot):
            r0 = pl.multiple_of((first + u * nblk) * MOE_TM, MOE_TM)
            return pltpu.make_async_copy(obuf.at[slot, pl.ds(0, nblk * MOE_TM), :],
                                         act_hbm.at[pl.ds(r0, nblk * MOE_TM), pl.ds(col, tn)], sem_o.at[slot])
        return make

    def compute(nblk):
        rows = nblk * MOE_TM

        def run(slot):
            half = pitch * LANES
            for c in range(pitch):
                w = xbuf[slot, pl.ds(c, rows, stride=pitch), :]
                lhs_scr[0:rows, c * LANES:(c + 1) * LANES] = _unpack_lo(w)
                lhs_scr[0:rows, half + c * LANES:half + (c + 1) * LANES] = _unpack_hi(w)
            x = lhs_scr[0:rows, :]
            gate = jnp.minimum(_dot(x, w_scr[wslot, 0]) + bg_ref[...], SWIGLU_LIMIT)
            up = jnp.clip(_dot(x, w_scr[wslot, 1]) + bu_ref[...], -SWIGLU_LIMIT, SWIGLU_LIMIT)
            obuf[slot, 0:rows, :] = (gate * jax.nn.sigmoid(SWIGLU_ALPHA * gate) * (up + 1.0)).astype(obuf.dtype)
        return run

    _stream_expert_rows(start, bc_ref[e], x_copy, o_copy, compute)

    @pl.when(e == N_EXPERTS - 1)
    def _():
        zero_blk = obuf.at[0, pl.ds(0, MOE_TM), :]

        def tail_copy(b):
            r0 = pl.multiple_of(b * MOE_TM, MOE_TM)
            return pltpu.make_async_copy(zero_blk, act_hbm.at[pl.ds(r0, MOE_TM), pl.ds(col, tn)], sem_o.at[0])
        _zero_fill_blocks(used_ref[0], n_blocks_max, zero_blk, tail_copy)


def _moe_up(x_rows_rm, w_up, b_up, blk_start, blk_count, n_used, n_blocks_max, tn=1024):
    n_exp, d, two_ff = w_up.shape
    d_ff = two_ff // 2
    pitch = d // 2 // LANES
    nt = d_ff // tn
    up_off = d_ff // tn
    grid_spec = pltpu.PrefetchScalarGridSpec(
        num_scalar_prefetch=3,
        grid=(n_exp * nt,),
        in_specs=[
            pl.BlockSpec(memory_space=pl.ANY),
            pl.BlockSpec(memory_space=pl.ANY),
            pl.BlockSpec((None, 1, tn), lambda g, *_: (g // nt, 0, g % nt)),
            pl.BlockSpec((None, 1, tn), lambda g, *_: (g // nt, 0, g % nt + up_off)),
        ],
        out_specs=pl.BlockSpec(memory_space=pl.ANY),
        scratch_shapes=[
            pltpu.VMEM((2, 2, d, tn), F32),
            pltpu.VMEM((MOE_NBUF, MOE_UNIT * MOE_TM * pitch, LANES), jnp.uint32),
            pltpu.VMEM((MOE_UNIT * MOE_TM, d), F32),
            pltpu.VMEM((MOE_NBUF, MOE_UNIT * MOE_TM, tn), BF16),
            pltpu.SemaphoreType.DMA((2, 2)),
            pltpu.SemaphoreType.DMA((MOE_NBUF,)),
            pltpu.SemaphoreType.DMA((MOE_NBUF,)),
        ],
    )
    return pl.pallas_call(
        functools.partial(_moe_up_kernel, nt=nt, tn=tn, d_ff=d_ff, pitch=pitch, n_blocks_max=n_blocks_max),
        grid_spec=grid_spec,
        out_shape=jax.ShapeDtypeStruct((n_blocks_max * MOE_TM, d_ff), BF16),
        compiler_params=_cparams(("arbitrary",)),
        name="moe_up",
    )(blk_start, blk_count, n_used, x_rows_rm, w_up, b_up, b_up)


def _moe_down_kernel(bs_ref, bc_ref, used_ref, a_hbm, w_hbm, b_ref, y_hbm,
                     w_scr, abuf, obuf, sem_w, sem_a, sem_o, *, pitch, n_blocks_max):
    e = pl.program_id(0)
    blk_rows = MOE_TM * pitch
    wslot = e % 2

    def w_copy(ee, slot):
        return pltpu.make_async_copy(w_hbm.at[ee], w_scr.at[slot], sem_w.at[slot])

    @pl.when(e == 0)
    def _():
        w_copy(0, 0).start(priority=1)

    @pl.when(e + 1 < pl.num_programs(0))
    def _():
        w_copy(e + 1, 1 - wslot).start(priority=1)

    w_copy(e, wslot).wait()
    start = bs_ref[e]

    def a_copy(first, nblk):
        def make(u, slot):
            r0 = pl.multiple_of((first + u * nblk) * MOE_TM, MOE_TM)
            return pltpu.make_async_copy(a_hbm.at[pl.ds(r0, nblk * MOE_TM), :],
                                         abuf.at[slot, pl.ds(0, nblk * MOE_TM), :], sem_a.at[slot])
        return make

    def o_copy(first, nblk):
        def make(u, slot):
            r0 = pl.multiple_of((first + u * nblk) * blk_rows, blk_rows)
            return pltpu.make_async_copy(obuf.at[slot, pl.ds(0, nblk * blk_rows), :],
                                         y_hbm.at[pl.ds(r0, nblk * blk_rows), :], sem_o.at[slot])
        return make

    def compute(nblk):
        rows = nblk * MOE_TM

        def run(slot):
            y = _dot(abuf[slot, 0:rows, :].astype(F32), w_scr[wslot]) + b_ref[...]
            _store_rowmajor(obuf.at[slot], _pack_bf16_pairs(y))
        return run

    _stream_expert_rows(start, bc_ref[e], a_copy, o_copy, compute)

    @pl.when(e == N_EXPERTS - 1)
    def _():
        zero_blk = obuf.at[0, pl.ds(0, blk_rows), :]

        def tail_copy(b):
            r0 = pl.multiple_of(b * blk_rows, blk_rows)
            return pltpu.make_async_copy(zero_blk, y_hbm.at[pl.ds(r0, blk_rows), :], sem_o.at[0])
        _zero_fill_blocks(used_ref[0], n_blocks_max, zero_blk, tail_copy)


def _moe_down(act, w_down, b_down, blk_start, blk_count, n_used, n_blocks_max):
    n_rows, d_ff = act.shape
    n_exp, _, d = w_down.shape
    pitch = d // 2 // LANES
    grid_spec = pltpu.PrefetchScalarGridSpec(
        num_scalar_prefetch=3,
        grid=(n_exp,),
        in_specs=[
            pl.BlockSpec(memory_space=pl.ANY),
            pl.BlockSpec(memory_space=pl.ANY),
            pl.BlockSpec((None, 1, d), lambda e, *_: (e, 0, 0)),
        ],
        out_specs=pl.BlockSpec(memory_space=pl.ANY),
        scratch_shapes=[
            pltpu.VMEM((2, d_ff, d), F32),
            pltpu.VMEM((MOE_NBUF, MOE_UNIT * MOE_TM, d_ff), BF16),
            pltpu.VMEM((MOE_NBUF, MOE_UNIT * MOE_TM * pitch, LANES), jnp.uint32),
            pltpu.SemaphoreType.DMA((2,)),
            pltpu.SemaphoreType.DMA((MOE_NBUF,)),
            pltpu.SemaphoreType.DMA((MOE_NBUF,)),
        ],
    )
    return pl.pallas_call(
        functools.partial(_moe_down_kernel, pitch=pitch, n_blocks_max=n_blocks_max),
        grid_spec=grid_spec,
        out_shape=jax.ShapeDtypeStruct((n_rows * pitch, LANES), jnp.uint32),
        compiler_params=_cparams(("arbitrary",)),
        name="moe_down",
    )(blk_start, blk_count, n_used, act, w_down, b_down)


def _combine_kernel(pos0_ref, posn_ref, y_hbm, x2_ref, gate_ref, g_ref, o_ref, buf_even, buf_odd, sem,
                    *, tb, pitch):
    i = pl.program_id(0)
    last = pl.num_programs(0) - 1
    n_rows = TOP_K * tb

    def row_copy(p_ref, r, buf, sem_slot):
        src = y_hbm.at[pl.ds(pl.multiple_of(p_ref[0, r] * pitch, pitch), pitch), :]
        return pltpu.make_async_copy(src, buf.at[pl.ds(r * pitch, pitch), :], sem.at[sem_slot])

    def wait_block(buf, sem_slot):
        pltpu.make_async_copy(y_hbm.at[pl.ds(0, n_rows * pitch), :], buf, sem.at[sem_slot]).wait()

    @pl.when(i == 0)
    def _():
        def issue(j, carry):
            for half in range(2):
                row_copy(pos0_ref, 2 * j + half, buf_even, 0).start(priority=half)
            return carry
        lax.fori_loop(0, n_rows // 2, issue, 0, unroll=4)

    def step(buf_cur, sem_cur, buf_nxt, sem_nxt):
        wait_block(buf_cur, sem_cur)
        for r in range(n_rows):
            row_copy(posn_ref, r, buf_nxt, sem_nxt).start(priority=r % 2)
        gates = gate_ref[...]
        half = pitch * LANES
        ssq = jnp.zeros((tb, 1), F32)
        for c in range(pitch):
            lo = slice(c * LANES, (c + 1) * LANES)
            hi = slice(half + c * LANES, half + (c + 1) * LANES)
            z_lo = x2_ref[:, lo]
            z_hi = x2_ref[:, hi]
            for k in range(TOP_K):
                w = buf_cur[pl.ds(k * tb * pitch + c, tb, stride=pitch), :]
                z_lo = z_lo + gates[:, k:k + 1] * _unpack_lo(w)
                z_hi = z_hi + gates[:, k:k + 1] * _unpack_hi(w)
            o_ref[:, lo] = z_lo
            o_ref[:, hi] = z_hi
            ssq = ssq + (jnp.sum(z_lo * z_lo, axis=-1, keepdims=True)
                         + jnp.sum(z_hi * z_hi, axis=-1, keepdims=True))
        o_ref[...] = o_ref[...] * lax.rsqrt(ssq * (1.0 / (2 * half)) + EPS) * g_ref[...]

        @pl.when(i == last)
        def _():
            wait_block(buf_nxt, sem_nxt)

    @pl.when(i % 2 == 0)
    def _():
        step(buf_even, 0, buf_odd, 1)

    @pl.when(i % 2 == 1)
    def _():
        step(buf_odd, 1, buf_even, 0)


def _combine(y_rows_rm, pos, gates, x2, g_final, tb=512):
    n, d = x2.shape
    nb = n // tb
    pitch = d // 2 // LANES
    pos_blk = pos.reshape(nb, tb, TOP_K).transpose(0, 2, 1).reshape(nb, 1, TOP_K * tb)
    pos_spec = lambda imap: pl.BlockSpec((None, 1, TOP_K * tb), imap, memory_space=pltpu.SMEM)
    return pl.pallas_call(
        functools.partial(_combine_kernel, tb=tb, pitch=pitch),
        grid=(nb,),
        in_specs=[
            pos_spec(lambda i: (0, 0, 0)),
            pos_spec(lambda i: (jnp.minimum(i + 1, nb - 1), 0, 0)),
            pl.BlockSpec(memory_space=pl.ANY),
            pl.BlockSpec((tb, d), lambda i: (i, 0)),
            pl.BlockSpec((tb, LANES), lambda i: (i, 0)),
            pl.BlockSpec((1, d), lambda i: (0, 0)),
        ],
        out_specs=pl.BlockSpec((tb, d), lambda i: (i, 0)),
        out_shape=jax.ShapeDtypeStruct((n, d), F32),
        scratch_shapes=[pltpu.VMEM((TOP_K * tb * pitch, LANES), jnp.uint32),
                        pltpu.VMEM((TOP_K * tb * pitch, LANES), jnp.uint32),
                        pltpu.SemaphoreType.DMA((2,))],
        compiler_params=_cparams(("arbitrary",)),
        name="combine",
    )(pos_blk, pos_blk, y_rows_rm, x2, gates, g_final)


def _routing_tables(idx, rank, cnt):
    counts = cnt[0, :N_EXPERTS]
    blk_count = (counts + MOE_TM - 1) // MOE_TM
    blk_end = jnp.cumsum(blk_count)
    blk_start = blk_end - blk_count
    hot = idx[:, :TOP_K, None] == jnp.arange(N_EXPERTS, dtype=jnp.int32)
    pos = jnp.sum(jnp.where(hot, blk_start * MOE_TM, 0), axis=-1) + rank[:, :TOP_K]
    return pos.astype(jnp.int32), blk_start.astype(jnp.int32), blk_count.astype(jnp.int32)


def kernel(x, mem, g_attn_norm, g_mem_norm, w_in, w_gla_a2, b_gla_a, g_gla_out, b_fox_f, g_fox_out,
           w_mem_kv, g_mem_out, w_out, g_ffn_norm, w_router, b_router, w_moe_up, b_moe_up,
           w_moe_down, b_moe_down, g_final):
    batch, seq, d = x.shape
    n_mem = mem.shape[1]
    n = batch * seq
    depth = w_in.shape[0]
    assert depth == 1, "the combine kernel applies the final norm, so exactly one layer is supported"
    kw = GLA_HEADS * GLA_DK
    vw = GLA_HEADS * GLA_DV
    fw = FOX_HEADS * FOX_DH
    mw = MEM_HEADS * MEM_DH
    o_q, o_k, o_v, o_g = 0, kw, 2 * kw, 2 * kw + vw
    o_a = o_g + vw
    o_fq = o_a + GLA_LOWRANK
    o_fk, o_fv = o_fq + fw, o_fq + 2 * fw
    o_ff = o_fq + 3 * fw
    o_mq = o_ff + FOX_HEADS
    f_lane = GLA_LOWRANK

    xf = x.reshape(n, d)
    for l in range(depth):
        fox_q_scale = FOX_DH ** -0.5 * LOG2E
        col_scale = jnp.ones((w_in.shape[2],), F32).at[o_fq:o_fk].set(fox_q_scale)
        wt = (jnp.transpose(w_in[l]) * col_scale[:, None]).astype(BF16)
        wb_t = jnp.zeros((LANES, d), BF16).at[:GLA_LOWRANK].set(wt[o_a:o_fq])
        wb_t = wb_t.at[f_lane:f_lane + FOX_HEADS].set(wt[o_ff:o_mq])
        in_tn = o_ff - o_fq
        assert (o_a - o_q) % in_tn == 0 and PROJ_FOX_Q == o_a - o_q
        row_offs = tuple(range(o_q, o_a, in_tn)) + (o_fq,)
        proj, projm, small = _in_proj(xf, g_attn_norm[l].reshape(1, d), wt, wt[o_mq:o_mq + mw], wb_t,
                                      row_offs, in_tn)

        wa2p = jnp.zeros((LANES, kw), F32).at[:GLA_LOWRANK].set(w_gla_a2[l]).astype(BF16)
        bfv = jnp.zeros((1, LANES), F32).at[0, f_lane:f_lane + FOX_HEADS].set(b_fox_f[l])
        gla, f_cum = _gla(proj, small, wa2p, b_gla_a[l].reshape(1, kw), bfv,
                          g_gla_out[l].reshape(1, vw), batch, seq)

        f_t = f_cum.reshape(batch, seq, LANES)[:, :, f_lane:f_lane + FOX_HEADS].transpose(0, 2, 1)
        f_t = jnp.concatenate([f_t, jnp.zeros_like(f_t)], axis=1)
        fox = _fox(proj, f_t, g_fox_out[l].reshape(1, fw), batch, seq)

        kv = _mem_kv(mem.reshape(batch * n_mem, d), g_mem_norm[l].reshape(1, d), w_mem_kv[l].astype(BF16))
        memo = _mem_attn(projm, kv, g_mem_out[l].reshape(1, mw), batch, seq, n_mem)

        wr = jnp.zeros((d, LANES), F32).at[:, :N_EXPERTS].set(w_router[l])
        wr_hi = wr.astype(BF16)
        wr = jnp.concatenate([wr_hi, (wr - wr_hi.astype(F32)).astype(BF16)], axis=1)
        br =jnp.zeros((1, LANES), F32).at[0, :N_EXPERTS].set(b_router[l])
        x2, xn, logits = _out_proj(xf, gla, fox, memo, w_out[l].astype(BF16),
                                   g_ffn_norm[l].reshape(1, d), wr, br)

        idx, gates, rank, cnt = _route(logits)
        pos, blk_start, blk_count = _routing_tables(idx, rank, cnt)
        n_blocks_max = -(-(n * TOP_K + N_EXPERTS * (MOE_TM - 1)) // MOE_TM)
        blk_end = blk_start + blk_count
        n_used = blk_end[-1:]
        x_rows = _dispatch(xn, pos, blk_end - 1, (blk_count > 0).astype(jnp.int32), n_used,
                           n_blocks_max, d // 2 // LANES)
        d_ff = w_moe_up.shape[3] // 2
        act = _moe_up(x_rows, w_moe_up[l], b_moe_up[l].reshape(N_EXPERTS, 1, 2 * d_ff),
                      blk_start, blk_count, n_used, n_blocks_max)
        y_rows = _moe_down(act, w_moe_down[l], b_moe_down[l].reshape(N_EXPERTS, 1, d),
                           blk_start, blk_count, n_used, n_blocks_max)
        xf = _combine(y_rows, pos, gates, x2, g_final.reshape(1, d))
    return xf.reshape(batch, seq, d)
```

```python
import functools

import jax
import jax.numpy as jnp
from jax import lax
from jax.experimental import pallas as pl
from jax.experimental.pallas import tpu as pltpu

EPS = 1e-5
CHUNK = 64
GLA_HEADS = 4
GLA_DK = 128
GLA_DV = 256
GLA_LOWRANK = 16
GLA_TAU = 16.0
FOX_HEADS = 4
FOX_DH = 128
MEM_HEADS = 4
MEM_DH = 128
N_EXPERTS = 32
TOP_K = 4
SWIGLU_LIMIT = 7.0
SWIGLU_ALPHA = 1.702
LANES = 128
BF16_SUBLANES = 16
MXU_COLS = 256
MOE_TM = 256
MOE_NBUF = 2
MOE_UNIT = 2
VMEM_LIMIT = 56 * 1024 * 1024

GLA_KW = GLA_HEADS * GLA_DK
GLA_VW = GLA_HEADS * GLA_DV
FOX_W = FOX_HEADS * FOX_DH
MEM_W = MEM_HEADS * MEM_DH
PROJ_GLA_Q = 0
PROJ_GLA_K = PROJ_GLA_Q + GLA_KW
PROJ_GLA_V = PROJ_GLA_K + GLA_KW
PROJ_GLA_GATE = PROJ_GLA_V + GLA_VW
PROJ_FOX_Q = PROJ_GLA_GATE + GLA_VW
PROJ_FOX_K = PROJ_FOX_Q + FOX_W
PROJ_FOX_V = PROJ_FOX_K + FOX_W

F32 = jnp.float32
BF16 = jnp.bfloat16


def _cparams(sem, vmem=VMEM_LIMIT):
    return pltpu.CompilerParams(dimension_semantics=sem, vmem_limit_bytes=vmem)


def _log_sigmoid(x):
    return jnp.minimum(x, 0.0) - jnp.log1p(jnp.exp(-jnp.abs(x)))


def _rms(x, g):
    return x * lax.rsqrt(jnp.mean(x * x, axis=-1, keepdims=True) + EPS) * g


def _dot(a, b, **kw):
    return jnp.dot(a, b, preferred_element_type=F32, **kw)


def _dot_nt(a, b):
    return lax.dot_general(a, b, (((1,), (1,)), ((), ())), preferred_element_type=F32)


def _dot_tn(a, b):
    return lax.dot_general(a, b, (((0,), (0,)), ((), ())), preferred_element_type=F32)


def _store_rowmajor(ref, x, base=0):
    rows, w = x.shape
    pitch = w // LANES
    for c in range(pitch):
        ref[pl.ds(base + c, rows, stride=pitch), :] = x[:, c * LANES:(c + 1) * LANES]


def _pack_bf16_pairs(x):
    half = x.shape[1] // 2
    bits = lambda v: lax.bitcast_convert_type(v.astype(BF16).astype(F32), jnp.uint32)
    return bits(x[:, half:]) | (bits(x[:, :half]) >> 16)


def _unpack_lo(w):
    return lax.bitcast_convert_type(w << 16, F32)


def _unpack_hi(w):
    return lax.bitcast_convert_type(w & jnp.uint32(0xFFFF0000), F32)


def _in_proj_kernel(offs_ref, x_ref, g_ref, wt_ref, wq_ref, wb_ref, proj_ref, projm_ref, small_ref, h_scr):
    del offs_ref
    @pl.when(pl.program_id(1) == 0)
    def _():
        hb = _rms(x_ref[...], g_ref[...]).astype(BF16)
        h_scr[...] = hb
        small_ref[...] = _dot_nt(hb, wb_ref[...])
        projm_ref[...] = _dot_nt(hb, wq_ref[...]).astype(projm_ref.dtype)

    proj_ref[...] = _dot_nt(h_scr[...], wt_ref[...]).astype(proj_ref.dtype)


def _in_proj(x2d, g, wt, wq_t, wb_t, row_offs, tn, tm=1024):
    n, d = x2d.shape
    nj = len(row_offs)
    mw = wq_t.shape[0]
    offs = jnp.asarray(row_offs, jnp.int32)
    grid_spec = pltpu.PrefetchScalarGridSpec(
        num_scalar_prefetch=1,
        grid=(n // tm, nj),
        in_specs=[
            pl.BlockSpec((tm, d), lambda i, j, offs: (i, 0)),
            pl.BlockSpec((1, d), lambda i, j, offs: (0, 0)),
            pl.BlockSpec((pl.Element(tn), pl.Element(d)),
                         lambda i, j, offs: (pl.multiple_of(offs[j], BF16_SUBLANES), 0)),
            pl.BlockSpec((mw, d), lambda i, j, offs: (0, 0)),
            pl.BlockSpec((LANES, d), lambda i, j, offs: (0, 0)),
        ],
        out_specs=[
            pl.BlockSpec((tm, tn), lambda i, j, offs: (i, j)),
            pl.BlockSpec((tm, mw), lambda i, j, offs: (i, 0)),
            pl.BlockSpec((tm, LANES), lambda i, j, offs: (i, 0)),
        ],
        scratch_shapes=[pltpu.VMEM((tm, d), BF16)],
    )
    return pl.pallas_call(
        _in_proj_kernel,
        grid_spec=grid_spec,
        out_shape=[
            jax.ShapeDtypeStruct((n, nj * tn), BF16),
            jax.ShapeDtypeStruct((n, mw), BF16),
            jax.ShapeDtypeStruct((n, LANES), F32),
        ],
        compiler_params=_cparams(("parallel", "arbitrary")),
        name="in_proj",
    )(offs, x2d, g, wt, wq_t, wb_t)


def _gla_kernel(q_ref, k_ref, v_ref, gate_ref, small_ref, wa2_ref, ba_ref, bf_ref, gout_ref,
                o_ref, f_ref, state_scr, fcar_scr, la_scr, lf_scr, *, n_chunks):
    @pl.when(pl.program_id(1) == 0)
    def _():
        state_scr[...] = jnp.zeros_like(state_scr)
        fcar_scr[...] = jnp.zeros_like(fcar_scr)

    small = small_ref[...]
    la_scr[...] = _log_sigmoid(_dot(small.astype(BF16), wa2_ref[...]) + ba_ref[...]) * (1.0 / GLA_TAU)
    lf_scr[...] = _log_sigmoid(small + bf_ref[...])
    row = lax.broadcasted_iota(jnp.int32, (CHUNK, CHUNK), 0)
    col = lax.broadcasted_iota(jnp.int32, (CHUNK, CHUNK), 1)
    tri = (col <= row).astype(F32)
    scale = GLA_DK ** -0.5

    def chunk_body(c, carry):
        r = pl.ds(pl.multiple_of(c * CHUNK, CHUNK), CHUNK)
        b = _dot(tri, la_scr[r, :], precision=lax.Precision.HIGHEST)
        b_end = b[CHUNK - 1:CHUNK, :]
        k_dec = k_ref[r, :].astype(F32) * jnp.exp(b_end - b)
        decay = jnp.exp(b_end)
        f_cum = _dot(tri, lf_scr[r, :], precision=lax.Precision.HIGHEST) + fcar_scr[...]
        f_ref[r, :] = f_cum
        fcar_scr[...] = f_cum[CHUNK - 1:CHUNK, :]
        heads = range(GLA_HEADS)
        ks = [slice(h * GLA_DK, (h + 1) * GLA_DK) for h in heads]
        vs = [slice(h * GLA_DV, (h + 1) * GLA_DV) for h in heads]
        old = [state_scr[h] for h in heads]
        new = [old[h] * decay[:, ks[h]] + _dot_tn(v_ref[r, vs[h]], k_dec[:, ks[h]].astype(BF16))
               for h in heads]
        for h in heads:
            state_scr[h] = new[h]
        for h in heads:
            o = _dot_nt(q_ref[r, ks[h]], new[h].astype(BF16)) * scale
            gt = gate_ref[r, vs[h]].astype(F32)
            o_ref[r, vs[h]] = (_rms(o, gout_ref[:, vs[h]]) * (gt * jax.nn.sigmoid(gt))).astype(o_ref.dtype)
        return carry

    lax.fori_loop(0, n_chunks, chunk_body, 0, unroll=True)


def _gla(proj, small, wa2p, ba, bfv, gout, batch, seq, ts=512):
    n = proj.shape[0]
    nsb = seq // ts
    kw, vw = GLA_KW, GLA_VW
    row = lambda b, s: b * nsb + s
    return pl.pallas_call(
        functools.partial(_gla_kernel, n_chunks=ts // CHUNK),
        grid=(batch, nsb),
        in_specs=[
            pl.BlockSpec((ts, kw), lambda b, s: (row(b, s), PROJ_GLA_Q // kw)),
            pl.BlockSpec((ts, kw), lambda b, s: (row(b, s), PROJ_GLA_K // kw)),
            pl.BlockSpec((ts, vw), lambda b, s: (row(b, s), PROJ_GLA_V // vw)),
            pl.BlockSpec((ts, vw), lambda b, s: (row(b, s), PROJ_GLA_GATE // vw)),
            pl.BlockSpec((ts, LANES), lambda b, s: (row(b, s), 0)),
            pl.BlockSpec((LANES, kw), lambda b, s: (0, 0)),
            pl.BlockSpec((1, kw), lambda b, s: (0, 0)),
            pl.BlockSpec((1, LANES), lambda b, s: (0, 0)),
            pl.BlockSpec((1, vw), lambda b, s: (0, 0)),
        ],
        out_specs=[
            pl.BlockSpec((ts, vw), lambda b, s: (row(b, s), 0)),
            pl.BlockSpec((ts, LANES), lambda b, s: (row(b, s), 0)),
        ],
        out_shape=[
            jax.ShapeDtypeStruct((n, vw), BF16),
            jax.ShapeDtypeStruct((n, LANES), F32),
        ],
        scratch_shapes=[
            pltpu.VMEM((GLA_HEADS, GLA_DV, GLA_DK), F32),
            pltpu.VMEM((1, LANES), F32),
            pltpu.VMEM((ts, kw), F32),
            pltpu.VMEM((ts, LANES), F32),
        ],
        compiler_params=_cparams(("parallel", "arbitrary")),
        name="gla",
    )(proj, proj, proj, proj, small, wa2p, ba, bfv, gout)


LOG2E = 1.4426950408889634
def _fox_kernel(q_ref, k_ref, v_ref, fk_ref, g_ref, o_ref, m_scr, l_scr, acc_scr, s_even, s_odd,
                *, tq, tk):
    i = pl.program_id(1)
    t = pl.program_id(2)

    @pl.when(t == 0)
    def _():
        m_scr[...] = jnp.full_like(m_scr, -jnp.inf)
        l_scr[...] = jnp.zeros_like(l_scr)
        acc_scr[...] = jnp.zeros_like(acc_scr)

    heads = range(FOX_HEADS)
    hs = [slice(h * FOX_DH, (h + 1) * FOX_DH) for h in heads]

    def score(s_out):
        for h in heads:
            s_out[h] = _dot_nt(q_ref[:, hs[h]], k_ref[:, hs[h]])

    def update(s_in, on_diagonal):
        m_prev = [m_scr[h] for h in heads]
        l_prev = [l_scr[h] for h in heads]
        acc_prev = [acc_scr[h] for h in heads]
        s = [s_in[h] - fk_ref[h:h + 1, :] * LOG2E for h in heads]
        if on_diagonal:
            row = lax.broadcasted_iota(jnp.int32, (tq, tk), 0)
            col = lax.broadcasted_iota(jnp.int32, (tq, tk), 1)
            s = [jnp.where(col <= row, sh, -jnp.inf) for sh in s]
        lane_chunks = [slice(c * LANES, (c + 1) * LANES) for c in range(tk // LANES)]
        m_new, l_new, acc_new = [], [], []
        for h in heads:
            chunks = [s[h][:, lc] for lc in lane_chunks]
            cmax = functools.reduce(jnp.maximum, chunks)
            m_h = jnp.maximum(m_prev[h], jnp.max(cmax, axis=-1, keepdims=True))
            alpha = jnp.exp2(m_prev[h] - m_h)
            p_chunks = [jnp.exp2(ch - m_h) for ch in chunks]
            row_sum = jnp.sum(functools.reduce(jnp.add, p_chunks), axis=-1, keepdims=True)
            p = jnp.concatenate(p_chunks, axis=1).astype(BF16)
            m_new.append(m_h)
            l_new.append(alpha * l_prev[h] + row_sum)
            acc_new.append(alpha * acc_prev[h] + _dot(p, v_ref[:, hs[h]]))
        return m_new, l_new, acc_new

    def step(s_write, s_read):
        @pl.when(t == 0)
        def _():
            score(s_write)

        @pl.when(jnp.logical_and(t >= 1, t <= i))
        def _():
            m_new, l_new, acc_new = update(s_read, False)
            score(s_write)
            for h in heads:
                m_scr[h] = m_new[h]
                l_scr[h] = l_new[h]
                acc_scr[h] = acc_new[h]

        @pl.when(t == i + 1)
        def _():
            _, l_new, acc_new = update(s_read, True)
            for h in heads:
                o_ref[:, hs[h]] = _rms(acc_new[h] / l_new[h], g_ref[:, hs[h]]).astype(o_ref.dtype)

    @pl.when(t % 2 == 0)
    def _():
        step(s_even, s_odd)

    @pl.when(t % 2 == 1)
    def _():
        step(s_odd, s_even)


def _fox(proj, f_t, g_fox, batch, seq, tq=512, tk=512):
    assert tq == tk, "the diagonal-block mask assumes square blocks"
    n = proj.shape[0]
    w = FOX_HEADS * FOX_DH
    nq, nk = seq // tq, seq // tk
    qcol, kcol, vcol = PROJ_FOX_Q // w, PROJ_FOX_K // w, PROJ_FOX_V // w
    return pl.pallas_call(
        functools.partial(_fox_kernel, tq=tq, tk=tk),
        grid=(batch, nq, nk + 1),
        in_specs=[
            pl.BlockSpec((tq, w), lambda b, i, t: (b * nq + i, qcol)),
            pl.BlockSpec((tk, w), lambda b, i, t: (b * nk + jnp.minimum(t, i), kcol)),
            pl.BlockSpec((tk, w), lambda b, i, t: (b * nk + jnp.clip(t - 1, 0, i), vcol)),
            pl.BlockSpec((None, 8, tk), lambda b, i, t: (b, 0, jnp.clip(t - 1, 0, i))),
            pl.BlockSpec((1, w), lambda b, i, t: (0, 0)),
        ],
        out_specs=pl.BlockSpec((tq, w), lambda b, i, t: (b * nq + i, 0)),
        out_shape=jax.ShapeDtypeStruct((n, w), BF16),
        scratch_shapes=[
            pltpu.VMEM((FOX_HEADS, tq, LANES), F32),
            pltpu.VMEM((FOX_HEADS, tq, LANES), F32),
            pltpu.VMEM((FOX_HEADS, tq, FOX_DH), F32),
            pltpu.VMEM((FOX_HEADS, tq, tk), F32),
            pltpu.VMEM((FOX_HEADS, tq, tk), F32),
        ],
        compiler_params=_cparams(("parallel", "parallel", "arbitrary")),
        name="fox",
    )(proj, proj, proj, f_t, g_fox)


def _mem_kv_kernel(m_ref, g_ref, w_ref, kv_ref):
    kv_ref[...] = _dot(_rms(m_ref[...], g_ref[...]).astype(BF16), w_ref[...]).astype(kv_ref.dtype)


def _mem_kv(mem2d, g, w, tm=256):
    n, d = mem2d.shape
    nw = w.shape[1]
    return pl.pallas_call(
        _mem_kv_kernel,
        grid=(n // tm,),
        in_specs=[
            pl.BlockSpec((tm, d), lambda i: (i, 0)),
            pl.BlockSpec((1, d), lambda i: (0, 0)),
            pl.BlockSpec((d, nw), lambda i: (0, 0)),
        ],
        out_specs=pl.BlockSpec((tm, nw), lambda i: (i, 0)),
        out_shape=jax.ShapeDtypeStruct((n, nw), BF16),
        compiler_params=_cparams(("parallel",)),
        name="mem_kv",
    )(mem2d, g, w)


def _mem_attn_kernel(q_ref, k_ref, v_ref, g_ref, o_ref):
    scale = MEM_DH ** -0.5
    for h in range(MEM_HEADS):
        hs = slice(h * MEM_DH, (h + 1) * MEM_DH)
        s = _dot_nt(q_ref[:, hs], k_ref[:, hs]) * scale
        p = jnp.exp(s - jnp.max(s, axis=-1, keepdims=True))
        l = jnp.sum(p, axis=-1, keepdims=True)
        o = _dot((p / l).astype(BF16), v_ref[:, hs])
        o_ref[:, hs] = _rms(o, g_ref[:, hs]).astype(o_ref.dtype)


def _mem_attn(proj, kv, g_mem_out, batch, seq, n_mem, tq=1024):
    n = proj.shape[0]
    w = MEM_HEADS * MEM_DH
    nq = seq // tq
    qcol = 0
    return pl.pallas_call(
        _mem_attn_kernel,
        grid=(n // tq,),
        in_specs=[
            pl.BlockSpec((tq, w), lambda i: (i, qcol)),
            pl.BlockSpec((n_mem, w), lambda i: (i // nq, 0)),
            pl.BlockSpec((n_mem, w), lambda i: (i // nq, 1)),
            pl.BlockSpec((1, w), lambda i: (0, 0)),
        ],
        out_specs=pl.BlockSpec((tq, w), lambda i: (i, 0)),
        out_shape=jax.ShapeDtypeStruct((n, w), BF16),
        compiler_params=_cparams(("parallel",)),
        name="mem_attn",
    )(proj, kv, kv, g_mem_out)


def _out_proj_kernel(x_ref, gla_ref, fox_ref, mem_ref, w1_ref, w2_ref, w3_ref, g_ref, wr_ref, br_ref,
                     x2_ref, xn_ref, logit_ref):
    x2 = (x_ref[...] + _dot(gla_ref[...], w1_ref[...]) + _dot(fox_ref[...], w2_ref[...])
          + _dot(mem_ref[...], w3_ref[...]))
    x2_ref[...] = x2
    xn = _rms(x2, g_ref[...])
    _store_rowmajor(xn_ref, _pack_bf16_pairs(xn))
    xh = xn.astype(BF16)
    xl = (xn - xh.astype(F32)).astype(BF16)
    hi = _dot(xh, wr_ref[...])
    logit_ref[...] = hi[:, :LANES] + hi[:, LANES:] + _dot(xl, wr_ref[:, :LANES]) + br_ref[...]


def _out_proj(x2d, gla, fox, memo, w_out, g_ffn, wr, br, tm=512):
    n, d = x2d.shape
    w1, w2 = gla.shape[1], fox.shape[1]
    const = lambda i: (0, 0)
    return pl.pallas_call(
        _out_proj_kernel,
        grid=(n // tm,),
        in_specs=[
            pl.BlockSpec((tm, d), lambda i: (i, 0)),
            pl.BlockSpec((tm, w1), lambda i: (i, 0)),
            pl.BlockSpec((tm, w2), lambda i: (i, 0)),
            pl.BlockSpec((tm, w2), lambda i: (i, 0)),
            pl.BlockSpec((w1, d), lambda i: (0, 0)),
            pl.BlockSpec((w2, d), lambda i: (w1 // w2, 0)),
            pl.BlockSpec((w2, d), lambda i: (w1 // w2 + 1, 0)),
            pl.BlockSpec((1, d), const),
            pl.BlockSpec((d, 2 * LANES), const),
            pl.BlockSpec((1, LANES), const),
        ],
        out_specs=[
            pl.BlockSpec((tm, d), lambda i: (i, 0)),
            pl.BlockSpec((tm * (d // 2 // LANES), LANES), lambda i: (i, 0)),
            pl.BlockSpec((tm, LANES), lambda i: (i, 0)),
        ],
        out_shape=[
            jax.ShapeDtypeStruct((n, d), F32),
            jax.ShapeDtypeStruct((n * (d // 2 // LANES), LANES), jnp.uint32),
            jax.ShapeDtypeStruct((n, LANES), F32),
        ],
        compiler_params=_cparams(("parallel",)),
        name="out_proj",
    )(x2d, gla, fox, memo, w_out, w_out, w_out, g_ffn, wr, br)


def _route_kernel(l_ref, idx_ref, gate_ref, rank_ref, cnt_ref, carry_scr, *, tb):
    @pl.when(pl.program_id(0) == 0)
    def _():
        carry_scr[...] = jnp.zeros_like(carry_scr)

    lane = lax.broadcasted_iota(jnp.int32, (tb, LANES), 1)
    logit = jnp.where(lane < N_EXPERTS, l_ref[...], -jnp.inf)
    vals, hots = [], []
    idx_out = jnp.zeros((tb, LANES), jnp.int32)
    for k in range(TOP_K):
        m = jnp.max(logit, axis=-1, keepdims=True)
        ik = jnp.min(jnp.where(logit == m, lane, LANES), axis=-1, keepdims=True)
        hot = lane == ik
        logit = jnp.where(hot, -jnp.inf, logit)
        vals.append(m)
        hots.append(hot)
        idx_out = jnp.where(lane == k, ik, idx_out)
    idx_ref[...] = idx_out

    e = [jnp.exp(v - vals[0]) for v in vals]
    den = e[0] + e[1] + e[2] + e[3]
    gate_out = jnp.zeros((tb, LANES), F32)
    for k in range(TOP_K):
        gate_out = jnp.where(lane == k, e[k] / den, gate_out)
    gate_ref[...] = gate_out

    member = jnp.zeros((tb, LANES), F32)
    for hot in hots:
        member = member + hot.astype(F32)
    row = lax.broadcasted_iota(jnp.int32, (tb, tb), 0)
    col = lax.broadcasted_iota(jnp.int32, (tb, tb), 1)
    before = (col < row).astype(BF16)
    rank = _dot(before, member.astype(BF16)) + carry_scr[...]
    rank_out = jnp.zeros((tb, LANES), F32)
    for k in range(TOP_K):
        rk = jnp.sum(jnp.where(hots[k], rank, 0.0), axis=-1, keepdims=True)
        rank_out = jnp.where(lane == k, rk, rank_out)
    rank_ref[...] = rank_out.astype(jnp.int32)
    total = carry_scr[...] + jnp.sum(member, axis=0, keepdims=True)
    carry_scr[...] = total
    cnt_ref[...] = total.astype(jnp.int32)


def _route(logits, tb=512):
    n = logits.shape[0]
    blk = pl.BlockSpec((tb, LANES), lambda i: (i, 0))
    return pl.pallas_call(
        functools.partial(_route_kernel, tb=tb),
        grid=(n // tb,),
        in_specs=[blk],
        out_specs=[blk, blk, blk, pl.BlockSpec((1, LANES), lambda i: (0, 0))],
        out_shape=[
            jax.ShapeDtypeStruct((n, LANES), jnp.int32),
            jax.ShapeDtypeStruct((n, LANES), F32),
            jax.ShapeDtypeStruct((n, LANES), jnp.int32),
            jax.ShapeDtypeStruct((1, LANES), jnp.int32),
        ],
        scratch_shapes=[pltpu.VMEM((1, LANES), F32)],
        compiler_params=_cparams(("arbitrary",)),
        name="route",
    )(logits)


def _dispatch_kernel(zblk_ref, zok_ref, used_ref, pos_ref, xn_ref, xr_hbm, zero_scr, sem,
                     *, tb, pitch, n_blocks_max):
    blk_rows = MOE_TM * pitch

    def zero_copy(b):
        dst = xr_hbm.at[pl.ds(pl.multiple_of(b * blk_rows, blk_rows), blk_rows), :]
        return pltpu.make_async_copy(zero_scr, dst, sem.at[1])

    @pl.when(pl.program_id(0) == 0)
    def _():
        zero_scr[...] = jnp.zeros_like(zero_scr)

        def per_expert(action):
            def body(e, carry):
                @pl.when(zok_ref[e] == 1)
                def _():
                    action(zero_copy(zblk_ref[e]))
                return carry
            lax.fori_loop(0, N_EXPERTS, body, 0)

        def per_tail(action):
            def body(b, carry):
                action(zero_copy(b))
                return carry
            lax.fori_loop(used_ref[0], n_blocks_max, body, 0)

        per_expert(lambda c: c.start())
        per_tail(lambda c: c.start())
        per_expert(lambda c: c.wait())
        per_tail(lambda c: c.wait())

    def issue(t, carry):
        src = xn_ref.at[pl.ds(pl.multiple_of(t * pitch, pitch), pitch), :]
        for k in range(TOP_K):
            p = pos_ref[0, t * TOP_K + k]
            dst = xr_hbm.at[pl.ds(pl.multiple_of(p * pitch, pitch), pitch), :]
            pltpu.make_async_copy(src, dst, sem.at[0]).start(priority=k % 2)
        return carry

    lax.fori_loop(0, tb, issue, 0, unroll=4)
    for k in range(TOP_K):
        pltpu.make_async_copy(xn_ref, xr_hbm.at[pl.ds(0, tb * pitch), :], sem.at[0]).wait()


def _dispatch(xn_rm, pos, zblk, zok, n_used, n_blocks_max, pitch, tb=1024):
    n = xn_rm.shape[0] // pitch
    nb = n // tb
    grid_spec = pltpu.PrefetchScalarGridSpec(
        num_scalar_prefetch=3,
        grid=(nb,),
        in_specs=[
            pl.BlockSpec((None, 1, TOP_K * tb), lambda i, *_: (i, 0, 0), memory_space=pltpu.SMEM),
            pl.BlockSpec((tb * pitch, LANES), lambda i, *_: (i, 0)),
        ],
        out_specs=pl.BlockSpec(memory_space=pl.ANY),
        scratch_shapes=[pltpu.VMEM((MOE_TM * pitch, LANES), xn_rm.dtype), pltpu.SemaphoreType.DMA((2,))],
    )
    return pl.pallas_call(
        functools.partial(_dispatch_kernel, tb=tb, pitch=pitch, n_blocks_max=n_blocks_max),
        grid_spec=grid_spec,
        out_shape=jax.ShapeDtypeStruct((n_blocks_max * MOE_TM * pitch, LANES), xn_rm.dtype),
        compiler_params=_cparams(("arbitrary",)),
        name="dispatch",
    )(zblk, zok, n_used, pos.reshape(nb, 1, TOP_K * tb), xn_rm)


def _stream_row_blocks(cnt, in_copy, out_copy, compute):
    for ahead in range(MOE_NBUF - 1):
        @pl.when(cnt > ahead)
        def _():
            in_copy(ahead, ahead).start()

    def body(b, carry):
        slot = b % MOE_NBUF

        @pl.when(b + MOE_NBUF - 1 < cnt)
        def _():
            in_copy(b + MOE_NBUF - 1, (b + MOE_NBUF - 1) % MOE_NBUF).start()

        in_copy(b, slot).wait()

        @pl.when(b >= MOE_NBUF)
        def _():
            out_copy(b - MOE_NBUF, slot).wait()

        compute(slot)
        out_copy(b, slot).start(priority=1)
        return carry

    lax.fori_loop(0, cnt, body, 0)

    for back in range(MOE_NBUF, 0, -1):
        @pl.when(cnt >= back)
        def _():
            out_copy(cnt - back, (cnt - back) % MOE_NBUF).wait()


def _stream_expert_rows(first, cnt, in_copy, out_copy, compute):
    n_units = cnt // MOE_UNIT
    _stream_row_blocks(n_units, in_copy(first, MOE_UNIT), out_copy(first, MOE_UNIT), compute(MOE_UNIT))
    done = n_units * MOE_UNIT
    _stream_row_blocks(cnt - done, in_copy(first + done, 1), out_copy(first + done, 1), compute(1))


def _zero_fill_blocks(first, last, zero_src, dst_copy):
    def start(b, carry):
        dst_copy(b).start()
        return carry

    def wait(b, carry):
        dst_copy(b).wait()
        return carry

    zero_src[...] = jnp.zeros_like(zero_src)
    lax.fori_loop(first, last, start, 0)
    lax.fori_loop(first, last, wait, 0)


def _moe_up_kernel(bs_ref, bc_ref, used_ref, x_hbm, w_hbm, bg_ref, bu_ref, act_hbm,
                   w_scr, xbuf, lhs_scr, obuf, sem_w, sem_x, sem_o,
                   *, nt, tn, d_ff, pitch, n_blocks_max):
    g = pl.program_id(0)
    ng = pl.num_programs(0)
    e = g // nt
    col = pl.multiple_of((g % nt) * tn, tn)
    blk_rows = MOE_TM * pitch
    wslot = g % 2

    def w_copies(step, slot):
        ee = step // nt
        cc = pl.multiple_of((step % nt) * tn, tn)
        return (pltpu.make_async_copy(w_hbm.at[ee, :, pl.ds(cc, tn)], w_scr.at[slot, 0], sem_w.at[slot, 0]),
                pltpu.make_async_copy(w_hbm.at[ee, :, pl.ds(d_ff + cc, tn)], w_scr.at[slot, 1],
                                      sem_w.at[slot, 1]))

    @pl.when(g == 0)
    def _():
        for c in w_copies(0, 0):
            c.start(priority=1)

    @pl.when(g + 1 < ng)
    def _():
        for c in w_copies(g + 1, 1 - wslot):
            c.start(priority=1)

    for c in w_copies(g, wslot):
        c.wait()

    start = bs_ref[e]

    def x_copy(first, nblk):
        def make(u, slot):
            r0 = pl.multiple_of((first + u * nblk) * blk_rows, blk_rows)
            return pltpu.make_async_copy(x_hbm.at[pl.ds(r0, nblk * blk_rows), :],
                                         xbuf.at[slot, pl.ds(0, nblk * blk_rows), :], sem_x.at[slot])
        return make

    def o_copy(first, nblk):
        def make(u, slot):
            r0 = pl.multiple_of((first + u * nblk) * MOE_TM, MOE_TM)
            return pltpu.make_async_copy(obuf.at[slot, pl.ds(0, nblk * MOE_TM), :],
                                         act_hbm.at[pl.ds(r0, nblk * MOE_TM), pl.ds(col, tn)], sem_o.at[slot])
        return make

    def compute(nblk):
        rows = nblk * MOE_TM

        def run(slot):
            half = pitch * LANES
            for c in range(pitch):
                w = xbuf[slot, pl.ds(c, rows, stride=pitch), :]
                lhs_scr[0:rows, c * LANES:(c + 1) * LANES] = _unpack_lo(w)
                lhs_scr[0:rows, half + c * LANES:half + (c + 1) * LANES] = _unpack_hi(w)
            x = lhs_scr[0:rows, :]
            gate = jnp.minimum(_dot(x, w_scr[wslot, 0]) + bg_ref[...], SWIGLU_LIMIT)
            up = jnp.clip(_dot(x, w_scr[wslot, 1]) + bu_ref[...], -SWIGLU_LIMIT, SWIGLU_LIMIT)
            obuf[slot, 0:rows, :] = (gate * jax.nn.sigmoid(SWIGLU_ALPHA * gate) * (up + 1.0)).astype(obuf.dtype)
        return run

    _stream_expert_rows(start, bc_ref[e], x_copy, o_copy, compute)

    @pl.when(e == N_EXPERTS - 1)
    def _():
        zero_blk = obuf.at[0, pl.ds(0, MOE_TM), :]

        def tail_copy(b):
            r0 = pl.multiple_of(b * MOE_TM, MOE_TM)
            return pltpu.make_async_copy(zero_blk, act_hbm.at[pl.ds(r0, MOE_TM), pl.ds(col, tn)], sem_o.at[0])
        _zero_fill_blocks(used_ref[0], n_blocks_max, zero_blk, tail_copy)


def _moe_up(x_rows_rm, w_up, b_up, blk_start, blk_count, n_used, n_blocks_max, tn=1024):
    n_exp, d, two_ff = w_up.shape
    d_ff = two_ff // 2
    pitch = d // 2 // LANES
    nt = d_ff // tn
    up_off = d_ff // tn
    grid_spec = pltpu.PrefetchScalarGridSpec(
        num_scalar_prefetch=3,
        grid=(n_exp * nt,),
        in_specs=[
            pl.BlockSpec(memory_space=pl.ANY),
            pl.BlockSpec(memory_space=pl.ANY),
            pl.BlockSpec((None, 1, tn), lambda g, *_: (g // nt, 0, g % nt)),
            pl.BlockSpec((None, 1, tn), lambda g, *_: (g // nt, 0, g % nt + up_off)),
        ],
        out_specs=pl.BlockSpec(memory_space=pl.ANY),
        scratch_shapes=[
            pltpu.VMEM((2, 2, d, tn), F32),
            pltpu.VMEM((MOE_NBUF, MOE_UNIT * MOE_TM * pitch, LANES), jnp.uint32),
            pltpu.VMEM((MOE_UNIT * MOE_TM, d), F32),
            pltpu.VMEM((MOE_NBUF, MOE_UNIT * MOE_TM, tn), BF16),
            pltpu.SemaphoreType.DMA((2, 2)),
            pltpu.SemaphoreType.DMA((MOE_NBUF,)),
            pltpu.SemaphoreType.DMA((MOE_NBUF,)),
        ],
    )
    return pl.pallas_call(
        functools.partial(_moe_up_kernel, nt=nt, tn=tn, d_ff=d_ff, pitch=pitch, n_blocks_max=n_blocks_max),
        grid_spec=grid_spec,
        out_shape=jax.ShapeDtypeStruct((n_blocks_max * MOE_TM, d_ff), BF16),
        compiler_params=_cparams(("arbitrary",)),
        name="moe_up",
    )(blk_start, blk_count, n_used, x_rows_rm, w_up, b_up, b_up)


def _moe_down_kernel(bs_ref, bc_ref, used_ref, a_hbm, w_hbm, b_ref, y_hbm,
                     w_scr, abuf, obuf, sem_w, sem_a, sem_o, *, pitch, n_blocks_max):
    e = pl.program_id(0)
    blk_rows = MOE_TM * pitch
    wslot = e % 2

    def w_copy(ee, slot):
        return pltpu.make_async_copy(w_hbm.at[ee], w_scr.at[slot], sem_w.at[slot])

    @pl.when(e == 0)
    def _():
        w_copy(0, 0).start(priority=1)

    @pl.when(e + 1 < pl.num_programs(0))
    def _():
        w_copy(e + 1, 1 - wslot).start(priority=1)

    w_copy(e, wslot).wait()
    start = bs_ref[e]

    def a_copy(first, nblk):
        def make(u, slot):
            r0 = pl.multiple_of((first + u * nblk) * MOE_TM, MOE_TM)
            return pltpu.make_async_copy(a_hbm.at[pl.ds(r0, nblk * MOE_TM), :],
                                         abuf.at[slot, pl.ds(0, nblk * MOE_TM), :], sem_a.at[slot])
        return make

    def o_copy(first, nblk):
        def make(u, slot):
            r0 = pl.multiple_of((first + u * nblk) * blk_rows, blk_rows)
            return pltpu.make_async_copy(obuf.at[slot, pl.ds(0, nblk * blk_rows), :],
                                         y_hbm.at[pl.ds(r0, nblk * blk_rows), :], sem_o.at[slot])
        return make

    def compute(nblk):
        rows = nblk * MOE_TM

        def run(slot):
            y = _dot(abuf[slot, 0:rows, :].astype(F32), w_scr[wslot]) + b_ref[...]
            _store_rowmajor(obuf.at[slot], _pack_bf16_pairs(y))
        return run

    _stream_expert_rows(start, bc_ref[e], a_copy, o_copy, compute)

    @pl.when(e == N_EXPERTS - 1)
    def _():
        zero_blk = obuf.at[0, pl.ds(0, blk_rows), :]

        def tail_copy(b):
            r0 = pl.multiple_of(b * blk_rows, blk_rows)
            return pltpu.make_async_copy(zero_blk, y_hbm.at[pl.ds(r0, blk_rows), :], sem_o.at[0])
        _zero_fill_blocks(used_ref[0], n_blocks_max, zero_blk, tail_copy)


def _moe_down(act, w_down, b_down, blk_start, blk_count, n_used, n_blocks_max):
    n_rows, d_ff = act.shape
    n_exp, _, d = w_down.shape
    pitch = d // 2 // LANES
    grid_spec = pltpu.PrefetchScalarGridSpec(
        num_scalar_prefetch=3,
        grid=(n_exp,),
        in_specs=[
            pl.BlockSpec(memory_space=pl.ANY),
            pl.BlockSpec(memory_space=pl.ANY),
            pl.BlockSpec((None, 1, d), lambda e, *_: (e, 0, 0)),
        ],
        out_specs=pl.BlockSpec(memory_space=pl.ANY),
        scratch_shapes=[
            pltpu.VMEM((2, d_ff, d), F32),
            pltpu.VMEM((MOE_NBUF, MOE_UNIT * MOE_TM, d_ff), BF16),
            pltpu.VMEM((MOE_NBUF, MOE_UNIT * MOE_TM * pitch, LANES), jnp.uint32),
            pltpu.SemaphoreType.DMA((2,)),
            pltpu.SemaphoreType.DMA((MOE_NBUF,)),
            pltpu.SemaphoreType.DMA((MOE_NBUF,)),
        ],
    )
    return pl.pallas_call(
        functools.partial(_moe_down_kernel, pitch=pitch, n_blocks_max=n_blocks_max),
        grid_spec=grid_spec,
        out_shape=jax.ShapeDtypeStruct((n_rows * pitch, LANES), jnp.uint32),
        compiler_params=_cparams(("arbitrary",)),
        name="moe_down",
    )(blk_start, blk_count, n_used, act, w_down, b_down)


def _combine_kernel(pos0_ref, posn_ref, y_hbm, x2_ref, gate_ref, g_ref, o_ref, buf_even, buf_odd, sem,
                    *, tb, pitch):
    i = pl.program_id(0)
    last = pl.num_programs(0) - 1
    n_rows = TOP_K * tb

    def row_copy(p_ref, r, buf, sem_slot):
        src = y_hbm.at[pl.ds(pl.multiple_of(p_ref[0, r] * pitch, pitch), pitch), :]
        return pltpu.make_async_copy(src, buf.at[pl.ds(r * pitch, pitch), :], sem.at[sem_slot])

    def wait_block(buf, sem_slot):
        pltpu.make_async_copy(y_hbm.at[pl.ds(0, n_rows * pitch), :], buf, sem.at[sem_slot]).wait()

    @pl.when(i == 0)
    def _():
        def issue(j, carry):
            for half in range(2):
                row_copy(pos0_ref, 2 * j + half, buf_even, 0).start(priority=half)
            return carry
        lax.fori_loop(0, n_rows // 2, issue, 0, unroll=4)

    def step(buf_cur, sem_cur, buf_nxt, sem_nxt):
        wait_block(buf_cur, sem_cur)
        for r in range(n_rows):
            row_copy(posn_ref, r, buf_nxt, sem_nxt).start(priority=r % 2)
        gates = gate_ref[...]
        half = pitch * LANES
        ssq = jnp.zeros((tb, 1), F32)
        for c in range(pitch):
            lo = slice(c * LANES, (c + 1) * LANES)
            hi = slice(half + c * LANES, half + (c + 1) * LANES)
            z_lo = x2_ref[:, lo]
            z_hi = x2_ref[:, hi]
            for k in range(TOP_K):
                w = buf_cur[pl.ds(k * tb * pitch + c, tb, stride=pitch), :]
                z_lo = z_lo + gates[:, k:k + 1] * _unpack_lo(w)
                z_hi = z_hi + gates[:, k:k + 1] * _unpack_hi(w)
            o_ref[:, lo] = z_lo
            o_ref[:, hi] = z_hi
            ssq = ssq + (jnp.sum(z_lo * z_lo, axis=-1, keepdims=True)
                         + jnp.sum(z_hi * z_hi, axis=-1, keepdims=True))
        o_ref[...] = o_ref[...] * lax.rsqrt(ssq * (1.0 / (2 * half)) + EPS) * g_ref[...]

        @pl.when(i == last)
        def _():
            wait_block(buf_nxt, sem_nxt)

    @pl.when(i % 2 == 0)
    def _():
        step(buf_even, 0, buf_odd, 1)

    @pl.when(i % 2 == 1)
    def _():
        step(buf_odd, 1, buf_even, 0)


def _combine(y_rows_rm, pos, gates, x2, g_final, tb=512):
    n, d = x2.shape
    nb = n // tb
    pitch = d // 2 // LANES
    pos_blk = pos.reshape(nb, tb, TOP_K).transpose(0, 2, 1).reshape(nb, 1, TOP_K * tb)
    pos_spec = lambda imap: pl.BlockSpec((None, 1, TOP_K * tb), imap, memory_space=pltpu.SMEM)
    return pl.pallas_call(
        functools.partial(_combine_kernel, tb=tb, pitch=pitch),
        grid=(nb,),
        in_specs=[
            pos_spec(lambda i: (0, 0, 0)),
            pos_spec(lambda i: (jnp.minimum(i + 1, nb - 1), 0, 0)),
            pl.BlockSpec(memory_space=pl.ANY),
            pl.BlockSpec((tb, d), lambda i: (i, 0)),
            pl.BlockSpec((tb, LANES), lambda i: (i, 0)),
            pl.BlockSpec((1, d), lambda i: (0, 0)),
        ],
        out_specs=pl.BlockSpec((tb, d), lambda i: (i, 0)),
        out_shape=jax.ShapeDtypeStruct((n, d), F32),
        scratch_shapes=[pltpu.VMEM((TOP_K * tb * pitch, LANES), jnp.uint32),
                        pltpu.VMEM((TOP_K * tb * pitch, LANES), jnp.uint32),
                        pltpu.SemaphoreType.DMA((2,))],
        compiler_params=_cparams(("arbitrary",)),
        name="combine",
    )(pos_blk, pos_blk, y_rows_rm, x2, gates, g_final)


def _routing_tables(idx, rank, cnt):
    counts = cnt[0, :N_EXPERTS]
    blk_count = (counts + MOE_TM - 1) // MOE_TM
    blk_end = jnp.cumsum(blk_count)
    blk_start = blk_end - blk_count
    hot = idx[:, :TOP_K, None] == jnp.arange(N_EXPERTS, dtype=jnp.int32)
    pos = jnp.sum(jnp.where(hot, blk_start * MOE_TM, 0), axis=-1) + rank[:, :TOP_K]
    return pos.astype(jnp.int32), blk_start.astype(jnp.int32), blk_count.astype(jnp.int32)


def kernel(x, mem, g_attn_norm, g_mem_norm, w_in, w_gla_a2, b_gla_a, g_gla_out, b_fox_f, g_fox_out,
           w_mem_kv, g_mem_out, w_out, g_ffn_norm, w_router, b_router, w_moe_up, b_moe_up,
           w_moe_down, b_moe_down, g_final):
    batch, seq, d = x.shape
    n_mem = mem.shape[1]
    n = batch * seq
    depth = w_in.shape[0]
    assert depth == 1, "the combine kernel applies the final norm, so exactly one layer is supported"
    kw = GLA_HEADS * GLA_DK
    vw = GLA_HEADS * GLA_DV
    fw = FOX_HEADS * FOX_DH
    mw = MEM_HEADS * MEM_DH
    o_q, o_k, o_v, o_g = 0, kw, 2 * kw, 2 * kw + vw
    o_a = o_g + vw
    o_fq = o_a + GLA_LOWRANK
    o_fk, o_fv = o_fq + fw, o_fq + 2 * fw
    o_ff = o_fq + 3 * fw
    o_mq = o_ff + FOX_HEADS
    f_lane = GLA_LOWRANK

    xf = x.reshape(n, d)
    for l in range(depth):
        fox_q_scale = FOX_DH ** -0.5 * LOG2E
        col_scale = jnp.ones((w_in.shape[2],), F32).at[o_fq:o_fk].set(fox_q_scale)
        wt = (jnp.transpose(w_in[l]) * col_scale[:, None]).astype(BF16)
        wb_t = jnp.zeros((LANES, d), BF16).at[:GLA_LOWRANK].set(wt[o_a:o_fq])
        wb_t = wb_t.at[f_lane:f_lane + FOX_HEADS].set(wt[o_ff:o_mq])
        in_tn = o_ff - o_fq
        assert (o_a - o_q) % in_tn == 0 and PROJ_FOX_Q == o_a - o_q
        row_offs = tuple(range(o_q, o_a, in_tn)) + (o_fq,)
        proj, projm, small = _in_proj(xf, g_attn_norm[l].reshape(1, d), wt, wt[o_mq:o_mq + mw], wb_t,
                                      row_offs, in_tn)

        wa2p = jnp.zeros((LANES, kw), F32).at[:GLA_LOWRANK].set(w_gla_a2[l]).astype(BF16)
        bfv = jnp.zeros((1, LANES), F32).at[0, f_lane:f_lane + FOX_HEADS].set(b_fox_f[l])
        gla, f_cum = _gla(proj, small, wa2p, b_gla_a[l].reshape(1, kw), bfv,
                          g_gla_out[l].reshape(1, vw), batch, seq)

        f_t = f_cum.reshape(batch, seq, LANES)[:, :, f_lane:f_lane + FOX_HEADS].transpose(0, 2, 1)
        f_t = jnp.concatenate([f_t, jnp.zeros_like(f_t)], axis=1)
        fox = _fox(proj, f_t, g_fox_out[l].reshape(1, fw), batch, seq)

        kv = _mem_kv(mem.reshape(batch * n_mem, d), g_mem_norm[l].reshape(1, d), w_mem_kv[l].astype(BF16))
        memo = _mem_attn(projm, kv, g_mem_out[l].reshape(1, mw), batch, seq, n_mem)

        wr = jnp.zeros((d, LANES), F32).at[:, :N_EXPERTS].set(w_router[l])
        wr_hi = wr.astype(BF16)
        wr = jnp.concatenate([wr_hi, (wr - wr_hi.astype(F32)).astype(BF16)], axis=1)
        br =jnp.zeros((1, LANES), F32).at[0, :N_EXPERTS].set(b_router[l])
        x2, xn, logits = _out_proj(xf, gla, fox, memo, w_out[l].astype(BF16),
                                   g_ffn_norm[l].reshape(1, d), wr, br)

        idx, gates, rank, cnt = _route(logits)
        pos, blk_start, blk_count = _routing_tables(idx, rank, cnt)
        n_blocks_max = -(-(n * TOP_K + N_EXPERTS * (MOE_TM - 1)) // MOE_TM)
        blk_end = blk_start + blk_count
        n_used = blk_end[-1:]
        x_rows = _dispatch(xn, pos, blk_end - 1, (blk_count > 0).astype(jnp.int32), n_used,
                           n_blocks_max, d // 2 // LANES)
        d_ff = w_moe_up.shape[3] // 2
        act = _moe_up(x_rows, w_moe_up[l], b_moe_up[l].reshape(N_EXPERTS, 1, 2 * d_ff),
                      blk_start, blk_count, n_used, n_blocks_max)
        y_rows = _moe_down(act, w_moe_down[l], b_moe_down[l].reshape(N_EXPERTS, 1, d),
                           blk_start, blk_count, n_used, n_blocks_max)
        xf = _combine(y_rows, pos, gates, x2, g_final.reshape(1, d))
    return xf.reshape(batch, seq, d)
```
